```python
import jax
import jax.numpy as jnp
from jax import lax
import numpy as np


D_MODEL = 1024
BATCH = 2
SEQ = 16384
DEPTH = 2

GRID_W = 64
CTX_LEN = 256
HEAD_DIM = 64
ATTN_HEADS = 8
ATTN_KV_HEADS = 2
ATTN_REP = ATTN_HEADS // ATTN_KV_HEADS
Q_BLOCK = 128
ROPE_THETA = 10000.0
ROPE_AXIS_DIM = HEAD_DIM // 2
HG_HEADS = 4
HG_DK = 64
HG_DV = 64
ML_HEADS = 4
ML_DK = 64
ML_DV = 64
ML_CONV = 3
CHUNK = 64
FF_DIM = 4 * D_MODEL
N_MOD = 6
EPS = 1e-6
ATTN_W = ATTN_HEADS * HEAD_DIM
KV_W = ATTN_KV_HEADS * HEAD_DIM
HG_W = HG_HEADS * HG_DK
HG_VW = HG_HEADS * HG_DV
ML_W = ML_HEADS * ML_DK
ML_VW = ML_HEADS * ML_DV
MIX_WIDTH = ATTN_W + HG_VW + ML_VW
IN_SIZES = (ATTN_W, KV_W, KV_W,
            HG_W, HG_VW, HG_W, HG_W, HG_VW,
            ML_W, ML_W, ML_VW, ML_VW, ML_HEADS, ML_HEADS, ML_HEADS, ML_HEADS)
N_IN = sum(IN_SIZES)
ML_FGATE_FWD = 14
ML_FGATE_BWD = 15

kernel_name = 'hybrid_parallel_heads_dit_block'


def split_columns(p):
    idx = np.cumsum(np.array(IN_SIZES))[:-1].tolist()
    return jnp.split(p, idx, axis=-1)


def rmsnorm(x, g):
    xf = x.astype(jnp.float32)
    y = xf * lax.rsqrt(jnp.mean(xf * xf, axis=-1, keepdims=True) + EPS)
    return (y * g.astype(jnp.float32)).astype(x.dtype)


def modulate(x, shift, scale):
    return x * (1 + scale) + shift


def to_heads(x, h):
    b, n, _ = x.shape
    return x.reshape(b, n, h, -1).transpose(0, 2, 1, 3)


def from_heads(x):
    b, h, n, d = x.shape
    return x.transpose(0, 2, 1, 3).reshape(b, n, h * d)


def rotate_pairs(x, ang):
    half = x.shape[-1] // 2
    cos = jnp.cos(ang)[:, None, :].astype(x.dtype)
    sin = jnp.sin(ang)[:, None, :].astype(x.dtype)
    x1, x2 = x[..., :half], x[..., half:]
    return jnp.concatenate([x1 * cos - x2 * sin, x2 * cos + x1 * sin], axis=-1)


def rope_2d(x, ang_row, ang_col):
    return jnp.concatenate([rotate_pairs(x[..., :ROPE_AXIS_DIM], ang_row),
                            rotate_pairs(x[..., ROPE_AXIS_DIM:], ang_col)], axis=-1)


def sdpa(q, k, v):
    s = jnp.einsum('bgrtd,bgmd->bgrtm', q, k).astype(jnp.float32) * (HEAD_DIM ** -0.5)
    p = jax.nn.softmax(s, axis=-1).astype(v.dtype)
    return jnp.einsum('bgrtm,bgmd->bgrtd', p, v)


def attention_mixer(lat, ctx, gq, gk, ang_row, ang_col, need_ctx):
    q, k, v = lat
    qc, kc, vc = ctx
    bsz, n, _ = q.shape
    nc = qc.shape[1]

    def split_heads(t, h):
        return t.reshape(t.shape[0], t.shape[1], h, HEAD_DIM)

    q = rope_2d(rmsnorm(split_heads(q, ATTN_HEADS), gq), ang_row, ang_col)
    k = rope_2d(rmsnorm(split_heads(k, ATTN_KV_HEADS), gk), ang_row, ang_col)
    v = split_heads(v, ATTN_KV_HEADS)
    kc = rmsnorm(split_heads(kc, ATTN_KV_HEADS), gk).transpose(0, 2, 1, 3)
    vc = split_heads(vc, ATTN_KV_HEADS).transpose(0, 2, 1, 3)
    keys = jnp.concatenate([kc, k.transpose(0, 2, 1, 3)], axis=2)
    vals = jnp.concatenate([vc, v.transpose(0, 2, 1, 3)], axis=2)
    nb = n // Q_BLOCK
    qb = q.reshape(bsz, nb, Q_BLOCK, ATTN_KV_HEADS, ATTN_REP, HEAD_DIM).transpose(1, 0, 3, 4, 2, 5)
    ob = lax.map(lambda qi: sdpa(qi, keys, vals), qb)
    out = ob.transpose(1, 0, 4, 2, 3, 5).reshape(bsz, n, ATTN_W)
    if not need_ctx:
        return out, None
    qcg = rmsnorm(split_heads(qc, ATTN_HEADS), gq)
    qcg = qcg.reshape(bsz, nc, ATTN_KV_HEADS, ATTN_REP, HEAD_DIM).transpose(0, 2, 3, 1, 4)
    oc = sdpa(qcg, kc, vc).transpose(0, 3, 1, 2, 4).reshape(bsz, nc, ATTN_W)
    return out, oc


def chunk_scan(step, init, xs):
    bsz, h, n = xs[0].shape[:3]
    nch = n // CHUNK
    xs_c = tuple(jnp.moveaxis(a.reshape((bsz, h, nch, CHUNK) + a.shape[3:]), 2, 0) for a in xs)
    state, out = lax.scan(step, init, xs_c)
    out = jnp.moveaxis(out, 0, 2)
    return out.reshape((bsz, h, n) + out.shape[4:]), state


def run_direction(step, init, ctx_xs, lat_xs, reverse):
    if reverse:
        ctx_xs = tuple(jnp.flip(a, axis=2) for a in ctx_xs)
        lat_xs = tuple(jnp.flip(a, axis=2) for a in lat_xs)
    out_c, state_c = chunk_scan(step, init, ctx_xs)
    out_l, _ = chunk_scan(step, state_c, lat_xs)
    if reverse:
        out_c, out_l = jnp.flip(out_c, axis=2), jnp.flip(out_l, axis=2)
    return out_c, out_l


def hgrn_step(S, xs):
    q, k, lf, v = xs
    L = q.shape[2]
    tri = jnp.tril(jnp.ones((L, L), bool))[:, :, None]
    b = jnp.cumsum(lf, axis=2)
    diff = b[:, :, :, None, :] - b[:, :, None, :, :]
    dec = jnp.where(tri, jnp.exp(jnp.where(tri, diff, 0.0)), 0.0)
    scores = jnp.einsum('bhtsk,bhtk,bhsk->bhts', dec, q, k)
    o = jnp.einsum('bhts,bhsv->bhtv', scores, v) + jnp.einsum('bhtk,bhkv->bhtv', q * jnp.exp(b), S)
    b_last = b[:, :, -1:, :]
    S_new = jnp.exp(b_last[:, :, 0, :, None]) * S + jnp.einsum('bhsk,bhsv->bhkv', k * jnp.exp(b_last - b), v)
    return S_new, o


def hgrn2_mixer(lat, ctx, lb_fwd, lb_bwd, g_norm, need_ctx):
    def prep(q, i, zf, zb):
        q = to_heads(q, HG_HEADS).astype(jnp.float32)
        v = to_heads(i, HG_HEADS).astype(jnp.float32)

        def gate(z, lb):
            lb = lb.astype(jnp.float32).reshape(HG_HEADS, 1, HG_DK)
            f = lb + (1.0 - lb) * jax.nn.sigmoid(to_heads(z, HG_HEADS).astype(jnp.float32))
            return 1.0 - f, jnp.log(f)

        k_f, lf_f = gate(zf, lb_fwd)
        k_b, lf_b = gate(zb, lb_bwd)
        return (q, k_f, lf_f, v), (q, k_b, lf_b, v)

    def readout(o, g):
        o = rmsnorm(o.transpose(0, 2, 1, 3), g_norm)
        return (o.reshape(g.shape) * jax.nn.silu(g.astype(jnp.float32))).astype(g.dtype)

    lat_f, lat_b = prep(*lat[:4])
    ctx_f, ctx_b = prep(*ctx[:4])
    s0 = jnp.zeros((lat[0].shape[0], HG_HEADS, HG_DK, HG_DV), jnp.float32)
    oc_f, o_f = run_direction(hgrn_step, s0, ctx_f, lat_f, False)
    oc_b, o_b = run_direction(hgrn_step, s0, ctx_b, lat_b, True)
    out = readout(o_f + o_b, lat[4])
    oc = readout(oc_f + oc_b, ctx[4]) if need_ctx else None
    return out, oc


def mlstm_step(state, xs):
    C, n, m = state
    q, k, v, ig, lf = xs
    L = q.shape[2]
    tri = jnp.tril(jnp.ones((L, L), bool))
    b = jnp.cumsum(lf, axis=-1)
    log_d = jnp.where(tri, b[..., :, None] - b[..., None, :] + ig[..., None, :], -jnp.inf)
    m_t = jnp.maximum(b + m[..., None], jnp.max(log_d, axis=-1))
    d = jnp.exp(log_d - m_t[..., None])
    g = jnp.exp(b + m[..., None] - m_t)
    s = jnp.einsum('bhtd,bhsd->bhts', q, k) * d
    num = g[..., None] * jnp.einsum('bhtk,bhkv->bhtv', q, C) + jnp.einsum('bhts,bhsv->bhtv', s, v)
    den = g * jnp.einsum('bhtk,bhk->bht', q, n) + jnp.sum(s, axis=-1)
    h = num / jnp.maximum(jnp.abs(den), jnp.exp(-m_t))[..., None]
    b_last = b[..., -1]
    log_e = b_last[..., None] - b + ig
    m_new = jnp.maximum(b_last + m, jnp.max(log_e, axis=-1))
    ke = k * jnp.exp(log_e - m_new[..., None])[..., None]
    g_c = jnp.exp(b_last + m - m_new)
    C_new = g_c[..., None, None] * C + jnp.einsum('bhsk,bhsv->bhkv', ke, v)
    n_new = g_c[..., None] * n + jnp.sum(ke, axis=2)
    return (C_new, n_new, m_new), h


def conv_centred(x, w):
    taps = w.shape[0]
    pad = taps // 2
    n = x.shape[1]
    xp = jnp.pad(x, ((0, 0), (pad, pad), (0, 0)))
    y = xp[:, 0:n] * w[0]
    for j in range(1, taps):
        y = y + xp[:, j:j + n] * w[j]
    return y


def mlstm_mixer(lat, ctx, conv_w, need_ctx):
    f32 = jnp.float32

    def prep(q, k, v, ig_f, ig_b, fg_f, fg_b):
        qk = jax.nn.silu(conv_centred(jnp.concatenate([q, k], axis=-1), conv_w))
        q, k = jnp.split(qk, 2, axis=-1)
        q = to_heads(q, ML_HEADS).astype(f32)
        k = to_heads(k, ML_HEADS).astype(f32) * (ML_DK ** -0.5)
        v = to_heads(v, ML_HEADS).astype(f32)
        heads = lambda t: t.astype(f32).transpose(0, 2, 1)
        return ((q, k, v, heads(ig_f), jax.nn.log_sigmoid(heads(fg_f))),
                (q, k, v, heads(ig_b), jax.nn.log_sigmoid(heads(fg_b))))

    q, k, v, o, ig_f, ig_b, fg_f, fg_b = lat
    lat_f, lat_b = prep(q, k, v, ig_f, ig_b, fg_f, fg_b)
    ctx_f, ctx_b = prep(ctx[0], ctx[1], ctx[2], *ctx[4:])
    bsz = q.shape[0]
    init = (jnp.zeros((bsz, ML_HEADS, ML_DK, ML_DV), f32),
            jnp.zeros((bsz, ML_HEADS, ML_DK), f32),
            jnp.zeros((bsz, ML_HEADS), f32))
    hc_f, h_f = run_direction(mlstm_step, init, ctx_f, lat_f, False)
    hc_b, h_b = run_direction(mlstm_step, init, ctx_b, lat_b, True)
    out = (jax.nn.sigmoid(o.astype(f32)) * from_heads(h_f + h_b)).astype(o.dtype)
    if not need_ctx:
        return out, None
    oc = (jax.nn.sigmoid(ctx[3].astype(f32)) * from_heads(hc_f + hc_b)).astype(ctx[3].dtype)
    return out, oc


def sq_relu_mlp(u, w1, w2):
    return jnp.square(jax.nn.relu(u @ w1)) @ w2


def setup_inputs(seed: int = 0) -> dict:
    key = jax.random.key(seed)
    ks = jax.random.split(key, 18)
    f32 = jnp.float32
    nrm = lambda k, shape, s: jax.random.normal(k, shape, f32) * s
    starts = np.concatenate([[0], np.cumsum(np.array(IN_SIZES))[:-1]])
    fg_off = np.zeros((N_IN,), np.float32)
    for idx in (ML_FGATE_FWD, ML_FGATE_BWD):
        fg_off[starts[idx]:starts[idx] + ML_HEADS] = np.linspace(3.0, 6.0, ML_HEADS)
    return {
        'x': nrm(ks[0], (BATCH, SEQ, D_MODEL), 1.0),
        'c': nrm(ks[1], (BATCH, D_MODEL), 1.0),
        'ctx': nrm(ks[2], (BATCH, CTX_LEN, D_MODEL), 1.0),
        'c_ctx': nrm(ks[3], (D_MODEL,), 1.0),
        'w_mod': nrm(ks[4], (DEPTH, D_MODEL, N_MOD * D_MODEL), 0.5 * D_MODEL ** -0.5),
        'b_mod': nrm(ks[5], (DEPTH, N_MOD * D_MODEL), 0.02),
        'norm_mix': 1.0 + nrm(ks[6], (DEPTH, D_MODEL), 0.02),
        'norm_ffn': 1.0 + nrm(ks[7], (DEPTH, D_MODEL), 0.02),
        'w_in': nrm(ks[8], (DEPTH, D_MODEL, N_IN), D_MODEL ** -0.5),
        'b_in': nrm(ks[9], (DEPTH, N_IN), 0.02) + jnp.asarray(fg_off),
        'q_norm': 1.0 + nrm(ks[10], (DEPTH, HEAD_DIM), 0.02),
        'k_norm': 1.0 + nrm(ks[11], (DEPTH, HEAD_DIM), 0.02),
        'hg_lb': nrm(ks[12], (DEPTH, 2, HG_W), 0.1),
        'hg_norm': 1.0 + nrm(ks[13], (DEPTH, HG_DV), 0.02),
        'ml_conv': nrm(ks[14], (DEPTH, ML_CONV, 2 * ML_W), ML_CONV ** -0.5),
        'w_out': nrm(ks[15], (DEPTH, MIX_WIDTH, D_MODEL), MIX_WIDTH ** -0.5),
        'w_ff1': nrm(ks[16], (DEPTH, D_MODEL, FF_DIM), D_MODEL ** -0.5),
        'w_ff2': nrm(ks[17], (DEPTH, FF_DIM, D_MODEL), FF_DIM ** -0.5),
    }


def reference(x, c, ctx, c_ctx, w_mod, b_mod, norm_mix, norm_ffn, w_in, b_in, q_norm, k_norm,
              hg_lb, hg_norm, ml_conv, w_out, w_ff1, w_ff2):
    f32 = jnp.float32
    n = x.shape[1]
    ROWS = n // GRID_W
    inv_freq = ROPE_THETA ** (-jnp.arange(0, ROPE_AXIS_DIM, 2, dtype=f32) / ROPE_AXIS_DIM)
    row = jnp.repeat(jnp.arange(ROWS), GRID_W).astype(f32)
    col = (jnp.arange(n) % GRID_W).astype(f32)
    ang_row = row[:, None] * inv_freq[None, :]
    ang_col = col[:, None] * inv_freq[None, :]
    lbs = jnp.cumsum(jax.nn.softmax(hg_lb.astype(f32), axis=0), axis=0)
    lbs = lbs - lbs[:1]
    h, hc = x, ctx
    for l in range(DEPTH):
        need_ctx = l < DEPTH - 1
        mod = (jax.nn.silu(c) @ w_mod[l] + b_mod[l])[:, None, :]
        mod_c = jax.nn.silu(c_ctx) @ w_mod[l] + b_mod[l]
        sh1, sc1, g1, sh2, sc2, g2 = jnp.split(mod, N_MOD, axis=-1)
        csh1, csc1, cg1, csh2, csc2, cg2 = jnp.split(mod_c, N_MOD, axis=-1)
        u = modulate(rmsnorm(h, norm_mix[l]), sh1, sc1)
        uc = modulate(rmsnorm(hc, norm_mix[l]), csh1, csc1)
        p = split_columns(u @ w_in[l] + b_in[l])
        pc = split_columns(uc @ w_in[l] + b_in[l])
        a, ac = attention_mixer(p[0:3], pc[0:3], q_norm[l], k_norm[l], ang_row, ang_col, need_ctx)
        r, rc = hgrn2_mixer(p[3:8], pc[3:8], lbs[l, 0], lbs[l, 1], hg_norm[l], need_ctx)
        m, mc = mlstm_mixer(p[8:16], pc[8:16], ml_conv[l], need_ctx)
        h = h + g1 * (jnp.concatenate([a, r, m], axis=-1) @ w_out[l])
        h = h + g2 * sq_relu_mlp(modulate(rmsnorm(h, norm_ffn[l]), sh2, sc2), w_ff1[l], w_ff2[l])
        if need_ctx:
            hc = hc + cg1 * (jnp.concatenate([ac, rc, mc], axis=-1) @ w_out[l])
            hc = hc + cg2 * sq_relu_mlp(modulate(rmsnorm(hc, norm_ffn[l]), csh2, csc2), w_ff1[l], w_ff2[l])
    return h
```

```python
import functools

import jax
import jax.numpy as jnp
import numpy as np
from jax import lax
from jax.experimental import pallas as pl
from jax.experimental.pallas import tpu as pltpu

F32 = jnp.float32
BF16 = jnp.bfloat16

GRID_W = 64
HEAD_DIM = 64
ATTN_HEADS = 8
ATTN_KV_HEADS = 2
ATTN_REP = ATTN_HEADS // ATTN_KV_HEADS
ROPE_THETA = 10000.0
ROPE_AXIS_DIM = HEAD_DIM // 2
HG_HEADS = 4
HG_DK = 64
ML_HEADS = 4
ML_DK = 64
N_MOD = 6
EPS = 1e-6
ATTN_W = ATTN_HEADS * HEAD_DIM
KV_W = ATTN_KV_HEADS * HEAD_DIM
HG_W = HG_HEADS * HG_DK
ML_W = ML_HEADS * ML_DK

OFF_AQ, OFF_AK, OFF_AV = 0, 512, 640
OFF_HG = 768
OFF_MQK = 2048
OFF_MVO = 2560
OFF_MG = 3072
N_IN = 3088
N_IN_PAD = 3200

TM = 256
TQ = 256
TK = 256
SCAN_BLK = 256
HG_CHUNK = 16
V_ROWS = 80
NEG_BIG = -1e30
VMEM_LIMIT = 56 * 1024 * 1024


def _cparams(sem):
    return pltpu.CompilerParams(dimension_semantics=sem, vmem_limit_bytes=VMEM_LIMIT)


def _split3(x):
    hi = x.astype(BF16)
    r1 = x - hi.astype(F32)
    mid = r1.astype(BF16)
    lo = (r1 - mid.astype(F32)).astype(BF16)
    return hi, mid, lo


def _dot(a, b):
    return jnp.dot(a, b, preferred_element_type=F32)


def _dot_nt(a, b):
    return lax.dot_general(a, b, (((1,), (1,)), ((), ())), preferred_element_type=F32)


def _dot_tn(a, b):
    return lax.dot_general(a, b, (((0,), (0,)), ((), ())), preferred_element_type=F32)


def _mask_dot(mask_bf16, x):
    hi, mid, lo = _split3(x)
    return _dot(mask_bf16, hi) + _dot(mask_bf16, mid) + _dot(mask_bf16, lo)


def _dot_mask(x, mask_bf16):
    hi, mid, lo = _split3(x)
    return _dot(hi, mask_bf16) + _dot(mid, mask_bf16) + _dot(lo, mask_bf16)


def _sigmoid(x):
    return 1.0 / (1.0 + jnp.exp(-x))


def _log_sigmoid(x):
    return jnp.minimum(x, 0.0) - jnp.log(1.0 + jnp.exp(-jnp.abs(x)))


def _mod_kernel(c_ref, w_ref, b_ref, o_ref):
    cv = c_ref[...]
    a = cv * _sigmoid(cv)
    o_ref[0] = jnp.dot(a, w_ref[0], preferred_element_type=F32, precision=lax.Precision.HIGHEST) + b_ref[0]


def _mod_call(cvec, w_mod, b_mod):
    depth, d, nm = w_mod.shape
    tn = 1024
    return pl.pallas_call(
        _mod_kernel,
        grid=(depth, nm // tn),
        in_specs=[pl.BlockSpec((8, d), lambda l, j: (0, 0)),
                  pl.BlockSpec((1, d, tn), lambda l, j: (l, 0, j)),
                  pl.BlockSpec((1, 1, tn), lambda l, j: (l, 0, j))],
        out_specs=pl.BlockSpec((1, 8, tn), lambda l, j: (l, 0, j)),
        out_shape=jax.ShapeDtypeStruct((depth, 8, nm), F32),
        compiler_params=_cparams(("arbitrary", "arbitrary")),
        name="mod_vectors",
    )(cvec, w_mod, b_mod.reshape(depth, 1, nm))


def _head_rms(t, gain, e):
    ss = _dot((t * t).astype(BF16), e)
    return t * lax.rsqrt(ss + EPS) * gain


def _rope(t, cos, sin):
    w = t.shape[1]
    reps = w // 128
    cosw = jnp.concatenate([cos] * reps, axis=1) if reps > 1 else cos
    sinw = jnp.concatenate([sin] * reps, axis=1) if reps > 1 else sin
    lane = lax.broadcasted_iota(jnp.int32, t.shape, 1)
    first_half = (lane % 32) < 16
    partner = jnp.where(first_half, pltpu.roll(t, w - 16, 1), pltpu.roll(t, 16, 1))
    return t * cosw + partner * sinw


def _in_kernel(h_ref, mod_ref, g_ref, w_ref, b_ref, cos_ref, sin_ref, gq_ref, gk_ref, e_ref,
               aq_ref, ak_ref, av_ref, hg_ref, mqk_ref, mvo_ref, mg_ref):
    x = h_ref[0]
    ms = jnp.mean(x * x, axis=-1, keepdims=True)
    y = x * lax.rsqrt(ms + EPS) * g_ref[...]
    mod = mod_ref[0, 0]
    u = y * (1.0 + mod[1:2]) + mod[0:1]
    p = _dot(u.astype(BF16), w_ref[...]) + b_ref[...]
    cos = cos_ref[...]
    sin = sin_ref[...]
    e = e_ref[...]
    q = _rope(_head_rms(p[:, OFF_AQ:OFF_AQ + ATTN_W], gq_ref[...], e), cos, sin)
    k = _rope(_head_rms(p[:, OFF_AK:OFF_AK + KV_W], gk_ref[...], e[:KV_W, :KV_W]), cos, sin)
    aq_ref[0] = (q * (HEAD_DIM ** -0.5)).astype(BF16)
    ak_ref[0] = k.astype(BF16)
    av_ref[0] = p[:, OFF_AV:OFF_AV + KV_W].astype(BF16)
    hg_ref[0] = p[:, OFF_HG:OFF_MQK]
    mqk_ref[0] = p[:, OFF_MQK:OFF_MVO]
    mvo_ref[0] = p[:, OFF_MVO:OFF_MG]
    mg_ref[0] = p[:, OFF_MG:OFF_MG + 128]


def _in_call(h, modsel, g, w, b, cos, sin, gq, gk, e, nct):
    bsz, t, d = h.shape
    nt = t // TM
    row = lambda width: pl.BlockSpec((1, TM, width), lambda bi, i: (bi, i, 0))
    full = lambda a: pl.BlockSpec(a.shape, lambda bi, i: (0,) * a.ndim)
    out_widths = (ATTN_W, KV_W, KV_W, OFF_MQK - OFF_HG, 2 * ML_W, 2 * ML_W, 128)
    out_dtypes = (BF16, BF16, BF16, F32, F32, F32, F32)
    return pl.pallas_call(
        _in_kernel,
        grid=(bsz, nt),
        in_specs=[row(d),
                  pl.BlockSpec((1, 1, N_MOD, d), lambda bi, i: (bi, (i >= nct).astype(jnp.int32), 0, 0)),
                  full(g), full(w), full(b),
                  pl.BlockSpec((TM, 128), lambda bi, i: (i, 0)),
                  pl.BlockSpec((TM, 128), lambda bi, i: (i, 0)),
                  full(gq), full(gk), full(e)],
        out_specs=[row(wd) for wd in out_widths],
        out_shape=[jax.ShapeDtypeStruct((bsz, t, wd), dt) for wd, dt in zip(out_widths, out_dtypes)],
        compiler_params=_cparams(("parallel", "parallel")),
        name="norm_mod_in_proj",
    )(h, modsel, g, w, b, cos, sin, gq, gk, e)


def _attn_kernel(qt_ref, k_ref, vt_ref, o_ref, m_ref, acc_ref, *, n_ctx_q, n_ctx_k, n_k):
    qi = pl.program_id(2)
    m_ref[...] = jnp.full(m_ref.shape, NEG_BIG, F32)
    acc_ref[...] = jnp.zeros(acc_ref.shape, F32)
    n_blocks = jnp.where(qi < n_ctx_q, n_ctx_k, n_k)

    def body(kb, carry):
        off = pl.multiple_of(kb * TK, TK)
        kt = k_ref[0, 0, pl.ds(off, TK), :]
        vt = vt_ref[0, 0, :, pl.ds(off, TK)]
        for hh in range(ATTN_REP):
            qh = qt_ref[0, hh * HEAD_DIM:(hh + 1) * HEAD_DIM, :]
            s = _dot(kt, qh)
            m_old = m_ref[hh]
            m_new = jnp.maximum(m_old, jnp.max(s, axis=0, keepdims=True))
            p = jnp.exp(s - m_new).astype(BF16)
            alpha = jnp.exp(m_old - m_new)
            acc_ref[hh] = alpha * acc_ref[hh] + _dot(vt, p)
            m_ref[hh] = m_new
        return carry

    lax.fori_loop(0, n_blocks, body, 0)
    for hh in range(ATTN_REP):
        acc = acc_ref[hh]
        o = acc[:HEAD_DIM] / acc[HEAD_DIM:HEAD_DIM + 1]
        o_ref[0, hh * HEAD_DIM:(hh + 1) * HEAD_DIM, :] = o.astype(o_ref.dtype)


def _attn_call(qt, k4, vt, n_ctx):
    bsz, _, t = qt.shape
    kern = functools.partial(_attn_kernel, n_ctx_q=n_ctx // TQ, n_ctx_k=n_ctx // TK, n_k=t // TK)
    gw = ATTN_REP * HEAD_DIM
    return pl.pallas_call(
        kern,
        grid=(bsz, ATTN_KV_HEADS, t // TQ),
        in_specs=[pl.BlockSpec((1, gw, TQ), lambda b, g, i: (b, g, i)),
                  pl.BlockSpec((1, 1, t, HEAD_DIM), lambda b, g, i: (b, g, 0, 0)),
                  pl.BlockSpec((1, 1, V_ROWS, t), lambda b, g, i: (b, g, 0, 0))],
        out_specs=pl.BlockSpec((1, gw, TQ), lambda b, g, i: (b, g, i)),
        out_shape=jax.ShapeDtypeStruct((bsz, ATTN_W, t), BF16),
        scratch_shapes=[pltpu.VMEM((ATTN_REP, 1, TQ), F32), pltpu.VMEM((ATTN_REP, V_ROWS, TQ), F32)],
        compiler_params=_cparams(("parallel", "parallel", "arbitrary")),
        name="attention",
    )(qt, k4, vt)


def _conv_kernel(x_ref, prev_ref, next_ref, w_ref, o_ref, *, nct, nb):
    i = pl.program_id(1)
    x = x_ref[0]
    rows = x.shape[0]
    left_edge = jnp.logical_or(i == 0, i == nct)
    right_edge = jnp.logical_or(i == nct - 1, i == nb - 1)
    prev_row = jnp.where(left_edge, 0.0, prev_ref[0, 7:8, :])
    next_row = jnp.where(right_edge, 0.0, next_ref[0, 0:1, :])
    ridx = lax.broadcasted_iota(jnp.int32, x.shape, 0)
    xm1 = jnp.where(ridx == 0, prev_row, pltpu.roll(x, 1, 0))
    xp1 = jnp.where(ridx == rows - 1, next_row, pltpu.roll(x, rows - 1, 0))
    w = w_ref[...]
    y = xm1 * w[0:1] + x * w[1:2] + xp1 * w[2:3]
    y = y * _sigmoid(y)
    lane = lax.broadcasted_iota(jnp.int32, x.shape, 1)
    y = jnp.where(lane >= ML_W, y * (ML_DK ** -0.5), y)
    o_ref[0] = y.astype(o_ref.dtype)


def _conv_call(mqk, w, nct):
    bsz, t, width = mqk.shape
    nb = t // SCAN_BLK
    per8 = SCAN_BLK // 8
    last8 = t // 8 - 1
    kern = functools.partial(_conv_kernel, nct=nct, nb=nb)
    return pl.pallas_call(
        kern,
        grid=(bsz, nb),
        in_specs=[pl.BlockSpec((1, SCAN_BLK, width), lambda b, i: (b, i, 0)),
                  pl.BlockSpec((1, 8, width), lambda b, i: (b, jnp.maximum(i * per8 - 1, 0), 0)),
                  pl.BlockSpec((1, 8, width), lambda b, i: (b, jnp.minimum((i + 1) * per8, last8), 0)),
                  pl.BlockSpec(w.shape, lambda b, i: (0, 0))],
        out_specs=pl.BlockSpec((1, SCAN_BLK, width), lambda b, i: (b, i, 0)),
        out_shape=jax.ShapeDtypeStruct((bsz, t, width), BF16),
        compiler_params=_cparams(("parallel", "parallel")),
        name="mlstm_qk_conv",
    )(mqk, mqk, mqk, w)


def _scan_block(d, j, nct, nb):
    bwd = jnp.where(j < nct, nct - 1 - j, nb - 1 - (j - nct))
    return jnp.where(d == 0, j, bwd)


def _hgrn_kernel(q_ref, v_ref, z_ref, lb_ref, tri_ref, pm_ref, e_ref, bd_ref, o_ref, st_ref):
    d = pl.program_id(1)
    j = pl.program_id(2)

    @pl.when(j == 0)
    def _():
        st_ref[...] = jnp.zeros(st_ref.shape, F32)

    lb = lb_ref[0]
    tri = tri_ref[0]
    e = e_ref[...]
    bd = bd_ref[...]
    c = HG_CHUNK
    n_chunks = SCAN_BLK // c

    def body(i, carry):
        ci = jnp.where(d == 0, i, n_chunks - 1 - i)
        off = pl.multiple_of(ci * c, c)
        q = q_ref[0, pl.ds(off, c), :]
        v = v_ref[0, pl.ds(off, c), :]
        z = z_ref[0, pl.ds(off, c), :]
        f = lb + (1.0 - lb) * _sigmoid(z)
        kk = 1.0 - f
        lf = jnp.log(f)
        b = _mask_dot(tri, lf)
        b_last = jnp.sum(lf, axis=0, keepdims=True)
        st = st_ref[...]
        o = _dot_nt((q * jnp.exp(b)).astype(BF16), st.astype(BF16))
        q_t = jnp.concatenate([q] * c, axis=0)
        b_t = jnp.concatenate([b] * c, axis=0)
        k_s = jnp.concatenate([jnp.broadcast_to(kk[s:s + 1], (c, kk.shape[1])) for s in range(c)], axis=0)
        b_s = jnp.concatenate([jnp.broadcast_to(b[s:s + 1], (c, b.shape[1])) for s in range(c)], axis=0)
        v_s = jnp.concatenate([jnp.broadcast_to(v[s:s + 1], (c, v.shape[1])) for s in range(c)], axis=0)
        pm = pm_ref[0]
        p = q_t * k_s * jnp.exp(jnp.minimum(b_t - b_s, 0.0)) * pm
        a = _dot(p.astype(BF16), e)
        av = a * v_s
        for s in range(c):
            o = o + av[s * c:(s + 1) * c]
        o_ref[0, 0, pl.ds(off, c), :] = o
        k_dec = kk * jnp.exp(b_last - b)
        upd = _dot_tn(v.astype(BF16), k_dec.astype(BF16))
        st_ref[...] = st * jnp.exp(b_last) + upd * bd
        return carry

    lax.fori_loop(0, n_chunks, body, 0)


def _hgrn_call(hgp, lbs, tri, pm, e, bd, nct):
    bsz, t, _ = hgp.shape
    nb = t // SCAN_BLK
    w = HG_W
    blk = lambda d, j: _scan_block(d, j, nct, nb)
    return pl.pallas_call(
        _hgrn_kernel,
        grid=(bsz, 2, nb),
        in_specs=[pl.BlockSpec((1, SCAN_BLK, w), lambda b, d, j: (b, blk(d, j), 0)),
                  pl.BlockSpec((1, SCAN_BLK, w), lambda b, d, j: (b, blk(d, j), 1)),
                  pl.BlockSpec((1, SCAN_BLK, w), lambda b, d, j: (b, blk(d, j), 2 + d)),
                  pl.BlockSpec((1, 1, w), lambda b, d, j: (d, 0, 0)),
                  pl.BlockSpec((1, HG_CHUNK, HG_CHUNK), lambda b, d, j: (d, 0, 0)),
                  pl.BlockSpec((1, HG_CHUNK * HG_CHUNK, w), lambda b, d, j: (d, 0, 0)),
                  pl.BlockSpec((w, w), lambda b, d, j: (0, 0)),
                  pl.BlockSpec((w, w), lambda b, d, j: (0, 0))],
        out_specs=pl.BlockSpec((1, 1, SCAN_BLK, w), lambda b, d, j: (d, b, blk(d, j), 0)),
        out_shape=jax.ShapeDtypeStruct((2, bsz, t, w), F32),
        scratch_shapes=[pltpu.VMEM((w, w), F32)],
        compiler_params=_cparams(("parallel", "parallel", "arbitrary")),
        name="hgrn2_scan",
    )(hgp, hgp, hgp, lbs, tri, pm, e, bd)


def _mlstm_kernel(q_ref, k_ref, v_ref, g_ref, gt_ref, m_ref, mt_ref, o_ref, c_ref, ms_ref):
    j = pl.program_id(2)

    @pl.when(j == 0)
    def _():
        c_ref[...] = jnp.zeros(c_ref.shape, F32)
        ms_ref[...] = jnp.zeros(ms_ref.shape, F32)

    length = SCAN_BLK
    mask = m_ref[0]
    mask_t = mt_ref[0]
    valid = mask > 0
    gates = g_ref[0, 0]
    gates_t = gt_ref[0, 0]
    b_col_all = _mask_dot(mask, _log_sigmoid(gates))
    lf_t = _log_sigmoid(gates_t)
    b_row_all = _dot_mask(lf_t, mask_t)
    tot_all = jnp.sum(lf_t, axis=1, keepdims=True)
    lane = lax.broadcasted_iota(jnp.int32, (length, 128), 1)
    for pair in range(ML_HEADS // 2):
        v2 = v_ref[0, :, pair * 128:(pair + 1) * 128]
        out_pair = jnp.zeros((length, 128), F32)
        for sub in range(2):
            hh = pair * 2 + sub
            own = (lane < 64) if sub == 0 else (lane >= 64)
            ones_lane = 64 if sub == 0 else 0
            qh = q_ref[0, :, hh * ML_DK:(hh + 1) * ML_DK]
            kh = k_ref[0, :, hh * ML_DK:(hh + 1) * ML_DK]
            v_aug = jnp.where(own, v2, jnp.where(lane == ones_lane, 1.0, 0.0)).astype(BF16)
            m_prev = ms_ref[hh, 0:1, 0:1]
            bc = b_col_all[:, 4 + hh:5 + hh]
            br = b_row_all[4 + hh:5 + hh, :]
            ig_row = gates_t[hh:hh + 1, :]
            ig_col = gates[:, hh:hh + 1]
            tot = tot_all[4 + hh:5 + hh, :]
            log_d = jnp.where(valid, bc - br + ig_row, NEG_BIG)
            m_t = jnp.maximum(bc + m_prev, jnp.max(log_d, axis=1, keepdims=True))
            dmat = jnp.exp(log_d - m_t)
            g = jnp.exp(bc + m_prev - m_t)
            s = _dot_nt(qh, kh) * dmat
            c_aug = c_ref[hh]
            r = g * _dot(qh, c_aug.astype(BF16)) + _dot(s.astype(BF16), v_aug)
            den = jnp.sum(jnp.where(lane == ones_lane, r, 0.0), axis=1, keepdims=True)
            hval = r / jnp.maximum(jnp.abs(den), jnp.exp(-m_t))
            out_pair = jnp.where(own, hval, out_pair)
            log_e = tot - bc + ig_col
            m_new = jnp.maximum(tot + m_prev, jnp.max(log_e, axis=0, keepdims=True))
            ke = (kh.astype(F32) * jnp.exp(log_e - m_new)).astype(BF16)
            g_c = jnp.exp(tot + m_prev - m_new)
            c_ref[hh] = g_c * c_aug + _dot_tn(ke, v_aug)
            ms_ref[hh] = jnp.broadcast_to(m_new, ms_ref.shape[1:])
        o_ref[0, 0, :, pair * 128:(pair + 1) * 128] = out_pair


def _mlstm_call(qk, mvo, gcol, grow, mask, mask_t, nct):
    bsz, t, _ = qk.shape
    nb = t // SCAN_BLK
    w = ML_W
    blk = lambda d, j: _scan_block(d, j, nct, nb)
    return pl.pallas_call(
        _mlstm_kernel,
        grid=(bsz, 2, nb),
        in_specs=[pl.BlockSpec((1, SCAN_BLK, w), lambda b, d, j: (b, blk(d, j), 0)),
                  pl.BlockSpec((1, SCAN_BLK, w), lambda b, d, j: (b, blk(d, j), 1)),
                  pl.BlockSpec((1, SCAN_BLK, w), lambda b, d, j: (b, blk(d, j), 0)),
                  pl.BlockSpec((1, 1, SCAN_BLK, 128), lambda b, d, j: (b, d, blk(d, j), 0)),
                  pl.BlockSpec((1, 1, 8, SCAN_BLK), lambda b, d, j: (b, d, 0, blk(d, j))),
                  pl.BlockSpec((1, SCAN_BLK, SCAN_BLK), lambda b, d, j: (d, 0, 0)),
                  pl.BlockSpec((1, SCAN_BLK, SCAN_BLK), lambda b, d, j: (d, 0, 0))],
        out_specs=pl.BlockSpec((1, 1, SCAN_BLK, w), lambda b, d, j: (d, b, blk(d, j), 0)),
        out_shape=jax.ShapeDtypeStruct((2, bsz, t, w), F32),
        scratch_shapes=[pltpu.VMEM((ML_HEADS, ML_DK, 128), F32), pltpu.VMEM((ML_HEADS, 8, 128), F32)],
        compiler_params=_cparams(("parallel", "parallel", "arbitrary")),
        name="mlstm_scan",
    )(qk, qk, mvo, gcol, grow, mask, mask_t)


def _out_kernel(h_ref, mod_ref, at_ref, hf_ref, hb_ref, hgate_ref, gn_ref, e_ref, mf_ref, mb_ref, mgate_ref,
                wa_ref, wr_ref, wm_ref, o_ref):
    mod = mod_ref[0, 0]
    o = hf_ref[0, 0] + hb_ref[0, 0]
    ss = _dot((o * o).astype(BF16), e_ref[...])
    gate = hgate_ref[0]
    r = o * lax.rsqrt(ss + EPS) * gn_ref[...] * (gate * _sigmoid(gate))
    m = _sigmoid(mgate_ref[0]) * (mf_ref[0, 0] + mb_ref[0, 0])
    y = _dot_tn(at_ref[0], wa_ref[...]) + _dot(r.astype(BF16), wr_ref[...]) + _dot(m.astype(BF16), wm_ref[...])
    o_ref[0] = h_ref[0] + mod[2:3] * y


def _out_call(h, modsel, at, hg_o, hgp, gn, e, ml_o, mvo, wa, wr, wm, nct, skip):
    bsz, t, d = h.shape
    nt = t // TM - skip
    full = lambda a: pl.BlockSpec(a.shape, lambda bi, i: (0,) * a.ndim)
    hg_col = (OFF_MQK - OFF_HG) // HG_W - 1
    return pl.pallas_call(
        _out_kernel,
        grid=(bsz, nt),
        in_specs=[pl.BlockSpec((1, TM, d), lambda bi, i: (bi, i + skip, 0)),
                  pl.BlockSpec((1, 1, N_MOD, d), lambda bi, i: (bi, (i + skip >= nct).astype(jnp.int32), 0, 0)),
                  pl.BlockSpec((1, ATTN_W, TM), lambda bi, i: (bi, 0, i + skip)),
                  pl.BlockSpec((1, 1, TM, HG_W), lambda bi, i: (0, bi, i + skip, 0)),
                  pl.BlockSpec((1, 1, TM, HG_W), lambda bi, i: (1, bi, i + skip, 0)),
                  pl.BlockSpec((1, TM, HG_W), lambda bi, i: (bi, i + skip, hg_col)),
                  full(gn), full(e),
                  pl.BlockSpec((1, 1, TM, ML_W), lambda bi, i: (0, bi, i + skip, 0)),
                  pl.BlockSpec((1, 1, TM, ML_W), lambda bi, i: (1, bi, i + skip, 0)),
                  pl.BlockSpec((1, TM, ML_W), lambda bi, i: (bi, i + skip, 1)),
                  full(wa), full(wr), full(wm)],
        out_specs=pl.BlockSpec((1, TM, d), lambda bi, i: (bi, i, 0)),
        out_shape=jax.ShapeDtypeStruct((bsz, nt * TM, d), F32),
        compiler_params=_cparams(("parallel", "parallel")),
        name="readout_out_proj",
    )(h, modsel, at, hg_o, hg_o, hgp, gn, e, ml_o, ml_o, mvo, wa, wr, wm)


def _ffn_kernel(h_ref, mod_ref, g_ref, w1_ref, w2_ref, o_ref):
    x = h_ref[0]
    mod = mod_ref[0, 0]
    ms = jnp.mean(x * x, axis=-1, keepdims=True)
    y = x * lax.rsqrt(ms + EPS) * g_ref[...]
    u = (y * (1.0 + mod[4:5]) + mod[3:4]).astype(BF16)
    a = jnp.maximum(_dot(u, w1_ref[...]), 0.0)
    a = (a * a).astype(BF16)
    o_ref[0] = x + mod[5:6] * _dot(a, w2_ref[...])


def _ffn_call(h, modsel, g, w1, w2, nct):
    bsz, t, d = h.shape
    nt = t // TM
    full = lambda a: pl.BlockSpec(a.shape, lambda bi, i: (0,) * a.ndim, pipeline_mode=pl.Buffered(1))
    return pl.pallas_call(
        _ffn_kernel,
        grid=(bsz, nt),
        in_specs=[pl.BlockSpec((1, TM, d), lambda bi, i: (bi, i, 0)),
                  pl.BlockSpec((1, 1, N_MOD, d), lambda bi, i: (bi, (i >= nct).astype(jnp.int32), 0, 0)),
                  pl.BlockSpec(g.shape, lambda bi, i: (0, 0)),
                  full(w1), full(w2)],
        out_specs=pl.BlockSpec((1, TM, d), lambda bi, i: (bi, i, 0)),
        out_shape=jax.ShapeDtypeStruct((bsz, t, d), F32),
        compiler_params=_cparams(("parallel", "parallel")),
        name="ffn",
    )(h, modsel, g, w1, w2)


def _rope_tables(n_ctx, n_lat):
    inv_freq = ROPE_THETA ** (-np.arange(0, ROPE_AXIS_DIM, 2, dtype=np.float32) / ROPE_AXIS_DIM)
    inv_freq = jnp.asarray(inv_freq, F32)
    tok = jnp.arange(n_lat)
    ang_row = (tok // GRID_W).astype(F32)[:, None] * inv_freq[None, :]
    ang_col = (tok % GRID_W).astype(F32)[:, None] * inv_freq[None, :]
    ang = jnp.concatenate([ang_row, ang_row, ang_col, ang_col], axis=1)
    ang = jnp.concatenate([jnp.zeros((n_ctx, HEAD_DIM), F32), ang], axis=0)
    sign = np.where((np.arange(HEAD_DIM) % 32) < 16, -1.0, 1.0).astype(np.float32)
    cos = jnp.cos(ang)
    sin = jnp.sin(ang) * jnp.asarray(sign)[None, :]
    return jnp.concatenate([cos, cos], axis=1), jnp.concatenate([sin, sin], axis=1)


def _block_ones(width, head, scale, dtype):
    idx = np.arange(width) // head
    return jnp.asarray((idx[:, None] == idx[None, :]).astype(np.float32) * scale, dtype)


def _scan_masks(n):
    tril = np.tril(np.ones((n, n), np.float32))
    return np.stack([tril, tril.T])


def kernel(x, c, ctx, c_ctx, w_mod, b_mod, norm_mix, norm_ffn, w_in, b_in, q_norm, k_norm,
           hg_lb, hg_norm, ml_conv, w_out, w_ff1, w_ff2):
    bsz, n_lat, d = x.shape
    n_ctx = ctx.shape[1]
    depth = w_mod.shape[0]
    t = n_ctx + n_lat
    nct = n_ctx // TM
    assert n_ctx % SCAN_BLK == 0 and n_lat % SCAN_BLK == 0 and n_lat % GRID_W == 0 and bsz + 1 <= 8

    cvec = jnp.concatenate([c, c_ctx[None, :], jnp.zeros((8 - bsz - 1, d), F32)], axis=0)
    mods = _mod_call(cvec, w_mod, b_mod).reshape(depth, 8, N_MOD, d)
    modsel = jnp.stack([jnp.broadcast_to(mods[:, bsz][:, None], (depth, bsz, N_MOD, d)), mods[:, :bsz]], axis=2)

    cos, sin = _rope_tables(n_ctx, n_lat)
    e_attn = _block_ones(ATTN_W, HEAD_DIM, 1.0 / HEAD_DIM, BF16)
    e_hg_mean = _block_ones(HG_W, HG_DK, 1.0 / HG_DK, BF16)
    e_hg = _block_ones(HG_W, HG_DK, 1.0, BF16)
    bd_hg = _block_ones(HG_W, HG_DK, 1.0, F32)
    tri16 = _scan_masks(HG_CHUNK)
    tri_hg = jnp.asarray(tri16, BF16)
    pair_mask = np.transpose(tri16, (0, 2, 1)).reshape(2, HG_CHUNK * HG_CHUNK, 1)
    pair_mask = jnp.asarray(np.broadcast_to(pair_mask, (2, HG_CHUNK * HG_CHUNK, HG_W)).copy(), F32)
    scan_mask = _scan_masks(SCAN_BLK)
    mask_ml = jnp.asarray(scan_mask, BF16)
    mask_ml_t = jnp.asarray(np.transpose(scan_mask, (0, 2, 1)).copy(), BF16)

    lbs = jnp.cumsum(jax.nn.softmax(hg_lb.astype(F32), axis=0), axis=0)
    lbs = (lbs - lbs[:1]).reshape(depth, 2, 1, HG_W)

    w_in_p = jnp.pad(w_in, ((0, 0), (0, 0), (0, N_IN_PAD - N_IN))).astype(BF16)
    b_in_p = jnp.pad(b_in, ((0, 0), (0, N_IN_PAD - N_IN))).reshape(depth, 1, N_IN_PAD)
    w_out_b = w_out.astype(BF16)
    w1_b = w_ff1.astype(BF16)
    w2_b = w_ff2.astype(BF16)
    gq = jnp.tile(q_norm, (1, ATTN_HEADS)).reshape(depth, 1, ATTN_W)
    gk = jnp.tile(k_norm, (1, ATTN_KV_HEADS)).reshape(depth, 1, KV_W)
    gn = jnp.tile(hg_norm, (1, HG_HEADS)).reshape(depth, 1, HG_W)

    h = jnp.concatenate([ctx, x], axis=1)
    for l in range(depth):
        last = l == depth - 1
        aq, ak, av, hgp, mqk, mvo, mg = _in_call(
            h, modsel[l], norm_mix[l].reshape(1, d), w_in_p[l], b_in_p[l], cos, sin, gq[l], gk[l], e_attn, nct)

        qt = aq.transpose(0, 2, 1)
        k4 = ak.reshape(bsz, t, ATTN_KV_HEADS, HEAD_DIM).transpose(0, 2, 1, 3)
        vt = av.reshape(bsz, t, ATTN_KV_HEADS, HEAD_DIM).transpose(0, 2, 3, 1)
        vt = jnp.concatenate([vt, jnp.ones((bsz, ATTN_KV_HEADS, 1, t), BF16),
                              jnp.zeros((bsz, ATTN_KV_HEADS, V_ROWS - HEAD_DIM - 1, t), BF16)], axis=2)
        at = _attn_call(qt, k4, vt, n_ctx)

        hg_o = _hgrn_call(hgp, lbs[l], tri_hg, pair_mask, e_hg, bd_hg, nct)

        qk = _conv_call(mqk, ml_conv[l], nct)
        g16 = mg[:, :, :16].reshape(bsz, t, 2, 2, ML_HEADS)
        gdir = g16.transpose(0, 3, 1, 2, 4).reshape(bsz, 2, t, 8)
        gcol = jnp.pad(gdir, ((0, 0), (0, 0), (0, 0), (0, 120)))
        grow = gdir.transpose(0, 1, 3, 2)
        ml_o = _mlstm_call(qk, mvo, gcol, grow, mask_ml, mask_ml_t, nct)

        skip = nct if last else 0
        h1 = _out_call(h, modsel[l], at, hg_o, hgp, gn[l], e_hg_mean, ml_o, mvo,
                       w_out_b[l, :ATTN_W], w_out_b[l, ATTN_W:ATTN_W + HG_W], w_out_b[l, ATTN_W + HG_W:], nct, skip)
        h = _ffn_call(h1, modsel[l], norm_ffn[l].reshape(1, d), w1_b[l], w2_b[l], 0 if last else nct)
    return h
```

```python
import functools

import jax
import jax.numpy as jnp
import numpy as np
from jax import lax
from jax.experimental import pallas as pl
from jax.experimental.pallas import tpu as pltpu

F32 = jnp.float32
BF16 = jnp.bfloat16

GRID_W = 64
HEAD_DIM = 64
ATTN_HEADS = 8
ATTN_KV_HEADS = 2
ATTN_REP = ATTN_HEADS // ATTN_KV_HEADS
ROPE_THETA = 10000.0
ROPE_AXIS_DIM = HEAD_DIM // 2
HG_HEADS = 4
HG_DK = 64
ML_HEADS = 4
ML_DK = 64
N_MOD = 6
EPS = 1e-6
ATTN_W = ATTN_HEADS * HEAD_DIM
KV_W = ATTN_KV_HEADS * HEAD_DIM
HG_W = HG_HEADS * HG_DK
ML_W = ML_HEADS * ML_DK

OFF_AQ, OFF_AK, OFF_AV = 0, 512, 640
OFF_HG = 768
OFF_MQK = 2048
OFF_MVO = 2560
OFF_MG = 3072
N_IN = 3088
N_IN_PAD = 3200

TM = 256
TQ = 256
TK = 256
SCAN_BLK = 256
HG_CHUNK = 16
V_ROWS = 80
NEG_BIG = -1e30
Q_SCALE = float(np.log2(np.e)) * HEAD_DIM ** -0.5
VMEM_LIMIT = 56 * 1024 * 1024


def _cparams(sem):
    return pltpu.CompilerParams(dimension_semantics=sem, vmem_limit_bytes=VMEM_LIMIT)


def _split3(x):
    hi = x.astype(BF16)
    r1 = x - hi.astype(F32)
    mid = r1.astype(BF16)
    lo = (r1 - mid.astype(F32)).astype(BF16)
    return hi, mid, lo


def _dot(a, b):
    return jnp.dot(a, b, preferred_element_type=F32)


def _dot_nt(a, b):
    return lax.dot_general(a, b, (((1,), (1,)), ((), ())), preferred_element_type=F32)


def _dot_tn(a, b):
    return lax.dot_general(a, b, (((0,), (0,)), ((), ())), preferred_element_type=F32)


def _mask_dot(mask_bf16, x):
    hi, mid, lo = _split3(x)
    return _dot(mask_bf16, hi) + _dot(mask_bf16, mid) + _dot(mask_bf16, lo)


def _dot_mask(x, mask_bf16):
    hi, mid, lo = _split3(x)
    return _dot(hi, mask_bf16) + _dot(mid, mask_bf16) + _dot(lo, mask_bf16)


def _sigmoid(x):
    return 1.0 / (1.0 + jnp.exp(-x))


def _log_sigmoid(x):
    return jnp.minimum(x, 0.0) - jnp.log(1.0 + jnp.exp(-jnp.abs(x)))


def _mod_kernel(c_ref, w_ref, b_ref, o_ref):
    cv = c_ref[...]
    a = cv * _sigmoid(cv)
    o_ref[0] = jnp.dot(a, w_ref[0], preferred_element_type=F32, precision=lax.Precision.HIGHEST) + b_ref[0]


def _mod_call(cvec, w_mod, b_mod):
    depth, d, nm = w_mod.shape
    tn = 1024
    return pl.pallas_call(
        _mod_kernel,
        grid=(depth, nm // tn),
        in_specs=[pl.BlockSpec((8, d), lambda l, j: (0, 0)),
                  pl.BlockSpec((1, d, tn), lambda l, j: (l, 0, j)),
                  pl.BlockSpec((1, 1, tn), lambda l, j: (l, 0, j))],
        out_specs=pl.BlockSpec((1, 8, tn), lambda l, j: (l, 0, j)),
        out_shape=jax.ShapeDtypeStruct((depth, 8, nm), F32),
        compiler_params=_cparams(("arbitrary", "arbitrary")),
        name="mod_vectors",
    )(cvec, w_mod, b_mod.reshape(depth, 1, nm))


def _head_rms(t, gain, e):
    ss = _dot((t * t).astype(BF16), e)
    return t * lax.rsqrt(ss + EPS) * gain


def _rope(t, cos, sin):
    w = t.shape[1]
    reps = w // 128
    cosw = jnp.concatenate([cos] * reps, axis=1) if reps > 1 else cos
    sinw = jnp.concatenate([sin] * reps, axis=1) if reps > 1 else sin
    lane = lax.broadcasted_iota(jnp.int32, t.shape, 1)
    first_half = (lane % 32) < 16
    partner = jnp.where(first_half, pltpu.roll(t, w - 16, 1), pltpu.roll(t, 16, 1))
    return t * cosw + partner * sinw


def _in_kernel(h_ref, mod_ref, g_ref, w_ref, b_ref, cos_ref, sin_ref, gq_ref, gk_ref, e_ref,
               aq_ref, ak_ref, av_ref, hg_ref, mqk_ref, mvo_ref, mg_ref):
    x = h_ref[0]
    ms = jnp.mean(x * x, axis=-1, keepdims=True)
    y = x * lax.rsqrt(ms + EPS) * g_ref[...]
    mod = mod_ref[0, 0]
    u = y * (1.0 + mod[1:2]) + mod[0:1]
    p = _dot(u.astype(BF16), w_ref[...]) + b_ref[...]
    cos = cos_ref[...]
    sin = sin_ref[...]
    e = e_ref[...]
    q = _rope(_head_rms(p[:, OFF_AQ:OFF_AQ + ATTN_W], gq_ref[...], e), cos, sin)
    k = _rope(_head_rms(p[:, OFF_AK:OFF_AK + KV_W], gk_ref[...], e[:KV_W, :KV_W]), cos, sin)
    aq_ref[0] = (q * Q_SCALE).astype(BF16)
    ak_ref[0] = k.astype(BF16)
    av_ref[0] = p[:, OFF_AV:OFF_AV + KV_W].astype(BF16)
    hg_ref[0] = p[:, OFF_HG:OFF_MQK]
    mqk_ref[0] = p[:, OFF_MQK:OFF_MVO]
    mvo_ref[0] = p[:, OFF_MVO:OFF_MG]
    mg_ref[0] = p[:, OFF_MG:OFF_MG + 128]


def _in_call(h, modsel, g, w, b, cos, sin, gq, gk, e, nct):
    bsz, t, d = h.shape
    nt = t // TM
    row = lambda width: pl.BlockSpec((1, TM, width), lambda bi, i: (bi, i, 0))
    full = lambda a: pl.BlockSpec(a.shape, lambda bi, i: (0,) * a.ndim)
    out_widths = (ATTN_W, KV_W, KV_W, OFF_MQK - OFF_HG, 2 * ML_W, 2 * ML_W, 128)
    out_dtypes = (BF16, BF16, BF16, F32, F32, F32, F32)
    return pl.pallas_call(
        _in_kernel,
        grid=(bsz, nt),
        in_specs=[row(d),
                  pl.BlockSpec((1, 1, N_MOD, d), lambda bi, i: (bi, (i >= nct).astype(jnp.int32), 0, 0)),
                  full(g), full(w), full(b),
                  pl.BlockSpec((TM, 128), lambda bi, i: (i, 0)),
                  pl.BlockSpec((TM, 128), lambda bi, i: (i, 0)),
                  full(gq), full(gk), full(e)],
        out_specs=[row(wd) for wd in out_widths],
        out_shape=[jax.ShapeDtypeStruct((bsz, t, wd), dt) for wd, dt in zip(out_widths, out_dtypes)],
        compiler_params=_cparams(("parallel", "parallel")),
        name="norm_mod_in_proj",
    )(h, modsel, g, w, b, cos, sin, gq, gk, e)


def _attn_kernel(qt_ref, k_ref, vt_ref, o_ref, m_ref, acc_ref, s_ref, *, n_ctx_q, n_ctx_k, n_k):
    qi = pl.program_id(2)
    m_ref[...] = jnp.full(m_ref.shape, NEG_BIG, F32)
    acc_ref[...] = jnp.zeros(acc_ref.shape, F32)
    n_blocks = jnp.where(qi < n_ctx_q, n_ctx_k, n_k)

    def q_head(hh):
        return qt_ref[0, hh * HEAD_DIM:(hh + 1) * HEAD_DIM, :]

    def key_tile(kb):
        return k_ref[0, 0, pl.ds(pl.multiple_of(kb * TK, TK), TK), :]

    def step(kb, cur, nxt):
        vt = vt_ref[0, 0, :, pl.ds(pl.multiple_of(kb * TK, TK), TK)]
        kt = key_tile(kb + 1) if nxt is not None else None
        for hh in range(ATTN_REP):
            if nxt is not None:
                s_ref[nxt, hh] = _dot(kt, q_head(hh))
            s = s_ref[cur, hh]
            m_old = m_ref[hh]
            m_new = jnp.maximum(m_old, jnp.max(s, axis=0, keepdims=True))
            p = jnp.exp2(s - m_new).astype(BF16)
            alpha = jnp.exp2(m_old - m_new)
            acc_ref[hh] = alpha * acc_ref[hh] + _dot(vt, p)
            m_ref[hh] = m_new

    kt0 = key_tile(0)
    for hh in range(ATTN_REP):
        s_ref[0, hh] = _dot(kt0, q_head(hh))

    def pair(i, carry):
        step(2 * i, 0, 1)
        step(2 * i + 1, 1, 0)
        return carry

    n_pairs = (n_blocks - 1) // 2
    lax.fori_loop(0, n_pairs, pair, 0)
    rest = n_blocks - 2 * n_pairs

    @pl.when(rest == 1)
    def _():
        step(n_blocks - 1, 0, None)

    @pl.when(rest == 2)
    def _():
        step(n_blocks - 2, 0, 1)
        step(n_blocks - 1, 1, None)

    for hh in range(ATTN_REP):
        acc = acc_ref[hh]
        o = acc[:HEAD_DIM] / acc[HEAD_DIM:HEAD_DIM + 1]
        o_ref[0, hh * HEAD_DIM:(hh + 1) * HEAD_DIM, :] = o.astype(o_ref.dtype)


def _attn_call(qt, k4, vt, n_ctx):
    bsz, _, t = qt.shape
    kern = functools.partial(_attn_kernel, n_ctx_q=n_ctx // TQ, n_ctx_k=n_ctx // TK, n_k=t // TK)
    gw = ATTN_REP * HEAD_DIM
    return pl.pallas_call(
        kern,
        grid=(bsz, ATTN_KV_HEADS, t // TQ),
        in_specs=[pl.BlockSpec((1, gw, TQ), lambda b, g, i: (b, g, i)),
                  pl.BlockSpec((1, 1, t, HEAD_DIM), lambda b, g, i: (b, g, 0, 0)),
                  pl.BlockSpec((1, 1, V_ROWS, t), lambda b, g, i: (b, g, 0, 0))],
        out_specs=pl.BlockSpec((1, gw, TQ), lambda b, g, i: (b, g, i)),
        out_shape=jax.ShapeDtypeStruct((bsz, ATTN_W, t), BF16),
        scratch_shapes=[pltpu.VMEM((ATTN_REP, 1, TQ), F32), pltpu.VMEM((ATTN_REP, V_ROWS, TQ), F32),
                        pltpu.VMEM((2, ATTN_REP, TK, TQ), F32)],
        compiler_params=_cparams(("parallel", "parallel", "arbitrary")),
        name="attention",
    )(qt, k4, vt)


def _conv_kernel(x_ref, prev_ref, next_ref, w_ref, o_ref, *, nct, nb):
    i = pl.program_id(1)
    x = x_ref[0]
    rows = x.shape[0]
    left_edge = jnp.logical_or(i == 0, i == nct)
    right_edge = jnp.logical_or(i == nct - 1, i == nb - 1)
    prev_row = jnp.where(left_edge, 0.0, prev_ref[0, 7:8, :])
    next_row = jnp.where(right_edge, 0.0, next_ref[0, 0:1, :])
    ridx = lax.broadcasted_iota(jnp.int32, x.shape, 0)
    xm1 = jnp.where(ridx == 0, prev_row, pltpu.roll(x, 1, 0))
    xp1 = jnp.where(ridx == rows - 1, next_row, pltpu.roll(x, rows - 1, 0))
    w = w_ref[...]
    y = xm1 * w[0:1] + x * w[1:2] + xp1 * w[2:3]
    y = y * _sigmoid(y)
    lane = lax.broadcasted_iota(jnp.int32, x.shape, 1)
    y = jnp.where(lane >= ML_W, y * (ML_DK ** -0.5), y)
    o_ref[0] = y.astype(o_ref.dtype)


def _conv_call(mqk, w, nct):
    bsz, t, width = mqk.shape
    nb = t // SCAN_BLK
    per8 = SCAN_BLK // 8
    last8 = t // 8 - 1
    kern = functools.partial(_conv_kernel, nct=nct, nb=nb)
    return pl.pallas_call(
        kern,
        grid=(bsz, nb),
        in_specs=[pl.BlockSpec((1, SCAN_BLK, width), lambda b, i: (b, i, 0)),
                  pl.BlockSpec((1, 8, width), lambda b, i: (b, jnp.maximum(i * per8 - 1, 0), 0)),
                  pl.BlockSpec((1, 8, width), lambda b, i: (b, jnp.minimum((i + 1) * per8, last8), 0)),
                  pl.BlockSpec(w.shape, lambda b, i: (0, 0))],
        out_specs=pl.BlockSpec((1, SCAN_BLK, width), lambda b, i: (b, i, 0)),
        out_shape=jax.ShapeDtypeStruct((bsz, t, width), BF16),
        compiler_params=_cparams(("parallel", "parallel")),
        name="mlstm_qk_conv",
    )(mqk, mqk, mqk, w)


def _scan_block(d, j, nct, nb):
    bwd = jnp.where(j < nct, nct - 1 - j, nb - 1 - (j - nct))
    return jnp.where(d == 0, j, bwd)


def _hgrn_kernel(q_ref, v_ref, z_ref, lb_ref, tri_ref, pm_ref, e_ref, bd_ref, o_ref, st_ref):
    d = pl.program_id(1)
    j = pl.program_id(2)

    @pl.when(j == 0)
    def _():
        st_ref[...] = jnp.zeros(st_ref.shape, F32)

    lb = lb_ref[0]
    tri = tri_ref[0]
    e = e_ref[...]
    bd = bd_ref[...]
    c = HG_CHUNK
    n_chunks = SCAN_BLK // c

    def body(i, carry):
        ci = jnp.where(d == 0, i, n_chunks - 1 - i)
        off = pl.multiple_of(ci * c, c)
        q = q_ref[0, pl.ds(off, c), :]
        v = v_ref[0, pl.ds(off, c), :]
        z = z_ref[0, pl.ds(off, c), :]
        f = lb + (1.0 - lb) * _sigmoid(z)
        kk = 1.0 - f
        lf = jnp.log(f)
        b = _mask_dot(tri, lf)
        b_last = jnp.sum(lf, axis=0, keepdims=True)
        st = st_ref[...]
        o = _dot_nt((q * jnp.exp(b)).astype(BF16), st.astype(BF16))
        q_t = jnp.concatenate([q] * c, axis=0)
        b_t = jnp.concatenate([b] * c, axis=0)
        k_s = jnp.concatenate([jnp.broadcast_to(kk[s:s + 1], (c, kk.shape[1])) for s in range(c)], axis=0)
        b_s = jnp.concatenate([jnp.broadcast_to(b[s:s + 1], (c, b.shape[1])) for s in range(c)], axis=0)
        v_s = jnp.concatenate([jnp.broadcast_to(v[s:s + 1], (c, v.shape[1])) for s in range(c)], axis=0)
        pm = pm_ref[0]
        p = q_t * k_s * jnp.exp(jnp.minimum(b_t - b_s, 0.0)) * pm
        a = _dot(p.astype(BF16), e)
        av = a * v_s
        for s in range(c):
            o = o + av[s * c:(s + 1) * c]
        o_ref[0, 0, pl.ds(off, c), :] = o
        k_dec = kk * jnp.exp(b_last - b)
        upd = _dot_tn(v.astype(BF16), k_dec.astype(BF16))
        st_ref[...] = st * jnp.exp(b_last) + upd * bd
        return carry

    lax.fori_loop(0, n_chunks, body, 0)


def _hgrn_call(hgp, lbs, tri, pm, e, bd, nct):
    bsz, t, _ = hgp.shape
    nb = t // SCAN_BLK
    w = HG_W
    blk = lambda d, j: _scan_block(d, j, nct, nb)
    return pl.pallas_call(
        _hgrn_kernel,
        grid=(bsz, 2, nb),
        in_specs=[pl.BlockSpec((1, SCAN_BLK, w), lambda b, d, j: (b, blk(d, j), 0)),
                  pl.BlockSpec((1, SCAN_BLK, w), lambda b, d, j: (b, blk(d, j), 1)),
                  pl.BlockSpec((1, SCAN_BLK, w), lambda b, d, j: (b, blk(d, j), 2 + d)),
                  pl.BlockSpec((1, 1, w), lambda b, d, j: (d, 0, 0)),
                  pl.BlockSpec((1, HG_CHUNK, HG_CHUNK), lambda b, d, j: (d, 0, 0)),
                  pl.BlockSpec((1, HG_CHUNK * HG_CHUNK, w), lambda b, d, j: (d, 0, 0)),
                  pl.BlockSpec((w, w), lambda b, d, j: (0, 0)),
                  pl.BlockSpec((w, w), lambda b, d, j: (0, 0))],
        out_specs=pl.BlockSpec((1, 1, SCAN_BLK, w), lambda b, d, j: (d, b, blk(d, j), 0)),
        out_shape=jax.ShapeDtypeStruct((2, bsz, t, w), F32),
        scratch_shapes=[pltpu.VMEM((w, w), F32)],
        compiler_params=_cparams(("parallel", "parallel", "arbitrary")),
        name="hgrn2_scan",
    )(hgp, hgp, hgp, lbs, tri, pm, e, bd)


def _mlstm_kernel(q_ref, k_ref, v_ref, g_ref, gt_ref, m_ref, mt_ref, o_ref, c_ref, ms_ref):
    j = pl.program_id(2)

    @pl.when(j == 0)
    def _():
        c_ref[...] = jnp.zeros(c_ref.shape, F32)
        ms_ref[...] = jnp.zeros(ms_ref.shape, F32)

    length = SCAN_BLK
    mask = m_ref[0]
    mask_t = mt_ref[0]
    valid = mask > 0
    gates = g_ref[0, 0]
    gates_t = gt_ref[0, 0]
    b_col_all = _mask_dot(mask, _log_sigmoid(gates))
    lf_t = _log_sigmoid(gates_t)
    b_row_all = _dot_mask(lf_t, mask_t)
    tot_all = jnp.sum(lf_t, axis=1, keepdims=True)
    lane = lax.broadcasted_iota(jnp.int32, (length, 128), 1)
    for pair in range(ML_HEADS // 2):
        v2 = v_ref[0, :, pair * 128:(pair + 1) * 128]
        out_pair = jnp.zeros((length, 128), F32)
        for sub in range(2):
            hh = pair * 2 + sub
            own = (lane < 64) if sub == 0 else (lane >= 64)
            ones_lane = 64 if sub == 0 else 0
            qh = q_ref[0, :, hh * ML_DK:(hh + 1) * ML_DK]
            kh = k_ref[0, :, hh * ML_DK:(hh + 1) * ML_DK]
            v_aug = jnp.where(own, v2, jnp.where(lane == ones_lane, 1.0, 0.0)).astype(BF16)
            m_prev = ms_ref[hh, 0:1, 0:1]
            bc = b_col_all[:, 4 + hh:5 + hh]
            br = b_row_all[4 + hh:5 + hh, :]
            ig_row = gates_t[hh:hh + 1, :]
            ig_col = gates[:, hh:hh + 1]
            tot = tot_all[4 + hh:5 + hh, :]
            log_d = jnp.where(valid, bc - br + ig_row, NEG_BIG)
            m_t = jnp.maximum(bc + m_prev, jnp.max(log_d, axis=1, keepdims=True))
            dmat = jnp.exp(log_d - m_t)
            g = jnp.exp(bc + m_prev - m_t)
            s = _dot_nt(qh, kh) * dmat
            c_aug = c_ref[hh]
            r = g * _dot(qh, c_aug.astype(BF16)) + _dot(s.astype(BF16), v_aug)
            den = jnp.sum(jnp.where(lane == ones_lane, r, 0.0), axis=1, keepdims=True)
            hval = r / jnp.maximum(jnp.abs(den), jnp.exp(-m_t))
            out_pair = jnp.where(own, hval, out_pair)
            log_e = tot - bc + ig_col
            m_new = jnp.maximum(tot + m_prev, jnp.max(log_e, axis=0, keepdims=True))
            ke = (kh.astype(F32) * jnp.exp(log_e - m_new)).astype(BF16)
            g_c = jnp.exp(tot + m_prev - m_new)
            c_ref[hh] = g_c * c_aug + _dot_tn(ke, v_aug)
            ms_ref[hh] = jnp.broadcast_to(m_new, ms_ref.shape[1:])
        o_ref[0, 0, :, pair * 128:(pair + 1) * 128] = out_pair


def _mlstm_call(qk, mvo, gcol, grow, mask, mask_t, nct):
    bsz, t, _ = qk.shape
    nb = t // SCAN_BLK
    w = ML_W
    blk = lambda d, j: _scan_block(d, j, nct, nb)
    return pl.pallas_call(
        _mlstm_kernel,
        grid=(bsz, 2, nb),
        in_specs=[pl.BlockSpec((1, SCAN_BLK, w), lambda b, d, j: (b, blk(d, j), 0)),
                  pl.BlockSpec((1, SCAN_BLK, w), lambda b, d, j: (b, blk(d, j), 1)),
                  pl.BlockSpec((1, SCAN_BLK, w), lambda b, d, j: (b, blk(d, j), 0)),
                  pl.BlockSpec((1, 1, SCAN_BLK, 128), lambda b, d, j: (b, d, blk(d, j), 0)),
                  pl.BlockSpec((1, 1, 8, SCAN_BLK), lambda b, d, j: (b, d, 0, blk(d, j))),
                  pl.BlockSpec((1, SCAN_BLK, SCAN_BLK), lambda b, d, j: (d, 0, 0)),
                  pl.BlockSpec((1, SCAN_BLK, SCAN_BLK), lambda b, d, j: (d, 0, 0))],
        out_specs=pl.BlockSpec((1, 1, SCAN_BLK, w), lambda b, d, j: (d, b, blk(d, j), 0)),
        out_shape=jax.ShapeDtypeStruct((2, bsz, t, w), F32),
        scratch_shapes=[pltpu.VMEM((ML_HEADS, ML_DK, 128), F32), pltpu.VMEM((ML_HEADS, 8, 128), F32)],
        compiler_params=_cparams(("parallel", "parallel", "arbitrary")),
        name="mlstm_scan",
    )(qk, qk, mvo, gcol, grow, mask, mask_t)


def _out_kernel(h_ref, mod_ref, at_ref, hf_ref, hb_ref, hgate_ref, gn_ref, e_ref, mf_ref, mb_ref, mgate_ref,
                wa_ref, wr_ref, wm_ref, o_ref):
    mod = mod_ref[0, 0]
    o = hf_ref[0, 0] + hb_ref[0, 0]
    ss = _dot((o * o).astype(BF16), e_ref[...])
    gate = hgate_ref[0]
    r = o * lax.rsqrt(ss + EPS) * gn_ref[...] * (gate * _sigmoid(gate))
    m = _sigmoid(mgate_ref[0]) * (mf_ref[0, 0] + mb_ref[0, 0])
    y = _dot_tn(at_ref[0], wa_ref[...]) + _dot(r.astype(BF16), wr_ref[...]) + _dot(m.astype(BF16), wm_ref[...])
    o_ref[0] = h_ref[0] + mod[2:3] * y


def _out_call(h, modsel, at, hg_o, hgp, gn, e, ml_o, mvo, wa, wr, wm, nct, skip):
    bsz, t, d = h.shape
    nt = t // TM - skip
    full = lambda a: pl.BlockSpec(a.shape, lambda bi, i: (0,) * a.ndim)
    hg_col = (OFF_MQK - OFF_HG) // HG_W - 1
    return pl.pallas_call(
        _out_kernel,
        grid=(bsz, nt),
        in_specs=[pl.BlockSpec((1, TM, d), lambda bi, i: (bi, i + skip, 0)),
                  pl.BlockSpec((1, 1, N_MOD, d), lambda bi, i: (bi, (i + skip >= nct).astype(jnp.int32), 0, 0)),
                  pl.BlockSpec((1, ATTN_W, TM), lambda bi, i: (bi, 0, i + skip)),
                  pl.BlockSpec((1, 1, TM, HG_W), lambda bi, i: (0, bi, i + skip, 0)),
                  pl.BlockSpec((1, 1, TM, HG_W), lambda bi, i: (1, bi, i + skip, 0)),
                  pl.BlockSpec((1, TM, HG_W), lambda bi, i: (bi, i + skip, hg_col)),
                  full(gn), full(e),
                  pl.BlockSpec((1, 1, TM, ML_W), lambda bi, i: (0, bi, i + skip, 0)),
                  pl.BlockSpec((1, 1, TM, ML_W), lambda bi, i: (1, bi, i + skip, 0)),
                  pl.BlockSpec((1, TM, ML_W), lambda bi, i: (bi, i + skip, 1)),
                  full(wa), full(wr), full(wm)],
        out_specs=pl.BlockSpec((1, TM, d), lambda bi, i: (bi, i, 0)),
        out_shape=jax.ShapeDtypeStruct((bsz, nt * TM, d), F32),
        compiler_params=_cparams(("parallel", "parallel")),
        name="readout_out_proj",
    )(h, modsel, at, hg_o, hg_o, hgp, gn, e, ml_o, ml_o, mvo, wa, wr, wm)


def _ffn_kernel(h_ref, mod_ref, g_ref, w1_ref, w2_ref, o_ref):
    x = h_ref[0]
    mod = mod_ref[0, 0]
    ms = jnp.mean(x * x, axis=-1, keepdims=True)
    y = x * lax.rsqrt(ms + EPS) * g_ref[...]
    u = (y * (1.0 + mod[4:5]) + mod[3:4]).astype(BF16)
    a = jnp.maximum(_dot(u, w1_ref[...]), 0.0)
    a = (a * a).astype(BF16)
    o_ref[0] = x + mod[5:6] * _dot(a, w2_ref[...])


def _ffn_call(h, modsel, g, w1, w2, nct):
    bsz, t, d = h.shape
    nt = t // TM
    full = lambda a: pl.BlockSpec(a.shape, lambda bi, i: (0,) * a.ndim, pipeline_mode=pl.Buffered(1))
    return pl.pallas_call(
        _ffn_kernel,
        grid=(bsz, nt),
        in_specs=[pl.BlockSpec((1, TM, d), lambda bi, i: (bi, i, 0)),
                  pl.BlockSpec((1, 1, N_MOD, d), lambda bi, i: (bi, (i >= nct).astype(jnp.int32), 0, 0)),
                  pl.BlockSpec(g.shape, lambda bi, i: (0, 0)),
                  full(w1), full(w2)],
        out_specs=pl.BlockSpec((1, TM, d), lambda bi, i: (bi, i, 0)),
        out_shape=jax.ShapeDtypeStruct((bsz, t, d), F32),
        compiler_params=_cparams(("parallel", "parallel")),
        name="ffn",
    )(h, modsel, g, w1, w2)


def _rope_tables(n_ctx, n_lat):
    inv_freq = ROPE_THETA ** (-np.arange(0, ROPE_AXIS_DIM, 2, dtype=np.float32) / ROPE_AXIS_DIM)
    inv_freq = jnp.asarray(inv_freq, F32)
    tok = jnp.arange(n_lat)
    ang_row = (tok // GRID_W).astype(F32)[:, None] * inv_freq[None, :]
    ang_col = (tok % GRID_W).astype(F32)[:, None] * inv_freq[None, :]
    ang = jnp.concatenate([ang_row, ang_row, ang_col, ang_col], axis=1)
    ang = jnp.concatenate([jnp.zeros((n_ctx, HEAD_DIM), F32), ang], axis=0)
    sign = np.where((np.arange(HEAD_DIM) % 32) < 16, -1.0, 1.0).astype(np.float32)
    cos = jnp.cos(ang)
    sin = jnp.sin(ang) * jnp.asarray(sign)[None, :]
    return jnp.concatenate([cos, cos], axis=1), jnp.concatenate([sin, sin], axis=1)


def _block_ones(width, head, scale, dtype):
    idx = np.arange(width) // head
    return jnp.asarray((idx[:, None] == idx[None, :]).astype(np.float32) * scale, dtype)


def _scan_masks(n):
    tril = np.tril(np.ones((n, n), np.float32))
    return np.stack([tril, tril.T])


def kernel(x, c, ctx, c_ctx, w_mod, b_mod, norm_mix, norm_ffn, w_in, b_in, q_norm, k_norm,
           hg_lb, hg_norm, ml_conv, w_out, w_ff1, w_ff2):
    bsz, n_lat, d = x.shape
    n_ctx = ctx.shape[1]
    depth = w_mod.shape[0]
    t = n_ctx + n_lat
    nct = n_ctx // TM
    assert n_ctx % SCAN_BLK == 0 and n_lat % SCAN_BLK == 0 and n_lat % GRID_W == 0 and bsz + 1 <= 8

    cvec = jnp.concatenate([c, c_ctx[None, :], jnp.zeros((8 - bsz - 1, d), F32)], axis=0)
    mods = _mod_call(cvec, w_mod, b_mod).reshape(depth, 8, N_MOD, d)
    modsel = jnp.stack([jnp.broadcast_to(mods[:, bsz][:, None], (depth, bsz, N_MOD, d)), mods[:, :bsz]], axis=2)

    cos, sin = _rope_tables(n_ctx, n_lat)
    e_attn = _block_ones(ATTN_W, HEAD_DIM, 1.0 / HEAD_DIM, BF16)
    e_hg_mean = _block_ones(HG_W, HG_DK, 1.0 / HG_DK, BF16)
    e_hg = _block_ones(HG_W, HG_DK, 1.0, BF16)
    bd_hg = _block_ones(HG_W, HG_DK, 1.0, F32)
    tri16 = _scan_masks(HG_CHUNK)
    tri_hg = jnp.asarray(tri16, BF16)
    pair_mask = np.transpose(tri16, (0, 2, 1)).reshape(2, HG_CHUNK * HG_CHUNK, 1)
    pair_mask = jnp.asarray(np.broadcast_to(pair_mask, (2, HG_CHUNK * HG_CHUNK, HG_W)).copy(), F32)
    scan_mask = _scan_masks(SCAN_BLK)
    mask_ml = jnp.asarray(scan_mask, BF16)
    mask_ml_t = jnp.asarray(np.transpose(scan_mask, (0, 2, 1)).copy(), BF16)

    lbs = jnp.cumsum(jax.nn.softmax(hg_lb.astype(F32), axis=0), axis=0)
    lbs = (lbs - lbs[:1]).reshape(depth, 2, 1, HG_W)

    w_in_p = jnp.pad(w_in, ((0, 0), (0, 0), (0, N_IN_PAD - N_IN))).astype(BF16)
    b_in_p = jnp.pad(b_in, ((0, 0), (0, N_IN_PAD - N_IN))).reshape(depth, 1, N_IN_PAD)
    w_out_b = w_out.astype(BF16)
    w1_b = w_ff1.astype(BF16)
    w2_b = w_ff2.astype(BF16)
    gq = jnp.tile(q_norm, (1, ATTN_HEADS)).reshape(depth, 1, ATTN_W)
    gk = jnp.tile(k_norm, (1, ATTN_KV_HEADS)).reshape(depth, 1, KV_W)
    gn = jnp.tile(hg_norm, (1, HG_HEADS)).reshape(depth, 1, HG_W)

    h = jnp.concatenate([ctx, x], axis=1)
    for l in range(depth):
        last = l == depth - 1
        aq, ak, av, hgp, mqk, mvo, mg = _in_call(
            h, modsel[l], norm_mix[l].reshape(1, d), w_in_p[l], b_in_p[l], cos, sin, gq[l], gk[l], e_attn, nct)

        qt = aq.transpose(0, 2, 1)
        k4 = ak.reshape(bsz, t, ATTN_KV_HEADS, HEAD_DIM).transpose(0, 2, 1, 3)
        vt = av.reshape(bsz, t, ATTN_KV_HEADS, HEAD_DIM).transpose(0, 2, 3, 1)
        vt = jnp.concatenate([vt, jnp.ones((bsz, ATTN_KV_HEADS, 1, t), BF16),
                              jnp.zeros((bsz, ATTN_KV_HEADS, V_ROWS - HEAD_DIM - 1, t), BF16)], axis=2)
        at = _attn_call(qt, k4, vt, n_ctx)

        hg_o = _hgrn_call(hgp, lbs[l], tri_hg, pair_mask, e_hg, bd_hg, nct)

        qk = _conv_call(mqk, ml_conv[l], nct)
        g16 = mg[:, :, :16].reshape(bsz, t, 2, 2, ML_HEADS)
        gdir = g16.transpose(0, 3, 1, 2, 4).reshape(bsz, 2, t, 8)
        gcol = jnp.pad(gdir, ((0, 0), (0, 0), (0, 0), (0, 120)))
        grow = gdir.transpose(0, 1, 3, 2)
        ml_o = _mlstm_call(qk, mvo, gcol, grow, mask_ml, mask_ml_t, nct)

        skip = nct if last else 0
        h1 = _out_call(h, modsel[l], at, hg_o, hgp, gn[l], e_hg_mean, ml_o, mvo,
                       w_out_b[l, :ATTN_W], w_out_b[l, ATTN_W:ATTN_W + HG_W], w_out_b[l, ATTN_W + HG_W:], nct, skip)
        h = _ffn_call(h1, modsel[l], norm_ffn[l].reshape(1, d), w1_b[l], w2_b[l], 0 if last else nct)
    return h
```

```python
import functools

import jax
import jax.numpy as jnp
import numpy as np
from jax import lax
from jax.experimental import pallas as pl
from jax.experimental.pallas import tpu as pltpu

F32 = jnp.float32
BF16 = jnp.bfloat16

GRID_W = 64
HEAD_DIM = 64
ATTN_HEADS = 8
ATTN_KV_HEADS = 2
ATTN_REP = ATTN_HEADS // ATTN_KV_HEADS
ROPE_THETA = 10000.0
ROPE_AXIS_DIM = HEAD_DIM // 2
HG_HEADS = 4
HG_DK = 64
ML_HEADS = 4
ML_DK = 64
N_MOD = 6
EPS = 1e-6
ATTN_W = ATTN_HEADS * HEAD_DIM
KV_W = ATTN_KV_HEADS * HEAD_DIM
HG_W = HG_HEADS * HG_DK
ML_W = ML_HEADS * ML_DK

OFF_AQ, OFF_AK, OFF_AV = 0, 512, 640
OFF_HG = 768
OFF_MQK = 2048
OFF_MVO = 2560
OFF_MG = 3072
N_IN = 3088
N_IN_PAD = 3200

TM = 256
TQ = 256
TK = 256
ATTN_UNROLL = 8
SCAN_BLK = 256
HG_CHUNK = 16
V_ROWS = 80
QK_ROWS = 80
MAX_FIXED_SHIFT = 40.0
NEG_BIG = -1e30
Q_SCALE = float(np.log2(np.e)) * HEAD_DIM ** -0.5
VMEM_LIMIT = 56 * 1024 * 1024


def _cparams(sem):
    return pltpu.CompilerParams(dimension_semantics=sem, vmem_limit_bytes=VMEM_LIMIT)


def _split3(x):
    hi = x.astype(BF16)
    r1 = x - hi.astype(F32)
    mid = r1.astype(BF16)
    lo = (r1 - mid.astype(F32)).astype(BF16)
    return hi, mid, lo


def _dot(a, b):
    return jnp.dot(a, b, preferred_element_type=F32)


def _dot_nt(a, b):
    return lax.dot_general(a, b, (((1,), (1,)), ((), ())), preferred_element_type=F32)


def _dot_tn(a, b):
    return lax.dot_general(a, b, (((0,), (0,)), ((), ())), preferred_element_type=F32)


def _mask_dot(mask_bf16, x):
    hi, mid, lo = _split3(x)
    return _dot(mask_bf16, hi) + _dot(mask_bf16, mid) + _dot(mask_bf16, lo)


def _dot_mask(x, mask_bf16):
    hi, mid, lo = _split3(x)
    return _dot(hi, mask_bf16) + _dot(mid, mask_bf16) + _dot(lo, mask_bf16)


def _sigmoid(x):
    return 1.0 / (1.0 + jnp.exp(-x))


def _log_sigmoid(x):
    return jnp.minimum(x, 0.0) - jnp.log(1.0 + jnp.exp(-jnp.abs(x)))


def _mod_kernel(c_ref, w_ref, b_ref, o_ref):
    cv = c_ref[...]
    a = cv * _sigmoid(cv)
    o_ref[0] = jnp.dot(a, w_ref[0], preferred_element_type=F32, precision=lax.Precision.HIGHEST) + b_ref[0]


def _mod_call(cvec, w_mod, b_mod):
    depth, d, nm = w_mod.shape
    tn = 1024
    return pl.pallas_call(
        _mod_kernel,
        grid=(depth, nm // tn),
        in_specs=[pl.BlockSpec((8, d), lambda l, j: (0, 0)),
                  pl.BlockSpec((1, d, tn), lambda l, j: (l, 0, j)),
                  pl.BlockSpec((1, 1, tn), lambda l, j: (l, 0, j))],
        out_specs=pl.BlockSpec((1, 8, tn), lambda l, j: (l, 0, j)),
        out_shape=jax.ShapeDtypeStruct((depth, 8, nm), F32),
        compiler_params=_cparams(("arbitrary", "arbitrary")),
        name="mod_vectors",
    )(cvec, w_mod, b_mod.reshape(depth, 1, nm))


def _head_rms(t, gain, e):
    ss = _dot((t * t).astype(BF16), e)
    return t * lax.rsqrt(ss + EPS) * gain


def _rope(t, cos, sin):
    w = t.shape[1]
    reps = w // 128
    cosw = jnp.concatenate([cos] * reps, axis=1) if reps > 1 else cos
    sinw = jnp.concatenate([sin] * reps, axis=1) if reps > 1 else sin
    lane = lax.broadcasted_iota(jnp.int32, t.shape, 1)
    first_half = (lane % 32) < 16
    partner = jnp.where(first_half, pltpu.roll(t, w - 16, 1), pltpu.roll(t, 16, 1))
    return t * cosw + partner * sinw


def _in_kernel(h_ref, mod_ref, g_ref, w_ref, b_ref, cos_ref, sin_ref, gq_ref, gk_ref, e_ref,
               aq_ref, ak_ref, av_ref, hg_ref, mqk_ref, mvo_ref, mg_ref):
    x = h_ref[0]
    ms = jnp.mean(x * x, axis=-1, keepdims=True)
    y = x * lax.rsqrt(ms + EPS) * g_ref[...]
    mod = mod_ref[0, 0]
    u = y * (1.0 + mod[1:2]) + mod[0:1]
    p = _dot(u.astype(BF16), w_ref[...]) + b_ref[...]
    cos = cos_ref[...]
    sin = sin_ref[...]
    e = e_ref[...]
    q = _rope(_head_rms(p[:, OFF_AQ:OFF_AQ + ATTN_W], gq_ref[...], e), cos, sin)
    k = _rope(_head_rms(p[:, OFF_AK:OFF_AK + KV_W], gk_ref[...], e[:KV_W, :KV_W]), cos, sin)
    aq_ref[0] = (q * Q_SCALE).astype(BF16)
    ak_ref[0] = k.astype(BF16)
    av_ref[0] = p[:, OFF_AV:OFF_AV + KV_W].astype(BF16)
    hg_ref[0] = p[:, OFF_HG:OFF_MQK]
    mqk_ref[0] = p[:, OFF_MQK:OFF_MVO]
    mvo_ref[0] = p[:, OFF_MVO:OFF_MG]
    mg_ref[0] = p[:, OFF_MG:OFF_MG + 128]


def _in_call(h, modsel, g, w, b, cos, sin, gq, gk, e, nct):
    bsz, t, d = h.shape
    nt = t // TM
    row = lambda width: pl.BlockSpec((1, TM, width), lambda bi, i: (bi, i, 0))
    full = lambda a: pl.BlockSpec(a.shape, lambda bi, i: (0,) * a.ndim)
    out_widths = (ATTN_W, KV_W, KV_W, OFF_MQK - OFF_HG, 2 * ML_W, 2 * ML_W, 128)
    out_dtypes = (BF16, BF16, BF16, F32, F32, F32, F32)
    return pl.pallas_call(
        _in_kernel,
        grid=(bsz, nt),
        in_specs=[row(d),
                  pl.BlockSpec((1, 1, N_MOD, d), lambda bi, i: (bi, (i >= nct).astype(jnp.int32), 0, 0)),
                  full(g), full(w), full(b),
                  pl.BlockSpec((TM, 128), lambda bi, i: (i, 0)),
                  pl.BlockSpec((TM, 128), lambda bi, i: (i, 0)),
                  full(gq), full(gk), full(e)],
        out_specs=[row(wd) for wd in out_widths],
        out_shape=[jax.ShapeDtypeStruct((bsz, t, wd), dt) for wd, dt in zip(out_widths, out_dtypes)],
        compiler_params=_cparams(("parallel", "parallel")),
        name="norm_mod_in_proj",
    )(h, modsel, g, w, b, cos, sin, gq, gk, e)


def _attn_kernel(qa_ref, ka_ref, vt_ref, o_ref, acc_ref, s_ref, *maybe_m_ref, online, n_ctx_q, n_ctx_k, n_k):
    qi = pl.program_id(2)
    acc_ref[...] = jnp.zeros(acc_ref.shape, F32)
    if online:
        m_ref, = maybe_m_ref
        m_ref[...] = jnp.full(m_ref.shape, NEG_BIG, F32)
    n_blocks = jnp.where(qi < n_ctx_q, n_ctx_k, n_k)

    def key_tile(kb):
        return ka_ref[0, 0, pl.ds(pl.multiple_of(kb * TK, TK), TK), :]

    def produce(kt, slot, hh):
        s = _dot(kt, qa_ref[0, hh])
        s_ref[slot, hh] = s if online else jnp.exp2(s).astype(s_ref.dtype)

    def step(kb, cur, nxt):
        vt = vt_ref[0, 0, :, pl.ds(pl.multiple_of(kb * TK, TK), TK)]
        kt = key_tile(jnp.minimum(kb + 1, n_blocks - 1))
        for hh in range(ATTN_REP):
            produce(kt, nxt, hh)
            if online:
                s = s_ref[cur, hh]
                m_old = m_ref[hh]
                m_new = jnp.maximum(m_old, jnp.max(s, axis=0, keepdims=True))
                p = jnp.exp2(s - m_new).astype(BF16)
                acc_ref[hh] = jnp.exp2(m_old - m_new) * acc_ref[hh] + _dot(vt, p)
                m_ref[hh] = m_new
            else:
                acc_ref[hh] += _dot(vt, s_ref[cur, hh])

    kt0 = key_tile(0)
    for hh in range(ATTN_REP):
        produce(kt0, 0, hh)

    def group(i, carry):
        for u in range(ATTN_UNROLL):
            step(ATTN_UNROLL * i + u, u % 2, (u + 1) % 2)
        return carry

    def pair(i, carry):
        step(2 * i, 0, 1)
        step(2 * i + 1, 1, 0)
        return carry

    n_groups = n_blocks // ATTN_UNROLL
    lax.fori_loop(0, n_groups, group, 0)
    lax.fori_loop(n_groups * (ATTN_UNROLL // 2), n_blocks // 2, pair, 0)

    @pl.when(n_blocks % 2 == 1)
    def _():
        step(n_blocks - 1, 0, 1)

    for hh in range(ATTN_REP):
        acc = acc_ref[hh]
        o = acc[:HEAD_DIM] / acc[HEAD_DIM:HEAD_DIM + 1]
        o_ref[0, hh * HEAD_DIM:(hh + 1) * HEAD_DIM, :] = o.astype(o_ref.dtype)


def _attn_call(qa, ka, vt, n_ctx, online):
    bsz, _, _, t = qa.shape
    kern = functools.partial(_attn_kernel, online=online, n_ctx_q=n_ctx // TQ, n_ctx_k=n_ctx // TK, n_k=t // TK)
    gw = ATTN_REP * HEAD_DIM
    scratch = [pltpu.VMEM((ATTN_REP, V_ROWS, TQ), F32), pltpu.VMEM((2, ATTN_REP, TK, TQ), F32 if online else BF16)]
    if online:
        scratch.append(pltpu.VMEM((ATTN_REP, 1, TQ), F32))
    return pl.pallas_call(
        kern,
        grid=(bsz, ATTN_KV_HEADS, t // TQ),
        in_specs=[pl.BlockSpec((1, ATTN_REP, QK_ROWS, TQ), lambda b, g, i: (b, g, 0, i)),
                  pl.BlockSpec((1, 1, t, QK_ROWS), lambda b, g, i: (b, g, 0, 0)),
                  pl.BlockSpec((1, 1, V_ROWS, t), lambda b, g, i: (b, g, 0, 0))],
        out_specs=pl.BlockSpec((1, gw, TQ), lambda b, g, i: (b, g, i)),
        out_shape=jax.ShapeDtypeStruct((bsz, ATTN_W, t), BF16),
        scratch_shapes=scratch,
        compiler_params=_cparams(("parallel", "parallel", "arbitrary")),
        name="attention_online" if online else "attention",
    )(qa, ka, vt)


def _attention(aq, ak, av, shift, n_ctx):
    bsz, t, _ = aq.shape
    qt = aq.transpose(0, 2, 1).reshape(bsz, ATTN_HEADS, HEAD_DIM, t)
    qa = jnp.concatenate([qt, jnp.broadcast_to((-shift).astype(BF16), (bsz, ATTN_HEADS, 1, t)),
                          jnp.zeros((bsz, ATTN_HEADS, QK_ROWS - HEAD_DIM - 1, t), BF16)], axis=2)
    k4 =ak.reshape(bsz, t, ATTN_KV_HEADS, HEAD_DIM).transpose(0, 2, 1, 3)
    ka = jnp.concatenate([k4, jnp.ones((bsz, ATTN_KV_HEADS, t, 1), BF16),
                          jnp.zeros((bsz, ATTN_KV_HEADS, t, QK_ROWS - HEAD_DIM - 1), BF16)], axis=3)
    vt = av.reshape(bsz, t, ATTN_KV_HEADS, HEAD_DIM).transpose(0, 2, 3, 1)
    vt = jnp.concatenate([vt, jnp.ones((bsz, ATTN_KV_HEADS, 1, t), BF16),
                          jnp.zeros((bsz, ATTN_KV_HEADS, V_ROWS - HEAD_DIM - 1, t), BF16)], axis=2)
    return lax.cond(shift <= MAX_FIXED_SHIFT,
                    lambda: _attn_call(qa, ka, vt, n_ctx, False),
                    lambda: _attn_call(qa, ka, vt, n_ctx, True))


def _conv_kernel(x_ref, prev_ref, next_ref, w_ref, o_ref, *, nct, nb):
    i = pl.program_id(1)
    x = x_ref[0]
    rows = x.shape[0]
    left_edge = jnp.logical_or(i == 0, i == nct)
    right_edge = jnp.logical_or(i == nct - 1, i == nb - 1)
    prev_row = jnp.where(left_edge, 0.0, prev_ref[0, 7:8, :])
    next_row = jnp.where(right_edge, 0.0, next_ref[0, 0:1, :])
    ridx = lax.broadcasted_iota(jnp.int32, x.shape, 0)
    xm1 = jnp.where(ridx == 0, prev_row, pltpu.roll(x, 1, 0))
    xp1 = jnp.where(ridx == rows - 1, next_row, pltpu.roll(x, rows - 1, 0))
    w = w_ref[...]
    y = xm1 * w[0:1] + x * w[1:2] + xp1 * w[2:3]
    y = y * _sigmoid(y)
    lane = lax.broadcasted_iota(jnp.int32, x.shape, 1)
    y = jnp.where(lane >= ML_W, y * (ML_DK ** -0.5), y)
    o_ref[0] = y.astype(o_ref.dtype)


def _conv_call(mqk, w, nct):
    bsz, t, width = mqk.shape
    nb = t // SCAN_BLK
    per8 = SCAN_BLK // 8
    last8 = t // 8 - 1
    kern = functools.partial(_conv_kernel, nct=nct, nb=nb)
    return pl.pallas_call(
        kern,
        grid=(bsz, nb),
        in_specs=[pl.BlockSpec((1, SCAN_BLK, width), lambda b, i: (b, i, 0)),
                  pl.BlockSpec((1, 8, width), lambda b, i: (b, jnp.maximum(i * per8 - 1, 0), 0)),
                  pl.BlockSpec((1, 8, width), lambda b, i: (b, jnp.minimum((i + 1) * per8, last8), 0)),
                  pl.BlockSpec(w.shape, lambda b, i: (0, 0))],
        out_specs=pl.BlockSpec((1, SCAN_BLK, width), lambda b, i: (b, i, 0)),
        out_shape=jax.ShapeDtypeStruct((bsz, t, width), BF16),
        compiler_params=_cparams(("parallel", "parallel")),
        name="mlstm_qk_conv",
    )(mqk, mqk, mqk, w)


def _scan_block(d, j, nct, nb):
    bwd = jnp.where(j < nct, nct - 1 - j, nb - 1 - (j - nct))
    return jnp.where(d == 0, j, bwd)


def _hgrn_kernel(q_ref, v_ref, z_ref, lb_ref, tri_ref, pm_ref, e_ref, bd_ref, o_ref, st_ref):
    d = pl.program_id(1)
    j = pl.program_id(2)

    @pl.when(j == 0)
    def _():
        st_ref[...] = jnp.zeros(st_ref.shape, F32)

    lb = lb_ref[0]
    tri = tri_ref[0]
    e = e_ref[...]
    bd = bd_ref[...]
    c = HG_CHUNK
    n_chunks = SCAN_BLK // c

    def body(i, carry):
        ci = jnp.where(d == 0, i, n_chunks - 1 - i)
        off = pl.multiple_of(ci * c, c)
        q = q_ref[0, pl.ds(off, c), :]
        v = v_ref[0, pl.ds(off, c), :]
        z = z_ref[0, pl.ds(off, c), :]
        f = lb + (1.0 - lb) * _sigmoid(z)
        kk = 1.0 - f
        lf = jnp.log(f)
        b = _mask_dot(tri, lf)
        b_last = jnp.sum(lf, axis=0, keepdims=True)
        st = st_ref[...]
        o = _dot_nt((q * jnp.exp(b)).astype(BF16), st.astype(BF16))
        q_t = jnp.concatenate([q] * c, axis=0)
        b_t = jnp.concatenate([b] * c, axis=0)
        k_s = jnp.concatenate([jnp.broadcast_to(kk[s:s + 1], (c, kk.shape[1])) for s in range(c)], axis=0)
        b_s = jnp.concatenate([jnp.broadcast_to(b[s:s + 1], (c, b.shape[1])) for s in range(c)], axis=0)
        v_s = jnp.concatenate([jnp.broadcast_to(v[s:s + 1], (c, v.shape[1])) for s in range(c)], axis=0)
        pm = pm_ref[0]
        p = q_t * k_s * jnp.exp(jnp.minimum(b_t - b_s, 0.0)) * pm
        a = _dot(p.astype(BF16), e)
        av = a * v_s
        for s in range(c):
            o = o + av[s * c:(s + 1) * c]
        o_ref[0, 0, pl.ds(off, c), :] = o
        k_dec = kk * jnp.exp(b_last - b)
        upd = _dot_tn(v.astype(BF16), k_dec.astype(BF16))
        st_ref[...] = st * jnp.exp(b_last) + upd * bd
        return carry

    lax.fori_loop(0, n_chunks, body, 0)


def _hgrn_call(hgp, lbs, tri, pm, e, bd, nct):
    bsz, t, _ = hgp.shape
    nb = t // SCAN_BLK
    w = HG_W
    blk = lambda d, j: _scan_block(d, j, nct, nb)
    return pl.pallas_call(
        _hgrn_kernel,
        grid=(bsz, 2, nb),
        in_specs=[pl.BlockSpec((1, SCAN_BLK, w), lambda b, d, j: (b, blk(d, j), 0)),
                  pl.BlockSpec((1, SCAN_BLK, w), lambda b, d, j: (b, blk(d, j), 1)),
                  pl.BlockSpec((1, SCAN_BLK, w), lambda b, d, j: (b, blk(d, j), 2 + d)),
                  pl.BlockSpec((1, 1, w), lambda b, d, j: (d, 0, 0)),
                  pl.BlockSpec((1, HG_CHUNK, HG_CHUNK), lambda b, d, j: (d, 0, 0)),
                  pl.BlockSpec((1, HG_CHUNK * HG_CHUNK, w), lambda b, d, j: (d, 0, 0)),
                  pl.BlockSpec((w, w), lambda b, d, j: (0, 0)),
                  pl.BlockSpec((w, w), lambda b, d, j: (0, 0))],
        out_specs=pl.BlockSpec((1, 1, SCAN_BLK, w), lambda b, d, j: (d, b, blk(d, j), 0)),
        out_shape=jax.ShapeDtypeStruct((2, bsz, t, w), F32),
        scratch_shapes=[pltpu.VMEM((w, w), F32)],
        compiler_params=_cparams(("parallel", "parallel", "arbitrary")),
        name="hgrn2_scan",
    )(hgp, hgp, hgp, lbs, tri, pm, e, bd)


def _mlstm_kernel(q_ref, k_ref, v_ref, g_ref, gt_ref, m_ref, mt_ref, o_ref, c_ref, ms_ref):
    j = pl.program_id(2)

    @pl.when(j == 0)
    def _():
        c_ref[...] = jnp.zeros(c_ref.shape, F32)
        ms_ref[...] = jnp.zeros(ms_ref.shape, F32)

    length = SCAN_BLK
    mask = m_ref[0]
    mask_t = mt_ref[0]
    valid = mask > 0
    gates = g_ref[0, 0]
    gates_t = gt_ref[0, 0]
    b_col_all = _mask_dot(mask, _log_sigmoid(gates))
    lf_t = _log_sigmoid(gates_t)
    b_row_all = _dot_mask(lf_t, mask_t)
    tot_all = jnp.sum(lf_t, axis=1, keepdims=True)
    lane = lax.broadcasted_iota(jnp.int32, (length, 128), 1)
    for pair in range(ML_HEADS // 2):
        v2 = v_ref[0, :, pair * 128:(pair + 1) * 128]
        out_pair = jnp.zeros((length, 128), F32)
        for sub in range(2):
            hh = pair * 2 + sub
            own = (lane < 64) if sub == 0 else (lane >= 64)
            ones_lane = 64 if sub == 0 else 0
            qh = q_ref[0, :, hh * ML_DK:(hh + 1) * ML_DK]
            kh = k_ref[0, :, hh * ML_DK:(hh + 1) * ML_DK]
            v_aug = jnp.where(own, v2, jnp.where(lane == ones_lane, 1.0, 0.0)).astype(BF16)
            m_prev = ms_ref[hh, 0:1, 0:1]
            bc = b_col_all[:, 4 + hh:5 + hh]
            br = b_row_all[4 + hh:5 + hh, :]
            ig_row = gates_t[hh:hh + 1, :]
            ig_col = gates[:, hh:hh + 1]
            tot = tot_all[4 + hh:5 + hh, :]
            log_d = jnp.where(valid, bc - br + ig_row, NEG_BIG)
            m_t = jnp.maximum(bc + m_prev, jnp.max(log_d, axis=1, keepdims=True))
            dmat = jnp.exp(log_d - m_t)
            g = jnp.exp(bc + m_prev - m_t)
            s = _dot_nt(qh, kh) * dmat
            c_aug = c_ref[hh]
            r = g * _dot(qh, c_aug.astype(BF16)) + _dot(s.astype(BF16), v_aug)
            den = jnp.sum(jnp.where(lane == ones_lane, r, 0.0), axis=1, keepdims=True)
            hval = r / jnp.maximum(jnp.abs(den), jnp.exp(-m_t))
            out_pair = jnp.where(own, hval, out_pair)
            log_e = tot - bc + ig_col
            m_new = jnp.maximum(tot + m_prev, jnp.max(log_e, axis=0, keepdims=True))
            ke = (kh.astype(F32) * jnp.exp(log_e - m_new)).astype(BF16)
            g_c = jnp.exp(tot + m_prev - m_new)
            c_ref[hh] = g_c * c_aug + _dot_tn(ke, v_aug)
            ms_ref[hh] = jnp.broadcast_to(m_new, ms_ref.shape[1:])
        o_ref[0, 0, :, pair * 128:(pair + 1) * 128] = out_pair


def _mlstm_call(qk, mvo, gcol, grow, mask, mask_t, nct):
    bsz, t, _ = qk.shape
    nb = t // SCAN_BLK
    w = ML_W
    blk = lambda d, j: _scan_block(d, j, nct, nb)
    return pl.pallas_call(
        _mlstm_kernel,
        grid=(bsz, 2, nb),
        in_specs=[pl.BlockSpec((1, SCAN_BLK, w), lambda b, d, j: (b, blk(d, j), 0)),
                  pl.BlockSpec((1, SCAN_BLK, w), lambda b, d, j: (b, blk(d, j), 1)),
                  pl.BlockSpec((1, SCAN_BLK, w), lambda b, d, j: (b, blk(d, j), 0)),
                  pl.BlockSpec((1, 1, SCAN_BLK, 128), lambda b, d, j: (b, d, blk(d, j), 0)),
                  pl.BlockSpec((1, 1, 8, SCAN_BLK), lambda b, d, j: (b, d, 0, blk(d, j))),
                  pl.BlockSpec((1, SCAN_BLK, SCAN_BLK), lambda b, d, j: (d, 0, 0)),
                  pl.BlockSpec((1, SCAN_BLK, SCAN_BLK), lambda b, d, j: (d, 0, 0))],
        out_specs=pl.BlockSpec((1, 1, SCAN_BLK, w), lambda b, d, j: (d, b, blk(d, j), 0)),
        out_shape=jax.ShapeDtypeStruct((2, bsz, t, w), F32),
        scratch_shapes=[pltpu.VMEM((ML_HEADS, ML_DK, 128), F32), pltpu.VMEM((ML_HEADS, 8, 128), F32)],
        compiler_params=_cparams(("parallel", "parallel", "arbitrary")),
        name="mlstm_scan",
    )(qk, qk, mvo, gcol, grow, mask, mask_t)


def _out_kernel(h_ref, mod_ref, at_ref, hf_ref, hb_ref, hgate_ref, gn_ref, e_ref, mf_ref, mb_ref, mgate_ref,
                wa_ref, wr_ref, wm_ref, o_ref):
    mod = mod_ref[0, 0]
    o = hf_ref[0, 0] + hb_ref[0, 0]
    ss = _dot((o * o).astype(BF16), e_ref[...])
    gate = hgate_ref[0]
    r = o * lax.rsqrt(ss + EPS) * gn_ref[...] * (gate * _sigmoid(gate))
    m = _sigmoid(mgate_ref[0]) * (mf_ref[0, 0] + mb_ref[0, 0])
    y = _dot_tn(at_ref[0], wa_ref[...]) + _dot(r.astype(BF16), wr_ref[...]) + _dot(m.astype(BF16), wm_ref[...])
    o_ref[0] = h_ref[0] + mod[2:3] * y


def _out_call(h, modsel, at, hg_o, hgp, gn, e, ml_o, mvo, wa, wr, wm, nct, skip):
    bsz, t, d = h.shape
    nt = t // TM - skip
    full = lambda a: pl.BlockSpec(a.shape, lambda bi, i: (0,) * a.ndim)
    hg_col = (OFF_MQK - OFF_HG) // HG_W - 1
    return pl.pallas_call(
        _out_kernel,
        grid=(bsz, nt),
        in_specs=[pl.BlockSpec((1, TM, d), lambda bi, i: (bi, i + skip, 0)),
                  pl.BlockSpec((1, 1, N_MOD, d), lambda bi, i: (bi, (i + skip >= nct).astype(jnp.int32), 0, 0)),
                  pl.BlockSpec((1, ATTN_W, TM), lambda bi, i: (bi, 0, i + skip)),
                  pl.BlockSpec((1, 1, TM, HG_W), lambda bi, i: (0, bi, i + skip, 0)),
                  pl.BlockSpec((1, 1, TM, HG_W), lambda bi, i: (1, bi, i + skip, 0)),
                  pl.BlockSpec((1, TM, HG_W), lambda bi, i: (bi, i + skip, hg_col)),
                  full(gn), full(e),
                  pl.BlockSpec((1, 1, TM, ML_W), lambda bi, i: (0, bi, i + skip, 0)),
                  pl.BlockSpec((1, 1, TM, ML_W), lambda bi, i: (1, bi, i + skip, 0)),
                  pl.BlockSpec((1, TM, ML_W), lambda bi, i: (bi, i + skip, 1)),
                  full(wa), full(wr), full(wm)],
        out_specs=pl.BlockSpec((1, TM, d), lambda bi, i: (bi, i, 0)),
        out_shape=jax.ShapeDtypeStruct((bsz, nt * TM, d), F32),
        compiler_params=_cparams(("parallel", "parallel")),
        name="readout_out_proj",
    )(h, modsel, at, hg_o, hg_o, hgp, gn, e, ml_o, ml_o, mvo, wa, wr, wm)


def _ffn_kernel(h_ref, mod_ref, g_ref, w1_ref, w2_ref, o_ref):
    x = h_ref[0]
    mod = mod_ref[0, 0]
    ms = jnp.mean(x * x, axis=-1, keepdims=True)
    y = x * lax.rsqrt(ms + EPS) * g_ref[...]
    u = (y * (1.0 + mod[4:5]) + mod[3:4]).astype(BF16)
    a = jnp.maximum(_dot(u, w1_ref[...]), 0.0)
    a = (a * a).astype(BF16)
    o_ref[0] = x + mod[5:6] * _dot(a, w2_ref[...])


def _ffn_call(h, modsel, g, w1, w2, nct):
    bsz, t, d = h.shape
    nt = t // TM
    full = lambda a: pl.BlockSpec(a.shape, lambda bi, i: (0,) * a.ndim, pipeline_mode=pl.Buffered(1))
    return pl.pallas_call(
        _ffn_kernel,
        grid=(bsz, nt),
        in_specs=[pl.BlockSpec((1, TM, d), lambda bi, i: (bi, i, 0)),
                  pl.BlockSpec((1, 1, N_MOD, d), lambda bi, i: (bi, (i >= nct).astype(jnp.int32), 0, 0)),
                  pl.BlockSpec(g.shape, lambda bi, i: (0, 0)),
                  full(w1), full(w2)],
        out_specs=pl.BlockSpec((1, TM, d), lambda bi, i: (bi, i, 0)),
        out_shape=jax.ShapeDtypeStruct((bsz, t, d), F32),
        compiler_params=_cparams(("parallel", "parallel")),
        name="ffn",
    )(h, modsel, g, w1, w2)


def _rope_tables(n_ctx, n_lat):
    inv_freq = ROPE_THETA ** (-np.arange(0, ROPE_AXIS_DIM, 2, dtype=np.float32) / ROPE_AXIS_DIM)
    inv_freq = jnp.asarray(inv_freq, F32)
    tok = jnp.arange(n_lat)
    ang_row = (tok // GRID_W).astype(F32)[:, None] * inv_freq[None, :]
    ang_col = (tok % GRID_W).astype(F32)[:, None] * inv_freq[None, :]
    ang = jnp.concatenate([ang_row, ang_row, ang_col, ang_col], axis=1)
    ang = jnp.concatenate([jnp.zeros((n_ctx, HEAD_DIM), F32), ang], axis=0)
    sign = np.where((np.arange(HEAD_DIM) % 32) < 16, -1.0, 1.0).astype(np.float32)
    cos = jnp.cos(ang)
    sin = jnp.sin(ang) * jnp.asarray(sign)[None, :]
    return jnp.concatenate([cos, cos], axis=1), jnp.concatenate([sin, sin], axis=1)


def _block_ones(width, head, scale, dtype):
    idx = np.arange(width) // head
    return jnp.asarray((idx[:, None] == idx[None, :]).astype(np.float32) * scale, dtype)


def _scan_masks(n):
    tril = np.tril(np.ones((n, n), np.float32))
    return np.stack([tril, tril.T])


def kernel(x, c, ctx, c_ctx, w_mod, b_mod, norm_mix, norm_ffn, w_in, b_in, q_norm, k_norm,
           hg_lb, hg_norm, ml_conv, w_out, w_ff1, w_ff2):
    bsz, n_lat, d = x.shape
    n_ctx = ctx.shape[1]
    depth = w_mod.shape[0]
    t = n_ctx + n_lat
    nct = n_ctx // TM
    assert n_ctx % SCAN_BLK == 0 and n_lat % SCAN_BLK == 0 and n_lat % GRID_W == 0 and bsz + 1 <= 8

    cvec = jnp.concatenate([c, c_ctx[None, :], jnp.zeros((8 - bsz - 1, d), F32)], axis=0)
    mods = _mod_call(cvec, w_mod, b_mod).reshape(depth, 8, N_MOD, d)
    modsel = jnp.stack([jnp.broadcast_to(mods[:, bsz][:, None], (depth, bsz, N_MOD, d)), mods[:, :bsz]], axis=2)

    cos, sin = _rope_tables(n_ctx, n_lat)
    e_attn = _block_ones(ATTN_W, HEAD_DIM, 1.0 / HEAD_DIM, BF16)
    e_hg_mean = _block_ones(HG_W, HG_DK, 1.0 / HG_DK, BF16)
    e_hg = _block_ones(HG_W, HG_DK, 1.0, BF16)
    bd_hg = _block_ones(HG_W, HG_DK, 1.0, F32)
    tri16 = _scan_masks(HG_CHUNK)
    tri_hg = jnp.asarray(tri16, BF16)
    pair_mask = np.transpose(tri16, (0, 2, 1)).reshape(2, HG_CHUNK * HG_CHUNK, 1)
    pair_mask = jnp.asarray(np.broadcast_to(pair_mask, (2, HG_CHUNK * HG_CHUNK, HG_W)).copy(), F32)
    scan_mask = _scan_masks(SCAN_BLK)
    mask_ml = jnp.asarray(scan_mask, BF16)
    mask_ml_t = jnp.asarray(np.transpose(scan_mask, (0, 2, 1)).copy(), BF16)

    lbs = jnp.cumsum(jax.nn.softmax(hg_lb.astype(F32), axis=0), axis=0)
    lbs = (lbs - lbs[:1]).reshape(depth, 2, 1, HG_W)

    w_in_p = jnp.pad(w_in, ((0, 0), (0, 0), (0, N_IN_PAD - N_IN))).astype(BF16)
    b_in_p = jnp.pad(b_in, ((0, 0), (0, N_IN_PAD - N_IN))).reshape(depth, 1, N_IN_PAD)
    w_out_b = w_out.astype(BF16)
    w1_b = w_ff1.astype(BF16)
    w2_b = w_ff2.astype(BF16)
    gq = jnp.tile(q_norm, (1, ATTN_HEADS)).reshape(depth, 1, ATTN_W)
    gk = jnp.tile(k_norm, (1, ATTN_KV_HEADS)).reshape(depth, 1, KV_W)
    gn = jnp.tile(hg_norm, (1, HG_HEADS)).reshape(depth, 1, HG_W)

    h = jnp.concatenate([ctx, x], axis=1)
    for l in range(depth):
        last = l == depth - 1
        aq, ak, av, hgp, mqk, mvo, mg = _in_call(
            h, modsel[l], norm_mix[l].reshape(1, d), w_in_p[l], b_in_p[l], cos, sin, gq[l], gk[l], e_attn, nct)

        shift = (HEAD_DIM * Q_SCALE) * jnp.max(jnp.abs(q_norm[l])) * jnp.max(jnp.abs(k_norm[l]))
        at = _attention(aq, ak, av, shift, n_ctx)

        hg_o = _hgrn_call(hgp, lbs[l], tri_hg, pair_mask, e_hg, bd_hg, nct)

        qk = _conv_call(mqk, ml_conv[l], nct)
        g16 = mg[:, :, :16].reshape(bsz, t, 2, 2, ML_HEADS)
        gdir = g16.transpose(0, 3, 1, 2, 4).reshape(bsz, 2, t, 8)
        gcol = jnp.pad(gdir, ((0, 0), (0, 0), (0, 0), (0, 120)))
        grow = gdir.transpose(0, 1, 3, 2)
        ml_o = _mlstm_call(qk, mvo, gcol, grow, mask_ml, mask_ml_t, nct)

        skip = nct if last else 0
        h1 = _out_call(h, modsel[l], at, hg_o, hgp, gn[l], e_hg_mean, ml_o, mvo,
                       w_out_b[l, :ATTN_W], w_out_b[l, ATTN_W:ATTN_W + HG_W], w_out_b[l, ATTN_W + HG_W:], nct, skip)
        h = _ffn_call(h1, modsel[l], norm_ffn[l].reshape(1, d), w1_b[l], w2_b[l], 0 if last else nct)
    return h
```

```python
import functools

import jax
import jax.numpy as jnp
import numpy as np
from jax import lax
from jax.experimental import pallas as pl
from jax.experimental.pallas import tpu as pltpu

F32 = jnp.float32
BF16 = jnp.bfloat16

GRID_W = 64
HEAD_DIM = 64
ATTN_HEADS = 8
ATTN_KV_HEADS = 2
ATTN_REP = ATTN_HEADS // ATTN_KV_HEADS
ROPE_THETA = 10000.0
ROPE_AXIS_DIM = HEAD_DIM // 2
HG_HEADS = 4
HG_DK = 64
ML_HEADS = 4
ML_DK = 64
N_MOD = 6
EPS = 1e-6
ATTN_W = ATTN_HEADS * HEAD_DIM
KV_W = ATTN_KV_HEADS * HEAD_DIM
HG_W = HG_HEADS * HG_DK
ML_W = ML_HEADS * ML_DK

OFF_AQ, OFF_AK, OFF_AV = 0, 512, 640
OFF_HG = 768
OFF_MQK = 2048
OFF_MVO = 2560
OFF_MG = 3072
N_IN = 3088
N_IN_PAD = 3200

TM = 256
TQ = 256
TK = 256
ATTN_UNROLL = 8
SCAN_BLK = 256
HG_CHUNK = 16
V_ROWS = 80
QK_ROWS = 80
MAX_FIXED_SHIFT = 40.0
NEG_BIG = -1e30
Q_SCALE = float(np.log2(np.e)) * HEAD_DIM ** -0.5
VMEM_LIMIT = 56 * 1024 * 1024


def _cparams(sem):
    return pltpu.CompilerParams(dimension_semantics=sem, vmem_limit_bytes=VMEM_LIMIT)


def _split3(x):
    hi = x.astype(BF16)
    r1 = x - hi.astype(F32)
    mid = r1.astype(BF16)
    lo = (r1 - mid.astype(F32)).astype(BF16)
    return hi, mid, lo


def _dot(a, b):
    return jnp.dot(a, b, preferred_element_type=F32)


def _dot_nt(a, b):
    return lax.dot_general(a, b, (((1,), (1,)), ((), ())), preferred_element_type=F32)


def _dot_tn(a, b):
    return lax.dot_general(a, b, (((0,), (0,)), ((), ())), preferred_element_type=F32)


def _mask_dot(mask_bf16, x):
    hi, mid, lo = _split3(x)
    return _dot(mask_bf16, hi) + _dot(mask_bf16, mid) + _dot(mask_bf16, lo)


def _dot_mask(x, mask_bf16):
    hi, mid, lo = _split3(x)
    return _dot(hi, mask_bf16) + _dot(mid, mask_bf16) + _dot(lo, mask_bf16)


def _sigmoid(x):
    return 1.0 / (1.0 + jnp.exp(-x))


def _log_sigmoid(x):
    return jnp.minimum(x, 0.0) - jnp.log(1.0 + jnp.exp(-jnp.abs(x)))


def _mod_kernel(c_ref, w_ref, b_ref, o_ref):
    cv = c_ref[...]
    a = cv * _sigmoid(cv)
    o_ref[0] = jnp.dot(a, w_ref[0], preferred_element_type=F32, precision=lax.Precision.HIGHEST) + b_ref[0]


def _mod_call(cvec, w_mod, b_mod):
    depth, d, nm = w_mod.shape
    tn = 1024
    return pl.pallas_call(
        _mod_kernel,
        grid=(depth, nm // tn),
        in_specs=[pl.BlockSpec((8, d), lambda l, j: (0, 0)),
                  pl.BlockSpec((1, d, tn), lambda l, j: (l, 0, j)),
                  pl.BlockSpec((1, 1, tn), lambda l, j: (l, 0, j))],
        out_specs=pl.BlockSpec((1, 8, tn), lambda l, j: (l, 0, j)),
        out_shape=jax.ShapeDtypeStruct((depth, 8, nm), F32),
        compiler_params=_cparams(("arbitrary", "arbitrary")),
        name="mod_vectors",
    )(cvec, w_mod, b_mod.reshape(depth, 1, nm))


def _head_rms(t, gain, e):
    ss = _dot((t * t).astype(BF16), e)
    return t * lax.rsqrt(ss + EPS) * gain


def _rope(t, cos, sin):
    w = t.shape[1]
    reps = w // 128
    cosw = jnp.concatenate([cos] * reps, axis=1) if reps > 1 else cos
    sinw = jnp.concatenate([sin] * reps, axis=1) if reps > 1 else sin
    lane = lax.broadcasted_iota(jnp.int32, t.shape, 1)
    first_half = (lane % 32) < 16
    partner = jnp.where(first_half, pltpu.roll(t, w - 16, 1), pltpu.roll(t, 16, 1))
    return t * cosw + partner * sinw


def _in_kernel(h_ref, mod_ref, g_ref, w_ref, b_ref, cos_ref, sin_ref, gq_ref, gk_ref, e_ref,
               aq_ref, ak_ref, av_ref, hg_ref, mqk_ref, mvo_ref, mg_ref):
    x = h_ref[0]
    ms = jnp.mean(x * x, axis=-1, keepdims=True)
    y = x * lax.rsqrt(ms + EPS) * g_ref[...]
    mod = mod_ref[0, 0]
    u = y * (1.0 + mod[1:2]) + mod[0:1]
    p = _dot(u.astype(BF16), w_ref[...]) + b_ref[...]
    cos = cos_ref[...]
    sin = sin_ref[...]
    e = e_ref[...]
    q = _rope(_head_rms(p[:, OFF_AQ:OFF_AQ + ATTN_W], gq_ref[...], e), cos, sin)
    k = _rope(_head_rms(p[:, OFF_AK:OFF_AK + KV_W], gk_ref[...], e[:KV_W, :KV_W]), cos, sin)
    aq_ref[0] = (q * Q_SCALE).astype(BF16)
    ak_ref[0] = k.astype(BF16)
    av_ref[0] = p[:, OFF_AV:OFF_AV + KV_W].astype(BF16)
    hg_ref[0] = p[:, OFF_HG:OFF_MQK]
    mqk_ref[0] = p[:, OFF_MQK:OFF_MVO]
    mvo_ref[0] = p[:, OFF_MVO:OFF_MG]
    mg_ref[0] = p[:, OFF_MG:OFF_MG + 128]


def _in_call(h, modsel, g, w, b, cos, sin, gq, gk, e, nct):
    bsz, t, d = h.shape
    nt = t // TM
    row = lambda width: pl.BlockSpec((1, TM, width), lambda bi, i: (bi, i, 0))
    full = lambda a: pl.BlockSpec(a.shape, lambda bi, i: (0,) * a.ndim)
    out_widths = (ATTN_W, KV_W, KV_W, OFF_MQK - OFF_HG, 2 * ML_W, 2 * ML_W, 128)
    out_dtypes = (BF16, BF16, BF16, F32, F32, F32, F32)
    return pl.pallas_call(
        _in_kernel,
        grid=(bsz, nt),
        in_specs=[row(d),
                  pl.BlockSpec((1, 1, N_MOD, d), lambda bi, i: (bi, (i >= nct).astype(jnp.int32), 0, 0)),
                  full(g), full(w), full(b),
                  pl.BlockSpec((TM, 128), lambda bi, i: (i, 0)),
                  pl.BlockSpec((TM, 128), lambda bi, i: (i, 0)),
                  full(gq), full(gk), full(e)],
        out_specs=[row(wd) for wd in out_widths],
        out_shape=[jax.ShapeDtypeStruct((bsz, t, wd), dt) for wd, dt in zip(out_widths, out_dtypes)],
        compiler_params=_cparams(("parallel", "parallel")),
        name="norm_mod_in_proj",
    )(h, modsel, g, w, b, cos, sin, gq, gk, e)


def _attn_kernel(qa_ref, ka_ref, vt_ref, o_ref, acc_ref, s_ref, *maybe_m_ref, online, n_ctx_q, n_ctx_k, n_k):
    qi = pl.program_id(2)
    acc_ref[...] = jnp.zeros(acc_ref.shape, F32)
    if online:
        m_ref, = maybe_m_ref
        m_ref[...] = jnp.full(m_ref.shape, NEG_BIG, F32)
    n_blocks = jnp.where(qi < n_ctx_q, n_ctx_k, n_k)

    def key_tile(kb):
        return ka_ref[0, 0, pl.ds(pl.multiple_of(kb * TK, TK), TK), :]

    def produce(kt, slot, hh):
        s = _dot(kt, qa_ref[0, hh])
        s_ref[slot, hh] = s if online else jnp.exp2(s).astype(s_ref.dtype)

    def step(kb, cur, nxt):
        vt = vt_ref[0, 0, :, pl.ds(pl.multiple_of(kb * TK, TK), TK)]
        kt = key_tile(jnp.minimum(kb + 1, n_blocks - 1))
        for hh in range(ATTN_REP):
            produce(kt, nxt, hh)
            if online:
                s = s_ref[cur, hh]
                m_old = m_ref[hh]
                m_new = jnp.maximum(m_old, jnp.max(s, axis=0, keepdims=True))
                p = jnp.exp2(s - m_new).astype(BF16)
                acc_ref[hh] = jnp.exp2(m_old - m_new) * acc_ref[hh] + _dot(vt, p)
                m_ref[hh] = m_new
            else:
                acc_ref[hh] += _dot(vt, s_ref[cur, hh])

    kt0 = key_tile(0)
    for hh in range(ATTN_REP):
        produce(kt0, 0, hh)

    def group(i, carry):
        for u in range(ATTN_UNROLL):
            step(ATTN_UNROLL * i + u, u % 2, (u + 1) % 2)
        return carry

    def pair(i, carry):
        step(2 * i, 0, 1)
        step(2 * i + 1, 1, 0)
        return carry

    n_groups = n_blocks // ATTN_UNROLL
    lax.fori_loop(0, n_groups, group, 0)
    lax.fori_loop(n_groups * (ATTN_UNROLL // 2), n_blocks // 2, pair, 0)

    @pl.when(n_blocks % 2 == 1)
    def _():
        step(n_blocks - 1, 0, 1)

    for hh in range(ATTN_REP):
        acc = acc_ref[hh]
        o = acc[:HEAD_DIM] / acc[HEAD_DIM:HEAD_DIM + 1]
        o_ref[0, hh * HEAD_DIM:(hh + 1) * HEAD_DIM, :] = o.astype(o_ref.dtype)


def _attn_call(qa, ka, vt, n_ctx, online):
    bsz, _, _, t = qa.shape
    kern = functools.partial(_attn_kernel, online=online, n_ctx_q=n_ctx // TQ, n_ctx_k=n_ctx // TK, n_k=t // TK)
    gw = ATTN_REP * HEAD_DIM
    scratch = [pltpu.VMEM((ATTN_REP, V_ROWS, TQ), F32), pltpu.VMEM((2, ATTN_REP, TK, TQ), F32 if online else BF16)]
    if online:
        scratch.append(pltpu.VMEM((ATTN_REP, 1, TQ), F32))
    return pl.pallas_call(
        kern,
        grid=(bsz, ATTN_KV_HEADS, t // TQ),
        in_specs=[pl.BlockSpec((1, ATTN_REP, QK_ROWS, TQ), lambda b, g, i: (b, g, 0, i)),
                  pl.BlockSpec((1, 1, t, QK_ROWS), lambda b, g, i: (b, g, 0, 0)),
                  pl.BlockSpec((1, 1, V_ROWS, t), lambda b, g, i: (b, g, 0, 0))],
        out_specs=pl.BlockSpec((1, gw, TQ), lambda b, g, i: (b, g, i)),
        out_shape=jax.ShapeDtypeStruct((bsz, ATTN_W, t), BF16),
        scratch_shapes=scratch,
        compiler_params=_cparams(("parallel", "parallel", "arbitrary")),
        name="attention_online" if online else "attention",
    )(qa, ka, vt)


def _attention(aq, ak, av, shift, n_ctx):
    bsz, t, _ = aq.shape
    qt = aq.transpose(0, 2, 1).reshape(bsz, ATTN_HEADS, HEAD_DIM, t)
    qa = jnp.concatenate([qt, jnp.broadcast_to((-shift).astype(BF16), (bsz, ATTN_HEADS, 1, t)),
                          jnp.zeros((bsz, ATTN_HEADS, QK_ROWS - HEAD_DIM - 1, t), BF16)], axis=2)
    k4 =ak.reshape(bsz, t, ATTN_KV_HEADS, HEAD_DIM).transpose(0, 2, 1, 3)
    ka = jnp.concatenate([k4, jnp.ones((bsz, ATTN_KV_HEADS, t, 1), BF16),
                          jnp.zeros((bsz, ATTN_KV_HEADS, t, QK_ROWS - HEAD_DIM - 1), BF16)], axis=3)
    vt = av.reshape(bsz, t, ATTN_KV_HEADS, HEAD_DIM).transpose(0, 2, 3, 1)
    vt = jnp.concatenate([vt, jnp.ones((bsz, ATTN_KV_HEADS, 1, t), BF16),
                          jnp.zeros((bsz, ATTN_KV_HEADS, V_ROWS - HEAD_DIM - 1, t), BF16)], axis=2)
    return lax.cond(shift <= MAX_FIXED_SHIFT,
                    lambda: _attn_call(qa, ka, vt, n_ctx, False),
                    lambda: _attn_call(qa, ka, vt, n_ctx, True))


def _conv_kernel(x_ref, prev_ref, next_ref, w_ref, o_ref, *, nct, nb):
    i = pl.program_id(1)
    x = x_ref[0]
    rows = x.shape[0]
    left_edge = jnp.logical_or(i == 0, i == nct)
    right_edge = jnp.logical_or(i == nct - 1, i == nb - 1)
    prev_row = jnp.where(left_edge, 0.0, prev_ref[0, 7:8, :])
    next_row = jnp.where(right_edge, 0.0, next_ref[0, 0:1, :])
    ridx = lax.broadcasted_iota(jnp.int32, x.shape, 0)
    xm1 = jnp.where(ridx == 0, prev_row, pltpu.roll(x, 1, 0))
    xp1 = jnp.where(ridx == rows - 1, next_row, pltpu.roll(x, rows - 1, 0))
    w = w_ref[...]
    y = xm1 * w[0:1] + x * w[1:2] + xp1 * w[2:3]
    y = y * _sigmoid(y)
    lane = lax.broadcasted_iota(jnp.int32, x.shape, 1)
    y = jnp.where(lane >= ML_W, y * (ML_DK ** -0.5), y)
    o_ref[0] = y.astype(o_ref.dtype)


def _conv_call(mqk, w, nct):
    bsz, t, width = mqk.shape
    nb = t // SCAN_BLK
    per8 = SCAN_BLK // 8
    last8 = t // 8 - 1
    kern = functools.partial(_conv_kernel, nct=nct, nb=nb)
    return pl.pallas_call(
        kern,
        grid=(bsz, nb),
        in_specs=[pl.BlockSpec((1, SCAN_BLK, width), lambda b, i: (b, i, 0)),
                  pl.BlockSpec((1, 8, width), lambda b, i: (b, jnp.maximum(i * per8 - 1, 0), 0)),
                  pl.BlockSpec((1, 8, width), lambda b, i: (b, jnp.minimum((i + 1) * per8, last8), 0)),
                  pl.BlockSpec(w.shape, lambda b, i: (0, 0))],
        out_specs=pl.BlockSpec((1, SCAN_BLK, width), lambda b, i: (b, i, 0)),
        out_shape=jax.ShapeDtypeStruct((bsz, t, width), BF16),
        compiler_params=_cparams(("parallel", "parallel")),
        name="mlstm_qk_conv",
    )(mqk, mqk, mqk, w)


def _scan_block(d, j, nct, nb):
    bwd = jnp.where(j < nct, nct - 1 - j, nb - 1 - (j - nct))
    return jnp.where(d == 0, j, bwd)


def _hgrn_kernel(q0_ref, q1_ref, v0_ref, v1_ref, z0_ref, z1_ref, lb_ref, e_ref, bd_ref, o_ref,
                 st_ref, qd_ref, kd_ref, od_ref, dt_ref, u_ref):
    d = pl.program_id(1)
    j = pl.program_id(2)

    @pl.when(j == 0)
    def _():
        st_ref[...] = jnp.zeros(st_ref.shape, F32)

    lb = lb_ref[0]
    e = e_ref[...]
    bd = bd_ref[...]
    c = HG_CHUNK
    nch = SCAN_BLK // c

    def rows(i):
        return pl.ds(jnp.where(d == 0, i, c - 1 - i), nch, stride=c)

    def load2(r0, r1, sl):
        return jnp.concatenate([r0[0, sl, :], r1[0, sl, :]], axis=1)

    def load_halves(ref, sl):
        return jnp.concatenate([ref[0, sl, :], ref[1, sl, :]], axis=1)

    def store_halves(ref, sl, x):
        ref[0, sl, :] = x[:, :128]
        ref[1, sl, :] = x[:, 128:]

    qs, ks, vs, bs = [], [], [], []
    b = None
    for i in range(c):
        f = lb + (1.0 - lb) * _sigmoid(load2(z0_ref, z1_ref, rows(i)))
        lf = jnp.log2(f)
        b = lf if b is None else b + lf
        qs.append(load2(q0_ref, q1_ref, rows(i)))
        vs.append(load2(v0_ref, v1_ref, rows(i)))
        ks.append(1.0 - f)
        bs.append(b)
    b_tot = bs[-1]
    dt_ref[...] = jnp.exp2(b_tot)
    for i in range(c):
        store_halves(qd_ref, rows(i), qs[i] * jnp.exp2(bs[i]))
        store_halves(kd_ref, rows(i), ks[i] * jnp.exp2(b_tot - bs[i]))
        ps = [qs[i] * ks[jj] * jnp.exp2(bs[i] - bs[jj]) for jj in range(i)] + [qs[i] * ks[i]]
        a = _dot(jnp.concatenate(ps, axis=0).astype(BF16), e)
        o = a[0:nch] * vs[0]
        for jj in range(1, i + 1):
            o = o + a[jj * nch:(jj + 1) * nch] * vs[jj]
        store_halves(od_ref, rows(i), o)

    for cc in range(nch):
        sl = pl.ds(cc * c, c)
        vc = load2(v0_ref, v1_ref, sl).astype(BF16)
        kc = load_halves(kd_ref, sl).astype(BF16)
        u_ref[cc] = _dot_tn(vc, kc) * bd
    for n in range(nch):
        cc = jnp.where(d == 0, n, nch - 1 - n)
        sl = pl.ds(pl.multiple_of(cc * c, c), c)
        st = st_ref[...]
        o_inter = _dot_nt(load_halves(qd_ref, sl).astype(BF16), st.astype(BF16))
        o_ref[0, 0, sl, :] = load_halves(od_ref, sl) + o_inter
        st_ref[...] = st * dt_ref[pl.ds(cc, 1), :] + u_ref[cc]


def _hgrn_call(hgp, lbs, e, bd, nct):
    bsz, t, _ = hgp.shape
    nb = t // SCAN_BLK
    w = HG_W
    blk = lambda d, j: _scan_block(d, j, nct, nb)
    half = lambda col: pl.BlockSpec((1, SCAN_BLK, 128), lambda b, d, j: (b, blk(d, j), col(d)))
    return pl.pallas_call(
        _hgrn_kernel,
        grid=(bsz, 2, nb),
        in_specs=[half(lambda d: 0), half(lambda d: 1),
                  half(lambda d: 2), half(lambda d: 3),
                  half(lambda d: 4 + 2 * d), half(lambda d: 5 + 2 * d),
                  pl.BlockSpec((1, 1, w), lambda b, d, j: (d, 0, 0)),
                  pl.BlockSpec((w, w), lambda b, d, j: (0, 0)),
                  pl.BlockSpec((w, w), lambda b, d, j: (0, 0))],
        out_specs=pl.BlockSpec((1, 1, SCAN_BLK, w), lambda b, d, j: (d, b, blk(d, j), 0)),
        out_shape=jax.ShapeDtypeStruct((2, bsz, t, w), F32),
        scratch_shapes=[pltpu.VMEM((w, w), F32),
                        pltpu.VMEM((2, SCAN_BLK, 128), F32),
                        pltpu.VMEM((2, SCAN_BLK, 128), F32),
                        pltpu.VMEM((2, SCAN_BLK, 128), F32),
                        pltpu.VMEM((SCAN_BLK // HG_CHUNK, w), F32),
                        pltpu.VMEM((SCAN_BLK // HG_CHUNK, w, w), F32)],
        compiler_params=_cparams(("parallel", "parallel", "arbitrary")),
        name="hgrn2_scan",
    )(hgp, hgp, hgp, hgp, hgp, hgp, lbs, e, bd)


def _mlstm_kernel(q_ref, k_ref, v_ref, g_ref, gt_ref, m_ref, mt_ref, o_ref, c_ref, ms_ref):
    j = pl.program_id(2)

    @pl.when(j == 0)
    def _():
        c_ref[...] = jnp.zeros(c_ref.shape, F32)
        ms_ref[...] = jnp.zeros(ms_ref.shape, F32)

    length = SCAN_BLK
    mask = m_ref[0]
    mask_t = mt_ref[0]
    valid = mask > 0
    gates = g_ref[0, 0]
    gates_t = gt_ref[0, 0]
    b_col_all = _mask_dot(mask, _log_sigmoid(gates))
    lf_t = _log_sigmoid(gates_t)
    b_row_all = _dot_mask(lf_t, mask_t)
    tot_all = jnp.sum(lf_t, axis=1, keepdims=True)
    lane = lax.broadcasted_iota(jnp.int32, (length, 128), 1)
    for pair in range(ML_HEADS // 2):
        v2 = v_ref[0, :, pair * 128:(pair + 1) * 128]
        out_pair = jnp.zeros((length, 128), F32)
        for sub in range(2):
            hh = pair * 2 + sub
            own = (lane < 64) if sub == 0 else (lane >= 64)
            ones_lane = 64 if sub == 0 else 0
            qh = q_ref[0, :, hh * ML_DK:(hh + 1) * ML_DK]
            kh = k_ref[0, :, hh * ML_DK:(hh + 1) * ML_DK]
            v_aug = jnp.where(own, v2, jnp.where(lane == ones_lane, 1.0, 0.0)).astype(BF16)
            m_prev = ms_ref[hh, 0:1, 0:1]
            bc = b_col_all[:, 4 + hh:5 + hh]
            br = b_row_all[4 + hh:5 + hh, :]
            ig_row = gates_t[hh:hh + 1, :]
            ig_col = gates[:, hh:hh + 1]
            tot = tot_all[4 + hh:5 + hh, :]
            log_d = jnp.where(valid, bc - br + ig_row, NEG_BIG)
            m_t = jnp.maximum(bc + m_prev, jnp.max(log_d, axis=1, keepdims=True))
            dmat = jnp.exp(log_d - m_t)
            g = jnp.exp(bc + m_prev - m_t)
            s = _dot_nt(qh, kh) * dmat
            c_aug = c_ref[hh]
            r = g * _dot(qh, c_aug.astype(BF16)) + _dot(s.astype(BF16), v_aug)
            den = jnp.sum(jnp.where(lane == ones_lane, r, 0.0), axis=1, keepdims=True)
            hval = r / jnp.maximum(jnp.abs(den), jnp.exp(-m_t))
            out_pair = jnp.where(own, hval, out_pair)
            log_e = tot - bc + ig_col
            m_new = jnp.maximum(tot + m_prev, jnp.max(log_e, axis=0, keepdims=True))
            ke = (kh.astype(F32) * jnp.exp(log_e - m_new)).astype(BF16)
            g_c = jnp.exp(tot + m_prev - m_new)
            c_ref[hh] = g_c * c_aug + _dot_tn(ke, v_aug)
            ms_ref[hh] = jnp.broadcast_to(m_new, ms_ref.shape[1:])
        o_ref[0, 0, :, pair * 128:(pair + 1) * 128] = out_pair


def _mlstm_call(qk, mvo, gcol, grow, mask, mask_t, nct):
    bsz, t, _ = qk.shape
    nb = t // SCAN_BLK
    w = ML_W
    blk = lambda d, j: _scan_block(d, j, nct, nb)
    return pl.pallas_call(
        _mlstm_kernel,
        grid=(bsz, 2, nb),
        in_specs=[pl.BlockSpec((1, SCAN_BLK, w), lambda b, d, j: (b, blk(d, j), 0)),
                  pl.BlockSpec((1, SCAN_BLK, w), lambda b, d, j: (b, blk(d, j), 1)),
                  pl.BlockSpec((1, SCAN_BLK, w), lambda b, d, j: (b, blk(d, j), 0)),
                  pl.BlockSpec((1, 1, SCAN_BLK, 128), lambda b, d, j: (b, d, blk(d, j), 0)),
                  pl.BlockSpec((1, 1, 8, SCAN_BLK), lambda b, d, j: (b, d, 0, blk(d, j))),
                  pl.BlockSpec((1, SCAN_BLK, SCAN_BLK), lambda b, d, j: (d, 0, 0)),
                  pl.BlockSpec((1, SCAN_BLK, SCAN_BLK), lambda b, d, j: (d, 0, 0))],
        out_specs=pl.BlockSpec((1, 1, SCAN_BLK, w), lambda b, d, j: (d, b, blk(d, j), 0)),
        out_shape=jax.ShapeDtypeStruct((2, bsz, t, w), F32),
        scratch_shapes=[pltpu.VMEM((ML_HEADS, ML_DK, 128), F32), pltpu.VMEM((ML_HEADS, 8, 128), F32)],
        compiler_params=_cparams(("parallel", "parallel", "arbitrary")),
        name="mlstm_scan",
    )(qk, qk, mvo, gcol, grow, mask, mask_t)


def _out_kernel(h_ref, mod_ref, at_ref, hf_ref, hb_ref, hgate_ref, gn_ref, e_ref, mf_ref, mb_ref, mgate_ref,
                wa_ref, wr_ref, wm_ref, o_ref):
    mod = mod_ref[0, 0]
    o = hf_ref[0, 0] + hb_ref[0, 0]
    ss = _dot((o * o).astype(BF16), e_ref[...])
    gate = hgate_ref[0]
    r = o * lax.rsqrt(ss + EPS) * gn_ref[...] * (gate * _sigmoid(gate))
    m = _sigmoid(mgate_ref[0]) * (mf_ref[0, 0] + mb_ref[0, 0])
    y = _dot_tn(at_ref[0], wa_ref[...]) + _dot(r.astype(BF16), wr_ref[...]) + _dot(m.astype(BF16), wm_ref[...])
    o_ref[0] = h_ref[0] + mod[2:3] * y


def _out_call(h, modsel, at, hg_o, hgp, gn, e, ml_o, mvo, wa, wr, wm, nct, skip):
    bsz, t, d = h.shape
    nt = t // TM - skip
    full = lambda a: pl.BlockSpec(a.shape, lambda bi, i: (0,) * a.ndim)
    hg_col = (OFF_MQK - OFF_HG) // HG_W - 1
    return pl.pallas_call(
        _out_kernel,
        grid=(bsz, nt),
        in_specs=[pl.BlockSpec((1, TM, d), lambda bi, i: (bi, i + skip, 0)),
                  pl.BlockSpec((1, 1, N_MOD, d), lambda bi, i: (bi, (i + skip >= nct).astype(jnp.int32), 0, 0)),
                  pl.BlockSpec((1, ATTN_W, TM), lambda bi, i: (bi, 0, i + skip)),
                  pl.BlockSpec((1, 1, TM, HG_W), lambda bi, i: (0, bi, i + skip, 0)),
                  pl.BlockSpec((1, 1, TM, HG_W), lambda bi, i: (1, bi, i + skip, 0)),
                  pl.BlockSpec((1, TM, HG_W), lambda bi, i: (bi, i + skip, hg_col)),
                  full(gn), full(e),
                  pl.BlockSpec((1, 1, TM, ML_W), lambda bi, i: (0, bi, i + skip, 0)),
                  pl.BlockSpec((1, 1, TM, ML_W), lambda bi, i: (1, bi, i + skip, 0)),
                  pl.BlockSpec((1, TM, ML_W), lambda bi, i: (bi, i + skip, 1)),
                  full(wa), full(wr), full(wm)],
        out_specs=pl.BlockSpec((1, TM, d), lambda bi, i: (bi, i, 0)),
        out_shape=jax.ShapeDtypeStruct((bsz, nt * TM, d), F32),
        compiler_params=_cparams(("parallel", "parallel")),
        name="readout_out_proj",
    )(h, modsel, at, hg_o, hg_o, hgp, gn, e, ml_o, ml_o, mvo, wa, wr, wm)


def _ffn_kernel(h_ref, mod_ref, g_ref, w1_ref, w2_ref, o_ref):
    x = h_ref[0]
    mod = mod_ref[0, 0]
    ms = jnp.mean(x * x, axis=-1, keepdims=True)
    y = x * lax.rsqrt(ms + EPS) * g_ref[...]
    u = (y * (1.0 + mod[4:5]) + mod[3:4]).astype(BF16)
    a = jnp.maximum(_dot(u, w1_ref[...]), 0.0)
    a = (a * a).astype(BF16)
    o_ref[0] = x + mod[5:6] * _dot(a, w2_ref[...])


def _ffn_call(h, modsel, g, w1, w2, nct):
    bsz, t, d = h.shape
    nt = t // TM
    full = lambda a: pl.BlockSpec(a.shape, lambda bi, i: (0,) * a.ndim, pipeline_mode=pl.Buffered(1))
    return pl.pallas_call(
        _ffn_kernel,
        grid=(bsz, nt),
        in_specs=[pl.BlockSpec((1, TM, d), lambda bi, i: (bi, i, 0)),
                  pl.BlockSpec((1, 1, N_MOD, d), lambda bi, i: (bi, (i >= nct).astype(jnp.int32), 0, 0)),
                  pl.BlockSpec(g.shape, lambda bi, i: (0, 0)),
                  full(w1), full(w2)],
        out_specs=pl.BlockSpec((1, TM, d), lambda bi, i: (bi, i, 0)),
        out_shape=jax.ShapeDtypeStruct((bsz, t, d), F32),
        compiler_params=_cparams(("parallel", "parallel")),
        name="ffn",
    )(h, modsel, g, w1, w2)


def _rope_tables(n_ctx, n_lat):
    inv_freq = ROPE_THETA ** (-np.arange(0, ROPE_AXIS_DIM, 2, dtype=np.float32) / ROPE_AXIS_DIM)
    inv_freq = jnp.asarray(inv_freq, F32)
    tok = jnp.arange(n_lat)
    ang_row = (tok // GRID_W).astype(F32)[:, None] * inv_freq[None, :]
    ang_col = (tok % GRID_W).astype(F32)[:, None] * inv_freq[None, :]
    ang = jnp.concatenate([ang_row, ang_row, ang_col, ang_col], axis=1)
    ang = jnp.concatenate([jnp.zeros((n_ctx, HEAD_DIM), F32), ang], axis=0)
    sign = np.where((np.arange(HEAD_DIM) % 32) < 16, -1.0, 1.0).astype(np.float32)
    cos = jnp.cos(ang)
    sin = jnp.sin(ang) * jnp.asarray(sign)[None, :]
    return jnp.concatenate([cos, cos], axis=1), jnp.concatenate([sin, sin], axis=1)


def _block_ones(width, head, scale, dtype):
    idx = np.arange(width) // head
    return jnp.asarray((idx[:, None] == idx[None, :]).astype(np.float32) * scale, dtype)


def _scan_masks(n):
    tril = np.tril(np.ones((n, n), np.float32))
    return np.stack([tril, tril.T])


def kernel(x, c, ctx, c_ctx, w_mod, b_mod, norm_mix, norm_ffn, w_in, b_in, q_norm, k_norm,
           hg_lb, hg_norm, ml_conv, w_out, w_ff1, w_ff2):
    bsz, n_lat, d = x.shape
    n_ctx = ctx.shape[1]
    depth = w_mod.shape[0]
    t = n_ctx + n_lat
    nct = n_ctx // TM
    assert n_ctx % SCAN_BLK == 0 and n_lat % SCAN_BLK == 0 and n_lat % GRID_W == 0 and bsz + 1 <= 8

    cvec = jnp.concatenate([c, c_ctx[None, :], jnp.zeros((8 - bsz - 1, d), F32)], axis=0)
    mods = _mod_call(cvec, w_mod, b_mod).reshape(depth, 8, N_MOD, d)
    modsel = jnp.stack([jnp.broadcast_to(mods[:, bsz][:, None], (depth, bsz, N_MOD, d)), mods[:, :bsz]], axis=2)

    cos, sin = _rope_tables(n_ctx, n_lat)
    e_attn = _block_ones(ATTN_W, HEAD_DIM, 1.0 / HEAD_DIM, BF16)
    e_hg_mean = _block_ones(HG_W, HG_DK, 1.0 / HG_DK, BF16)
    e_hg = _block_ones(HG_W, HG_DK, 1.0, BF16)
    bd_hg = _block_ones(HG_W, HG_DK, 1.0, F32)
    scan_mask = _scan_masks(SCAN_BLK)
    mask_ml = jnp.asarray(scan_mask, BF16)
    mask_ml_t = jnp.asarray(np.transpose(scan_mask, (0, 2, 1)).copy(), BF16)

    lbs = jnp.cumsum(jax.nn.softmax(hg_lb.astype(F32), axis=0), axis=0)
    lbs = (lbs - lbs[:1]).reshape(depth, 2, 1, HG_W)

    w_in_p = jnp.pad(w_in, ((0, 0), (0, 0), (0, N_IN_PAD - N_IN))).astype(BF16)
    b_in_p = jnp.pad(b_in, ((0, 0), (0, N_IN_PAD - N_IN))).reshape(depth, 1, N_IN_PAD)
    w_out_b = w_out.astype(BF16)
    w1_b = w_ff1.astype(BF16)
    w2_b = w_ff2.astype(BF16)
    gq = jnp.tile(q_norm, (1, ATTN_HEADS)).reshape(depth, 1, ATTN_W)
    gk = jnp.tile(k_norm, (1, ATTN_KV_HEADS)).reshape(depth, 1, KV_W)
    gn = jnp.tile(hg_norm, (1, HG_HEADS)).reshape(depth, 1, HG_W)

    h = jnp.concatenate([ctx, x], axis=1)
    for l in range(depth):
        last = l == depth - 1
        aq, ak, av, hgp, mqk, mvo, mg = _in_call(
            h, modsel[l], norm_mix[l].reshape(1, d), w_in_p[l], b_in_p[l], cos, sin, gq[l], gk[l], e_attn, nct)

        shift = (HEAD_DIM * Q_SCALE) * jnp.max(jnp.abs(q_norm[l])) * jnp.max(jnp.abs(k_norm[l]))
        at = _attention(aq, ak, av, shift, n_ctx)

        hg_o = _hgrn_call(hgp, lbs[l], e_hg, bd_hg, nct)

        qk = _conv_call(mqk, ml_conv[l], nct)
        g16 = mg[:, :, :16].reshape(bsz, t, 2, 2, ML_HEADS)
        gdir = g16.transpose(0, 3, 1, 2, 4).reshape(bsz, 2, t, 8)
        gcol = jnp.pad(gdir, ((0, 0), (0, 0), (0, 0), (0, 120)))
        grow = gdir.transpose(0, 1, 3, 2)
        ml_o = _mlstm_call(qk, mvo, gcol, grow, mask_ml, mask_ml_t, nct)

        skip = nct if last else 0
        h1 = _out_call(h, modsel[l], at, hg_o, hgp, gn[l], e_hg_mean, ml_o, mvo,
                       w_out_b[l, :ATTN_W], w_out_b[l, ATTN_W:ATTN_W + HG_W], w_out_b[l, ATTN_W + HG_W:], nct, skip)
        h = _ffn_call(h1, modsel[l], norm_ffn[l].reshape(1, d), w1_b[l], w2_b[l], 0 if last else nct)
    return h
```

```python
import functools

import jax
import jax.numpy as jnp
import numpy as np
from jax import lax
from jax.experimental import pallas as pl
from jax.experimental.pallas import tpu as pltpu

F32 = jnp.float32
BF16 = jnp.bfloat16

GRID_W = 64
HEAD_DIM = 64
ATTN_HEADS = 8
ATTN_KV_HEADS = 2
ATTN_REP = ATTN_HEADS // ATTN_KV_HEADS
ROPE_THETA = 10000.0
ROPE_AXIS_DIM = HEAD_DIM // 2
HG_HEADS = 4
HG_DK = 64
ML_HEADS = 4
ML_DK = 64
N_MOD = 6
EPS = 1e-6
ATTN_W = ATTN_HEADS * HEAD_DIM
KV_W = ATTN_KV_HEADS * HEAD_DIM
HG_W = HG_HEADS * HG_DK
ML_W = ML_HEADS * ML_DK

OFF_AQ, OFF_AK, OFF_AV = 0, 512, 640
OFF_HG = 768
OFF_MQK = 2048
OFF_MVO = 2560
OFF_MG = 3072
N_IN = 3088
N_IN_PAD = 3200

TM = 256
TQ = 256
TK = 256
ATTN_UNROLL = 8
SCAN_BLK = 256
HG_CHUNK = 16
V_ROWS = 80
QK_ROWS = 80
MAX_FIXED_SHIFT = 40.0
NEG_BIG = -1e30
Q_SCALE = float(np.log2(np.e)) * HEAD_DIM ** -0.5
VMEM_LIMIT = 56 * 1024 * 1024


def _cparams(sem):
    return pltpu.CompilerParams(dimension_semantics=sem, vmem_limit_bytes=VMEM_LIMIT)


def _split3(x):
    hi = x.astype(BF16)
    r1 = x - hi.astype(F32)
    mid = r1.astype(BF16)
    lo = (r1 - mid.astype(F32)).astype(BF16)
    return hi, mid, lo


def _dot(a, b):
    return jnp.dot(a, b, preferred_element_type=F32)


def _dot_nt(a, b):
    return lax.dot_general(a, b, (((1,), (1,)), ((), ())), preferred_element_type=F32)


def _dot_tn(a, b):
    return lax.dot_general(a, b, (((0,), (0,)), ((), ())), preferred_element_type=F32)


def _mask_dot(mask_bf16, x):
    hi, mid, lo = _split3(x)
    return _dot(mask_bf16, hi) + _dot(mask_bf16, mid) + _dot(mask_bf16, lo)


def _dot_mask(x, mask_bf16):
    hi, mid, lo = _split3(x)
    return _dot(hi, mask_bf16) + _dot(mid, mask_bf16) + _dot(lo, mask_bf16)


def _sigmoid(x):
    return 1.0 / (1.0 + jnp.exp(-x))


def _log_sigmoid(x):
    return jnp.minimum(x, 0.0) - jnp.log(1.0 + jnp.exp(-jnp.abs(x)))


def _mod_kernel(c_ref, w_ref, b_ref, o_ref):
    cv = c_ref[...]
    a = cv * _sigmoid(cv)
    o_ref[0] = jnp.dot(a, w_ref[0], preferred_element_type=F32, precision=lax.Precision.HIGHEST) + b_ref[0]


def _mod_call(cvec, w_mod, b_mod):
    depth, d, nm = w_mod.shape
    tn = 1024
    return pl.pallas_call(
        _mod_kernel,
        grid=(depth, nm // tn),
        in_specs=[pl.BlockSpec((8, d), lambda l, j: (0, 0)),
                  pl.BlockSpec((1, d, tn), lambda l, j: (l, 0, j)),
                  pl.BlockSpec((1, 1, tn), lambda l, j: (l, 0, j))],
        out_specs=pl.BlockSpec((1, 8, tn), lambda l, j: (l, 0, j)),
        out_shape=jax.ShapeDtypeStruct((depth, 8, nm), F32),
        compiler_params=_cparams(("arbitrary", "arbitrary")),
        name="mod_vectors",
    )(cvec, w_mod, b_mod.reshape(depth, 1, nm))


def _head_rms(t, gain, e):
    ss = _dot((t * t).astype(BF16), e)
    return t * lax.rsqrt(ss + EPS) * gain


def _rope(t, cos, sin):
    w = t.shape[1]
    reps = w // 128
    cosw = jnp.concatenate([cos] * reps, axis=1) if reps > 1 else cos
    sinw = jnp.concatenate([sin] * reps, axis=1) if reps > 1 else sin
    lane = lax.broadcasted_iota(jnp.int32, t.shape, 1)
    first_half = (lane % 32) < 16
    partner = jnp.where(first_half, pltpu.roll(t, w - 16, 1), pltpu.roll(t, 16, 1))
    return t * cosw + partner * sinw


def _in_kernel(h_ref, mod_ref, g_ref, w_ref, b_ref, cos_ref, sin_ref, gq_ref, gk_ref, e_ref,
               aq_ref, ak_ref, av_ref, hg_ref, mqk_ref, mvo_ref, mg_ref, mvt_ref):
    x = h_ref[0]
    ms = jnp.mean(x * x, axis=-1, keepdims=True)
    y = x * lax.rsqrt(ms + EPS) * g_ref[...]
    mod = mod_ref[0, 0]
    u = y * (1.0 + mod[1:2]) + mod[0:1]
    p = _dot(u.astype(BF16), w_ref[...]) + b_ref[...]
    cos = cos_ref[...]
    sin = sin_ref[...]
    e = e_ref[...]
    q = _rope(_head_rms(p[:, OFF_AQ:OFF_AQ + ATTN_W], gq_ref[...], e), cos, sin)
    k = _rope(_head_rms(p[:, OFF_AK:OFF_AK + KV_W], gk_ref[...], e[:KV_W, :KV_W]), cos, sin)
    aq_ref[0] = (q * Q_SCALE).astype(BF16)
    ak_ref[0] = k.astype(BF16)
    av_ref[0] = p[:, OFF_AV:OFF_AV + KV_W].astype(BF16)
    hg_ref[0] = p[:, OFF_HG:OFF_MQK]
    mqk_ref[0] = p[:, OFF_MQK:OFF_MVO]
    mvo_ref[0] = p[:, OFF_MVO:OFF_MG]
    mg_ref[0] = p[:, OFF_MG:OFF_MG + 128]
    mvt_ref[0] = p[:, OFF_MVO:OFF_MVO + ML_W].T.astype(BF16)


def _in_call(h, modsel, g, w, b, cos, sin, gq, gk, e, nct):
    bsz, t, d = h.shape
    nt = t // TM
    row = lambda width: pl.BlockSpec((1, TM, width), lambda bi, i: (bi, i, 0))
    full = lambda a: pl.BlockSpec(a.shape, lambda bi, i: (0,) * a.ndim)
    out_widths = (ATTN_W, KV_W, KV_W, OFF_MQK - OFF_HG, 2 * ML_W, 2 * ML_W, 128)
    out_dtypes = (BF16, BF16, BF16, F32, F32, F32, F32)
    return pl.pallas_call(
        _in_kernel,
        grid=(bsz, nt),
        in_specs=[row(d),
                  pl.BlockSpec((1, 1, N_MOD, d), lambda bi, i: (bi, (i >= nct).astype(jnp.int32), 0, 0)),
                  full(g), full(w), full(b),
                  pl.BlockSpec((TM, 128), lambda bi, i: (i, 0)),
                  pl.BlockSpec((TM, 128), lambda bi, i: (i, 0)),
                  full(gq), full(gk), full(e)],
        out_specs=[row(wd) for wd in out_widths] + [pl.BlockSpec((1, ML_W, TM), lambda bi, i: (bi, 0, i))],
        out_shape=[jax.ShapeDtypeStruct((bsz, t, wd), dt) for wd, dt in zip(out_widths, out_dtypes)]
        + [jax.ShapeDtypeStruct((bsz, ML_W, t), BF16)],
        compiler_params=_cparams(("parallel", "parallel")),
        name="norm_mod_in_proj",
    )(h, modsel, g, w, b, cos, sin, gq, gk, e)


def _attn_kernel(qa_ref, ka_ref, vt_ref, o_ref, acc_ref, s_ref, *maybe_m_ref, online, n_ctx_q, n_ctx_k, n_k):
    qi = pl.program_id(2)
    acc_ref[...] = jnp.zeros(acc_ref.shape, F32)
    if online:
        m_ref, = maybe_m_ref
        m_ref[...] = jnp.full(m_ref.shape, NEG_BIG, F32)
    n_blocks = jnp.where(qi < n_ctx_q, n_ctx_k, n_k)

    def key_tile(kb):
        return ka_ref[0, 0, pl.ds(pl.multiple_of(kb * TK, TK), TK), :]

    def produce(kt, slot, hh):
        s = _dot(kt, qa_ref[0, hh])
        s_ref[slot, hh] = s if online else jnp.exp2(s).astype(s_ref.dtype)

    def step(kb, cur, nxt):
        vt = vt_ref[0, 0, :, pl.ds(pl.multiple_of(kb * TK, TK), TK)]
        kt = key_tile(jnp.minimum(kb + 1, n_blocks - 1))
        for hh in range(ATTN_REP):
            produce(kt, nxt, hh)
            if online:
                s = s_ref[cur, hh]
                m_old = m_ref[hh]
                m_new = jnp.maximum(m_old, jnp.max(s, axis=0, keepdims=True))
                p = jnp.exp2(s - m_new).astype(BF16)
                acc_ref[hh] = jnp.exp2(m_old - m_new) * acc_ref[hh] + _dot(vt, p)
                m_ref[hh] = m_new
            else:
                acc_ref[hh] += _dot(vt, s_ref[cur, hh])

    kt0 = key_tile(0)
    for hh in range(ATTN_REP):
        produce(kt0, 0, hh)

    def group(i, carry):
        for u in range(ATTN_UNROLL):
            step(ATTN_UNROLL * i + u, u % 2, (u + 1) % 2)
        return carry

    def pair(i, carry):
        step(2 * i, 0, 1)
        step(2 * i + 1, 1, 0)
        return carry

    n_groups = n_blocks // ATTN_UNROLL
    lax.fori_loop(0, n_groups, group, 0)
    lax.fori_loop(n_groups * (ATTN_UNROLL // 2), n_blocks // 2, pair, 0)

    @pl.when(n_blocks % 2 == 1)
    def _():
        step(n_blocks - 1, 0, 1)

    for hh in range(ATTN_REP):
        acc = acc_ref[hh]
        o = acc[:HEAD_DIM] / acc[HEAD_DIM:HEAD_DIM + 1]
        o_ref[0, hh * HEAD_DIM:(hh + 1) * HEAD_DIM, :] = o.astype(o_ref.dtype)


def _attn_call(qa, ka, vt, n_ctx, online):
    bsz, _, _, t = qa.shape
    kern = functools.partial(_attn_kernel, online=online, n_ctx_q=n_ctx // TQ, n_ctx_k=n_ctx // TK, n_k=t // TK)
    gw = ATTN_REP * HEAD_DIM
    scratch = [pltpu.VMEM((ATTN_REP, V_ROWS, TQ), F32), pltpu.VMEM((2, ATTN_REP, TK, TQ), F32 if online else BF16)]
    if online:
        scratch.append(pltpu.VMEM((ATTN_REP, 1, TQ), F32))
    return pl.pallas_call(
        kern,
        grid=(bsz, ATTN_KV_HEADS, t // TQ),
        in_specs=[pl.BlockSpec((1, ATTN_REP, QK_ROWS, TQ), lambda b, g, i: (b, g, 0, i)),
                  pl.BlockSpec((1, 1, t, QK_ROWS), lambda b, g, i: (b, g, 0, 0)),
                  pl.BlockSpec((1, 1, V_ROWS, t), lambda b, g, i: (b, g, 0, 0))],
        out_specs=pl.BlockSpec((1, gw, TQ), lambda b, g, i: (b, g, i)),
        out_shape=jax.ShapeDtypeStruct((bsz, ATTN_W, t), BF16),
        scratch_shapes=scratch,
        compiler_params=_cparams(("parallel", "parallel", "arbitrary")),
        name="attention_online" if online else "attention",
    )(qa, ka, vt)


def _attention(aq, ak, av, shift, n_ctx):
    bsz, t, _ = aq.shape
    qt = aq.transpose(0, 2, 1).reshape(bsz, ATTN_HEADS, HEAD_DIM, t)
    qa = jnp.concatenate([qt, jnp.broadcast_to((-shift).astype(BF16), (bsz, ATTN_HEADS, 1, t)),
                          jnp.zeros((bsz, ATTN_HEADS, QK_ROWS - HEAD_DIM - 1, t), BF16)], axis=2)
    k4 =ak.reshape(bsz, t, ATTN_KV_HEADS, HEAD_DIM).transpose(0, 2, 1, 3)
    ka = jnp.concatenate([k4, jnp.ones((bsz, ATTN_KV_HEADS, t, 1), BF16),
                          jnp.zeros((bsz, ATTN_KV_HEADS, t, QK_ROWS - HEAD_DIM - 1), BF16)], axis=3)
    vt = av.reshape(bsz, t, ATTN_KV_HEADS, HEAD_DIM).transpose(0, 2, 3, 1)
    vt = jnp.concatenate([vt, jnp.ones((bsz, ATTN_KV_HEADS, 1, t), BF16),
                          jnp.zeros((bsz, ATTN_KV_HEADS, V_ROWS - HEAD_DIM - 1, t), BF16)], axis=2)
    return lax.cond(shift <= MAX_FIXED_SHIFT,
                    lambda: _attn_call(qa, ka, vt, n_ctx, False),
                    lambda: _attn_call(qa, ka, vt, n_ctx, True))


def _conv_kernel(x_ref, prev_ref, next_ref, w_ref, o_ref, *, nct, nb):
    i = pl.program_id(1)
    x = x_ref[0]
    rows = x.shape[0]
    left_edge = jnp.logical_or(i == 0, i == nct)
    right_edge = jnp.logical_or(i == nct - 1, i == nb - 1)
    prev_row = jnp.where(left_edge, 0.0, prev_ref[0, 7:8, :])
    next_row = jnp.where(right_edge, 0.0, next_ref[0, 0:1, :])
    ridx = lax.broadcasted_iota(jnp.int32, x.shape, 0)
    xm1 = jnp.where(ridx == 0, prev_row, pltpu.roll(x, 1, 0))
    xp1 = jnp.where(ridx == rows - 1, next_row, pltpu.roll(x, rows - 1, 0))
    w = w_ref[...]
    y = xm1 * w[0:1] + x * w[1:2] + xp1 * w[2:3]
    y = y * _sigmoid(y)
    lane = lax.broadcasted_iota(jnp.int32, x.shape, 1)
    y = jnp.where(lane >= ML_W, y * (ML_DK ** -0.5), y)
    o_ref[0] = y.astype(o_ref.dtype)


def _conv_call(mqk, w, nct):
    bsz, t, width = mqk.shape
    nb = t // SCAN_BLK
    per8 = SCAN_BLK // 8
    last8 = t // 8 - 1
    kern = functools.partial(_conv_kernel, nct=nct, nb=nb)
    return pl.pallas_call(
        kern,
        grid=(bsz, nb),
        in_specs=[pl.BlockSpec((1, SCAN_BLK, width), lambda b, i: (b, i, 0)),
                  pl.BlockSpec((1, 8, width), lambda b, i: (b, jnp.maximum(i * per8 - 1, 0), 0)),
                  pl.BlockSpec((1, 8, width), lambda b, i: (b, jnp.minimum((i + 1) * per8, last8), 0)),
                  pl.BlockSpec(w.shape, lambda b, i: (0, 0))],
        out_specs=pl.BlockSpec((1, SCAN_BLK, width), lambda b, i: (b, i, 0)),
        out_shape=jax.ShapeDtypeStruct((bsz, t, width), BF16),
        compiler_params=_cparams(("parallel", "parallel")),
        name="mlstm_qk_conv",
    )(mqk, mqk, mqk, w)


def _scan_block(d, j, nct, nb):
    bwd = jnp.where(j < nct, nct - 1 - j, nb - 1 - (j - nct))
    return jnp.where(d == 0, j, bwd)


def _hgrn_kernel(q0_ref, q1_ref, v0_ref, v1_ref, z0_ref, z1_ref, lb_ref, e_ref, bd_ref, o_ref,
                 st_ref, qd_ref, kd_ref, od_ref, dt_ref, u_ref):
    d = pl.program_id(1)
    j = pl.program_id(2)

    @pl.when(j == 0)
    def _():
        st_ref[...] = jnp.zeros(st_ref.shape, F32)

    lb = lb_ref[0]
    e = e_ref[...]
    bd = bd_ref[...]
    c = HG_CHUNK
    nch = SCAN_BLK // c

    def rows(i):
        return pl.ds(jnp.where(d == 0, i, c - 1 - i), nch, stride=c)

    def load2(r0, r1, sl):
        return jnp.concatenate([r0[0, sl, :], r1[0, sl, :]], axis=1)

    def load_halves(ref, sl):
        return jnp.concatenate([ref[0, sl, :], ref[1, sl, :]], axis=1)

    def store_halves(ref, sl, x):
        ref[0, sl, :] = x[:, :128]
        ref[1, sl, :] = x[:, 128:]

    qs, ks, vs, bs = [], [], [], []
    b = None
    for i in range(c):
        f = lb + (1.0 - lb) * _sigmoid(load2(z0_ref, z1_ref, rows(i)))
        lf = jnp.log2(f)
        b = lf if b is None else b + lf
        qs.append(load2(q0_ref, q1_ref, rows(i)))
        vs.append(load2(v0_ref, v1_ref, rows(i)))
        ks.append(1.0 - f)
        bs.append(b)
    b_tot = bs[-1]
    dt_ref[...] = jnp.exp2(b_tot)
    for i in range(c):
        store_halves(qd_ref, rows(i), qs[i] * jnp.exp2(bs[i]))
        store_halves(kd_ref, rows(i), ks[i] * jnp.exp2(b_tot - bs[i]))
        ps = [qs[i] * ks[jj] * jnp.exp2(bs[i] - bs[jj]) for jj in range(i)] + [qs[i] * ks[i]]
        a = _dot(jnp.concatenate(ps, axis=0).astype(BF16), e)
        o = a[0:nch] * vs[0]
        for jj in range(1, i + 1):
            o = o + a[jj * nch:(jj + 1) * nch] * vs[jj]
        store_halves(od_ref, rows(i), o)

    for cc in range(nch):
        sl = pl.ds(cc * c, c)
        vc = load2(v0_ref, v1_ref, sl).astype(BF16)
        kc = load_halves(kd_ref, sl).astype(BF16)
        u_ref[cc] = _dot_tn(vc, kc) * bd
    for n in range(nch):
        cc = jnp.where(d == 0, n, nch - 1 - n)
        sl = pl.ds(pl.multiple_of(cc * c, c), c)
        st = st_ref[...]
        o_inter = _dot_nt(load_halves(qd_ref, sl).astype(BF16), st.astype(BF16))
        o_ref[0, 0, sl, :] = load_halves(od_ref, sl) + o_inter
        st_ref[...] = st * dt_ref[pl.ds(cc, 1), :] + u_ref[cc]


def _hgrn_call(hgp, lbs, e, bd, nct):
    bsz, t, _ = hgp.shape
    nb = t // SCAN_BLK
    w = HG_W
    blk = lambda d, j: _scan_block(d, j, nct, nb)
    half = lambda col: pl.BlockSpec((1, SCAN_BLK, 128), lambda b, d, j: (b, blk(d, j), col(d)))
    return pl.pallas_call(
        _hgrn_kernel,
        grid=(bsz, 2, nb),
        in_specs=[half(lambda d: 0), half(lambda d: 1),
                  half(lambda d: 2), half(lambda d: 3),
                  half(lambda d: 4 + 2 * d), half(lambda d: 5 + 2 * d),
                  pl.BlockSpec((1, 1, w), lambda b, d, j: (d, 0, 0)),
                  pl.BlockSpec((w, w), lambda b, d, j: (0, 0)),
                  pl.BlockSpec((w, w), lambda b, d, j: (0, 0))],
        out_specs=pl.BlockSpec((1, 1, SCAN_BLK, w), lambda b, d, j: (d, b, blk(d, j), 0)),
        out_shape=jax.ShapeDtypeStruct((2, bsz, t, w), F32),
        scratch_shapes=[pltpu.VMEM((w, w), F32),
                        pltpu.VMEM((2, SCAN_BLK, 128), F32),
                        pltpu.VMEM((2, SCAN_BLK, 128), F32),
                        pltpu.VMEM((2, SCAN_BLK, 128), F32),
                        pltpu.VMEM((SCAN_BLK // HG_CHUNK, w), F32),
                        pltpu.VMEM((SCAN_BLK // HG_CHUNK, w, w), F32)],
        compiler_params=_cparams(("parallel", "parallel", "arbitrary")),
        name="hgrn2_scan",
    )(hgp, hgp, hgp, hgp, hgp, hgp, lbs, e, bd)


def _mlstm_kernel(q_ref, k_ref, vt_ref, g_ref, gt_ref, m_ref, mt_ref, o_ref, c_ref, ms_ref):
    j = pl.program_id(2)

    @pl.when(j == 0)
    def _():
        c_ref[...] = jnp.zeros(c_ref.shape, F32)
        ms_ref[...] = jnp.zeros(ms_ref.shape, F32)

    length = SCAN_BLK
    mask = m_ref[0]
    mask_t = mt_ref[0]
    valid_t = mask_t > 0
    gates = g_ref[0, 0]
    gates_t = gt_ref[0, 0]
    b_col_all = _mask_dot(mask, _log_sigmoid(gates))
    lf_t = _log_sigmoid(gates_t)
    b_row_all = _dot_mask(lf_t, mask_t)
    tot_all = jnp.sum(lf_t, axis=1, keepdims=True)
    row = lax.broadcasted_iota(jnp.int32, (128, length), 0)

    heads = range(ML_HEADS)
    kh = [k_ref[0, :, hh * ML_DK:(hh + 1) * ML_DK] for hh in heads]
    qh = [q_ref[0, :, hh * ML_DK:(hh + 1) * ML_DK] for hh in heads]
    s_raw = [_dot_nt(kh[hh], qh[hh]) for hh in heads]
    inter = [_dot_nt(c_ref[hh].astype(BF16), qh[hh]) for hh in heads]
    for pair in range(ML_HEADS // 2):
        v2t = vt_ref[0, pair * 128:(pair + 1) * 128, :]
        out_t = jnp.zeros((128, length), F32)
        for sub in range(2):
            hh = pair * 2 + sub
            own = (row < 64) if sub == 0 else (row >= 64)
            ones_row = 64 if sub == 0 else 0
            v_aug_t = jnp.where(own, v2t, jnp.where(row == ones_row, 1.0, 0.0).astype(BF16))
            m_prev = ms_ref[hh, 0:1, 0:1]
            cs = gates[:, hh:hh + 1] - b_col_all[:, 4 + hh:5 + hh]
            br = b_row_all[4 + hh:5 + hh, :]
            tot = tot_all[4 + hh:5 + hh, :]
            log_d = jnp.where(valid_t, br + cs, NEG_BIG)
            m_t = jnp.maximum(br + m_prev, jnp.max(log_d, axis=0, keepdims=True))
            s_t = (s_raw[hh] * jnp.exp(log_d - m_t)).astype(BF16)
            r_t = jnp.exp(br + m_prev - m_t) * inter[hh] + _dot(v_aug_t, s_t)
            den = r_t[ones_row:ones_row + 1, :]
            out_t = jnp.where(own, r_t / jnp.maximum(jnp.abs(den), jnp.exp(-m_t)), out_t)
            log_e = tot + cs
            m_new = jnp.maximum(tot + m_prev, jnp.max(log_e, axis=0, keepdims=True))
            ke = (kh[hh].astype(F32) * jnp.exp(log_e - m_new)).astype(BF16)
            c_ref[hh] = jnp.exp(tot + m_prev - m_new) * c_ref[hh] + _dot(v_aug_t, ke)
            ms_ref[hh] = jnp.broadcast_to(m_new, ms_ref.shape[1:])
        o_ref[0, 0, :, pair * 128:(pair + 1) * 128] = out_t.T


def _mlstm_call(qk, vt, gcol, grow, mask, mask_t, nct):
    bsz, t, _ = qk.shape
    nb = t // SCAN_BLK
    w = ML_W
    blk = lambda d, j: _scan_block(d, j, nct, nb)
    return pl.pallas_call(
        _mlstm_kernel,
        grid=(bsz, 2, nb),
        in_specs=[pl.BlockSpec((1, SCAN_BLK, w), lambda b, d, j: (b, blk(d, j), 0)),
                  pl.BlockSpec((1, SCAN_BLK, w), lambda b, d, j: (b, blk(d, j), 1)),
                  pl.BlockSpec((1, w, SCAN_BLK), lambda b, d, j: (b, 0, blk(d, j))),
                  pl.BlockSpec((1, 1, SCAN_BLK, 128), lambda b, d, j: (b, d, blk(d, j), 0)),
                  pl.BlockSpec((1, 1, 8, SCAN_BLK), lambda b, d, j: (b, d, 0, blk(d, j))),
                  pl.BlockSpec((1, SCAN_BLK, SCAN_BLK), lambda b, d, j: (d, 0, 0)),
                  pl.BlockSpec((1, SCAN_BLK, SCAN_BLK), lambda b, d, j: (d, 0, 0))],
        out_specs=pl.BlockSpec((1, 1, SCAN_BLK, w), lambda b, d, j: (d, b, blk(d, j), 0)),
        out_shape=jax.ShapeDtypeStruct((2, bsz, t, w), F32),
        scratch_shapes=[pltpu.VMEM((ML_HEADS, 128, ML_DK), F32),
                        pltpu.VMEM((ML_HEADS, 8, 128), F32)],
        compiler_params=_cparams(("parallel", "parallel", "arbitrary")),
        name="mlstm_scan",
    )(qk, qk, vt, gcol, grow, mask, mask_t)


def _out_kernel(h_ref, mod_ref, at_ref, hf_ref, hb_ref, hgate_ref, gn_ref, e_ref, mf_ref, mb_ref, mgate_ref,
                wa_ref, wr_ref, wm_ref, o_ref):
    mod = mod_ref[0, 0]
    o = hf_ref[0, 0] + hb_ref[0, 0]
    ss = _dot((o * o).astype(BF16), e_ref[...])
    gate = hgate_ref[0]
    r = o * lax.rsqrt(ss + EPS) * gn_ref[...] * (gate * _sigmoid(gate))
    m = _sigmoid(mgate_ref[0]) * (mf_ref[0, 0] + mb_ref[0, 0])
    y = _dot_tn(at_ref[0], wa_ref[...]) + _dot(r.astype(BF16), wr_ref[...]) + _dot(m.astype(BF16), wm_ref[...])
    o_ref[0] = h_ref[0] + mod[2:3] * y


def _out_call(h, modsel, at, hg_o, hgp, gn, e, ml_o, mvo, wa, wr, wm, nct, skip):
    bsz, t, d = h.shape
    nt = t // TM - skip
    full = lambda a: pl.BlockSpec(a.shape, lambda bi, i: (0,) * a.ndim)
    hg_col = (OFF_MQK - OFF_HG) // HG_W - 1
    return pl.pallas_call(
        _out_kernel,
        grid=(bsz, nt),
        in_specs=[pl.BlockSpec((1, TM, d), lambda bi, i: (bi, i + skip, 0)),
                  pl.BlockSpec((1, 1, N_MOD, d), lambda bi, i: (bi, (i + skip >= nct).astype(jnp.int32), 0, 0)),
                  pl.BlockSpec((1, ATTN_W, TM), lambda bi, i: (bi, 0, i + skip)),
                  pl.BlockSpec((1, 1, TM, HG_W), lambda bi, i: (0, bi, i + skip, 0)),
                  pl.BlockSpec((1, 1, TM, HG_W), lambda bi, i: (1, bi, i + skip, 0)),
                  pl.BlockSpec((1, TM, HG_W), lambda bi, i: (bi, i + skip, hg_col)),
                  full(gn), full(e),
                  pl.BlockSpec((1, 1, TM, ML_W), lambda bi, i: (0, bi, i + skip, 0)),
                  pl.BlockSpec((1, 1, TM, ML_W), lambda bi, i: (1, bi, i + skip, 0)),
                  pl.BlockSpec((1, TM, ML_W), lambda bi, i: (bi, i + skip, 1)),
                  full(wa), full(wr), full(wm)],
        out_specs=pl.BlockSpec((1, TM, d), lambda bi, i: (bi, i, 0)),
        out_shape=jax.ShapeDtypeStruct((bsz, nt * TM, d), F32),
        compiler_params=_cparams(("parallel", "parallel")),
        name="readout_out_proj",
    )(h, modsel, at, hg_o, hg_o, hgp, gn, e, ml_o, ml_o, mvo, wa, wr, wm)


def _ffn_kernel(h_ref, mod_ref, g_ref, w1_ref, w2_ref, o_ref):
    x = h_ref[0]
    mod = mod_ref[0, 0]
    ms = jnp.mean(x * x, axis=-1, keepdims=True)
    y = x * lax.rsqrt(ms + EPS) * g_ref[...]
    u = (y * (1.0 + mod[4:5]) + mod[3:4]).astype(BF16)
    a = jnp.maximum(_dot(u, w1_ref[...]), 0.0)
    a = (a * a).astype(BF16)
    o_ref[0] = x + mod[5:6] * _dot(a, w2_ref[...])


def _ffn_call(h, modsel, g, w1, w2, nct):
    bsz, t, d = h.shape
    nt = t // TM
    full = lambda a: pl.BlockSpec(a.shape, lambda bi, i: (0,) * a.ndim, pipeline_mode=pl.Buffered(1))
    return pl.pallas_call(
        _ffn_kernel,
        grid=(bsz, nt),
        in_specs=[pl.BlockSpec((1, TM, d), lambda bi, i: (bi, i, 0)),
                  pl.BlockSpec((1, 1, N_MOD, d), lambda bi, i: (bi, (i >= nct).astype(jnp.int32), 0, 0)),
                  pl.BlockSpec(g.shape, lambda bi, i: (0, 0)),
                  full(w1), full(w2)],
        out_specs=pl.BlockSpec((1, TM, d), lambda bi, i: (bi, i, 0)),
        out_shape=jax.ShapeDtypeStruct((bsz, t, d), F32),
        compiler_params=_cparams(("parallel", "parallel")),
        name="ffn",
    )(h, modsel, g, w1, w2)


def _rope_tables(n_ctx, n_lat):
    inv_freq = ROPE_THETA ** (-np.arange(0, ROPE_AXIS_DIM, 2, dtype=np.float32) / ROPE_AXIS_DIM)
    inv_freq = jnp.asarray(inv_freq, F32)
    tok = jnp.arange(n_lat)
    ang_row = (tok // GRID_W).astype(F32)[:, None] * inv_freq[None, :]
    ang_col = (tok % GRID_W).astype(F32)[:, None] * inv_freq[None, :]
    ang = jnp.concatenate([ang_row, ang_row, ang_col, ang_col], axis=1)
    ang = jnp.concatenate([jnp.zeros((n_ctx, HEAD_DIM), F32), ang], axis=0)
    sign = np.where((np.arange(HEAD_DIM) % 32) < 16, -1.0, 1.0).astype(np.float32)
    cos = jnp.cos(ang)
    sin = jnp.sin(ang) * jnp.asarray(sign)[None, :]
    return jnp.concatenate([cos, cos], axis=1), jnp.concatenate([sin, sin], axis=1)


def _block_ones(width, head, scale, dtype):
    idx = np.arange(width) // head
    return jnp.asarray((idx[:, None] == idx[None, :]).astype(np.float32) * scale, dtype)


def _scan_masks(n):
    tril = np.tril(np.ones((n, n), np.float32))
    return np.stack([tril, tril.T])


def kernel(x, c, ctx, c_ctx, w_mod, b_mod, norm_mix, norm_ffn, w_in, b_in, q_norm, k_norm,
           hg_lb, hg_norm, ml_conv, w_out, w_ff1, w_ff2):
    bsz, n_lat, d = x.shape
    n_ctx = ctx.shape[1]
    depth = w_mod.shape[0]
    t = n_ctx + n_lat
    nct = n_ctx // TM
    assert n_ctx % SCAN_BLK == 0 and n_lat % SCAN_BLK == 0 and n_lat % GRID_W == 0 and bsz + 1 <= 8

    cvec = jnp.concatenate([c, c_ctx[None, :], jnp.zeros((8 - bsz - 1, d), F32)], axis=0)
    mods = _mod_call(cvec, w_mod, b_mod).reshape(depth, 8, N_MOD, d)
    modsel = jnp.stack([jnp.broadcast_to(mods[:, bsz][:, None], (depth, bsz, N_MOD, d)), mods[:, :bsz]], axis=2)

    cos, sin = _rope_tables(n_ctx, n_lat)
    e_attn = _block_ones(ATTN_W, HEAD_DIM, 1.0 / HEAD_DIM, BF16)
    e_hg_mean = _block_ones(HG_W, HG_DK, 1.0 / HG_DK, BF16)
    e_hg = _block_ones(HG_W, HG_DK, 1.0, BF16)
    bd_hg = _block_ones(HG_W, HG_DK, 1.0, F32)
    scan_mask = _scan_masks(SCAN_BLK)
    mask_ml = jnp.asarray(scan_mask, BF16)
    mask_ml_t = jnp.asarray(np.transpose(scan_mask, (0, 2, 1)).copy(), BF16)

    lbs = jnp.cumsum(jax.nn.softmax(hg_lb.astype(F32), axis=0), axis=0)
    lbs = (lbs - lbs[:1]).reshape(depth, 2, 1, HG_W)

    w_in_p = jnp.pad(w_in, ((0, 0), (0, 0), (0, N_IN_PAD - N_IN))).astype(BF16)
    b_in_p = jnp.pad(b_in, ((0, 0), (0, N_IN_PAD - N_IN))).reshape(depth, 1, N_IN_PAD)
    w_out_b = w_out.astype(BF16)
    w1_b = w_ff1.astype(BF16)
    w2_b = w_ff2.astype(BF16)
    gq = jnp.tile(q_norm, (1, ATTN_HEADS)).reshape(depth, 1, ATTN_W)
    gk = jnp.tile(k_norm, (1, ATTN_KV_HEADS)).reshape(depth, 1, KV_W)
    gn = jnp.tile(hg_norm, (1, HG_HEADS)).reshape(depth, 1, HG_W)

    h = jnp.concatenate([ctx, x], axis=1)
    for l in range(depth):
        last = l == depth - 1
        aq, ak, av, hgp, mqk, mvo, mg, mvt = _in_call(
            h, modsel[l], norm_mix[l].reshape(1, d), w_in_p[l], b_in_p[l], cos, sin, gq[l], gk[l], e_attn, nct)

        shift = (HEAD_DIM * Q_SCALE) * jnp.max(jnp.abs(q_norm[l])) * jnp.max(jnp.abs(k_norm[l]))
        at = _attention(aq, ak, av, shift, n_ctx)

        hg_o = _hgrn_call(hgp, lbs[l], e_hg, bd_hg, nct)

        qk = _conv_call(mqk, ml_conv[l], nct)
        g16 = mg[:, :, :16].reshape(bsz, t, 2, 2, ML_HEADS)
        gdir = g16.transpose(0, 3, 1, 2, 4).reshape(bsz, 2, t, 8)
        gcol = jnp.pad(gdir, ((0, 0), (0, 0), (0, 0), (0, 120)))
        grow = gdir.transpose(0, 1, 3, 2)
        ml_o = _mlstm_call(qk, mvt, gcol, grow, mask_ml, mask_ml_t, nct)

        skip = nct if last else 0
        h1 = _out_call(h, modsel[l], at, hg_o, hgp, gn[l], e_hg_mean, ml_o, mvo,
                       w_out_b[l, :ATTN_W], w_out_b[l, ATTN_W:ATTN_W + HG_W], w_out_b[l, ATTN_W + HG_W:], nct, skip)
        h = _ffn_call(h1, modsel[l], norm_ffn[l].reshape(1, d), w1_b[l], w2_b[l], 0 if last else nct)
    return h
```

```python
import functools

import jax
import jax.numpy as jnp
import numpy as np
from jax import lax
from jax.experimental import pallas as pl
from jax.experimental.pallas import tpu as pltpu

F32 = jnp.float32
BF16 = jnp.bfloat16

GRID_W = 64
HEAD_DIM = 64
ATTN_HEADS = 8
ATTN_KV_HEADS = 2
ATTN_REP = ATTN_HEADS // ATTN_KV_HEADS
ROPE_THETA = 10000.0
ROPE_AXIS_DIM = HEAD_DIM // 2
HG_HEADS = 4
HG_DK = 64
ML_HEADS = 4
ML_DK = 64
N_MOD = 6
EPS = 1e-6
ATTN_W = ATTN_HEADS * HEAD_DIM
KV_W = ATTN_KV_HEADS * HEAD_DIM
HG_W = HG_HEADS * HG_DK
ML_W = ML_HEADS * ML_DK

OFF_AQ, OFF_AK, OFF_AV = 0, 512, 640
OFF_HG = 768
OFF_MQK = 2048
OFF_MVO = 2560
OFF_MG = 3072
N_IN = 3088
N_IN_PAD = 3200

TM = 256
TQ = 256
TK = 256
ATTN_UNROLL = 16
SCAN_BLK = 256
HG_CHUNK = 16
QK_ROWS = 80
MAX_FIXED_SHIFT = 40.0
NEG_BIG = -1e30
Q_SCALE = float(np.log2(np.e)) * HEAD_DIM ** -0.5
VMEM_LIMIT = 56 * 1024 * 1024


def _cparams(sem):
    return pltpu.CompilerParams(dimension_semantics=sem, vmem_limit_bytes=VMEM_LIMIT)


def _split3(x):
    hi = x.astype(BF16)
    r1 = x - hi.astype(F32)
    mid = r1.astype(BF16)
    lo = (r1 - mid.astype(F32)).astype(BF16)
    return hi, mid, lo


def _dot(a, b):
    return jnp.dot(a, b, preferred_element_type=F32)


def _dot_nt(a, b):
    return lax.dot_general(a, b, (((1,), (1,)), ((), ())), preferred_element_type=F32)


def _dot_tn(a, b):
    return lax.dot_general(a, b, (((0,), (0,)), ((), ())), preferred_element_type=F32)


def _mask_dot(mask_bf16, x):
    hi, mid, lo = _split3(x)
    return _dot(mask_bf16, hi) + _dot(mask_bf16, mid) + _dot(mask_bf16, lo)


def _dot_mask(x, mask_bf16):
    hi, mid, lo = _split3(x)
    return _dot(hi, mask_bf16) + _dot(mid, mask_bf16) + _dot(lo, mask_bf16)


def _sigmoid(x):
    return 1.0 / (1.0 + jnp.exp(-x))


def _log_sigmoid(x):
    return jnp.minimum(x, 0.0) - jnp.log(1.0 + jnp.exp(-jnp.abs(x)))


def _mod_kernel(c_ref, w_ref, b_ref, o_ref):
    cv = c_ref[...]
    a = cv * _sigmoid(cv)
    o_ref[0] = jnp.dot(a, w_ref[0], preferred_element_type=F32, precision=lax.Precision.HIGHEST) + b_ref[0]


def _mod_call(cvec, w_mod, b_mod):
    depth, d, nm = w_mod.shape
    tn = 1024
    return pl.pallas_call(
        _mod_kernel,
        grid=(depth, nm // tn),
        in_specs=[pl.BlockSpec((8, d), lambda l, j: (0, 0)),
                  pl.BlockSpec((1, d, tn), lambda l, j: (l, 0, j)),
                  pl.BlockSpec((1, 1, tn), lambda l, j: (l, 0, j))],
        out_specs=pl.BlockSpec((1, 8, tn), lambda l, j: (l, 0, j)),
        out_shape=jax.ShapeDtypeStruct((depth, 8, nm), F32),
        compiler_params=_cparams(("arbitrary", "arbitrary")),
        name="mod_vectors",
    )(cvec, w_mod, b_mod.reshape(depth, 1, nm))


def _head_rms(t, gain, e):
    ss = _dot((t * t).astype(BF16), e)
    return t * lax.rsqrt(ss + EPS) * gain


def _rope(t, cos, sin):
    w = t.shape[1]
    reps = w // 128
    cosw = jnp.concatenate([cos] * reps, axis=1) if reps > 1 else cos
    sinw = jnp.concatenate([sin] * reps, axis=1) if reps > 1 else sin
    lane = lax.broadcasted_iota(jnp.int32, t.shape, 1)
    first_half = (lane % 32) < 16
    partner = jnp.where(first_half, pltpu.roll(t, w - 16, 1), pltpu.roll(t, 16, 1))
    return t * cosw + partner * sinw


def _in_kernel(h_ref, mod_ref, g_ref, w_ref, b_ref, cos_ref, sin_ref, gq_ref, gk_ref, e_ref,
               aq_ref, ak_ref, av_ref, hg_ref, mqk_ref, mvo_ref, mg_ref, mvt_ref):
    x = h_ref[0]
    ms = jnp.mean(x * x, axis=-1, keepdims=True)
    y = x * lax.rsqrt(ms + EPS) * g_ref[...]
    mod = mod_ref[0, 0]
    u = y * (1.0 + mod[1:2]) + mod[0:1]
    p = _dot(u.astype(BF16), w_ref[...]) + b_ref[...]
    cos = cos_ref[...]
    sin = sin_ref[...]
    e = e_ref[...]
    q = _rope(_head_rms(p[:, OFF_AQ:OFF_AQ + ATTN_W], gq_ref[...], e), cos, sin)
    k = _rope(_head_rms(p[:, OFF_AK:OFF_AK + KV_W], gk_ref[...], e[:KV_W, :KV_W]), cos, sin)
    aq_ref[0] = (q * Q_SCALE).astype(BF16)
    ak_ref[0] = k.astype(BF16)
    av_ref[0] = p[:, OFF_AV:OFF_AV + KV_W].astype(BF16)
    hg_ref[0] = p[:, OFF_HG:OFF_MQK]
    mqk_ref[0] = p[:, OFF_MQK:OFF_MVO]
    mvo_ref[0] = p[:, OFF_MVO:OFF_MG]
    mg_ref[0] = p[:, OFF_MG:OFF_MG + 128]
    mvt_ref[0] = p[:, OFF_MVO:OFF_MVO + ML_W].T.astype(BF16)


def _in_call(h, modsel, g, w, b, cos, sin, gq, gk, e, nct):
    bsz, t, d = h.shape
    nt = t // TM
    row = lambda width: pl.BlockSpec((1, TM, width), lambda bi, i: (bi, i, 0))
    full = lambda a: pl.BlockSpec(a.shape, lambda bi, i: (0,) * a.ndim)
    out_widths = (ATTN_W, KV_W, KV_W, OFF_MQK - OFF_HG, 2 * ML_W, 2 * ML_W, 128)
    out_dtypes = (BF16, BF16, BF16, F32, F32, F32, F32)
    return pl.pallas_call(
        _in_kernel,
        grid=(bsz, nt),
        in_specs=[row(d),
                  pl.BlockSpec((1, 1, N_MOD, d), lambda bi, i: (bi, (i >= nct).astype(jnp.int32), 0, 0)),
                  full(g), full(w), full(b),
                  pl.BlockSpec((TM, 128), lambda bi, i: (i, 0)),
                  pl.BlockSpec((TM, 128), lambda bi, i: (i, 0)),
                  full(gq), full(gk), full(e)],
        out_specs=[row(wd) for wd in out_widths] + [pl.BlockSpec((1, ML_W, TM), lambda bi, i: (bi, 0, i))],
        out_shape=[jax.ShapeDtypeStruct((bsz, t, wd), dt) for wd, dt in zip(out_widths, out_dtypes)]
        + [jax.ShapeDtypeStruct((bsz, ML_W, t), BF16)],
        compiler_params=_cparams(("parallel", "parallel")),
        name="norm_mod_in_proj",
    )(h, modsel, g, w, b, cos, sin, gq, gk, e)


def _attn_kernel(qa_ref, ka_ref, vt_ref, o_ref, acc_ref, l_ref, s_ref, *maybe_m_ref, online, n_ctx_q, n_ctx_k, n_k):
    qi = pl.program_id(2)
    acc_ref[...] = jnp.zeros(acc_ref.shape, F32)
    l_ref[...] = jnp.zeros(l_ref.shape, F32)
    if online:
        m_ref, = maybe_m_ref
        m_ref[...] = jnp.full(m_ref.shape, NEG_BIG, F32)
    n_blocks = jnp.where(qi < n_ctx_q, n_ctx_k, n_k)

    def key_tile(kb):
        return ka_ref[0, 0, pl.ds(pl.multiple_of(kb * TK, TK), TK), :]

    def key_sums(p):
        return jnp.sum(p.reshape(TK // 8, 8, TQ), axis=0)

    def produce(kt, slot, hh, counted):
        s = _dot(kt, qa_ref[0, hh])
        if online:
            s_ref[slot, hh] = s
        else:
            p = jnp.exp2(s)
            l_ref[hh] += key_sums(p) if counted is True else jnp.where(counted, key_sums(p), 0.0)
            s_ref[slot, hh] = p.astype(BF16)

    def step(kb, cur, nxt):
        vt = vt_ref[0, 0, :, pl.ds(pl.multiple_of(kb * TK, TK), TK)]
        kt = key_tile(jnp.minimum(kb + 1, n_blocks - 1))
        for hh in range(ATTN_REP):
            produce(kt, nxt, hh, kb + 1 < n_blocks)
            if online:
                s = s_ref[cur, hh]
                m_old = m_ref[hh]
                m_new = jnp.maximum(m_old, jnp.max(s, axis=0, keepdims=True))
                p = jnp.exp2(s - m_new)
                alpha = jnp.exp2(m_old - m_new)
                l_ref[hh] = alpha * l_ref[hh] + key_sums(p)
                acc_ref[hh] = alpha * acc_ref[hh] + _dot(vt, p.astype(BF16))
                m_ref[hh] = m_new
            else:
                acc_ref[hh] += _dot(vt, s_ref[cur, hh])

    kt0 = key_tile(0)
    for hh in range(ATTN_REP):
        produce(kt0, 0, hh, True)

    def group(i, carry):
        for u in range(ATTN_UNROLL):
            step(ATTN_UNROLL * i + u, u % 2, (u + 1) % 2)
        return carry

    def pair(i, carry):
        step(2 * i, 0, 1)
        step(2 * i + 1, 1, 0)
        return carry

    n_groups = n_blocks // ATTN_UNROLL
    lax.fori_loop(0, n_groups, group, 0)
    lax.fori_loop(n_groups * (ATTN_UNROLL // 2), n_blocks // 2, pair, 0)

    @pl.when(n_blocks % 2 == 1)
    def _():
        step(n_blocks - 1, 0, 1)

    for hh in range(ATTN_REP):
        o = acc_ref[hh] / jnp.sum(l_ref[hh], axis=0, keepdims=True)
        o_ref[0, hh * HEAD_DIM:(hh + 1) * HEAD_DIM, :] = o.astype(o_ref.dtype)


def _attn_call(qa, ka, vt, n_ctx, online):
    bsz, _, _, t = qa.shape
    kern = functools.partial(_attn_kernel, online=online, n_ctx_q=n_ctx // TQ, n_ctx_k=n_ctx // TK, n_k=t // TK)
    gw = ATTN_REP * HEAD_DIM
    scratch = [pltpu.VMEM((ATTN_REP, HEAD_DIM, TQ), F32),
               pltpu.VMEM((ATTN_REP, 8, TQ), F32),
               pltpu.VMEM((2, ATTN_REP, TK, TQ), F32 if online else BF16)]
    if online:
        scratch.append(pltpu.VMEM((ATTN_REP, 1, TQ), F32))
    return pl.pallas_call(
        kern,
        grid=(bsz, ATTN_KV_HEADS, t // TQ),
        in_specs=[pl.BlockSpec((1, ATTN_REP, QK_ROWS, TQ), lambda b, g, i: (b, g, 0, i)),
                  pl.BlockSpec((1, 1, t, QK_ROWS), lambda b, g, i: (b, g, 0, 0)),
                  pl.BlockSpec((1, 1, HEAD_DIM, t), lambda b, g, i: (b, g, 0, 0))],
        out_specs=pl.BlockSpec((1, gw, TQ), lambda b, g, i: (b, g, i)),
        out_shape=jax.ShapeDtypeStruct((bsz, ATTN_W, t), BF16),
        scratch_shapes=scratch,
        compiler_params=_cparams(("parallel", "parallel", "arbitrary")),
        name="attention_online" if online else "attention",
    )(qa, ka, vt)


def _attention(aq, ak, av, shift, n_ctx):
    bsz, t, _ = aq.shape
    qt = aq.transpose(0, 2, 1).reshape(bsz, ATTN_HEADS, HEAD_DIM, t)
    qa = jnp.concatenate([qt, jnp.broadcast_to((-shift).astype(BF16), (bsz, ATTN_HEADS, 1, t)),
                          jnp.zeros((bsz, ATTN_HEADS, QK_ROWS - HEAD_DIM - 1, t), BF16)], axis=2)
    k4 =ak.reshape(bsz, t, ATTN_KV_HEADS, HEAD_DIM).transpose(0, 2, 1, 3)
    ka = jnp.concatenate([k4, jnp.ones((bsz, ATTN_KV_HEADS, t, 1), BF16),
                          jnp.zeros((bsz, ATTN_KV_HEADS, t, QK_ROWS - HEAD_DIM - 1), BF16)], axis=3)
    vt = av.reshape(bsz, t, ATTN_KV_HEADS, HEAD_DIM).transpose(0, 2, 3, 1)
    return lax.cond(shift <= MAX_FIXED_SHIFT,
                    lambda: _attn_call(qa, ka, vt, n_ctx, False),
                    lambda: _attn_call(qa, ka, vt, n_ctx, True))


def _conv_kernel(x_ref, prev_ref, next_ref, w_ref, o_ref, *, nct, nb):
    i = pl.program_id(1)
    x = x_ref[0]
    rows = x.shape[0]
    left_edge = jnp.logical_or(i == 0, i == nct)
    right_edge = jnp.logical_or(i == nct - 1, i == nb - 1)
    prev_row = jnp.where(left_edge, 0.0, prev_ref[0, 7:8, :])
    next_row = jnp.where(right_edge, 0.0, next_ref[0, 0:1, :])
    ridx = lax.broadcasted_iota(jnp.int32, x.shape, 0)
    xm1 = jnp.where(ridx == 0, prev_row, pltpu.roll(x, 1, 0))
    xp1 = jnp.where(ridx == rows - 1, next_row, pltpu.roll(x, rows - 1, 0))
    w = w_ref[...]
    y = xm1 * w[0:1] + x * w[1:2] + xp1 * w[2:3]
    y = y * _sigmoid(y)
    lane = lax.broadcasted_iota(jnp.int32, x.shape, 1)
    y = jnp.where(lane >= ML_W, y * (ML_DK ** -0.5), y)
    o_ref[0] = y.astype(o_ref.dtype)


def _conv_call(mqk, w, nct):
    bsz, t, width = mqk.shape
    nb = t // SCAN_BLK
    per8 = SCAN_BLK // 8
    last8 = t // 8 - 1
    kern = functools.partial(_conv_kernel, nct=nct, nb=nb)
    return pl.pallas_call(
        kern,
        grid=(bsz, nb),
        in_specs=[pl.BlockSpec((1, SCAN_BLK, width), lambda b, i: (b, i, 0)),
                  pl.BlockSpec((1, 8, width), lambda b, i: (b, jnp.maximum(i * per8 - 1, 0), 0)),
                  pl.BlockSpec((1, 8, width), lambda b, i: (b, jnp.minimum((i + 1) * per8, last8), 0)),
                  pl.BlockSpec(w.shape, lambda b, i: (0, 0))],
        out_specs=pl.BlockSpec((1, SCAN_BLK, width), lambda b, i: (b, i, 0)),
        out_shape=jax.ShapeDtypeStruct((bsz, t, width), BF16),
        compiler_params=_cparams(("parallel", "parallel")),
        name="mlstm_qk_conv",
    )(mqk, mqk, mqk, w)


def _scan_block(d, j, nct, nb):
    bwd = jnp.where(j < nct, nct - 1 - j, nb - 1 - (j - nct))
    return jnp.where(d == 0, j, bwd)


def _hgrn_kernel(q0_ref, q1_ref, v0_ref, v1_ref, z0_ref, z1_ref, lb_ref, e_ref, bd_ref, o_ref,
                 st_ref, qd_ref, kd_ref, od_ref, dt_ref, u_ref):
    d = pl.program_id(1)
    j = pl.program_id(2)

    @pl.when(j == 0)
    def _():
        st_ref[...] = jnp.zeros(st_ref.shape, F32)

    lb = lb_ref[0]
    e = e_ref[...]
    bd = bd_ref[...]
    c = HG_CHUNK
    nch = SCAN_BLK // c

    def rows(i):
        return pl.ds(jnp.where(d == 0, i, c - 1 - i), nch, stride=c)

    def load2(r0, r1, sl):
        return jnp.concatenate([r0[0, sl, :], r1[0, sl, :]], axis=1)

    def load_halves(ref, sl):
        return jnp.concatenate([ref[0, sl, :], ref[1, sl, :]], axis=1)

    def store_halves(ref, sl, x):
        ref[0, sl, :] = x[:, :128]
        ref[1, sl, :] = x[:, 128:]

    qs, ks, vs, bs = [], [], [], []
    b = None
    for i in range(c):
        f = lb + (1.0 - lb) * _sigmoid(load2(z0_ref, z1_ref, rows(i)))
        lf = jnp.log2(f)
        b = lf if b is None else b + lf
        qs.append(load2(q0_ref, q1_ref, rows(i)))
        vs.append(load2(v0_ref, v1_ref, rows(i)))
        ks.append(1.0 - f)
        bs.append(b)
    b_tot = bs[-1]
    dt_ref[...] = jnp.exp2(b_tot)
    for i in range(c):
        store_halves(qd_ref, rows(i), qs[i] * jnp.exp2(bs[i]))
        store_halves(kd_ref, rows(i), ks[i] * jnp.exp2(b_tot - bs[i]))
        ps = [qs[i] * ks[jj] * jnp.exp2(bs[i] - bs[jj]) for jj in range(i)] + [qs[i] * ks[i]]
        a = _dot(jnp.concatenate(ps, axis=0).astype(BF16), e)
        o = a[0:nch] * vs[0]
        for jj in range(1, i + 1):
            o = o + a[jj * nch:(jj + 1) * nch] * vs[jj]
        store_halves(od_ref, rows(i), o)

    for cc in range(nch):
        sl = pl.ds(cc * c, c)
        vc = load2(v0_ref, v1_ref, sl).astype(BF16)
        kc = load_halves(kd_ref, sl).astype(BF16)
        u_ref[cc] = _dot_tn(vc, kc) * bd
    for n in range(nch):
        cc = jnp.where(d == 0, n, nch - 1 - n)
        sl = pl.ds(pl.multiple_of(cc * c, c), c)
        st = st_ref[...]
        o_inter = _dot_nt(load_halves(qd_ref, sl).astype(BF16), st.astype(BF16))
        o_ref[0, 0, sl, :] = load_halves(od_ref, sl) + o_inter
        st_ref[...] = st * dt_ref[pl.ds(cc, 1), :] + u_ref[cc]


def _hgrn_call(hgp, lbs, e, bd, nct):
    bsz, t, _ = hgp.shape
    nb = t // SCAN_BLK
    w = HG_W
    blk = lambda d, j: _scan_block(d, j, nct, nb)
    half = lambda col: pl.BlockSpec((1, SCAN_BLK, 128), lambda b, d, j: (b, blk(d, j), col(d)))
    return pl.pallas_call(
        _hgrn_kernel,
        grid=(bsz, 2, nb),
        in_specs=[half(lambda d: 0), half(lambda d: 1),
                  half(lambda d: 2), half(lambda d: 3),
                  half(lambda d: 4 + 2 * d), half(lambda d: 5 + 2 * d),
                  pl.BlockSpec((1, 1, w), lambda b, d, j: (d, 0, 0)),
                  pl.BlockSpec((w, w), lambda b, d, j: (0, 0)),
                  pl.BlockSpec((w, w), lambda b, d, j: (0, 0))],
        out_specs=pl.BlockSpec((1, 1, SCAN_BLK, w), lambda b, d, j: (d, b, blk(d, j), 0)),
        out_shape=jax.ShapeDtypeStruct((2, bsz, t, w), F32),
        scratch_shapes=[pltpu.VMEM((w, w), F32),
                        pltpu.VMEM((2, SCAN_BLK, 128), F32),
                        pltpu.VMEM((2, SCAN_BLK, 128), F32),
                        pltpu.VMEM((2, SCAN_BLK, 128), F32),
                        pltpu.VMEM((SCAN_BLK // HG_CHUNK, w), F32),
                        pltpu.VMEM((SCAN_BLK // HG_CHUNK, w, w), F32)],
        compiler_params=_cparams(("parallel", "parallel", "arbitrary")),
        name="hgrn2_scan",
    )(hgp, hgp, hgp, hgp, hgp, hgp, lbs, e, bd)


def _mlstm_kernel(q_ref, k_ref, vt_ref, g_ref, gt_ref, m_ref, mt_ref, o_ref, c_ref, ms_ref):
    j = pl.program_id(2)

    @pl.when(j == 0)
    def _():
        c_ref[...] = jnp.zeros(c_ref.shape, F32)
        ms_ref[...] = jnp.zeros(ms_ref.shape, F32)

    length = SCAN_BLK
    mask = m_ref[0]
    mask_t = mt_ref[0]
    valid_t = mask_t > 0
    gates = g_ref[0, 0]
    gates_t = gt_ref[0, 0]
    b_col_all = _mask_dot(mask, _log_sigmoid(gates))
    lf_t = _log_sigmoid(gates_t)
    b_row_all = _dot_mask(lf_t, mask_t)
    tot_all = jnp.sum(lf_t, axis=1, keepdims=True)
    row = lax.broadcasted_iota(jnp.int32, (128, length), 0)

    heads = range(ML_HEADS)
    kh = [k_ref[0, :, hh * ML_DK:(hh + 1) * ML_DK] for hh in heads]
    qh = [q_ref[0, :, hh * ML_DK:(hh + 1) * ML_DK] for hh in heads]
    s_raw = [_dot_nt(kh[hh], qh[hh]) for hh in heads]
    inter = [_dot_nt(c_ref[hh].astype(BF16), qh[hh]) for hh in heads]
    for pair in range(ML_HEADS // 2):
        v2t = vt_ref[0, pair * 128:(pair + 1) * 128, :]
        out_t = jnp.zeros((128, length), F32)
        for sub in range(2):
            hh = pair * 2 + sub
            own = (row < 64) if sub == 0 else (row >= 64)
            ones_row = 64 if sub == 0 else 0
            v_aug_t = jnp.where(own, v2t, jnp.where(row == ones_row, 1.0, 0.0).astype(BF16))
            m_prev = ms_ref[hh, 0:1, 0:1]
            cs = gates[:, hh:hh + 1] - b_col_all[:, 4 + hh:5 + hh]
            br = b_row_all[4 + hh:5 + hh, :]
            tot = tot_all[4 + hh:5 + hh, :]
            log_d = jnp.where(valid_t, br + cs, NEG_BIG)
            m_t = jnp.maximum(br + m_prev, jnp.max(log_d, axis=0, keepdims=True))
            s_t = (s_raw[hh] * jnp.exp(log_d - m_t)).astype(BF16)
            r_t = jnp.exp(br + m_prev - m_t) * inter[hh] + _dot(v_aug_t, s_t)
            den = r_t[ones_row:ones_row + 1, :]
            out_t = jnp.where(own, r_t / jnp.maximum(jnp.abs(den), jnp.exp(-m_t)), out_t)
            log_e = tot + cs
            m_new = jnp.maximum(tot + m_prev, jnp.max(log_e, axis=0, keepdims=True))
            ke = (kh[hh].astype(F32) * jnp.exp(log_e - m_new)).astype(BF16)
            c_ref[hh] = jnp.exp(tot + m_prev - m_new) * c_ref[hh] + _dot(v_aug_t, ke)
            ms_ref[hh] = jnp.broadcast_to(m_new, ms_ref.shape[1:])
        o_ref[0, 0, :, pair * 128:(pair + 1) * 128] = out_t.T


def _mlstm_call(qk, vt, gcol, grow, mask, mask_t, nct):
    bsz, t, _ = qk.shape
    nb = t // SCAN_BLK
    w = ML_W
    blk = lambda d, j: _scan_block(d, j, nct, nb)
    return pl.pallas_call(
        _mlstm_kernel,
        grid=(bsz, 2, nb),
        in_specs=[pl.BlockSpec((1, SCAN_BLK, w), lambda b, d, j: (b, blk(d, j), 0)),
                  pl.BlockSpec((1, SCAN_BLK, w), lambda b, d, j: (b, blk(d, j), 1)),
                  pl.BlockSpec((1, w, SCAN_BLK), lambda b, d, j: (b, 0, blk(d, j))),
                  pl.BlockSpec((1, 1, SCAN_BLK, 128), lambda b, d, j: (b, d, blk(d, j), 0)),
                  pl.BlockSpec((1, 1, 8, SCAN_BLK), lambda b, d, j: (b, d, 0, blk(d, j))),
                  pl.BlockSpec((1, SCAN_BLK, SCAN_BLK), lambda b, d, j: (d, 0, 0)),
                  pl.BlockSpec((1, SCAN_BLK, SCAN_BLK), lambda b, d, j: (d, 0, 0))],
        out_specs=pl.BlockSpec((1, 1, SCAN_BLK, w), lambda b, d, j: (d, b, blk(d, j), 0)),
        out_shape=jax.ShapeDtypeStruct((2, bsz, t, w), F32),
        scratch_shapes=[pltpu.VMEM((ML_HEADS, 128, ML_DK), F32),
                        pltpu.VMEM((ML_HEADS, 8, 128), F32)],
        compiler_params=_cparams(("parallel", "parallel", "arbitrary")),
        name="mlstm_scan",
    )(qk, qk, vt, gcol, grow, mask, mask_t)


def _out_kernel(h_ref, mod_ref, at_ref, hf_ref, hb_ref, hgate_ref, gn_ref, e_ref, mf_ref, mb_ref, mgate_ref,
                wa_ref, wr_ref, wm_ref, o_ref):
    mod = mod_ref[0, 0]
    o = hf_ref[0, 0] + hb_ref[0, 0]
    ss = _dot((o * o).astype(BF16), e_ref[...])
    gate = hgate_ref[0]
    r = o * lax.rsqrt(ss + EPS) * gn_ref[...] * (gate * _sigmoid(gate))
    m = _sigmoid(mgate_ref[0]) * (mf_ref[0, 0] + mb_ref[0, 0])
    y = _dot_tn(at_ref[0], wa_ref[...]) + _dot(r.astype(BF16), wr_ref[...]) + _dot(m.astype(BF16), wm_ref[...])
    o_ref[0] = h_ref[0] + mod[2:3] * y


def _out_call(h, modsel, at, hg_o, hgp, gn, e, ml_o, mvo, wa, wr, wm, nct, skip):
    bsz, t, d = h.shape
    nt = t // TM - skip
    full = lambda a: pl.BlockSpec(a.shape, lambda bi, i: (0,) * a.ndim)
    hg_col = (OFF_MQK - OFF_HG) // HG_W - 1
    return pl.pallas_call(
        _out_kernel,
        grid=(bsz, nt),
        in_specs=[pl.BlockSpec((1, TM, d), lambda bi, i: (bi, i + skip, 0)),
                  pl.BlockSpec((1, 1, N_MOD, d), lambda bi, i: (bi, (i + skip >= nct).astype(jnp.int32), 0, 0)),
                  pl.BlockSpec((1, ATTN_W, TM), lambda bi, i: (bi, 0, i + skip)),
                  pl.BlockSpec((1, 1, TM, HG_W), lambda bi, i: (0, bi, i + skip, 0)),
                  pl.BlockSpec((1, 1, TM, HG_W), lambda bi, i: (1, bi, i + skip, 0)),
                  pl.BlockSpec((1, TM, HG_W), lambda bi, i: (bi, i + skip, hg_col)),
                  full(gn), full(e),
                  pl.BlockSpec((1, 1, TM, ML_W), lambda bi, i: (0, bi, i + skip, 0)),
                  pl.BlockSpec((1, 1, TM, ML_W), lambda bi, i: (1, bi, i + skip, 0)),
                  pl.BlockSpec((1, TM, ML_W), lambda bi, i: (bi, i + skip, 1)),
                  full(wa), full(wr), full(wm)],
        out_specs=pl.BlockSpec((1, TM, d), lambda bi, i: (bi, i, 0)),
        out_shape=jax.ShapeDtypeStruct((bsz, nt * TM, d), F32),
        compiler_params=_cparams(("parallel", "parallel")),
        name="readout_out_proj",
    )(h, modsel, at, hg_o, hg_o, hgp, gn, e, ml_o, ml_o, mvo, wa, wr, wm)


def _ffn_kernel(h_ref, mod_ref, g_ref, w1_ref, w2_ref, o_ref):
    x = h_ref[0]
    mod = mod_ref[0, 0]
    ms = jnp.mean(x * x, axis=-1, keepdims=True)
    y = x * lax.rsqrt(ms + EPS) * g_ref[...]
    u = (y * (1.0 + mod[4:5]) + mod[3:4]).astype(BF16)
    a = jnp.maximum(_dot(u, w1_ref[...]), 0.0)
    a = (a * a).astype(BF16)
    o_ref[0] = x + mod[5:6] * _dot(a, w2_ref[...])


def _ffn_call(h, modsel, g, w1, w2, nct):
    bsz, t, d = h.shape
    nt = t // TM
    full = lambda a: pl.BlockSpec(a.shape, lambda bi, i: (0,) * a.ndim, pipeline_mode=pl.Buffered(1))
    return pl.pallas_call(
        _ffn_kernel,
        grid=(bsz, nt),
        in_specs=[pl.BlockSpec((1, TM, d), lambda bi, i: (bi, i, 0)),
                  pl.BlockSpec((1, 1, N_MOD, d), lambda bi, i: (bi, (i >= nct).astype(jnp.int32), 0, 0)),
                  pl.BlockSpec(g.shape, lambda bi, i: (0, 0)),
                  full(w1), full(w2)],
        out_specs=pl.BlockSpec((1, TM, d), lambda bi, i: (bi, i, 0)),
        out_shape=jax.ShapeDtypeStruct((bsz, t, d), F32),
        compiler_params=_cparams(("parallel", "parallel")),
        name="ffn",
    )(h, modsel, g, w1, w2)


def _rope_tables(n_ctx, n_lat):
    inv_freq = ROPE_THETA ** (-np.arange(0, ROPE_AXIS_DIM, 2, dtype=np.float32) / ROPE_AXIS_DIM)
    inv_freq = jnp.asarray(inv_freq, F32)
    tok = jnp.arange(n_lat)
    ang_row = (tok // GRID_W).astype(F32)[:, None] * inv_freq[None, :]
    ang_col = (tok % GRID_W).astype(F32)[:, None] * inv_freq[None, :]
    ang = jnp.concatenate([ang_row, ang_row, ang_col, ang_col], axis=1)
    ang = jnp.concatenate([jnp.zeros((n_ctx, HEAD_DIM), F32), ang], axis=0)
    sign = np.where((np.arange(HEAD_DIM) % 32) < 16, -1.0, 1.0).astype(np.float32)
    cos = jnp.cos(ang)
    sin = jnp.sin(ang) * jnp.asarray(sign)[None, :]
    return jnp.concatenate([cos, cos], axis=1), jnp.concatenate([sin, sin], axis=1)


def _block_ones(width, head, scale, dtype):
    idx = np.arange(width) // head
    return jnp.asarray((idx[:, None] == idx[None, :]).astype(np.float32) * scale, dtype)


def _scan_masks(n):
    tril = np.tril(np.ones((n, n), np.float32))
    return np.stack([tril, tril.T])


def kernel(x, c, ctx, c_ctx, w_mod, b_mod, norm_mix, norm_ffn, w_in, b_in, q_norm, k_norm,
           hg_lb, hg_norm, ml_conv, w_out, w_ff1, w_ff2):
    bsz, n_lat, d = x.shape
    n_ctx = ctx.shape[1]
    depth = w_mod.shape[0]
    t = n_ctx + n_lat
    nct = n_ctx // TM
    assert n_ctx % SCAN_BLK == 0 and n_lat % SCAN_BLK == 0 and n_lat % GRID_W == 0 and bsz + 1 <= 8

    cvec = jnp.concatenate([c, c_ctx[None, :], jnp.zeros((8 - bsz - 1, d), F32)], axis=0)
    mods = _mod_call(cvec, w_mod, b_mod).reshape(depth, 8, N_MOD, d)
    modsel = jnp.stack([jnp.broadcast_to(mods[:, bsz][:, None], (depth, bsz, N_MOD, d)), mods[:, :bsz]], axis=2)

    cos, sin = _rope_tables(n_ctx, n_lat)
    e_attn = _block_ones(ATTN_W, HEAD_DIM, 1.0 / HEAD_DIM, BF16)
    e_hg_mean = _block_ones(HG_W, HG_DK, 1.0 / HG_DK, BF16)
    e_hg = _block_ones(HG_W, HG_DK, 1.0, BF16)
    bd_hg = _block_ones(HG_W, HG_DK, 1.0, F32)
    scan_mask = _scan_masks(SCAN_BLK)
    mask_ml = jnp.asarray(scan_mask, BF16)
    mask_ml_t = jnp.asarray(np.transpose(scan_mask, (0, 2, 1)).copy(), BF16)

    lbs = jnp.cumsum(jax.nn.softmax(hg_lb.astype(F32), axis=0), axis=0)
    lbs = (lbs - lbs[:1]).reshape(depth, 2, 1, HG_W)

    w_in_p = jnp.pad(w_in, ((0, 0), (0, 0), (0, N_IN_PAD - N_IN))).astype(BF16)
    b_in_p = jnp.pad(b_in, ((0, 0), (0, N_IN_PAD - N_IN))).reshape(depth, 1, N_IN_PAD)
    w_out_b = w_out.astype(BF16)
    w1_b = w_ff1.astype(BF16)
    w2_b = w_ff2.astype(BF16)
    gq = jnp.tile(q_norm, (1, ATTN_HEADS)).reshape(depth, 1, ATTN_W)
    gk = jnp.tile(k_norm, (1, ATTN_KV_HEADS)).reshape(depth, 1, KV_W)
    gn = jnp.tile(hg_norm, (1, HG_HEADS)).reshape(depth, 1, HG_W)

    h = jnp.concatenate([ctx, x], axis=1)
    for l in range(depth):
        last = l == depth - 1
        aq, ak, av, hgp, mqk, mvo, mg, mvt = _in_call(
            h, modsel[l], norm_mix[l].reshape(1, d), w_in_p[l], b_in_p[l], cos, sin, gq[l], gk[l], e_attn, nct)

        shift = (HEAD_DIM * Q_SCALE) * jnp.max(jnp.abs(q_norm[l])) * jnp.max(jnp.abs(k_norm[l]))
        at = _attention(aq, ak, av, shift, n_ctx)

        hg_o = _hgrn_call(hgp, lbs[l], e_hg, bd_hg, nct)

        qk = _conv_call(mqk, ml_conv[l], nct)
        g16 = mg[:, :, :16].reshape(bsz, t, 2, 2, ML_HEADS)
        gdir = g16.transpose(0, 3, 1, 2, 4).reshape(bsz, 2, t, 8)
        gcol = jnp.pad(gdir, ((0, 0), (0, 0), (0, 0), (0, 120)))
        grow = gdir.transpose(0, 1, 3, 2)
        ml_o = _mlstm_call(qk, mvt, gcol, grow, mask_ml, mask_ml_t, nct)

        skip = nct if last else 0
        h1 = _out_call(h, modsel[l], at, hg_o, hgp, gn[l], e_hg_mean, ml_o, mvo,
                       w_out_b[l, :ATTN_W], w_out_b[l, ATTN_W:ATTN_W + HG_W], w_out_b[l, ATTN_W + HG_W:], nct, skip)
        h = _ffn_call(h1, modsel[l], norm_ffn[l].reshape(1, d), w1_b[l], w2_b[l], 0 if last else nct)
    return h
```

```python
import functools

import jax
import jax.numpy as jnp
import numpy as np
from jax import lax
from jax.experimental import pallas as pl
from jax.experimental.pallas import tpu as pltpu

F32 = jnp.float32
BF16 = jnp.bfloat16

GRID_W = 64
HEAD_DIM = 64
ATTN_HEADS = 8
ATTN_KV_HEADS = 2
ATTN_REP = ATTN_HEADS // ATTN_KV_HEADS
ROPE_THETA = 10000.0
ROPE_AXIS_DIM = HEAD_DIM // 2
HG_HEADS = 4
HG_DK = 64
ML_HEADS = 4
ML_DK = 64
N_MOD = 6
EPS = 1e-6
ATTN_W = ATTN_HEADS * HEAD_DIM
KV_W = ATTN_KV_HEADS * HEAD_DIM
HG_W = HG_HEADS * HG_DK
ML_W = ML_HEADS * ML_DK

OFF_AQ, OFF_AK, OFF_AV = 0, 512, 640
OFF_HG = 768
OFF_MQK = 2048
OFF_MVO = 2560
OFF_MG = 3072
N_IN = 3088
N_IN_PAD = 3200

TM = 256
TQ = 256
TK = 256
ATTN_UNROLL = 32
SCAN_BLK = 256
HG_CHUNK = 16
QK_ROWS = 80
MAX_FIXED_SHIFT = 40.0
NEG_BIG = -1e30
Q_SCALE = float(np.log2(np.e)) * HEAD_DIM ** -0.5
VMEM_LIMIT = 56 * 1024 * 1024


def _cparams(sem):
    return pltpu.CompilerParams(dimension_semantics=sem, vmem_limit_bytes=VMEM_LIMIT)


def _split3(x):
    hi = x.astype(BF16)
    r1 = x - hi.astype(F32)
    mid = r1.astype(BF16)
    lo = (r1 - mid.astype(F32)).astype(BF16)
    return hi, mid, lo


def _dot(a, b):
    return jnp.dot(a, b, preferred_element_type=F32)


def _dot_nt(a, b):
    return lax.dot_general(a, b, (((1,), (1,)), ((), ())), preferred_element_type=F32)


def _dot_tn(a, b):
    return lax.dot_general(a, b, (((0,), (0,)), ((), ())), preferred_element_type=F32)


def _mask_dot(mask_bf16, x):
    hi, mid, lo = _split3(x)
    return _dot(mask_bf16, hi) + _dot(mask_bf16, mid) + _dot(mask_bf16, lo)


def _dot_mask(x, mask_bf16):
    hi, mid, lo = _split3(x)
    return _dot(hi, mask_bf16) + _dot(mid, mask_bf16) + _dot(lo, mask_bf16)


def _sigmoid(x):
    return 1.0 / (1.0 + jnp.exp(-x))


def _log_sigmoid(x):
    return jnp.minimum(x, 0.0) - jnp.log(1.0 + jnp.exp(-jnp.abs(x)))


def _mod_kernel(c_ref, w_ref, b_ref, o_ref):
    cv = c_ref[...]
    a = cv * _sigmoid(cv)
    o_ref[0] = jnp.dot(a, w_ref[0], preferred_element_type=F32, precision=lax.Precision.HIGHEST) + b_ref[0]


def _mod_call(cvec, w_mod, b_mod):
    depth, d, nm = w_mod.shape
    tn = 1024
    return pl.pallas_call(
        _mod_kernel,
        grid=(depth, nm // tn),
        in_specs=[pl.BlockSpec((8, d), lambda l, j: (0, 0)),
                  pl.BlockSpec((1, d, tn), lambda l, j: (l, 0, j)),
                  pl.BlockSpec((1, 1, tn), lambda l, j: (l, 0, j))],
        out_specs=pl.BlockSpec((1, 8, tn), lambda l, j: (l, 0, j)),
        out_shape=jax.ShapeDtypeStruct((depth, 8, nm), F32),
        compiler_params=_cparams(("arbitrary", "arbitrary")),
        name="mod_vectors",
    )(cvec, w_mod, b_mod.reshape(depth, 1, nm))


def _head_rms(t, gain, e):
    ss = _dot((t * t).astype(BF16), e)
    return t * lax.rsqrt(ss + EPS) * gain


def _rope(t, cos, sin):
    w = t.shape[1]
    reps = w // 128
    cosw = jnp.concatenate([cos] * reps, axis=1) if reps > 1 else cos
    sinw = jnp.concatenate([sin] * reps, axis=1) if reps > 1 else sin
    lane = lax.broadcasted_iota(jnp.int32, t.shape, 1)
    first_half = (lane % 32) < 16
    partner = jnp.where(first_half, pltpu.roll(t, w - 16, 1), pltpu.roll(t, 16, 1))
    return t * cosw + partner * sinw


def _in_kernel(h_ref, mod_ref, g_ref, w_ref, b_ref, cos_ref, sin_ref, gq_ref, gk_ref, e_ref,
               aq_ref, ak_ref, av_ref, hg_ref, mqk_ref, mvo_ref, mg_ref, mvt_ref):
    x = h_ref[0]
    ms = jnp.mean(x * x, axis=-1, keepdims=True)
    y = x * lax.rsqrt(ms + EPS) * g_ref[...]
    mod = mod_ref[0, 0]
    u = y * (1.0 + mod[1:2]) + mod[0:1]
    p = _dot(u.astype(BF16), w_ref[...]) + b_ref[...]
    cos = cos_ref[...]
    sin = sin_ref[...]
    e = e_ref[...]
    q = _rope(_head_rms(p[:, OFF_AQ:OFF_AQ + ATTN_W], gq_ref[...], e), cos, sin)
    k = _rope(_head_rms(p[:, OFF_AK:OFF_AK + KV_W], gk_ref[...], e[:KV_W, :KV_W]), cos, sin)
    aq_ref[0] = (q * Q_SCALE).astype(BF16)
    ak_ref[0] = k.astype(BF16)
    av_ref[0] = p[:, OFF_AV:OFF_AV + KV_W].astype(BF16)
    hg_ref[0] = p[:, OFF_HG:OFF_MQK]
    mqk_ref[0] = p[:, OFF_MQK:OFF_MVO]
    mvo_ref[0] = p[:, OFF_MVO:OFF_MG]
    mg_ref[0] = p[:, OFF_MG:OFF_MG + 128]
    mvt_ref[0] = p[:, OFF_MVO:OFF_MVO + ML_W].T.astype(BF16)


def _in_call(h, modsel, g, w, b, cos, sin, gq, gk, e, nct):
    bsz, t, d = h.shape
    nt = t // TM
    row = lambda width: pl.BlockSpec((1, TM, width), lambda bi, i: (bi, i, 0))
    full = lambda a: pl.BlockSpec(a.shape, lambda bi, i: (0,) * a.ndim)
    out_widths = (ATTN_W, KV_W, KV_W, OFF_MQK - OFF_HG, 2 * ML_W, 2 * ML_W, 128)
    out_dtypes = (BF16, BF16, BF16, F32, F32, F32, F32)
    return pl.pallas_call(
        _in_kernel,
        grid=(bsz, nt),
        in_specs=[row(d),
                  pl.BlockSpec((1, 1, N_MOD, d), lambda bi, i: (bi, (i >= nct).astype(jnp.int32), 0, 0)),
                  full(g), full(w), full(b),
                  pl.BlockSpec((TM, 128), lambda bi, i: (i, 0)),
                  pl.BlockSpec((TM, 128), lambda bi, i: (i, 0)),
                  full(gq), full(gk), full(e)],
        out_specs=[row(wd) for wd in out_widths] + [pl.BlockSpec((1, ML_W, TM), lambda bi, i: (bi, 0, i))],
        out_shape=[jax.ShapeDtypeStruct((bsz, t, wd), dt) for wd, dt in zip(out_widths, out_dtypes)]
        + [jax.ShapeDtypeStruct((bsz, ML_W, t), BF16)],
        compiler_params=_cparams(("parallel", "parallel")),
        name="norm_mod_in_proj",
    )(h, modsel, g, w, b, cos, sin, gq, gk, e)


def _attn_kernel(qa_ref, ka_ref, vt_ref, o_ref, acc_ref, l_ref, s_ref, *maybe_m_ref, online, n_ctx_q, n_ctx_k, n_k):
    qi = pl.program_id(2)
    acc_ref[...] = jnp.zeros(acc_ref.shape, F32)
    l_ref[...] = jnp.zeros(l_ref.shape, F32)
    if online:
        m_ref, = maybe_m_ref
        m_ref[...] = jnp.full(m_ref.shape, NEG_BIG, F32)
    n_blocks = jnp.where(qi < n_ctx_q, n_ctx_k, n_k)

    def key_tile(kb):
        return ka_ref[0, 0, pl.ds(pl.multiple_of(kb * TK, TK), TK), :]

    def key_sums(p):
        return jnp.sum(p.reshape(TK // 8, 8, TQ), axis=0)

    def produce(kt, slot, hh, counted):
        s = _dot(kt, qa_ref[0, hh])
        if online:
            s_ref[slot, hh] = s
        else:
            p = jnp.exp2(s)
            l_ref[hh] += key_sums(p) if counted is True else jnp.where(counted, key_sums(p), 0.0)
            s_ref[slot, hh] = p.astype(BF16)

    def step(kb, cur, nxt):
        vt = vt_ref[0, 0, :, pl.ds(pl.multiple_of(kb * TK, TK), TK)]
        kt = key_tile(jnp.minimum(kb + 1, n_blocks - 1))
        for hh in range(ATTN_REP):
            produce(kt, nxt, hh, kb + 1 < n_blocks)
            if online:
                s = s_ref[cur, hh]
                m_old = m_ref[hh]
                m_new = jnp.maximum(m_old, jnp.max(s, axis=0, keepdims=True))
                p = jnp.exp2(s - m_new)
                alpha = jnp.exp2(m_old - m_new)
                l_ref[hh] = alpha * l_ref[hh] + key_sums(p)
                acc_ref[hh] = alpha * acc_ref[hh] + _dot(vt, p.astype(BF16))
                m_ref[hh] = m_new
            else:
                acc_ref[hh] += _dot(vt, s_ref[cur, hh])

    kt0 = key_tile(0)
    for hh in range(ATTN_REP):
        produce(kt0, 0, hh, True)

    def group(i, carry):
        for u in range(ATTN_UNROLL):
            step(ATTN_UNROLL * i + u, u % 2, (u + 1) % 2)
        return carry

    def pair(i, carry):
        step(2 * i, 0, 1)
        step(2 * i + 1, 1, 0)
        return carry

    n_groups = n_blocks // ATTN_UNROLL
    lax.fori_loop(0, n_groups, group, 0)
    lax.fori_loop(n_groups * (ATTN_UNROLL // 2), n_blocks // 2, pair, 0)

    @pl.when(n_blocks % 2 == 1)
    def _():
        step(n_blocks - 1, 0, 1)

    for hh in range(ATTN_REP):
        o = acc_ref[hh] / jnp.sum(l_ref[hh], axis=0, keepdims=True)
        o_ref[0, hh * HEAD_DIM:(hh + 1) * HEAD_DIM, :] = o.astype(o_ref.dtype)


def _attn_call(qa, ka, vt, n_ctx, online):
    bsz, _, _, t = qa.shape
    kern = functools.partial(_attn_kernel, online=online, n_ctx_q=n_ctx // TQ, n_ctx_k=n_ctx // TK, n_k=t // TK)
    gw = ATTN_REP * HEAD_DIM
    scratch = [pltpu.VMEM((ATTN_REP, HEAD_DIM, TQ), F32),
               pltpu.VMEM((ATTN_REP, 8, TQ), F32),
               pltpu.VMEM((2, ATTN_REP, TK, TQ), F32 if online else BF16)]
    if online:
        scratch.append(pltpu.VMEM((ATTN_REP, 1, TQ), F32))
    return pl.pallas_call(
        kern,
        grid=(bsz, ATTN_KV_HEADS, t // TQ),
        in_specs=[pl.BlockSpec((1, ATTN_REP, QK_ROWS, TQ), lambda b, g, i: (b, g, 0, i)),
                  pl.BlockSpec((1, 1, t, QK_ROWS), lambda b, g, i: (b, g, 0, 0)),
                  pl.BlockSpec((1, 1, HEAD_DIM, t), lambda b, g, i: (b, g, 0, 0))],
        out_specs=pl.BlockSpec((1, gw, TQ), lambda b, g, i: (b, g, i)),
        out_shape=jax.ShapeDtypeStruct((bsz, ATTN_W, t), BF16),
        scratch_shapes=scratch,
        compiler_params=_cparams(("parallel", "parallel", "arbitrary")),
        name="attention_online" if online else "attention",
    )(qa, ka, vt)


def _attention(aq, ak, av, shift, n_ctx):
    bsz, t, _ = aq.shape
    qt = aq.transpose(0, 2, 1).reshape(bsz, ATTN_HEADS, HEAD_DIM, t)
    qa = jnp.concatenate([qt, jnp.broadcast_to((-shift).astype(BF16), (bsz, ATTN_HEADS, 1, t)),
                          jnp.zeros((bsz, ATTN_HEADS, QK_ROWS - HEAD_DIM - 1, t), BF16)], axis=2)
    k4 =ak.reshape(bsz, t, ATTN_KV_HEADS, HEAD_DIM).transpose(0, 2, 1, 3)
    ka = jnp.concatenate([k4, jnp.ones((bsz, ATTN_KV_HEADS, t, 1), BF16),
                          jnp.zeros((bsz, ATTN_KV_HEADS, t, QK_ROWS - HEAD_DIM - 1), BF16)], axis=3)
    vt = av.reshape(bsz, t, ATTN_KV_HEADS, HEAD_DIM).transpose(0, 2, 3, 1)
    return lax.cond(shift <= MAX_FIXED_SHIFT,
                    lambda: _attn_call(qa, ka, vt, n_ctx, False),
                    lambda: _attn_call(qa, ka, vt, n_ctx, True))


def _conv_kernel(x_ref, prev_ref, next_ref, w_ref, o_ref, *, nct, nb):
    i = pl.program_id(1)
    x = x_ref[0]
    rows = x.shape[0]
    left_edge = jnp.logical_or(i == 0, i == nct)
    right_edge = jnp.logical_or(i == nct - 1, i == nb - 1)
    prev_row = jnp.where(left_edge, 0.0, prev_ref[0, 7:8, :])
    next_row = jnp.where(right_edge, 0.0, next_ref[0, 0:1, :])
    ridx = lax.broadcasted_iota(jnp.int32, x.shape, 0)
    xm1 = jnp.where(ridx == 0, prev_row, pltpu.roll(x, 1, 0))
    xp1 = jnp.where(ridx == rows - 1, next_row, pltpu.roll(x, rows - 1, 0))
    w = w_ref[...]
    y = xm1 * w[0:1] + x * w[1:2] + xp1 * w[2:3]
    y = y * _sigmoid(y)
    lane = lax.broadcasted_iota(jnp.int32, x.shape, 1)
    y = jnp.where(lane >= ML_W, y * (ML_DK ** -0.5), y)
    o_ref[0] = y.astype(o_ref.dtype)


def _conv_call(mqk, w, nct):
    bsz, t, width = mqk.shape
    nb = t // SCAN_BLK
    per8 = SCAN_BLK // 8
    last8 = t // 8 - 1
    kern = functools.partial(_conv_kernel, nct=nct, nb=nb)
    return pl.pallas_call(
        kern,
        grid=(bsz, nb),
        in_specs=[pl.BlockSpec((1, SCAN_BLK, width), lambda b, i: (b, i, 0)),
                  pl.BlockSpec((1, 8, width), lambda b, i: (b, jnp.maximum(i * per8 - 1, 0), 0)),
                  pl.BlockSpec((1, 8, width), lambda b, i: (b, jnp.minimum((i + 1) * per8, last8), 0)),
                  pl.BlockSpec(w.shape, lambda b, i: (0, 0))],
        out_specs=pl.BlockSpec((1, SCAN_BLK, width), lambda b, i: (b, i, 0)),
        out_shape=jax.ShapeDtypeStruct((bsz, t, width), BF16),
        compiler_params=_cparams(("parallel", "parallel")),
        name="mlstm_qk_conv",
    )(mqk, mqk, mqk, w)


def _scan_block(d, j, nct, nb):
    bwd = jnp.where(j < nct, nct - 1 - j, nb - 1 - (j - nct))
    return jnp.where(d == 0, j, bwd)


def _hgrn_kernel(q0_ref, q1_ref, v0_ref, v1_ref, z0_ref, z1_ref, lb_ref, e_ref, bd_ref, o_ref,
                 st_ref, qd_ref, kd_ref, od_ref, dt_ref, u_ref):
    d = pl.program_id(1)
    j = pl.program_id(2)

    @pl.when(j == 0)
    def _():
        st_ref[...] = jnp.zeros(st_ref.shape, F32)

    lb = lb_ref[0]
    e = e_ref[...]
    bd = bd_ref[...]
    c = HG_CHUNK
    nch = SCAN_BLK // c

    def rows(i):
        return pl.ds(jnp.where(d == 0, i, c - 1 - i), nch, stride=c)

    def load2(r0, r1, sl):
        return jnp.concatenate([r0[0, sl, :], r1[0, sl, :]], axis=1)

    def load_halves(ref, sl):
        return jnp.concatenate([ref[0, sl, :], ref[1, sl, :]], axis=1)

    def store_halves(ref, sl, x):
        ref[0, sl, :] = x[:, :128]
        ref[1, sl, :] = x[:, 128:]

    qs, ks, vs, bs = [], [], [], []
    b = None
    for i in range(c):
        f = lb + (1.0 - lb) * _sigmoid(load2(z0_ref, z1_ref, rows(i)))
        lf = jnp.log2(f)
        b = lf if b is None else b + lf
        qs.append(load2(q0_ref, q1_ref, rows(i)))
        vs.append(load2(v0_ref, v1_ref, rows(i)))
        ks.append(1.0 - f)
        bs.append(b)
    b_tot = bs[-1]
    dt_ref[...] = jnp.exp2(b_tot)
    for i in range(c):
        store_halves(qd_ref, rows(i), qs[i] * jnp.exp2(bs[i]))
        store_halves(kd_ref, rows(i), ks[i] * jnp.exp2(b_tot - bs[i]))
    for cc in range(nch):
        sl = pl.ds(cc * c, c)
        vc = load2(v0_ref, v1_ref, sl).astype(BF16)
        kc = load_halves(kd_ref, sl).astype(BF16)
        u_ref[cc] = _dot_tn(vc, kc) * bd
    for i in range(c):
        ps = [qs[i] * ks[jj] * jnp.exp2(bs[i] - bs[jj]) for jj in range(i)] + [qs[i] * ks[i]]
        a = _dot(jnp.concatenate(ps, axis=0).astype(BF16), e)
        o = a[0:nch] * vs[0]
        for jj in range(1, i + 1):
            o = o + a[jj * nch:(jj + 1) * nch] * vs[jj]
        store_halves(od_ref, rows(i), o)

    for n in range(nch):
        cc = jnp.where(d == 0, n, nch - 1 - n)
        sl = pl.ds(pl.multiple_of(cc * c, c), c)
        st = st_ref[...]
        o_inter = _dot_nt(load_halves(qd_ref, sl).astype(BF16), st.astype(BF16))
        o_ref[0, 0, sl, :] = load_halves(od_ref, sl) + o_inter
        st_ref[...] = st * dt_ref[pl.ds(cc, 1), :] + u_ref[cc]


def _hgrn_call(hgp, lbs, e, bd, nct):
    bsz, t, _ = hgp.shape
    nb = t // SCAN_BLK
    w = HG_W
    blk = lambda d, j: _scan_block(d, j, nct, nb)
    half = lambda col: pl.BlockSpec((1, SCAN_BLK, 128), lambda b, d, j: (b, blk(d, j), col(d)))
    return pl.pallas_call(
        _hgrn_kernel,
        grid=(bsz, 2, nb),
        in_specs=[half(lambda d: 0), half(lambda d: 1),
                  half(lambda d: 2), half(lambda d: 3),
                  half(lambda d: 4 + 2 * d), half(lambda d: 5 + 2 * d),
                  pl.BlockSpec((1, 1, w), lambda b, d, j: (d, 0, 0)),
                  pl.BlockSpec((w, w), lambda b, d, j: (0, 0)),
                  pl.BlockSpec((w, w), lambda b, d, j: (0, 0))],
        out_specs=pl.BlockSpec((1, 1, SCAN_BLK, w), lambda b, d, j: (d, b, blk(d, j), 0)),
        out_shape=jax.ShapeDtypeStruct((2, bsz, t, w), F32),
        scratch_shapes=[pltpu.VMEM((w, w), F32),
                        pltpu.VMEM((2, SCAN_BLK, 128), F32),
                        pltpu.VMEM((2, SCAN_BLK, 128), F32),
                        pltpu.VMEM((2, SCAN_BLK, 128), F32),
                        pltpu.VMEM((SCAN_BLK // HG_CHUNK, w), F32),
                        pltpu.VMEM((SCAN_BLK // HG_CHUNK, w, w), F32)],
        compiler_params=_cparams(("parallel", "parallel", "arbitrary")),
        name="hgrn2_scan",
    )(hgp, hgp, hgp, hgp, hgp, hgp, lbs, e, bd)


def _mlstm_kernel(q_ref, k_ref, vt_ref, g_ref, gt_ref, m_ref, mt_ref, o_ref, c_ref, ms_ref):
    j = pl.program_id(2)

    @pl.when(j == 0)
    def _():
        c_ref[...] = jnp.zeros(c_ref.shape, F32)
        ms_ref[...] = jnp.zeros(ms_ref.shape, F32)

    length = SCAN_BLK
    mask = m_ref[0]
    mask_t = mt_ref[0]
    valid_t = mask_t > 0
    gates = g_ref[0, 0]
    gates_t = gt_ref[0, 0]
    b_col_all = _mask_dot(mask, _log_sigmoid(gates))
    lf_t = _log_sigmoid(gates_t)
    b_row_all = _dot_mask(lf_t, mask_t)
    tot_all = jnp.sum(lf_t, axis=1, keepdims=True)
    row = lax.broadcasted_iota(jnp.int32, (128, length), 0)

    heads = range(ML_HEADS)
    kh = [k_ref[0, :, hh * ML_DK:(hh + 1) * ML_DK] for hh in heads]
    qh = [q_ref[0, :, hh * ML_DK:(hh + 1) * ML_DK] for hh in heads]
    s_raw = [_dot_nt(kh[hh], qh[hh]) for hh in heads]
    inter = [_dot_nt(c_ref[hh].astype(BF16), qh[hh]) for hh in heads]
    for pair in range(ML_HEADS // 2):
        v2t = vt_ref[0, pair * 128:(pair + 1) * 128, :]
        out_t = jnp.zeros((128, length), F32)
        for sub in range(2):
            hh = pair * 2 + sub
            own = (row < 64) if sub == 0 else (row >= 64)
            ones_row = 64 if sub == 0 else 0
            v_aug_t = jnp.where(own, v2t, jnp.where(row == ones_row, 1.0, 0.0).astype(BF16))
            m_prev = ms_ref[hh, 0:1, 0:1]
            cs = gates[:, hh:hh + 1] - b_col_all[:, 4 + hh:5 + hh]
            br = b_row_all[4 + hh:5 + hh, :]
            tot = tot_all[4 + hh:5 + hh, :]
            log_d = jnp.where(valid_t, br + cs, NEG_BIG)
            m_t = jnp.maximum(br + m_prev, jnp.max(log_d, axis=0, keepdims=True))
            s_t = (s_raw[hh] * jnp.exp(log_d - m_t)).astype(BF16)
            r_t = jnp.exp(br + m_prev - m_t) * inter[hh] + _dot(v_aug_t, s_t)
            den = r_t[ones_row:ones_row + 1, :]
            out_t = jnp.where(own, r_t / jnp.maximum(jnp.abs(den), jnp.exp(-m_t)), out_t)
            log_e = tot + cs
            m_new = jnp.maximum(tot + m_prev, jnp.max(log_e, axis=0, keepdims=True))
            ke = (kh[hh].astype(F32) * jnp.exp(log_e - m_new)).astype(BF16)
            c_ref[hh] = jnp.exp(tot + m_prev - m_new) * c_ref[hh] + _dot(v_aug_t, ke)
            ms_ref[hh] = jnp.broadcast_to(m_new, ms_ref.shape[1:])
        o_ref[0, 0, :, pair * 128:(pair + 1) * 128] = out_t.T


def _mlstm_call(qk, vt, gcol, grow, mask, mask_t, nct):
    bsz, t, _ = qk.shape
    nb = t // SCAN_BLK
    w = ML_W
    blk = lambda d, j: _scan_block(d, j, nct, nb)
    return pl.pallas_call(
        _mlstm_kernel,
        grid=(bsz, 2, nb),
        in_specs=[pl.BlockSpec((1, SCAN_BLK, w), lambda b, d, j: (b, blk(d, j), 0)),
                  pl.BlockSpec((1, SCAN_BLK, w), lambda b, d, j: (b, blk(d, j), 1)),
                  pl.BlockSpec((1, w, SCAN_BLK), lambda b, d, j: (b, 0, blk(d, j))),
                  pl.BlockSpec((1, 1, SCAN_BLK, 128), lambda b, d, j: (b, d, blk(d, j), 0)),
                  pl.BlockSpec((1, 1, 8, SCAN_BLK), lambda b, d, j: (b, d, 0, blk(d, j))),
                  pl.BlockSpec((1, SCAN_BLK, SCAN_BLK), lambda b, d, j: (d, 0, 0)),
                  pl.BlockSpec((1, SCAN_BLK, SCAN_BLK), lambda b, d, j: (d, 0, 0))],
        out_specs=pl.BlockSpec((1, 1, SCAN_BLK, w), lambda b, d, j: (d, b, blk(d, j), 0)),
        out_shape=jax.ShapeDtypeStruct((2, bsz, t, w), F32),
        scratch_shapes=[pltpu.VMEM((ML_HEADS, 128, ML_DK), F32),
                        pltpu.VMEM((ML_HEADS, 8, 128), F32)],
        compiler_params=_cparams(("parallel", "parallel", "arbitrary")),
        name="mlstm_scan",
    )(qk, qk, vt, gcol, grow, mask, mask_t)


def _out_kernel(h_ref, mod_ref, at_ref, hf_ref, hb_ref, hgate_ref, gn_ref, e_ref, mf_ref, mb_ref, mgate_ref,
                wa_ref, wr_ref, wm_ref, o_ref):
    mod = mod_ref[0, 0]
    o = hf_ref[0, 0] + hb_ref[0, 0]
    ss = _dot((o * o).astype(BF16), e_ref[...])
    gate = hgate_ref[0]
    r = o * lax.rsqrt(ss + EPS) * gn_ref[...] * (gate * _sigmoid(gate))
    m = _sigmoid(mgate_ref[0]) * (mf_ref[0, 0] + mb_ref[0, 0])
    y = _dot_tn(at_ref[0], wa_ref[...]) + _dot(r.astype(BF16), wr_ref[...]) + _dot(m.astype(BF16), wm_ref[...])
    o_ref[0] = h_ref[0] + mod[2:3] * y


def _out_call(h, modsel, at, hg_o, hgp, gn, e, ml_o, mvo, wa, wr, wm, nct, skip):
    bsz, t, d = h.shape
    nt = t // TM - skip
    full = lambda a: pl.BlockSpec(a.shape, lambda bi, i: (0,) * a.ndim)
    hg_col = (OFF_MQK - OFF_HG) // HG_W - 1
    return pl.pallas_call(
        _out_kernel,
        grid=(bsz, nt),
        in_specs=[pl.BlockSpec((1, TM, d), lambda bi, i: (bi, i + skip, 0)),
                  pl.BlockSpec((1, 1, N_MOD, d), lambda bi, i: (bi, (i + skip >= nct).astype(jnp.int32), 0, 0)),
                  pl.BlockSpec((1, ATTN_W, TM), lambda bi, i: (bi, 0, i + skip)),
                  pl.BlockSpec((1, 1, TM, HG_W), lambda bi, i: (0, bi, i + skip, 0)),
                  pl.BlockSpec((1, 1, TM, HG_W), lambda bi, i: (1, bi, i + skip, 0)),
                  pl.BlockSpec((1, TM, HG_W), lambda bi, i: (bi, i + skip, hg_col)),
                  full(gn), full(e),
                  pl.BlockSpec((1, 1, TM, ML_W), lambda bi, i: (0, bi, i + skip, 0)),
                  pl.BlockSpec((1, 1, TM, ML_W), lambda bi, i: (1, bi, i + skip, 0)),
                  pl.BlockSpec((1, TM, ML_W), lambda bi, i: (bi, i + skip, 1)),
                  full(wa), full(wr), full(wm)],
        out_specs=pl.BlockSpec((1, TM, d), lambda bi, i: (bi, i, 0)),
        out_shape=jax.ShapeDtypeStruct((bsz, nt * TM, d), F32),
        compiler_params=_cparams(("parallel", "parallel")),
        name="readout_out_proj",
    )(h, modsel, at, hg_o, hg_o, hgp, gn, e, ml_o, ml_o, mvo, wa, wr, wm)


def _ffn_kernel(h_ref, mod_ref, g_ref, w1_ref, w2_ref, o_ref):
    x = h_ref[0]
    mod = mod_ref[0, 0]
    ms = jnp.mean(x * x, axis=-1, keepdims=True)
    y = x * lax.rsqrt(ms + EPS) * g_ref[...]
    u = (y * (1.0 + mod[4:5]) + mod[3:4]).astype(BF16)
    a = jnp.maximum(_dot(u, w1_ref[...]), 0.0)
    a = (a * a).astype(BF16)
    o_ref[0] = x + mod[5:6] * _dot(a, w2_ref[...])


def _ffn_call(h, modsel, g, w1, w2, nct):
    bsz, t, d = h.shape
    nt = t // TM
    full = lambda a: pl.BlockSpec(a.shape, lambda bi, i: (0,) * a.ndim, pipeline_mode=pl.Buffered(1))
    return pl.pallas_call(
        _ffn_kernel,
        grid=(bsz, nt),
        in_specs=[pl.BlockSpec((1, TM, d), lambda bi, i: (bi, i, 0)),
                  pl.BlockSpec((1, 1, N_MOD, d), lambda bi, i: (bi, (i >= nct).astype(jnp.int32), 0, 0)),
                  pl.BlockSpec(g.shape, lambda bi, i: (0, 0)),
                  full(w1), full(w2)],
        out_specs=pl.BlockSpec((1, TM, d), lambda bi, i: (bi, i, 0)),
        out_shape=jax.ShapeDtypeStruct((bsz, t, d), F32),
        compiler_params=_cparams(("parallel", "parallel")),
        name="ffn",
    )(h, modsel, g, w1, w2)


def _rope_tables(n_ctx, n_lat):
    inv_freq = ROPE_THETA ** (-np.arange(0, ROPE_AXIS_DIM, 2, dtype=np.float32) / ROPE_AXIS_DIM)
    inv_freq = jnp.asarray(inv_freq, F32)
    tok = jnp.arange(n_lat)
    ang_row = (tok // GRID_W).astype(F32)[:, None] * inv_freq[None, :]
    ang_col = (tok % GRID_W).astype(F32)[:, None] * inv_freq[None, :]
    ang = jnp.concatenate([ang_row, ang_row, ang_col, ang_col], axis=1)
    ang = jnp.concatenate([jnp.zeros((n_ctx, HEAD_DIM), F32), ang], axis=0)
    sign = np.where((np.arange(HEAD_DIM) % 32) < 16, -1.0, 1.0).astype(np.float32)
    cos = jnp.cos(ang)
    sin = jnp.sin(ang) * jnp.asarray(sign)[None, :]
    return jnp.concatenate([cos, cos], axis=1), jnp.concatenate([sin, sin], axis=1)


def _block_ones(width, head, scale, dtype):
    idx = np.arange(width) // head
    return jnp.asarray((idx[:, None] == idx[None, :]).astype(np.float32) * scale, dtype)


def _scan_masks(n):
    tril = np.tril(np.ones((n, n), np.float32))
    return np.stack([tril, tril.T])


def kernel(x, c, ctx, c_ctx, w_mod, b_mod, norm_mix, norm_ffn, w_in, b_in, q_norm, k_norm,
           hg_lb, hg_norm, ml_conv, w_out, w_ff1, w_ff2):
    bsz, n_lat, d = x.shape
    n_ctx = ctx.shape[1]
    depth = w_mod.shape[0]
    t = n_ctx + n_lat
    nct = n_ctx // TM
    assert n_ctx % SCAN_BLK == 0 and n_lat % SCAN_BLK == 0 and n_lat % GRID_W == 0 and bsz + 1 <= 8

    cvec = jnp.concatenate([c, c_ctx[None, :], jnp.zeros((8 - bsz - 1, d), F32)], axis=0)
    mods = _mod_call(cvec, w_mod, b_mod).reshape(depth, 8, N_MOD, d)
    modsel = jnp.stack([jnp.broadcast_to(mods[:, bsz][:, None], (depth, bsz, N_MOD, d)), mods[:, :bsz]], axis=2)

    cos, sin = _rope_tables(n_ctx, n_lat)
    e_attn = _block_ones(ATTN_W, HEAD_DIM, 1.0 / HEAD_DIM, BF16)
    e_hg_mean = _block_ones(HG_W, HG_DK, 1.0 / HG_DK, BF16)
    e_hg = _block_ones(HG_W, HG_DK, 1.0, BF16)
    bd_hg = _block_ones(HG_W, HG_DK, 1.0, F32)
    scan_mask = _scan_masks(SCAN_BLK)
    mask_ml = jnp.asarray(scan_mask, BF16)
    mask_ml_t = jnp.asarray(np.transpose(scan_mask, (0, 2, 1)).copy(), BF16)

    lbs = jnp.cumsum(jax.nn.softmax(hg_lb.astype(F32), axis=0), axis=0)
    lbs = (lbs - lbs[:1]).reshape(depth, 2, 1, HG_W)

    w_in_p = jnp.pad(w_in, ((0, 0), (0, 0), (0, N_IN_PAD - N_IN))).astype(BF16)
    b_in_p = jnp.pad(b_in, ((0, 0), (0, N_IN_PAD - N_IN))).reshape(depth, 1, N_IN_PAD)
    w_out_b = w_out.astype(BF16)
    w1_b = w_ff1.astype(BF16)
    w2_b = w_ff2.astype(BF16)
    gq = jnp.tile(q_norm, (1, ATTN_HEADS)).reshape(depth, 1, ATTN_W)
    gk = jnp.tile(k_norm, (1, ATTN_KV_HEADS)).reshape(depth, 1, KV_W)
    gn = jnp.tile(hg_norm, (1, HG_HEADS)).reshape(depth, 1, HG_W)

    h = jnp.concatenate([ctx, x], axis=1)
    for l in range(depth):
        last = l == depth - 1
        aq, ak, av, hgp, mqk, mvo, mg, mvt = _in_call(
            h, modsel[l], norm_mix[l].reshape(1, d), w_in_p[l], b_in_p[l], cos, sin, gq[l], gk[l], e_attn, nct)

        shift = (HEAD_DIM * Q_SCALE) * jnp.max(jnp.abs(q_norm[l])) * jnp.max(jnp.abs(k_norm[l]))
        at = _attention(aq, ak, av, shift, n_ctx)

        hg_o = _hgrn_call(hgp, lbs[l], e_hg, bd_hg, nct)

        qk = _conv_call(mqk, ml_conv[l], nct)
        g16 = mg[:, :, :16].reshape(bsz, t, 2, 2, ML_HEADS)
        gdir = g16.transpose(0, 3, 1, 2, 4).reshape(bsz, 2, t, 8)
        gcol = jnp.pad(gdir, ((0, 0), (0, 0), (0, 0), (0, 120)))
        grow = gdir.transpose(0, 1, 3, 2)
        ml_o = _mlstm_call(qk, mvt, gcol, grow, mask_ml, mask_ml_t, nct)

        skip = nct if last else 0
        h1 = _out_call(h, modsel[l], at, hg_o, hgp, gn[l], e_hg_mean, ml_o, mvo,
                       w_out_b[l, :ATTN_W], w_out_b[l, ATTN_W:ATTN_W + HG_W], w_out_b[l, ATTN_W + HG_W:], nct, skip)
        h = _ffn_call(h1, modsel[l], norm_ffn[l].reshape(1, d), w1_b[l], w2_b[l], 0 if last else nct)
    return h
```

```python
import functools

import jax
import jax.numpy as jnp
import numpy as np
from jax import lax
from jax.experimental import pallas as pl
from jax.experimental.pallas import tpu as pltpu

F32 = jnp.float32
BF16 = jnp.bfloat16

GRID_W = 64
HEAD_DIM = 64
ATTN_HEADS = 8
ATTN_KV_HEADS = 2
ATTN_REP = ATTN_HEADS // ATTN_KV_HEADS
ROPE_THETA = 10000.0
ROPE_AXIS_DIM = HEAD_DIM // 2
HG_HEADS = 4
HG_DK = 64
ML_HEADS = 4
ML_DK = 64
N_MOD = 6
EPS = 1e-6
ATTN_W = ATTN_HEADS * HEAD_DIM
KV_W = ATTN_KV_HEADS * HEAD_DIM
HG_W = HG_HEADS * HG_DK
ML_W = ML_HEADS * ML_DK

OFF_AQ, OFF_AK, OFF_AV = 0, 512, 640
OFF_HG = 768
OFF_MQK = 2048
OFF_MVO = 2560
OFF_MG = 3072
N_IN = 3088
N_IN_PAD = 3200

TM = 256
TQ = 256
TK = 256
ATTN_UNROLL = 32
SCAN_BLK = 256
HG_CHUNK = 16
FP8 = jnp.float8_e4m3fn
QK_ROWS = 256
SHIFT_ROWS = QK_ROWS - 3 * HEAD_DIM
FP8_TARGET = 128.0
MAX_FIXED_SHIFT = 32.0
NEG_BIG = -1e30
Q_SCALE = float(np.log2(np.e)) * HEAD_DIM ** -0.5
VMEM_LIMIT = 56 * 1024 * 1024


def _cparams(sem):
    return pltpu.CompilerParams(dimension_semantics=sem, vmem_limit_bytes=VMEM_LIMIT)


def _split3(x):
    hi = x.astype(BF16)
    r1 = x - hi.astype(F32)
    mid = r1.astype(BF16)
    lo = (r1 - mid.astype(F32)).astype(BF16)
    return hi, mid, lo


def _dot(a, b):
    return jnp.dot(a, b, preferred_element_type=F32)


def _dot_nt(a, b):
    return lax.dot_general(a, b, (((1,), (1,)), ((), ())), preferred_element_type=F32)


def _dot_tn(a, b):
    return lax.dot_general(a, b, (((0,), (0,)), ((), ())), preferred_element_type=F32)


def _mask_dot(mask_bf16, x):
    hi, mid, lo = _split3(x)
    return _dot(mask_bf16, hi) + _dot(mask_bf16, mid) + _dot(mask_bf16, lo)


def _dot_mask(x, mask_bf16):
    hi, mid, lo = _split3(x)
    return _dot(hi, mask_bf16) + _dot(mid, mask_bf16) + _dot(lo, mask_bf16)


def _sigmoid(x):
    return 1.0 / (1.0 + jnp.exp(-x))


def _log_sigmoid(x):
    return jnp.minimum(x, 0.0) - jnp.log(1.0 + jnp.exp(-jnp.abs(x)))


def _mod_kernel(c_ref, w_ref, b_ref, o_ref):
    cv = c_ref[...]
    a = cv * _sigmoid(cv)
    o_ref[0] = jnp.dot(a, w_ref[0], preferred_element_type=F32, precision=lax.Precision.HIGHEST) + b_ref[0]


def _mod_call(cvec, w_mod, b_mod):
    depth, d, nm = w_mod.shape
    tn = 1024
    return pl.pallas_call(
        _mod_kernel,
        grid=(depth, nm // tn),
        in_specs=[pl.BlockSpec((8, d), lambda l, j: (0, 0)),
                  pl.BlockSpec((1, d, tn), lambda l, j: (l, 0, j)),
                  pl.BlockSpec((1, 1, tn), lambda l, j: (l, 0, j))],
        out_specs=pl.BlockSpec((1, 8, tn), lambda l, j: (l, 0, j)),
        out_shape=jax.ShapeDtypeStruct((depth, 8, nm), F32),
        compiler_params=_cparams(("arbitrary", "arbitrary")),
        name="mod_vectors",
    )(cvec, w_mod, b_mod.reshape(depth, 1, nm))


def _head_rms(t, gain, e):
    ss = _dot((t * t).astype(BF16), e)
    return t * lax.rsqrt(ss + EPS) * gain


def _rope(t, cos, sin):
    w = t.shape[1]
    reps = w // 128
    cosw = jnp.concatenate([cos] * reps, axis=1) if reps > 1 else cos
    sinw = jnp.concatenate([sin] * reps, axis=1) if reps > 1 else sin
    lane = lax.broadcasted_iota(jnp.int32, t.shape, 1)
    first_half = (lane % 32) < 16
    partner = jnp.where(first_half, pltpu.roll(t, w - 16, 1), pltpu.roll(t, 16, 1))
    return t * cosw + partner * sinw


def _in_kernel(h_ref, mod_ref, g_ref, w_ref, b_ref, cos_ref, sin_ref, gq_ref, gk_ref, e_ref,
               aq_ref, ak_ref, av_ref, hg_ref, mqk_ref, mvo_ref, mg_ref, mvt_ref):
    x = h_ref[0]
    ms = jnp.mean(x * x, axis=-1, keepdims=True)
    y = x * lax.rsqrt(ms + EPS) * g_ref[...]
    mod = mod_ref[0, 0]
    u = y * (1.0 + mod[1:2]) + mod[0:1]
    p = _dot(u.astype(BF16), w_ref[...]) + b_ref[...]
    cos = cos_ref[...]
    sin = sin_ref[...]
    e = e_ref[...]
    q = _rope(_head_rms(p[:, OFF_AQ:OFF_AQ + ATTN_W], gq_ref[...], e), cos, sin)
    k = _rope(_head_rms(p[:, OFF_AK:OFF_AK + KV_W], gk_ref[...], e[:KV_W, :KV_W]), cos, sin)
    aq_ref[0] = (q * Q_SCALE).astype(BF16)
    ak_ref[0] = k.astype(BF16)
    av_ref[0] = p[:, OFF_AV:OFF_AV + KV_W].astype(BF16)
    hg_ref[0] = p[:, OFF_HG:OFF_MQK]
    mqk_ref[0] = p[:, OFF_MQK:OFF_MVO]
    mvo_ref[0] = p[:, OFF_MVO:OFF_MG]
    mg_ref[0] = p[:, OFF_MG:OFF_MG + 128]
    mvt_ref[0] = p[:, OFF_MVO:OFF_MVO + ML_W].T.astype(BF16)


def _in_call(h, modsel, g, w, b, cos, sin, gq, gk, e, nct):
    bsz, t, d = h.shape
    nt = t // TM
    row = lambda width: pl.BlockSpec((1, TM, width), lambda bi, i: (bi, i, 0))
    full = lambda a: pl.BlockSpec(a.shape, lambda bi, i: (0,) * a.ndim)
    out_widths = (ATTN_W, KV_W, KV_W, OFF_MQK - OFF_HG, 2 * ML_W, 2 * ML_W, 128)
    out_dtypes = (BF16, BF16, BF16, F32, F32, F32, F32)
    return pl.pallas_call(
        _in_kernel,
        grid=(bsz, nt),
        in_specs=[row(d),
                  pl.BlockSpec((1, 1, N_MOD, d), lambda bi, i: (bi, (i >= nct).astype(jnp.int32), 0, 0)),
                  full(g), full(w), full(b),
                  pl.BlockSpec((TM, 128), lambda bi, i: (i, 0)),
                  pl.BlockSpec((TM, 128), lambda bi, i: (i, 0)),
                  full(gq), full(gk), full(e)],
        out_specs=[row(wd) for wd in out_widths] + [pl.BlockSpec((1, ML_W, TM), lambda bi, i: (bi, 0, i))],
        out_shape=[jax.ShapeDtypeStruct((bsz, t, wd), dt) for wd, dt in zip(out_widths, out_dtypes)]
        + [jax.ShapeDtypeStruct((bsz, ML_W, t), BF16)],
        compiler_params=_cparams(("parallel", "parallel")),
        name="norm_mod_in_proj",
    )(h, modsel, g, w, b, cos, sin, gq, gk, e)


def _attn_kernel(qa_ref, ka_ref, vt_ref, unscale_ref, o_ref, acc_ref, l_ref, s_ref, *maybe_m_ref,
                 online, n_ctx_q, n_ctx_k, n_k):
    qi = pl.program_id(2)
    unscale = unscale_ref[...]
    acc_ref[...] = jnp.zeros(acc_ref.shape, F32)
    l_ref[...] = jnp.zeros(l_ref.shape, F32)
    if online:
        m_ref, = maybe_m_ref
        m_ref[...] = jnp.full(m_ref.shape, NEG_BIG, F32)
    n_blocks = jnp.where(qi < n_ctx_q, n_ctx_k, n_k)

    def key_tile(kb):
        return ka_ref[0, 0, pl.ds(pl.multiple_of(kb * TK, TK), TK), :]

    def key_sums(p):
        return jnp.sum(p.reshape(TK // 8, 8, TQ), axis=0)

    def produce(kt, slot, hh, counted):
        s = _dot_nt(kt, qa_ref[0, hh]) * unscale
        if online:
            s_ref[slot, hh] = s
        else:
            p = jnp.exp2(s)
            l_ref[hh] += key_sums(p) if counted is True else jnp.where(counted, key_sums(p), 0.0)
            s_ref[slot, hh] = p.astype(BF16)

    def step(kb, cur, nxt):
        vt = vt_ref[0, 0, :, pl.ds(pl.multiple_of(kb * TK, TK), TK)]
        kt = key_tile(jnp.minimum(kb + 1, n_blocks - 1))
        for hh in range(ATTN_REP):
            produce(kt, nxt, hh, kb + 1 < n_blocks)
            if online:
                s = s_ref[cur, hh]
                m_old = m_ref[hh]
                m_new = jnp.maximum(m_old, jnp.max(s, axis=0, keepdims=True))
                p = jnp.exp2(s - m_new)
                alpha = jnp.exp2(m_old - m_new)
                l_ref[hh] = alpha * l_ref[hh] + key_sums(p)
                acc_ref[hh] = alpha * acc_ref[hh] + _dot(vt, p.astype(BF16))
                m_ref[hh] = m_new
            else:
                acc_ref[hh] += _dot(vt, s_ref[cur, hh])

    kt0 = key_tile(0)
    for hh in range(ATTN_REP):
        produce(kt0, 0, hh, True)

    def group(i, carry):
        for u in range(ATTN_UNROLL):
            step(ATTN_UNROLL * i + u, u % 2, (u + 1) % 2)
        return carry

    def pair(i, carry):
        step(2 * i, 0, 1)
        step(2 * i + 1, 1, 0)
        return carry

    n_groups = n_blocks // ATTN_UNROLL
    lax.fori_loop(0, n_groups, group, 0)
    lax.fori_loop(n_groups * (ATTN_UNROLL // 2), n_blocks // 2, pair, 0)

    @pl.when(n_blocks % 2 == 1)
    def _():
        step(n_blocks - 1, 0, 1)

    for hh in range(ATTN_REP):
        o = acc_ref[hh] / jnp.sum(l_ref[hh], axis=0, keepdims=True)
        o_ref[0, hh * HEAD_DIM:(hh + 1) * HEAD_DIM, :] = o.astype(o_ref.dtype)


def _attn_call(qa, ka, vt, unscale, n_ctx, online):
    bsz, _, t, _ = qa.shape
    kern = functools.partial(_attn_kernel, online=online, n_ctx_q=n_ctx // TQ, n_ctx_k=n_ctx // TK, n_k=t // TK)
    gw = ATTN_REP * HEAD_DIM
    scratch = [pltpu.VMEM((ATTN_REP, HEAD_DIM, TQ), F32),
               pltpu.VMEM((ATTN_REP, 8, TQ), F32),
               pltpu.VMEM((2, ATTN_REP, TK, TQ), F32 if online else BF16)]
    if online:
        scratch.append(pltpu.VMEM((ATTN_REP, 1, TQ), F32))
    return pl.pallas_call(
        kern,
        grid=(bsz, ATTN_KV_HEADS, t // TQ),
        in_specs=[pl.BlockSpec((1, ATTN_REP, TQ, QK_ROWS), lambda b, g, i: (b, g, i, 0)),
                  pl.BlockSpec((1, 1, t, QK_ROWS), lambda b, g, i: (b, g, 0, 0)),
                  pl.BlockSpec((1, 1, HEAD_DIM, t), lambda b, g, i: (b, g, 0, 0)),
                  pl.BlockSpec((1, 1), lambda b, g, i: (0, 0))],
        out_specs=pl.BlockSpec((1, gw, TQ), lambda b, g, i: (b, g, i)),
        out_shape=jax.ShapeDtypeStruct((bsz, ATTN_W, t), BF16),
        scratch_shapes=scratch,
        compiler_params=_cparams(("parallel", "parallel", "arbitrary")),
        name="attention_online" if online else "attention",
    )(qa, ka, vt, unscale)


def _attention(aq, ak, av, q_gain, k_gain, n_ctx):
    bsz, t, _ = aq.shape
    qmax = jnp.maximum(jnp.sqrt(float(HEAD_DIM)) * Q_SCALE * jnp.max(jnp.abs(q_gain)), 1e-6)
    kmax = jnp.maximum(jnp.sqrt(float(HEAD_DIM)) * jnp.max(jnp.abs(k_gain)), 1e-6)
    shift = qmax * kmax
    q_scale = jnp.exp2(jnp.floor(jnp.log2(FP8_TARGET / qmax)))
    k_scale = jnp.exp2(jnp.floor(jnp.log2(FP8_TARGET / kmax)))

    def split(x, scale):
        xs = x.astype(F32) * scale
        hi = xs.astype(FP8)
        return hi, (xs - hi.astype(F32)).astype(FP8)

    q1, q2 = split(aq.reshape(bsz, t, ATTN_HEADS, HEAD_DIM).transpose(0, 2, 1, 3), q_scale)
    r = (jnp.minimum(shift, MAX_FIXED_SHIFT) * q_scale * k_scale * (1.07 / SHIFT_ROWS)).astype(FP8)
    qa = jnp.concatenate([q1, q1, q2, jnp.broadcast_to(-r, (bsz, ATTN_HEADS, t, SHIFT_ROWS))], axis=3)
    k1, k2 = split(ak.reshape(bsz, t, ATTN_KV_HEADS, HEAD_DIM).transpose(0, 2, 1, 3), k_scale)
    ka = jnp.concatenate([k1, k2, k1, jnp.ones((bsz, ATTN_KV_HEADS, t, SHIFT_ROWS), FP8)], axis=3)
    vt = av.reshape(bsz, t, ATTN_KV_HEADS, HEAD_DIM).transpose(0, 2, 3, 1)
    unscale = (1.0 / (q_scale * k_scale)).reshape(1, 1)
    return lax.cond(shift <= MAX_FIXED_SHIFT,
                    lambda: _attn_call(qa, ka, vt, unscale, n_ctx, False),
                    lambda: _attn_call(qa, ka, vt, unscale, n_ctx, True))


def _conv_kernel(x_ref, prev_ref, next_ref, w_ref, o_ref, *, nct, nb):
    i = pl.program_id(1)
    x = x_ref[0]
    rows = x.shape[0]
    left_edge = jnp.logical_or(i == 0, i == nct)
    right_edge = jnp.logical_or(i == nct - 1, i == nb - 1)
    prev_row = jnp.where(left_edge, 0.0, prev_ref[0, 7:8, :])
    next_row = jnp.where(right_edge, 0.0, next_ref[0, 0:1, :])
    ridx = lax.broadcasted_iota(jnp.int32, x.shape, 0)
    xm1 = jnp.where(ridx == 0, prev_row, pltpu.roll(x, 1, 0))
    xp1 = jnp.where(ridx == rows - 1, next_row, pltpu.roll(x, rows - 1, 0))
    w = w_ref[...]
    y = xm1 * w[0:1] + x * w[1:2] + xp1 * w[2:3]
    y = y * _sigmoid(y)
    lane = lax.broadcasted_iota(jnp.int32, x.shape, 1)
    y = jnp.where(lane >= ML_W, y * (ML_DK ** -0.5), y)
    o_ref[0] = y.astype(o_ref.dtype)


def _conv_call(mqk, w, nct):
    bsz, t, width = mqk.shape
    nb = t // SCAN_BLK
    per8 = SCAN_BLK // 8
    last8 = t // 8 - 1
    kern = functools.partial(_conv_kernel, nct=nct, nb=nb)
    return pl.pallas_call(
        kern,
        grid=(bsz, nb),
        in_specs=[pl.BlockSpec((1, SCAN_BLK, width), lambda b, i: (b, i, 0)),
                  pl.BlockSpec((1, 8, width), lambda b, i: (b, jnp.maximum(i * per8 - 1, 0), 0)),
                  pl.BlockSpec((1, 8, width), lambda b, i: (b, jnp.minimum((i + 1) * per8, last8), 0)),
                  pl.BlockSpec(w.shape, lambda b, i: (0, 0))],
        out_specs=pl.BlockSpec((1, SCAN_BLK, width), lambda b, i: (b, i, 0)),
        out_shape=jax.ShapeDtypeStruct((bsz, t, width), BF16),
        compiler_params=_cparams(("parallel", "parallel")),
        name="mlstm_qk_conv",
    )(mqk, mqk, mqk, w)


def _scan_block(d, j, nct, nb):
    bwd = jnp.where(j < nct, nct - 1 - j, nb - 1 - (j - nct))
    return jnp.where(d == 0, j, bwd)


def _hgrn_kernel(q0_ref, q1_ref, v0_ref, v1_ref, z0_ref, z1_ref, lb_ref, e_ref, bd_ref, o_ref,
                 st_ref, qd_ref, kd_ref, od_ref, dt_ref, u_ref):
    d = pl.program_id(1)
    j = pl.program_id(2)

    @pl.when(j == 0)
    def _():
        st_ref[...] = jnp.zeros(st_ref.shape, F32)

    lb = lb_ref[0]
    e = e_ref[...]
    bd = bd_ref[...]
    c = HG_CHUNK
    nch = SCAN_BLK // c

    def rows(i):
        return pl.ds(jnp.where(d == 0, i, c - 1 - i), nch, stride=c)

    def load2(r0, r1, sl):
        return jnp.concatenate([r0[0, sl, :], r1[0, sl, :]], axis=1)

    def load_halves(ref, sl):
        return jnp.concatenate([ref[0, sl, :], ref[1, sl, :]], axis=1)

    def store_halves(ref, sl, x):
        ref[0, sl, :] = x[:, :128]
        ref[1, sl, :] = x[:, 128:]

    qs, ks, vs, bs = [], [], [], []
    b = None
    for i in range(c):
        f = lb + (1.0 - lb) * _sigmoid(load2(z0_ref, z1_ref, rows(i)))
        lf = jnp.log2(f)
        b = lf if b is None else b + lf
        qs.append(load2(q0_ref, q1_ref, rows(i)))
        vs.append(load2(v0_ref, v1_ref, rows(i)))
        ks.append(1.0 - f)
        bs.append(b)
    b_tot = bs[-1]
    dt_ref[...] = jnp.exp2(b_tot)
    for i in range(c):
        store_halves(qd_ref, rows(i), qs[i] * jnp.exp2(bs[i]))
        store_halves(kd_ref, rows(i), ks[i] * jnp.exp2(b_tot - bs[i]))
    for cc in range(nch):
        sl = pl.ds(cc * c, c)
        vc = load2(v0_ref, v1_ref, sl).astype(BF16)
        kc = load_halves(kd_ref, sl).astype(BF16)
        u_ref[cc] = _dot_tn(vc, kc) * bd
    for i in range(c):
        ps = [qs[i] * ks[jj] * jnp.exp2(bs[i] - bs[jj]) for jj in range(i)] + [qs[i] * ks[i]]
        a = _dot(jnp.concatenate(ps, axis=0).astype(BF16), e)
        o = a[0:nch] * vs[0]
        for jj in range(1, i + 1):
            o = o + a[jj * nch:(jj + 1) * nch] * vs[jj]
        store_halves(od_ref, rows(i), o)

    for n in range(nch):
        cc = jnp.where(d == 0, n, nch - 1 - n)
        sl = pl.ds(pl.multiple_of(cc * c, c), c)
        st = st_ref[...]
        o_inter = _dot_nt(load_halves(qd_ref, sl).astype(BF16), st.astype(BF16))
        o_ref[0, 0, sl, :] = load_halves(od_ref, sl) + o_inter
        st_ref[...] = st * dt_ref[pl.ds(cc, 1), :] + u_ref[cc]


def _hgrn_call(hgp, lbs, e, bd, nct):
    bsz, t, _ = hgp.shape
    nb = t // SCAN_BLK
    w = HG_W
    blk = lambda d, j: _scan_block(d, j, nct, nb)
    half = lambda col: pl.BlockSpec((1, SCAN_BLK, 128), lambda b, d, j: (b, blk(d, j), col(d)))
    return pl.pallas_call(
        _hgrn_kernel,
        grid=(bsz, 2, nb),
        in_specs=[half(lambda d: 0), half(lambda d: 1),
                  half(lambda d: 2), half(lambda d: 3),
                  half(lambda d: 4 + 2 * d), half(lambda d: 5 + 2 * d),
                  pl.BlockSpec((1, 1, w), lambda b, d, j: (d, 0, 0)),
                  pl.BlockSpec((w, w), lambda b, d, j: (0, 0)),
                  pl.BlockSpec((w, w), lambda b, d, j: (0, 0))],
        out_specs=pl.BlockSpec((1, 1, SCAN_BLK, w), lambda b, d, j: (d, b, blk(d, j), 0)),
        out_shape=jax.ShapeDtypeStruct((2, bsz, t, w), F32),
        scratch_shapes=[pltpu.VMEM((w, w), F32),
                        pltpu.VMEM((2, SCAN_BLK, 128), F32),
                        pltpu.VMEM((2, SCAN_BLK, 128), F32),
                        pltpu.VMEM((2, SCAN_BLK, 128), F32),
                        pltpu.VMEM((SCAN_BLK // HG_CHUNK, w), F32),
                        pltpu.VMEM((SCAN_BLK // HG_CHUNK, w, w), F32)],
        compiler_params=_cparams(("parallel", "parallel", "arbitrary")),
        name="hgrn2_scan",
    )(hgp, hgp, hgp, hgp, hgp, hgp, lbs, e, bd)


def _mlstm_kernel(q_ref, k_ref, vt_ref, g_ref, gt_ref, m_ref, mt_ref, o_ref, c_ref, ms_ref):
    j = pl.program_id(2)

    @pl.when(j == 0)
    def _():
        c_ref[...] = jnp.zeros(c_ref.shape, F32)
        ms_ref[...] = jnp.zeros(ms_ref.shape, F32)

    length = SCAN_BLK
    mask = m_ref[0]
    mask_t = mt_ref[0]
    valid_t = mask_t > 0
    gates = g_ref[0, 0]
    gates_t = gt_ref[0, 0]
    b_col_all = _mask_dot(mask, _log_sigmoid(gates))
    lf_t = _log_sigmoid(gates_t)
    b_row_all = _dot_mask(lf_t, mask_t)
    tot_all = jnp.sum(lf_t, axis=1, keepdims=True)
    row = lax.broadcasted_iota(jnp.int32, (128, length), 0)

    heads = range(ML_HEADS)
    kh = [k_ref[0, :, hh * ML_DK:(hh + 1) * ML_DK] for hh in heads]
    qh = [q_ref[0, :, hh * ML_DK:(hh + 1) * ML_DK] for hh in heads]
    s_raw = [_dot_nt(kh[hh], qh[hh]) for hh in heads]
    inter = [_dot_nt(c_ref[hh].astype(BF16), qh[hh]) for hh in heads]
    for pair in range(ML_HEADS // 2):
        v2t = vt_ref[0, pair * 128:(pair + 1) * 128, :]
        out_t = jnp.zeros((128, length), F32)
        for sub in range(2):
            hh = pair * 2 + sub
            own = (row < 64) if sub == 0 else (row >= 64)
            ones_row = 64 if sub == 0 else 0
            v_aug_t = jnp.where(own, v2t, jnp.where(row == ones_row, 1.0, 0.0).astype(BF16))
            m_prev = ms_ref[hh, 0:1, 0:1]
            cs = gates[:, hh:hh + 1] - b_col_all[:, 4 + hh:5 + hh]
            br = b_row_all[4 + hh:5 + hh, :]
            tot = tot_all[4 + hh:5 + hh, :]
            log_d = jnp.where(valid_t, br + cs, NEG_BIG)
            m_t = jnp.maximum(br + m_prev, jnp.max(log_d, axis=0, keepdims=True))
            s_t = (s_raw[hh] * jnp.exp(log_d - m_t)).astype(BF16)
            r_t = jnp.exp(br + m_prev - m_t) * inter[hh] + _dot(v_aug_t, s_t)
            den = r_t[ones_row:ones_row + 1, :]
            out_t = jnp.where(own, r_t / jnp.maximum(jnp.abs(den), jnp.exp(-m_t)), out_t)
            log_e = tot + cs
            m_new = jnp.maximum(tot + m_prev, jnp.max(log_e, axis=0, keepdims=True))
            ke = (kh[hh].astype(F32) * jnp.exp(log_e - m_new)).astype(BF16)
            c_ref[hh] = jnp.exp(tot + m_prev - m_new) * c_ref[hh] + _dot(v_aug_t, ke)
            ms_ref[hh] = jnp.broadcast_to(m_new, ms_ref.shape[1:])
        o_ref[0, 0, :, pair * 128:(pair + 1) * 128] = out_t.T


def _mlstm_call(qk, vt, gcol, grow, mask, mask_t, nct):
    bsz, t, _ = qk.shape
    nb = t // SCAN_BLK
    w = ML_W
    blk = lambda d, j: _scan_block(d, j, nct, nb)
    return pl.pallas_call(
        _mlstm_kernel,
        grid=(bsz, 2, nb),
        in_specs=[pl.BlockSpec((1, SCAN_BLK, w), lambda b, d, j: (b, blk(d, j), 0)),
                  pl.BlockSpec((1, SCAN_BLK, w), lambda b, d, j: (b, blk(d, j), 1)),
                  pl.BlockSpec((1, w, SCAN_BLK), lambda b, d, j: (b, 0, blk(d, j))),
                  pl.BlockSpec((1, 1, SCAN_BLK, 128), lambda b, d, j: (b, d, blk(d, j), 0)),
                  pl.BlockSpec((1, 1, 8, SCAN_BLK), lambda b, d, j: (b, d, 0, blk(d, j))),
                  pl.BlockSpec((1, SCAN_BLK, SCAN_BLK), lambda b, d, j: (d, 0, 0)),
                  pl.BlockSpec((1, SCAN_BLK, SCAN_BLK), lambda b, d, j: (d, 0, 0))],
        out_specs=pl.BlockSpec((1, 1, SCAN_BLK, w), lambda b, d, j: (d, b, blk(d, j), 0)),
        out_shape=jax.ShapeDtypeStruct((2, bsz, t, w), F32),
        scratch_shapes=[pltpu.VMEM((ML_HEADS, 128, ML_DK), F32),
                        pltpu.VMEM((ML_HEADS, 8, 128), F32)],
        compiler_params=_cparams(("parallel", "parallel", "arbitrary")),
        name="mlstm_scan",
    )(qk, qk, vt, gcol, grow, mask, mask_t)


def _out_kernel(h_ref, mod_ref, at_ref, hf_ref, hb_ref, hgate_ref, gn_ref, e_ref, mf_ref, mb_ref, mgate_ref,
                wa_ref, wr_ref, wm_ref, o_ref):
    mod = mod_ref[0, 0]
    o = hf_ref[0, 0] + hb_ref[0, 0]
    ss = _dot((o * o).astype(BF16), e_ref[...])
    gate = hgate_ref[0]
    r = o * lax.rsqrt(ss + EPS) * gn_ref[...] * (gate * _sigmoid(gate))
    m = _sigmoid(mgate_ref[0]) * (mf_ref[0, 0] + mb_ref[0, 0])
    y = _dot_tn(at_ref[0], wa_ref[...]) + _dot(r.astype(BF16), wr_ref[...]) + _dot(m.astype(BF16), wm_ref[...])
    o_ref[0] = h_ref[0] + mod[2:3] * y


def _out_call(h, modsel, at, hg_o, hgp, gn, e, ml_o, mvo, wa, wr, wm, nct, skip):
    bsz, t, d = h.shape
    nt = t // TM - skip
    full = lambda a: pl.BlockSpec(a.shape, lambda bi, i: (0,) * a.ndim)
    hg_col = (OFF_MQK - OFF_HG) // HG_W - 1
    return pl.pallas_call(
        _out_kernel,
        grid=(bsz, nt),
        in_specs=[pl.BlockSpec((1, TM, d), lambda bi, i: (bi, i + skip, 0)),
                  pl.BlockSpec((1, 1, N_MOD, d), lambda bi, i: (bi, (i + skip >= nct).astype(jnp.int32), 0, 0)),
                  pl.BlockSpec((1, ATTN_W, TM), lambda bi, i: (bi, 0, i + skip)),
                  pl.BlockSpec((1, 1, TM, HG_W), lambda bi, i: (0, bi, i + skip, 0)),
                  pl.BlockSpec((1, 1, TM, HG_W), lambda bi, i: (1, bi, i + skip, 0)),
                  pl.BlockSpec((1, TM, HG_W), lambda bi, i: (bi, i + skip, hg_col)),
                  full(gn), full(e),
                  pl.BlockSpec((1, 1, TM, ML_W), lambda bi, i: (0, bi, i + skip, 0)),
                  pl.BlockSpec((1, 1, TM, ML_W), lambda bi, i: (1, bi, i + skip, 0)),
                  pl.BlockSpec((1, TM, ML_W), lambda bi, i: (bi, i + skip, 1)),
                  full(wa), full(wr), full(wm)],
        out_specs=pl.BlockSpec((1, TM, d), lambda bi, i: (bi, i, 0)),
        out_shape=jax.ShapeDtypeStruct((bsz, nt * TM, d), F32),
        compiler_params=_cparams(("parallel", "parallel")),
        name="readout_out_proj",
    )(h, modsel, at, hg_o, hg_o, hgp, gn, e, ml_o, ml_o, mvo, wa, wr, wm)


def _ffn_kernel(h_ref, mod_ref, g_ref, w1_ref, w2_ref, o_ref):
    x = h_ref[0]
    mod = mod_ref[0, 0]
    ms = jnp.mean(x * x, axis=-1, keepdims=True)
    y = x * lax.rsqrt(ms + EPS) * g_ref[...]
    u = (y * (1.0 + mod[4:5]) + mod[3:4]).astype(BF16)
    a = jnp.maximum(_dot(u, w1_ref[...]), 0.0)
    a = (a * a).astype(BF16)
    o_ref[0] = x + mod[5:6] * _dot(a, w2_ref[...])


def _ffn_call(h, modsel, g, w1, w2, nct):
    bsz, t, d = h.shape
    nt = t // TM
    full = lambda a: pl.BlockSpec(a.shape, lambda bi, i: (0,) * a.ndim, pipeline_mode=pl.Buffered(1))
    return pl.pallas_call(
        _ffn_kernel,
        grid=(bsz, nt),
        in_specs=[pl.BlockSpec((1, TM, d), lambda bi, i: (bi, i, 0)),
                  pl.BlockSpec((1, 1, N_MOD, d), lambda bi, i: (bi, (i >= nct).astype(jnp.int32), 0, 0)),
                  pl.BlockSpec(g.shape, lambda bi, i: (0, 0)),
                  full(w1), full(w2)],
        out_specs=pl.BlockSpec((1, TM, d), lambda bi, i: (bi, i, 0)),
        out_shape=jax.ShapeDtypeStruct((bsz, t, d), F32),
        compiler_params=_cparams(("parallel", "parallel")),
        name="ffn",
    )(h, modsel, g, w1, w2)


def _rope_tables(n_ctx, n_lat):
    inv_freq = ROPE_THETA ** (-np.arange(0, ROPE_AXIS_DIM, 2, dtype=np.float32) / ROPE_AXIS_DIM)
    inv_freq = jnp.asarray(inv_freq, F32)
    tok = jnp.arange(n_lat)
    ang_row = (tok // GRID_W).astype(F32)[:, None] * inv_freq[None, :]
    ang_col = (tok % GRID_W).astype(F32)[:, None] * inv_freq[None, :]
    ang = jnp.concatenate([ang_row, ang_row, ang_col, ang_col], axis=1)
    ang = jnp.concatenate([jnp.zeros((n_ctx, HEAD_DIM), F32), ang], axis=0)
    sign = np.where((np.arange(HEAD_DIM) % 32) < 16, -1.0, 1.0).astype(np.float32)
    cos = jnp.cos(ang)
    sin = jnp.sin(ang) * jnp.asarray(sign)[None, :]
    return jnp.concatenate([cos, cos], axis=1), jnp.concatenate([sin, sin], axis=1)


def _block_ones(width, head, scale, dtype):
    idx = np.arange(width) // head
    return jnp.asarray((idx[:, None] == idx[None, :]).astype(np.float32) * scale, dtype)


def _scan_masks(n):
    tril = np.tril(np.ones((n, n), np.float32))
    return np.stack([tril, tril.T])


def kernel(x, c, ctx, c_ctx, w_mod, b_mod, norm_mix, norm_ffn, w_in, b_in, q_norm, k_norm,
           hg_lb, hg_norm, ml_conv, w_out, w_ff1, w_ff2):
    bsz, n_lat, d = x.shape
    n_ctx = ctx.shape[1]
    depth = w_mod.shape[0]
    t = n_ctx + n_lat
    nct = n_ctx // TM
    assert n_ctx % SCAN_BLK == 0 and n_lat % SCAN_BLK == 0 and n_lat % GRID_W == 0 and bsz + 1 <= 8

    cvec = jnp.concatenate([c, c_ctx[None, :], jnp.zeros((8 - bsz - 1, d), F32)], axis=0)
    mods = _mod_call(cvec, w_mod, b_mod).reshape(depth, 8, N_MOD, d)
    modsel = jnp.stack([jnp.broadcast_to(mods[:, bsz][:, None], (depth, bsz, N_MOD, d)), mods[:, :bsz]], axis=2)

    cos, sin = _rope_tables(n_ctx, n_lat)
    e_attn = _block_ones(ATTN_W, HEAD_DIM, 1.0 / HEAD_DIM, BF16)
    e_hg_mean = _block_ones(HG_W, HG_DK, 1.0 / HG_DK, BF16)
    e_hg = _block_ones(HG_W, HG_DK, 1.0, BF16)
    bd_hg = _block_ones(HG_W, HG_DK, 1.0, F32)
    scan_mask = _scan_masks(SCAN_BLK)
    mask_ml = jnp.asarray(scan_mask, BF16)
    mask_ml_t = jnp.asarray(np.transpose(scan_mask, (0, 2, 1)).copy(), BF16)

    lbs = jnp.cumsum(jax.nn.softmax(hg_lb.astype(F32), axis=0), axis=0)
    lbs = (lbs - lbs[:1]).reshape(depth, 2, 1, HG_W)

    w_in_p = jnp.pad(w_in, ((0, 0), (0, 0), (0, N_IN_PAD - N_IN))).astype(BF16)
    b_in_p = jnp.pad(b_in, ((0, 0), (0, N_IN_PAD - N_IN))).reshape(depth, 1, N_IN_PAD)
    w_out_b = w_out.astype(BF16)
    w1_b = w_ff1.astype(BF16)
    w2_b = w_ff2.astype(BF16)
    gq = jnp.tile(q_norm, (1, ATTN_HEADS)).reshape(depth, 1, ATTN_W)
    gk = jnp.tile(k_norm, (1, ATTN_KV_HEADS)).reshape(depth, 1, KV_W)
    gn = jnp.tile(hg_norm, (1, HG_HEADS)).reshape(depth, 1, HG_W)

    h = jnp.concatenate([ctx, x], axis=1)
    for l in range(depth):
        last = l == depth - 1
        aq, ak, av, hgp, mqk, mvo, mg, mvt = _in_call(
            h, modsel[l], norm_mix[l].reshape(1, d), w_in_p[l], b_in_p[l], cos, sin, gq[l], gk[l], e_attn, nct)

        at = _attention(aq, ak, av, q_norm[l], k_norm[l], n_ctx)

        hg_o = _hgrn_call(hgp, lbs[l], e_hg, bd_hg, nct)

        qk = _conv_call(mqk, ml_conv[l], nct)
        g16 = mg[:, :, :16].reshape(bsz, t, 2, 2, ML_HEADS)
        gdir = g16.transpose(0, 3, 1, 2, 4).reshape(bsz, 2, t, 8)
        gcol = jnp.pad(gdir, ((0, 0), (0, 0), (0, 0), (0, 120)))
        grow = gdir.transpose(0, 1, 3, 2)
        ml_o = _mlstm_call(qk, mvt, gcol, grow, mask_ml, mask_ml_t, nct)

        skip = nct if last else 0
        h1 = _out_call(h, modsel[l], at, hg_o, hgp, gn[l], e_hg_mean, ml_o, mvo,
                       w_out_b[l, :ATTN_W], w_out_b[l, ATTN_W:ATTN_W + HG_W], w_out_b[l, ATTN_W + HG_W:], nct, skip)
        h = _ffn_call(h1, modsel[l], norm_ffn[l].reshape(1, d), w1_b[l], w2_b[l], 0 if last else nct)
    return h
```

```python
import functools

import jax
import jax.numpy as jnp
import numpy as np
from jax import lax
from jax.experimental import pallas as pl
from jax.experimental.pallas import tpu as pltpu

F32 = jnp.float32
BF16 = jnp.bfloat16

GRID_W = 64
HEAD_DIM = 64
ATTN_HEADS = 8
ATTN_KV_HEADS = 2
ATTN_REP = ATTN_HEADS // ATTN_KV_HEADS
ROPE_THETA = 10000.0
ROPE_AXIS_DIM = HEAD_DIM // 2
HG_HEADS = 4
HG_DK = 64
ML_HEADS = 4
ML_DK = 64
N_MOD = 6
EPS = 1e-6
ATTN_W = ATTN_HEADS * HEAD_DIM
KV_W = ATTN_KV_HEADS * HEAD_DIM
HG_W = HG_HEADS * HG_DK
ML_W = ML_HEADS * ML_DK

OFF_AQ, OFF_AK, OFF_AV = 0, 512, 640
OFF_HG = 768
OFF_MQK = 2048
OFF_MVO = 2560
OFF_MG = 3072
N_IN = 3088
N_IN_PAD = 3200

TM = 256
TQ = 256
TK = 256
ATTN_UNROLL = 32
SCAN_BLK = 256
HG_CHUNK = 16
FP8 = jnp.float8_e4m3fn
QK_ROWS = 256
SHIFT_ROWS = QK_ROWS - 3 * HEAD_DIM
FP8_TARGET = 128.0
MAX_FIXED_SHIFT = 32.0
NEG_BIG = -1e30
Q_SCALE = float(np.log2(np.e)) * HEAD_DIM ** -0.5
VMEM_LIMIT = 56 * 1024 * 1024


def _cparams(sem):
    return pltpu.CompilerParams(dimension_semantics=sem, vmem_limit_bytes=VMEM_LIMIT)


def _split3(x):
    hi = x.astype(BF16)
    r1 = x - hi.astype(F32)
    mid = r1.astype(BF16)
    lo = (r1 - mid.astype(F32)).astype(BF16)
    return hi, mid, lo


def _dot(a, b):
    return jnp.dot(a, b, preferred_element_type=F32)


def _dot_nt(a, b):
    return lax.dot_general(a, b, (((1,), (1,)), ((), ())), preferred_element_type=F32)


def _dot_tn(a, b):
    return lax.dot_general(a, b, (((0,), (0,)), ((), ())), preferred_element_type=F32)


def _mask_dot(mask_bf16, x):
    hi, mid, lo = _split3(x)
    return _dot(mask_bf16, hi) + _dot(mask_bf16, mid) + _dot(mask_bf16, lo)


def _dot_mask(x, mask_bf16):
    hi, mid, lo = _split3(x)
    return _dot(hi, mask_bf16) + _dot(mid, mask_bf16) + _dot(lo, mask_bf16)


def _sigmoid(x):
    return 1.0 / (1.0 + jnp.exp(-x))


def _log_sigmoid(x):
    return jnp.minimum(x, 0.0) - jnp.log(1.0 + jnp.exp(-jnp.abs(x)))


def _mod_kernel(c_ref, w_ref, b_ref, o_ref):
    cv = c_ref[...]
    a = cv * _sigmoid(cv)
    o_ref[0] = jnp.dot(a, w_ref[0], preferred_element_type=F32, precision=lax.Precision.HIGHEST) + b_ref[0]


def _mod_call(cvec, w_mod, b_mod):
    depth, d, nm = w_mod.shape
    tn = 1024
    return pl.pallas_call(
        _mod_kernel,
        grid=(depth, nm // tn),
        in_specs=[pl.BlockSpec((8, d), lambda l, j: (0, 0)),
                  pl.BlockSpec((1, d, tn), lambda l, j: (l, 0, j)),
                  pl.BlockSpec((1, 1, tn), lambda l, j: (l, 0, j))],
        out_specs=pl.BlockSpec((1, 8, tn), lambda l, j: (l, 0, j)),
        out_shape=jax.ShapeDtypeStruct((depth, 8, nm), F32),
        compiler_params=_cparams(("arbitrary", "arbitrary")),
        name="mod_vectors",
    )(cvec, w_mod, b_mod.reshape(depth, 1, nm))


def _head_rms(t, gain, e):
    ss = _dot((t * t).astype(BF16), e)
    return t * lax.rsqrt(ss + EPS) * gain


def _rope(t, cos, sin):
    w = t.shape[1]
    reps = w // 128
    cosw = jnp.concatenate([cos] * reps, axis=1) if reps > 1 else cos
    sinw = jnp.concatenate([sin] * reps, axis=1) if reps > 1 else sin
    lane = lax.broadcasted_iota(jnp.int32, t.shape, 1)
    first_half = (lane % 32) < 16
    partner = jnp.where(first_half, pltpu.roll(t, w - 16, 1), pltpu.roll(t, 16, 1))
    return t * cosw + partner * sinw


def _fp8_operand(d, first_half, fill, rem_first):
    rem = d - d.astype(FP8).astype(F32)
    mixed = jnp.where(first_half, rem, d) if rem_first else jnp.where(first_half, d, rem)
    filled = jnp.where(first_half, d, fill)
    return jnp.concatenate([filled.astype(FP8), mixed.astype(FP8)], axis=1)


def _in_kernel(h_ref, mod_ref, g_ref, w_ref, b_ref, cos_ref, sin_ref, gq_ref, gk_ref, e_ref, ac_ref,
               qa_ref, ka_ref, vt_ref, hg_ref, mqk_ref, mvo_ref, mg_ref, mvt_ref):
    x = h_ref[0]
    ms = jnp.mean(x * x, axis=-1, keepdims=True)
    y = x * lax.rsqrt(ms + EPS) * g_ref[...]
    mod = mod_ref[0, 0]
    u = y * (1.0 + mod[1:2]) + mod[0:1]
    p = _dot(u.astype(BF16), w_ref[...]) + b_ref[...]
    cos = cos_ref[...]
    sin = sin_ref[...]
    e = e_ref[...]
    q = _rope(_head_rms(p[:, OFF_AQ:OFF_AQ + ATTN_W], gq_ref[...], e), cos, sin)
    k = _rope(_head_rms(p[:, OFF_AK:OFF_AK + KV_W], gk_ref[...], e[:KV_W, :KV_W]), cos, sin)

    consts = ac_ref[...]
    first_half = lax.broadcasted_iota(jnp.int32, (x.shape[0], 128), 1) < HEAD_DIM
    qs = q * consts[:, 0:1]
    neg_r = -consts[:, 2:3]
    for g in range(ATTN_W // 128):
        grp = qs[:, g * 128:(g + 1) * 128]
        swp = pltpu.roll(grp, HEAD_DIM, 1)
        qa_ref[0, 2 * g] = _fp8_operand(jnp.where(first_half, grp, swp), first_half, neg_r, False)
        qa_ref[0, 2 * g + 1] = _fp8_operand(jnp.where(first_half, swp, grp), first_half, neg_r, False)
    ks = k * consts[:, 1:2]
    ksw = pltpu.roll(ks, HEAD_DIM, 1)
    ka_ref[0, 0] = _fp8_operand(jnp.where(first_half, ks, ksw), first_half, 1.0, True)
    ka_ref[0, 1] = _fp8_operand(jnp.where(first_half, ksw, ks), first_half, 1.0, True)
    v_t = p[:, OFF_AV:OFF_AV + KV_W].T.astype(BF16)
    for hh in range(ATTN_KV_HEADS):
        vt_ref[0, hh] = v_t[hh * HEAD_DIM:(hh + 1) * HEAD_DIM]
    hg_ref[0] = p[:, OFF_HG:OFF_MQK]
    mqk_ref[0] = p[:, OFF_MQK:OFF_MVO]
    mvo_ref[0] = p[:, OFF_MVO:OFF_MG]
    mg_ref[0] = p[:, OFF_MG:OFF_MG + 128]
    mvt_ref[0] = p[:, OFF_MVO:OFF_MVO + ML_W].T.astype(BF16)


def _in_call(h, modsel, g, w, b, cos, sin, gq, gk, e, attn_consts, nct):
    bsz, t, d = h.shape
    nt = t // TM
    row = lambda width: pl.BlockSpec((1, TM, width), lambda bi, i: (bi, i, 0))
    full = lambda a: pl.BlockSpec(a.shape, lambda bi, i: (0,) * a.ndim)
    out_widths = (OFF_MQK - OFF_HG, 2 * ML_W, 2 * ML_W, 128)
    return pl.pallas_call(
        _in_kernel,
        grid=(bsz, nt),
        in_specs=[row(d),
                  pl.BlockSpec((1, 1, N_MOD, d), lambda bi, i: (bi, (i >= nct).astype(jnp.int32), 0, 0)),
                  full(g), full(w), full(b),
                  pl.BlockSpec((TM, 128), lambda bi, i: (i, 0)),
                  pl.BlockSpec((TM, 128), lambda bi, i: (i, 0)),
                  full(gq), full(gk), full(e), full(attn_consts)],
        out_specs=[pl.BlockSpec((1, ATTN_HEADS, TM, QK_ROWS), lambda bi, i: (bi, 0, i, 0)),
                   pl.BlockSpec((1, ATTN_KV_HEADS, TM, QK_ROWS), lambda bi, i: (bi, 0, i, 0)),
                   pl.BlockSpec((1, ATTN_KV_HEADS, HEAD_DIM, TM), lambda bi, i: (bi, 0, 0, i))]
        + [row(wd) for wd in out_widths] + [pl.BlockSpec((1, ML_W, TM), lambda bi, i: (bi, 0, i))],
        out_shape=[jax.ShapeDtypeStruct((bsz, ATTN_HEADS, t, QK_ROWS), FP8),
                   jax.ShapeDtypeStruct((bsz, ATTN_KV_HEADS, t, QK_ROWS), FP8),
                   jax.ShapeDtypeStruct((bsz, ATTN_KV_HEADS, HEAD_DIM, t), BF16)]
        + [jax.ShapeDtypeStruct((bsz, t, wd), F32) for wd in out_widths]
        + [jax.ShapeDtypeStruct((bsz, ML_W, t), BF16)],
        compiler_params=_cparams(("parallel", "parallel")),
        name="norm_mod_in_proj",
    )(h, modsel, g, w, b, cos, sin, gq, gk, e, attn_consts)


def _attn_kernel(qa_ref, ka_ref, vt_ref, unscale_ref, o_ref, acc_ref, l_ref, s_ref, *maybe_m_ref,
                 online, n_ctx_q, n_ctx_k, n_k):
    qi = pl.program_id(2)
    unscale = unscale_ref[...]
    acc_ref[...] = jnp.zeros(acc_ref.shape, F32)
    l_ref[...] = jnp.zeros(l_ref.shape, F32)
    if online:
        m_ref, = maybe_m_ref
        m_ref[...] = jnp.full(m_ref.shape, NEG_BIG, F32)
    n_blocks = jnp.where(qi < n_ctx_q, n_ctx_k, n_k)

    def key_tile(kb):
        return ka_ref[0, 0, pl.ds(pl.multiple_of(kb * TK, TK), TK), :]

    def key_sums(p):
        return jnp.sum(p.reshape(TK // 8, 8, TQ), axis=0)

    def produce(kt, slot, hh, counted):
        s = _dot_nt(kt, qa_ref[0, hh]) * unscale
        if online:
            s_ref[slot, hh] = s
        else:
            p = jnp.exp2(s)
            l_ref[hh] += key_sums(p) if counted is True else jnp.where(counted, key_sums(p), 0.0)
            s_ref[slot, hh] = p.astype(BF16)

    def step(kb, cur, nxt):
        vt = vt_ref[0, 0, :, pl.ds(pl.multiple_of(kb * TK, TK), TK)]
        kt = key_tile(jnp.minimum(kb + 1, n_blocks - 1))
        for hh in range(ATTN_REP):
            produce(kt, nxt, hh, kb + 1 < n_blocks)
            if online:
                s = s_ref[cur, hh]
                m_old = m_ref[hh]
                m_new = jnp.maximum(m_old, jnp.max(s, axis=0, keepdims=True))
                p = jnp.exp2(s - m_new)
                alpha = jnp.exp2(m_old - m_new)
                l_ref[hh] = alpha * l_ref[hh] + key_sums(p)
                acc_ref[hh] = alpha * acc_ref[hh] + _dot(vt, p.astype(BF16))
                m_ref[hh] = m_new
            else:
                acc_ref[hh] += _dot(vt, s_ref[cur, hh])

    kt0 = key_tile(0)
    for hh in range(ATTN_REP):
        produce(kt0, 0, hh, True)

    def group(i, carry):
        for u in range(ATTN_UNROLL):
            step(ATTN_UNROLL * i + u, u % 2, (u + 1) % 2)
        return carry

    def pair(i, carry):
        step(2 * i, 0, 1)
        step(2 * i + 1, 1, 0)
        return carry

    n_groups = n_blocks // ATTN_UNROLL
    lax.fori_loop(0, n_groups, group, 0)
    lax.fori_loop(n_groups * (ATTN_UNROLL // 2), n_blocks // 2, pair, 0)

    @pl.when(n_blocks % 2 == 1)
    def _():
        step(n_blocks - 1, 0, 1)

    for hh in range(ATTN_REP):
        o = acc_ref[hh] / jnp.sum(l_ref[hh], axis=0, keepdims=True)
        o_ref[0, hh * HEAD_DIM:(hh + 1) * HEAD_DIM, :] = o.astype(o_ref.dtype)


def _attn_call(qa, ka, vt, unscale, n_ctx, online):
    bsz, _, t, _ = qa.shape
    kern = functools.partial(_attn_kernel, online=online, n_ctx_q=n_ctx // TQ, n_ctx_k=n_ctx // TK, n_k=t // TK)
    gw = ATTN_REP * HEAD_DIM
    scratch = [pltpu.VMEM((ATTN_REP, HEAD_DIM, TQ), F32),
               pltpu.VMEM((ATTN_REP, 8, TQ), F32),
               pltpu.VMEM((2, ATTN_REP, TK, TQ), F32 if online else BF16)]
    if online:
        scratch.append(pltpu.VMEM((ATTN_REP, 1, TQ), F32))
    return pl.pallas_call(
        kern,
        grid=(bsz, ATTN_KV_HEADS, t // TQ),
        in_specs=[pl.BlockSpec((1, ATTN_REP, TQ, QK_ROWS), lambda b, g, i: (b, g, i, 0)),
                  pl.BlockSpec((1, 1, t, QK_ROWS), lambda b, g, i: (b, g, 0, 0)),
                  pl.BlockSpec((1, 1, HEAD_DIM, t), lambda b, g, i: (b, g, 0, 0)),
                  pl.BlockSpec((1, 1), lambda b, g, i: (0, 0))],
        out_specs=pl.BlockSpec((1, gw, TQ), lambda b, g, i: (b, g, i)),
        out_shape=jax.ShapeDtypeStruct((bsz, ATTN_W, t), BF16),
        scratch_shapes=scratch,
        compiler_params=_cparams(("parallel", "parallel", "arbitrary")),
        name="attention_online" if online else "attention",
    )(qa, ka, vt, unscale)


def _attn_constants(q_gain, k_gain):
    qmax = jnp.maximum(jnp.sqrt(float(HEAD_DIM)) * Q_SCALE * jnp.max(jnp.abs(q_gain)), 1e-6)
    kmax = jnp.maximum(jnp.sqrt(float(HEAD_DIM)) * jnp.max(jnp.abs(k_gain)), 1e-6)
    shift = qmax * kmax
    q_scale = jnp.exp2(jnp.floor(jnp.log2(FP8_TARGET / qmax)))
    k_scale = jnp.exp2(jnp.floor(jnp.log2(FP8_TARGET / kmax)))
    r = (jnp.minimum(shift, MAX_FIXED_SHIFT) * q_scale * k_scale * (1.07 / SHIFT_ROWS)).astype(FP8).astype(F32)
    in_proj_consts = jnp.pad(jnp.stack([Q_SCALE * q_scale, k_scale, r]), (0, 125)).reshape(1, 128)
    unscale = (1.0 / (q_scale * k_scale)).reshape(1, 1)
    return in_proj_consts, unscale, shift <= MAX_FIXED_SHIFT


def _attention(qa, ka, vt, unscale, use_fixed_shift, n_ctx):
    return lax.cond(use_fixed_shift,
                    lambda: _attn_call(qa, ka, vt, unscale, n_ctx, False),
                    lambda: _attn_call(qa, ka, vt, unscale, n_ctx, True))


def _conv_kernel(x_ref, prev_ref, next_ref, w_ref, o_ref, *, nct, nb):
    i = pl.program_id(1)
    x = x_ref[0]
    rows = x.shape[0]
    left_edge = jnp.logical_or(i == 0, i == nct)
    right_edge = jnp.logical_or(i == nct - 1, i == nb - 1)
    prev_row = jnp.where(left_edge, 0.0, prev_ref[0, 7:8, :])
    next_row = jnp.where(right_edge, 0.0, next_ref[0, 0:1, :])
    ridx = lax.broadcasted_iota(jnp.int32, x.shape, 0)
    xm1 = jnp.where(ridx == 0, prev_row, pltpu.roll(x, 1, 0))
    xp1 = jnp.where(ridx == rows - 1, next_row, pltpu.roll(x, rows - 1, 0))
    w = w_ref[...]
    y = xm1 * w[0:1] + x * w[1:2] + xp1 * w[2:3]
    y = y * _sigmoid(y)
    lane = lax.broadcasted_iota(jnp.int32, x.shape, 1)
    y = jnp.where(lane >= ML_W, y * (ML_DK ** -0.5), y)
    o_ref[0] = y.astype(o_ref.dtype)


def _conv_call(mqk, w, nct):
    bsz, t, width = mqk.shape
    nb = t // SCAN_BLK
    per8 = SCAN_BLK // 8
    last8 = t // 8 - 1
    kern = functools.partial(_conv_kernel, nct=nct, nb=nb)
    return pl.pallas_call(
        kern,
        grid=(bsz, nb),
        in_specs=[pl.BlockSpec((1, SCAN_BLK, width), lambda b, i: (b, i, 0)),
                  pl.BlockSpec((1, 8, width), lambda b, i: (b, jnp.maximum(i * per8 - 1, 0), 0)),
                  pl.BlockSpec((1, 8, width), lambda b, i: (b, jnp.minimum((i + 1) * per8, last8), 0)),
                  pl.BlockSpec(w.shape, lambda b, i: (0, 0))],
        out_specs=pl.BlockSpec((1, SCAN_BLK, width), lambda b, i: (b, i, 0)),
        out_shape=jax.ShapeDtypeStruct((bsz, t, width), BF16),
        compiler_params=_cparams(("parallel", "parallel")),
        name="mlstm_qk_conv",
    )(mqk, mqk, mqk, w)


def _scan_block(d, j, nct, nb):
    bwd = jnp.where(j < nct, nct - 1 - j, nb - 1 - (j - nct))
    return jnp.where(d == 0, j, bwd)


def _hgrn_kernel(q0_ref, q1_ref, v0_ref, v1_ref, z0_ref, z1_ref, lb_ref, e_ref, bd_ref, o_ref,
                 st_ref, qd_ref, kd_ref, od_ref, dt_ref, u_ref):
    d = pl.program_id(1)
    j = pl.program_id(2)

    @pl.when(j == 0)
    def _():
        st_ref[...] = jnp.zeros(st_ref.shape, F32)

    lb = lb_ref[0]
    e = e_ref[...]
    bd = bd_ref[...]
    c = HG_CHUNK
    nch = SCAN_BLK // c

    def rows(i):
        return pl.ds(jnp.where(d == 0, i, c - 1 - i), nch, stride=c)

    def load2(r0, r1, sl):
        return jnp.concatenate([r0[0, sl, :], r1[0, sl, :]], axis=1)

    def load_halves(ref, sl):
        return jnp.concatenate([ref[0, sl, :], ref[1, sl, :]], axis=1)

    def store_halves(ref, sl, x):
        ref[0, sl, :] = x[:, :128]
        ref[1, sl, :] = x[:, 128:]

    qs, ks, vs, bs = [], [], [], []
    b = None
    for i in range(c):
        f = lb + (1.0 - lb) * _sigmoid(load2(z0_ref, z1_ref, rows(i)))
        lf = jnp.log2(f)
        b = lf if b is None else b + lf
        qs.append(load2(q0_ref, q1_ref, rows(i)))
        vs.append(load2(v0_ref, v1_ref, rows(i)))
        ks.append(1.0 - f)
        bs.append(b)
    b_tot = bs[-1]
    dt_ref[...] = jnp.exp2(b_tot)
    for i in range(c):
        store_halves(qd_ref, rows(i), qs[i] * jnp.exp2(bs[i]))
        store_halves(kd_ref, rows(i), ks[i] * jnp.exp2(b_tot - bs[i]))
    for cc in range(nch):
        sl = pl.ds(cc * c, c)
        vc = load2(v0_ref, v1_ref, sl).astype(BF16)
        kc = load_halves(kd_ref, sl).astype(BF16)
        u_ref[cc] = _dot_tn(vc, kc) * bd
    for i in range(c):
        ps = [qs[i] * ks[jj] * jnp.exp2(bs[i] - bs[jj]) for jj in range(i)] + [qs[i] * ks[i]]
        a = _dot(jnp.concatenate(ps, axis=0).astype(BF16), e)
        o = a[0:nch] * vs[0]
        for jj in range(1, i + 1):
            o = o + a[jj * nch:(jj + 1) * nch] * vs[jj]
        store_halves(od_ref, rows(i), o)

    for n in range(nch):
        cc = jnp.where(d == 0, n, nch - 1 - n)
        sl = pl.ds(pl.multiple_of(cc * c, c), c)
        st = st_ref[...]
        o_inter = _dot_nt(load_halves(qd_ref, sl).astype(BF16), st.astype(BF16))
        o_ref[0, 0, sl, :] = load_halves(od_ref, sl) + o_inter
        st_ref[...] = st * dt_ref[pl.ds(cc, 1), :] + u_ref[cc]


def _hgrn_call(hgp, lbs, e, bd, nct):
    bsz, t, _ = hgp.shape
    nb = t // SCAN_BLK
    w = HG_W
    blk = lambda d, j: _scan_block(d, j, nct, nb)
    half = lambda col: pl.BlockSpec((1, SCAN_BLK, 128), lambda b, d, j: (b, blk(d, j), col(d)))
    return pl.pallas_call(
        _hgrn_kernel,
        grid=(bsz, 2, nb),
        in_specs=[half(lambda d: 0), half(lambda d: 1),
                  half(lambda d: 2), half(lambda d: 3),
                  half(lambda d: 4 + 2 * d), half(lambda d: 5 + 2 * d),
                  pl.BlockSpec((1, 1, w), lambda b, d, j: (d, 0, 0)),
                  pl.BlockSpec((w, w), lambda b, d, j: (0, 0)),
                  pl.BlockSpec((w, w), lambda b, d, j: (0, 0))],
        out_specs=pl.BlockSpec((1, 1, SCAN_BLK, w), lambda b, d, j: (d, b, blk(d, j), 0)),
        out_shape=jax.ShapeDtypeStruct((2, bsz, t, w), F32),
        scratch_shapes=[pltpu.VMEM((w, w), F32),
                        pltpu.VMEM((2, SCAN_BLK, 128), F32),
                        pltpu.VMEM((2, SCAN_BLK, 128), F32),
                        pltpu.VMEM((2, SCAN_BLK, 128), F32),
                        pltpu.VMEM((SCAN_BLK // HG_CHUNK, w), F32),
                        pltpu.VMEM((SCAN_BLK // HG_CHUNK, w, w), F32)],
        compiler_params=_cparams(("parallel", "parallel", "arbitrary")),
        name="hgrn2_scan",
    )(hgp, hgp, hgp, hgp, hgp, hgp, lbs, e, bd)


def _mlstm_kernel(q_ref, k_ref, vt_ref, g_ref, gt_ref, m_ref, mt_ref, o_ref, c_ref, ms_ref):
    j = pl.program_id(2)

    @pl.when(j == 0)
    def _():
        c_ref[...] = jnp.zeros(c_ref.shape, F32)
        ms_ref[...] = jnp.zeros(ms_ref.shape, F32)

    length = SCAN_BLK
    mask = m_ref[0]
    mask_t = mt_ref[0]
    valid_t = mask_t > 0
    gates = g_ref[0, 0]
    gates_t = gt_ref[0, 0]
    b_col_all = _mask_dot(mask, _log_sigmoid(gates))
    lf_t = _log_sigmoid(gates_t)
    b_row_all = _dot_mask(lf_t, mask_t)
    tot_all = jnp.sum(lf_t, axis=1, keepdims=True)
    row = lax.broadcasted_iota(jnp.int32, (128, length), 0)

    heads = range(ML_HEADS)
    kh = [k_ref[0, :, hh * ML_DK:(hh + 1) * ML_DK] for hh in heads]
    qh = [q_ref[0, :, hh * ML_DK:(hh + 1) * ML_DK] for hh in heads]
    s_raw = [_dot_nt(kh[hh], qh[hh]) for hh in heads]
    inter = [_dot_nt(c_ref[hh].astype(BF16), qh[hh]) for hh in heads]
    for pair in range(ML_HEADS // 2):
        v2t = vt_ref[0, pair * 128:(pair + 1) * 128, :]
        out_t = jnp.zeros((128, length), F32)
        for sub in range(2):
            hh = pair * 2 + sub
            own = (row < 64) if sub == 0 else (row >= 64)
            ones_row = 64 if sub == 0 else 0
            v_aug_t = jnp.where(own, v2t, jnp.where(row == ones_row, 1.0, 0.0).astype(BF16))
            m_prev = ms_ref[hh, 0:1, 0:1]
            cs = gates[:, hh:hh + 1] - b_col_all[:, 4 + hh:5 + hh]
            br = b_row_all[4 + hh:5 + hh, :]
            tot = tot_all[4 + hh:5 + hh, :]
            log_d = jnp.where(valid_t, br + cs, NEG_BIG)
            m_t = jnp.maximum(br + m_prev, jnp.max(log_d, axis=0, keepdims=True))
            s_t = (s_raw[hh] * jnp.exp(log_d - m_t)).astype(BF16)
            r_t = jnp.exp(br + m_prev - m_t) * inter[hh] + _dot(v_aug_t, s_t)
            den = r_t[ones_row:ones_row + 1, :]
            out_t = jnp.where(own, r_t / jnp.maximum(jnp.abs(den), jnp.exp(-m_t)), out_t)
            log_e = tot + cs
            m_new = jnp.maximum(tot + m_prev, jnp.max(log_e, axis=0, keepdims=True))
            ke = (kh[hh].astype(F32) * jnp.exp(log_e - m_new)).astype(BF16)
            c_ref[hh] = jnp.exp(tot + m_prev - m_new) * c_ref[hh] + _dot(v_aug_t, ke)
            ms_ref[hh] = jnp.broadcast_to(m_new, ms_ref.shape[1:])
        o_ref[0, 0, :, pair * 128:(pair + 1) * 128] = out_t.T


def _mlstm_call(qk, vt, gcol, grow, mask, mask_t, nct):
    bsz, t, _ = qk.shape
    nb = t // SCAN_BLK
    w = ML_W
    blk = lambda d, j: _scan_block(d, j, nct, nb)
    return pl.pallas_call(
        _mlstm_kernel,
        grid=(bsz, 2, nb),
        in_specs=[pl.BlockSpec((1, SCAN_BLK, w), lambda b, d, j: (b, blk(d, j), 0)),
                  pl.BlockSpec((1, SCAN_BLK, w), lambda b, d, j: (b, blk(d, j), 1)),
                  pl.BlockSpec((1, w, SCAN_BLK), lambda b, d, j: (b, 0, blk(d, j))),
                  pl.BlockSpec((1, 1, SCAN_BLK, 128), lambda b, d, j: (b, d, blk(d, j), 0)),
                  pl.BlockSpec((1, 1, 8, SCAN_BLK), lambda b, d, j: (b, d, 0, blk(d, j))),
                  pl.BlockSpec((1, SCAN_BLK, SCAN_BLK), lambda b, d, j: (d, 0, 0)),
                  pl.BlockSpec((1, SCAN_BLK, SCAN_BLK), lambda b, d, j: (d, 0, 0))],
        out_specs=pl.BlockSpec((1, 1, SCAN_BLK, w), lambda b, d, j: (d, b, blk(d, j), 0)),
        out_shape=jax.ShapeDtypeStruct((2, bsz, t, w), F32),
        scratch_shapes=[pltpu.VMEM((ML_HEADS, 128, ML_DK), F32),
                        pltpu.VMEM((ML_HEADS, 8, 128), F32)],
        compiler_params=_cparams(("parallel", "parallel", "arbitrary")),
        name="mlstm_scan",
    )(qk, qk, vt, gcol, grow, mask, mask_t)


def _out_kernel(h_ref, mod_ref, at_ref, hf_ref, hb_ref, hgate_ref, gn_ref, e_ref, mf_ref, mb_ref, mgate_ref,
                wa_ref, wr_ref, wm_ref, o_ref):
    mod = mod_ref[0, 0]
    o = hf_ref[0, 0] + hb_ref[0, 0]
    ss = _dot((o * o).astype(BF16), e_ref[...])
    gate = hgate_ref[0]
    r = o * lax.rsqrt(ss + EPS) * gn_ref[...] * (gate * _sigmoid(gate))
    m = _sigmoid(mgate_ref[0]) * (mf_ref[0, 0] + mb_ref[0, 0])
    y = _dot_tn(at_ref[0], wa_ref[...]) + _dot(r.astype(BF16), wr_ref[...]) + _dot(m.astype(BF16), wm_ref[...])
    o_ref[0] = h_ref[0] + mod[2:3] * y


def _out_call(h, modsel, at, hg_o, hgp, gn, e, ml_o, mvo, wa, wr, wm, nct, skip):
    bsz, t, d = h.shape
    nt = t // TM - skip
    full = lambda a: pl.BlockSpec(a.shape, lambda bi, i: (0,) * a.ndim)
    hg_col = (OFF_MQK - OFF_HG) // HG_W - 1
    return pl.pallas_call(
        _out_kernel,
        grid=(bsz, nt),
        in_specs=[pl.BlockSpec((1, TM, d), lambda bi, i: (bi, i + skip, 0)),
                  pl.BlockSpec((1, 1, N_MOD, d), lambda bi, i: (bi, (i + skip >= nct).astype(jnp.int32), 0, 0)),
                  pl.BlockSpec((1, ATTN_W, TM), lambda bi, i: (bi, 0, i + skip)),
                  pl.BlockSpec((1, 1, TM, HG_W), lambda bi, i: (0, bi, i + skip, 0)),
                  pl.BlockSpec((1, 1, TM, HG_W), lambda bi, i: (1, bi, i + skip, 0)),
                  pl.BlockSpec((1, TM, HG_W), lambda bi, i: (bi, i + skip, hg_col)),
                  full(gn), full(e),
                  pl.BlockSpec((1, 1, TM, ML_W), lambda bi, i: (0, bi, i + skip, 0)),
                  pl.BlockSpec((1, 1, TM, ML_W), lambda bi, i: (1, bi, i + skip, 0)),
                  pl.BlockSpec((1, TM, ML_W), lambda bi, i: (bi, i + skip, 1)),
                  full(wa), full(wr), full(wm)],
        out_specs=pl.BlockSpec((1, TM, d), lambda bi, i: (bi, i, 0)),
        out_shape=jax.ShapeDtypeStruct((bsz, nt * TM, d), F32),
        compiler_params=_cparams(("parallel", "parallel")),
        name="readout_out_proj",
    )(h, modsel, at, hg_o, hg_o, hgp, gn, e, ml_o, ml_o, mvo, wa, wr, wm)


def _ffn_kernel(h_ref, mod_ref, g_ref, w1_ref, w2_ref, o_ref):
    x = h_ref[0]
    mod = mod_ref[0, 0]
    ms = jnp.mean(x * x, axis=-1, keepdims=True)
    y = x * lax.rsqrt(ms + EPS) * g_ref[...]
    u = (y * (1.0 + mod[4:5]) + mod[3:4]).astype(BF16)
    a = jnp.maximum(_dot(u, w1_ref[...]), 0.0)
    a = (a * a).astype(BF16)
    o_ref[0] = x + mod[5:6] * _dot(a, w2_ref[...])


def _ffn_call(h, modsel, g, w1, w2, nct):
    bsz, t, d = h.shape
    nt = t // TM
    full = lambda a: pl.BlockSpec(a.shape, lambda bi, i: (0,) * a.ndim, pipeline_mode=pl.Buffered(1))
    return pl.pallas_call(
        _ffn_kernel,
        grid=(bsz, nt),
        in_specs=[pl.BlockSpec((1, TM, d), lambda bi, i: (bi, i, 0)),
                  pl.BlockSpec((1, 1, N_MOD, d), lambda bi, i: (bi, (i >= nct).astype(jnp.int32), 0, 0)),
                  pl.BlockSpec(g.shape, lambda bi, i: (0, 0)),
                  full(w1), full(w2)],
        out_specs=pl.BlockSpec((1, TM, d), lambda bi, i: (bi, i, 0)),
        out_shape=jax.ShapeDtypeStruct((bsz, t, d), F32),
        compiler_params=_cparams(("parallel", "parallel")),
        name="ffn",
    )(h, modsel, g, w1, w2)


def _rope_tables(n_ctx, n_lat):
    inv_freq = ROPE_THETA ** (-np.arange(0, ROPE_AXIS_DIM, 2, dtype=np.float32) / ROPE_AXIS_DIM)
    inv_freq = jnp.asarray(inv_freq, F32)
    tok = jnp.arange(n_lat)
    ang_row = (tok // GRID_W).astype(F32)[:, None] * inv_freq[None, :]
    ang_col = (tok % GRID_W).astype(F32)[:, None] * inv_freq[None, :]
    ang = jnp.concatenate([ang_row, ang_row, ang_col, ang_col], axis=1)
    ang = jnp.concatenate([jnp.zeros((n_ctx, HEAD_DIM), F32), ang], axis=0)
    sign = np.where((np.arange(HEAD_DIM) % 32) < 16, -1.0, 1.0).astype(np.float32)
    cos = jnp.cos(ang)
    sin = jnp.sin(ang) * jnp.asarray(sign)[None, :]
    return jnp.concatenate([cos, cos], axis=1), jnp.concatenate([sin, sin], axis=1)


def _block_ones(width, head, scale, dtype):
    idx = np.arange(width) // head
    return jnp.asarray((idx[:, None] == idx[None, :]).astype(np.float32) * scale, dtype)


def _scan_masks(n):
    tril = np.tril(np.ones((n, n), np.float32))
    return np.stack([tril, tril.T])


def kernel(x, c, ctx, c_ctx, w_mod, b_mod, norm_mix, norm_ffn, w_in, b_in, q_norm, k_norm,
           hg_lb, hg_norm, ml_conv, w_out, w_ff1, w_ff2):
    bsz, n_lat, d = x.shape
    n_ctx = ctx.shape[1]
    depth = w_mod.shape[0]
    t = n_ctx + n_lat
    nct = n_ctx // TM
    assert n_ctx % SCAN_BLK == 0 and n_lat % SCAN_BLK == 0 and n_lat % GRID_W == 0 and bsz + 1 <= 8

    cvec = jnp.concatenate([c, c_ctx[None, :], jnp.zeros((8 - bsz - 1, d), F32)], axis=0)
    mods = _mod_call(cvec, w_mod, b_mod).reshape(depth, 8, N_MOD, d)
    modsel = jnp.stack([jnp.broadcast_to(mods[:, bsz][:, None], (depth, bsz, N_MOD, d)), mods[:, :bsz]], axis=2)

    cos, sin = _rope_tables(n_ctx, n_lat)
    e_attn = _block_ones(ATTN_W, HEAD_DIM, 1.0 / HEAD_DIM, BF16)
    e_hg_mean = _block_ones(HG_W, HG_DK, 1.0 / HG_DK, BF16)
    e_hg = _block_ones(HG_W, HG_DK, 1.0, BF16)
    bd_hg = _block_ones(HG_W, HG_DK, 1.0, F32)
    scan_mask = _scan_masks(SCAN_BLK)
    mask_ml = jnp.asarray(scan_mask, BF16)
    mask_ml_t = jnp.asarray(np.transpose(scan_mask, (0, 2, 1)).copy(), BF16)

    lbs = jnp.cumsum(jax.nn.softmax(hg_lb.astype(F32), axis=0), axis=0)
    lbs = (lbs - lbs[:1]).reshape(depth, 2, 1, HG_W)

    w_in_p = jnp.pad(w_in, ((0, 0), (0, 0), (0, N_IN_PAD - N_IN))).astype(BF16)
    b_in_p = jnp.pad(b_in, ((0, 0), (0, N_IN_PAD - N_IN))).reshape(depth, 1, N_IN_PAD)
    w_out_b = w_out.astype(BF16)
    w1_b = w_ff1.astype(BF16)
    w2_b = w_ff2.astype(BF16)
    gq = jnp.tile(q_norm, (1, ATTN_HEADS)).reshape(depth, 1, ATTN_W)
    gk = jnp.tile(k_norm, (1, ATTN_KV_HEADS)).reshape(depth, 1, KV_W)
    gn = jnp.tile(hg_norm, (1, HG_HEADS)).reshape(depth, 1, HG_W)

    h = jnp.concatenate([ctx, x], axis=1)
    for l in range(depth):
        last = l == depth - 1
        attn_consts, unscale, use_fixed_shift = _attn_constants(q_norm[l], k_norm[l])
        qa, ka, vt, hgp, mqk, mvo, mg, mvt = _in_call(
            h, modsel[l], norm_mix[l].reshape(1, d), w_in_p[l], b_in_p[l], cos, sin, gq[l], gk[l], e_attn,
            attn_consts, nct)

        at = _attention(qa, ka, vt, unscale, use_fixed_shift, n_ctx)

        hg_o = _hgrn_call(hgp, lbs[l], e_hg, bd_hg, nct)

        qk = _conv_call(mqk, ml_conv[l], nct)
        g16 = mg[:, :, :16].reshape(bsz, t, 2, 2, ML_HEADS)
        gdir = g16.transpose(0, 3, 1, 2, 4).reshape(bsz, 2, t, 8)
        gcol = jnp.pad(gdir, ((0, 0), (0, 0), (0, 0), (0, 120)))
        grow = gdir.transpose(0, 1, 3, 2)
        ml_o = _mlstm_call(qk, mvt, gcol, grow, mask_ml, mask_ml_t, nct)

        skip = nct if last else 0
        h1 = _out_call(h, modsel[l], at, hg_o, hgp, gn[l], e_hg_mean, ml_o, mvo,
                       w_out_b[l, :ATTN_W], w_out_b[l, ATTN_W:ATTN_W + HG_W], w_out_b[l, ATTN_W + HG_W:], nct, skip)
        h = _ffn_call(h1, modsel[l], norm_ffn[l].reshape(1, d), w1_b[l], w2_b[l], 0 if last else nct)
    return h
```

```python
import functools

import jax
import jax.numpy as jnp
import numpy as np
from jax import lax
from jax.experimental import pallas as pl
from jax.experimental.pallas import tpu as pltpu

F32 = jnp.float32
BF16 = jnp.bfloat16

GRID_W = 64
HEAD_DIM = 64
ATTN_HEADS = 8
ATTN_KV_HEADS = 2
ATTN_REP = ATTN_HEADS // ATTN_KV_HEADS
ROPE_THETA = 10000.0
ROPE_AXIS_DIM = HEAD_DIM // 2
HG_HEADS = 4
HG_DK = 64
ML_HEADS = 4
ML_DK = 64
N_MOD = 6
EPS = 1e-6
ATTN_W = ATTN_HEADS * HEAD_DIM
KV_W = ATTN_KV_HEADS * HEAD_DIM
HG_W = HG_HEADS * HG_DK
ML_W = ML_HEADS * ML_DK

OFF_AQ, OFF_AK, OFF_AV = 0, 512, 640
OFF_HG = 768
OFF_MQK = 2048
OFF_MVO = 2560
OFF_MG = 3072
N_IN = 3088
N_IN_PAD = 3200

TM = 256
TQ = 256
TK = 256
ATTN_UNROLL = 32
SCAN_BLK = 256
HG_CHUNK = 16
V_ROWS = 80
FP8 = jnp.float8_e4m3fn
QK_ROWS = 256
SHIFT_ROWS = QK_ROWS - 3 * HEAD_DIM
FP8_TARGET = 128.0
MAX_FIXED_SHIFT = 32.0
SHIFT_FRACTION = 0.5
NEG_BIG = -1e30
Q_SCALE = float(np.log2(np.e)) * HEAD_DIM ** -0.5
VMEM_LIMIT = 56 * 1024 * 1024


def _cparams(sem):
    return pltpu.CompilerParams(dimension_semantics=sem, vmem_limit_bytes=VMEM_LIMIT)


def _split3(x):
    hi = x.astype(BF16)
    r1 = x - hi.astype(F32)
    mid = r1.astype(BF16)
    lo = (r1 - mid.astype(F32)).astype(BF16)
    return hi, mid, lo


def _dot(a, b):
    return jnp.dot(a, b, preferred_element_type=F32)


def _dot_nt(a, b):
    return lax.dot_general(a, b, (((1,), (1,)), ((), ())), preferred_element_type=F32)


def _dot_tn(a, b):
    return lax.dot_general(a, b, (((0,), (0,)), ((), ())), preferred_element_type=F32)


def _mask_dot(mask_bf16, x):
    hi, mid, lo = _split3(x)
    return _dot(mask_bf16, hi) + _dot(mask_bf16, mid) + _dot(mask_bf16, lo)


def _dot_mask(x, mask_bf16):
    hi, mid, lo = _split3(x)
    return _dot(hi, mask_bf16) + _dot(mid, mask_bf16) + _dot(lo, mask_bf16)


def _sigmoid(x):
    return 1.0 / (1.0 + jnp.exp(-x))


def _log_sigmoid(x):
    return jnp.minimum(x, 0.0) - jnp.log(1.0 + jnp.exp(-jnp.abs(x)))


def _mod_kernel(c_ref, w_ref, b_ref, o_ref):
    cv = c_ref[...]
    a = cv * _sigmoid(cv)
    o_ref[0] = jnp.dot(a, w_ref[0], preferred_element_type=F32, precision=lax.Precision.HIGHEST) + b_ref[0]


def _mod_call(cvec, w_mod, b_mod):
    depth, d, nm = w_mod.shape
    tn = 1024
    return pl.pallas_call(
        _mod_kernel,
        grid=(depth, nm // tn),
        in_specs=[pl.BlockSpec((8, d), lambda l, j: (0, 0)),
                  pl.BlockSpec((1, d, tn), lambda l, j: (l, 0, j)),
                  pl.BlockSpec((1, 1, tn), lambda l, j: (l, 0, j))],
        out_specs=pl.BlockSpec((1, 8, tn), lambda l, j: (l, 0, j)),
        out_shape=jax.ShapeDtypeStruct((depth, 8, nm), F32),
        compiler_params=_cparams(("arbitrary", "arbitrary")),
        name="mod_vectors",
    )(cvec, w_mod, b_mod.reshape(depth, 1, nm))


def _head_rms(t, gain, e):
    ss = _dot((t * t).astype(BF16), e)
    return t * lax.rsqrt(ss + EPS) * gain


def _rope(t, cos, sin):
    w = t.shape[1]
    reps = w // 128
    cosw = jnp.concatenate([cos] * reps, axis=1) if reps > 1 else cos
    sinw = jnp.concatenate([sin] * reps, axis=1) if reps > 1 else sin
    lane = lax.broadcasted_iota(jnp.int32, t.shape, 1)
    first_half = (lane % 32) < 16
    partner = jnp.where(first_half, pltpu.roll(t, w - 16, 1), pltpu.roll(t, 16, 1))
    return t * cosw + partner * sinw


def _fp8_operand(d, first_half, fill, rem_first):
    rem = d - d.astype(FP8).astype(F32)
    mixed = jnp.where(first_half, rem, d) if rem_first else jnp.where(first_half, d, rem)
    filled = jnp.where(first_half, d, fill)
    return jnp.concatenate([filled.astype(FP8), mixed.astype(FP8)], axis=1)


def _stream_spec(nct, lat_off, skip=0):
    d_last = lambda a: a.shape[-1]
    ctx_spec = lambda a: pl.BlockSpec((1, TM, d_last(a)), lambda bi, i: (bi, jnp.minimum(i + skip, nct - 1), 0))
    lat_spec = lambda a: pl.BlockSpec((1, TM, d_last(a)), lambda bi, i: (bi, jnp.maximum(i + skip - nct, 0) + lat_off, 0))
    return ctx_spec, lat_spec


def _in_kernel(hc_ref, hl_ref, mod_ref, g_ref, w_ref, b_ref, cos_ref, sin_ref, gq_ref, gk_ref, e_ref, ac_ref,
               qa_ref, ka_ref, vt_ref, hg_ref, mqk_ref, mvo_ref, mg_ref, mvt_ref, *, nct):
    x = jnp.where(pl.program_id(1) < nct, hc_ref[0], hl_ref[0])
    ms = jnp.mean(x * x, axis=-1, keepdims=True)
    y = x * lax.rsqrt(ms + EPS) * g_ref[...]
    mod = mod_ref[0, 0]
    u = y * (1.0 + mod[1:2]) + mod[0:1]
    p = _dot(u.astype(BF16), w_ref[...]) + b_ref[...]
    cos = cos_ref[...]
    sin = sin_ref[...]
    e = e_ref[...]
    q = _rope(_head_rms(p[:, OFF_AQ:OFF_AQ + ATTN_W], gq_ref[...], e), cos, sin)
    k = _rope(_head_rms(p[:, OFF_AK:OFF_AK + KV_W], gk_ref[...], e[:KV_W, :KV_W]), cos, sin)

    consts = ac_ref[...]
    first_half = lax.broadcasted_iota(jnp.int32, (x.shape[0], 128), 1) < HEAD_DIM
    qs = q * consts[:, 0:1]
    neg_r = -consts[:, 2:3]
    for g in range(ATTN_W // 128):
        grp = qs[:, g * 128:(g + 1) * 128]
        swp = pltpu.roll(grp, HEAD_DIM, 1)
        qa_ref[0, 2 * g] = _fp8_operand(jnp.where(first_half, grp, swp), first_half, neg_r, False)
        qa_ref[0, 2 * g + 1] = _fp8_operand(jnp.where(first_half, swp, grp), first_half, neg_r, False)
    ks = k * consts[:, 1:2]
    ksw = pltpu.roll(ks, HEAD_DIM, 1)
    ka_ref[0, 0] = _fp8_operand(jnp.where(first_half, ks, ksw), first_half, 1.0, True)
    ka_ref[0, 1] = _fp8_operand(jnp.where(first_half, ksw, ks), first_half, 1.0, True)
    v_t = p[:, OFF_AV:OFF_AV + KV_W].T.astype(BF16)
    for hh in range(ATTN_KV_HEADS):
        vt_ref[0, hh, :HEAD_DIM, :] = v_t[hh * HEAD_DIM:(hh + 1) * HEAD_DIM]
        pad_row = lax.broadcasted_iota(jnp.int32, (V_ROWS - HEAD_DIM, x.shape[0]), 0)
        vt_ref[0, hh, HEAD_DIM:, :] = jnp.where(pad_row == 0, 1.0, 0.0).astype(BF16)
    hg_ref[0] = p[:, OFF_HG:OFF_MQK]
    mqk_ref[0] = p[:, OFF_MQK:OFF_MVO]
    mvo_ref[0] = p[:, OFF_MVO:OFF_MG]
    mg_ref[0] = p[:, OFF_MG:OFF_MG + 128]
    mvt_ref[0] = p[:, OFF_MVO:OFF_MVO + ML_W].T.astype(BF16)


def _in_call(h_ctx, h_lat, lat_off, t, modsel, g, w, b, cos, sin, gq, gk, e, attn_consts, nct):
    bsz, _, d = h_ctx.shape
    nt = t // TM
    row = lambda width: pl.BlockSpec((1, TM, width), lambda bi, i: (bi, i, 0))
    full = lambda a: pl.BlockSpec(a.shape, lambda bi, i: (0,) * a.ndim)
    out_widths = (OFF_MQK - OFF_HG, 2 * ML_W, 2 * ML_W, 128)
    ctx_spec, lat_spec = _stream_spec(nct, lat_off)
    return pl.pallas_call(
        functools.partial(_in_kernel, nct=nct),
        grid=(bsz, nt),
        in_specs=[ctx_spec(h_ctx), lat_spec(h_lat),
                  pl.BlockSpec((1, 1, N_MOD, d), lambda bi, i: (bi, (i >= nct).astype(jnp.int32), 0, 0)),
                  full(g), full(w), full(b),
                  pl.BlockSpec((TM, 128), lambda bi, i: (i, 0)),
                  pl.BlockSpec((TM, 128), lambda bi, i: (i, 0)),
                  full(gq), full(gk), full(e), full(attn_consts)],
        out_specs=[pl.BlockSpec((1, ATTN_HEADS, TM, QK_ROWS), lambda bi, i: (bi, 0, i, 0)),
                   pl.BlockSpec((1, ATTN_KV_HEADS, TM, QK_ROWS), lambda bi, i: (bi, 0, i, 0)),
                   pl.BlockSpec((1, ATTN_KV_HEADS, V_ROWS, TM), lambda bi, i: (bi, 0, 0, i))]
        + [row(wd) for wd in out_widths] + [pl.BlockSpec((1, ML_W, TM), lambda bi, i: (bi, 0, i))],
        out_shape=[jax.ShapeDtypeStruct((bsz, ATTN_HEADS, t, QK_ROWS), FP8),
                   jax.ShapeDtypeStruct((bsz, ATTN_KV_HEADS, t, QK_ROWS), FP8),
                   jax.ShapeDtypeStruct((bsz, ATTN_KV_HEADS, V_ROWS, t), BF16)]
        + [jax.ShapeDtypeStruct((bsz, t, wd), F32) for wd in out_widths]
        + [jax.ShapeDtypeStruct((bsz, ML_W, t), BF16)],
        compiler_params=_cparams(("parallel", "parallel")),
        name="norm_mod_in_proj",
    )(h_ctx, h_lat, modsel, g, w, b, cos, sin, gq, gk, e, attn_consts)


def _attn_kernel(qa_ref, ka_ref, vt_ref, unscale_ref, o_ref, acc_ref, s_ref, *maybe_m_ref,
                 online, n_ctx_q, n_ctx_k, n_k):
    qi = pl.program_id(2)
    unscale = unscale_ref[...]
    acc_ref[...] = jnp.zeros(acc_ref.shape, F32)
    if online:
        m_ref, = maybe_m_ref
        m_ref[...] = jnp.full(m_ref.shape, NEG_BIG, F32)
    n_blocks = jnp.where(qi < n_ctx_q, n_ctx_k, n_k)

    def key_tile(kb):
        return ka_ref[0, 0, pl.ds(pl.multiple_of(kb * TK, TK), TK), :]

    def produce(kt, slot, hh):
        s = _dot_nt(kt, qa_ref[0, hh]) * unscale
        s_ref[slot, hh] = s if online else jnp.exp2(s.astype(BF16))

    def step(kb, cur, nxt):
        vt = vt_ref[0, 0, :, pl.ds(pl.multiple_of(kb * TK, TK), TK)]
        kt = key_tile(jnp.minimum(kb + 1, n_blocks - 1))
        for hh in range(ATTN_REP):
            produce(kt, nxt, hh)
            if online:
                s = s_ref[cur, hh]
                m_old = m_ref[hh]
                m_new = jnp.maximum(m_old, jnp.max(s, axis=0, keepdims=True))
                p = jnp.exp2(s - m_new).astype(BF16)
                acc_ref[hh] = jnp.exp2(m_old - m_new) * acc_ref[hh] + _dot(vt, p)
                m_ref[hh] = m_new
            else:
                acc_ref[hh] += _dot(vt, s_ref[cur, hh])

    kt0 = key_tile(0)
    for hh in range(ATTN_REP):
        produce(kt0, 0, hh)

    def group(i, carry):
        for u in range(ATTN_UNROLL):
            step(ATTN_UNROLL * i + u, u % 2, (u + 1) % 2)
        return carry

    def pair(i, carry):
        step(2 * i, 0, 1)
        step(2 * i + 1, 1, 0)
        return carry

    n_groups = n_blocks // ATTN_UNROLL
    lax.fori_loop(0, n_groups, group, 0)
    lax.fori_loop(n_groups * (ATTN_UNROLL // 2), n_blocks // 2, pair, 0)

    @pl.when(n_blocks % 2 == 1)
    def _():
        step(n_blocks - 1, 0, 1)

    for hh in range(ATTN_REP):
        acc = acc_ref[hh]
        o = acc[:HEAD_DIM] / acc[HEAD_DIM:HEAD_DIM + 1]
        o_ref[0, hh * HEAD_DIM:(hh + 1) * HEAD_DIM, :] = o.astype(o_ref.dtype)


def _attn_call(qa, ka, vt, unscale, n_ctx, online):
    bsz, _, t, _ = qa.shape
    kern = functools.partial(_attn_kernel, online=online, n_ctx_q=n_ctx // TQ, n_ctx_k=n_ctx // TK, n_k=t // TK)
    gw = ATTN_REP * HEAD_DIM
    scratch = [pltpu.VMEM((ATTN_REP, V_ROWS, TQ), F32),
               pltpu.VMEM((2, ATTN_REP, TK, TQ), F32 if online else BF16)]
    if online:
        scratch.append(pltpu.VMEM((ATTN_REP, 1, TQ), F32))
    return pl.pallas_call(
        kern,
        grid=(bsz, ATTN_KV_HEADS, t // TQ),
        in_specs=[pl.BlockSpec((1, ATTN_REP, TQ, QK_ROWS), lambda b, g, i: (b, g, i, 0)),
                  pl.BlockSpec((1, 1, t, QK_ROWS), lambda b, g, i: (b, g, 0, 0)),
                  pl.BlockSpec((1, 1, V_ROWS, t), lambda b, g, i: (b, g, 0, 0)),
                  pl.BlockSpec((1, 1), lambda b, g, i: (0, 0))],
        out_specs=pl.BlockSpec((1, gw, TQ), lambda b, g, i: (b, g, i)),
        out_shape=jax.ShapeDtypeStruct((bsz, ATTN_W, t), BF16),
        scratch_shapes=scratch,
        compiler_params=_cparams(("parallel", "parallel", "arbitrary")),
        name="attention_online" if online else "attention",
    )(qa, ka, vt, unscale)


def _attn_constants(q_gain, k_gain):
    qmax = jnp.maximum(jnp.sqrt(float(HEAD_DIM)) * Q_SCALE * jnp.max(jnp.abs(q_gain)), 1e-6)
    kmax = jnp.maximum(jnp.sqrt(float(HEAD_DIM)) * jnp.max(jnp.abs(k_gain)), 1e-6)
    shift = qmax * kmax
    q_scale = jnp.exp2(jnp.floor(jnp.log2(FP8_TARGET / qmax)))
    k_scale = jnp.exp2(jnp.floor(jnp.log2(FP8_TARGET / kmax)))
    r = (jnp.minimum(shift, MAX_FIXED_SHIFT) * q_scale * k_scale * (SHIFT_FRACTION / SHIFT_ROWS)).astype(FP8).astype(F32)
    in_proj_consts = jnp.pad(jnp.stack([Q_SCALE * q_scale, k_scale, r]), (0, 125)).reshape(1, 128)
    unscale = (1.0 / (q_scale * k_scale)).reshape(1, 1)
    return in_proj_consts, unscale, shift <= MAX_FIXED_SHIFT


def _attention(qa, ka, vt, unscale, use_fixed_shift, n_ctx):
    return lax.cond(use_fixed_shift,
                    lambda: _attn_call(qa, ka, vt, unscale, n_ctx, False),
                    lambda: _attn_call(qa, ka, vt, unscale, n_ctx, True))


def _conv_kernel(x_ref, prev_ref, next_ref, w_ref, o_ref, *, nct, nb):
    i = pl.program_id(1)
    x = x_ref[0]
    rows = x.shape[0]
    left_edge = jnp.logical_or(i == 0, i == nct)
    right_edge = jnp.logical_or(i == nct - 1, i == nb - 1)
    prev_row = jnp.where(left_edge, 0.0, prev_ref[0, 7:8, :])
    next_row = jnp.where(right_edge, 0.0, next_ref[0, 0:1, :])
    ridx = lax.broadcasted_iota(jnp.int32, x.shape, 0)
    xm1 = jnp.where(ridx == 0, prev_row, pltpu.roll(x, 1, 0))
    xp1 = jnp.where(ridx == rows - 1, next_row, pltpu.roll(x, rows - 1, 0))
    w = w_ref[...]
    y = xm1 * w[0:1] + x * w[1:2] + xp1 * w[2:3]
    y = y * _sigmoid(y)
    lane = lax.broadcasted_iota(jnp.int32, x.shape, 1)
    y = jnp.where(lane >= ML_W, y * (ML_DK ** -0.5), y)
    o_ref[0] = y.astype(o_ref.dtype)


def _conv_call(mqk, w, nct):
    bsz, t, width = mqk.shape
    nb = t // SCAN_BLK
    per8 = SCAN_BLK // 8
    last8 = t // 8 - 1
    kern = functools.partial(_conv_kernel, nct=nct, nb=nb)
    return pl.pallas_call(
        kern,
        grid=(bsz, nb),
        in_specs=[pl.BlockSpec((1, SCAN_BLK, width), lambda b, i: (b, i, 0)),
                  pl.BlockSpec((1, 8, width), lambda b, i: (b, jnp.maximum(i * per8 - 1, 0), 0)),
                  pl.BlockSpec((1, 8, width), lambda b, i: (b, jnp.minimum((i + 1) * per8, last8), 0)),
                  pl.BlockSpec(w.shape, lambda b, i: (0, 0))],
        out_specs=pl.BlockSpec((1, SCAN_BLK, width), lambda b, i: (b, i, 0)),
        out_shape=jax.ShapeDtypeStruct((bsz, t, width), BF16),
        compiler_params=_cparams(("parallel", "parallel")),
        name="mlstm_qk_conv",
    )(mqk, mqk, mqk, w)


def _scan_block(d, j, nct, nb):
    bwd = jnp.where(j < nct, nct - 1 - j, nb - 1 - (j - nct))
    return jnp.where(d == 0, j, bwd)


def _hgrn_kernel(q0_ref, q1_ref, v0_ref, v1_ref, z0_ref, z1_ref, lb_ref, e_ref, bd_ref, o_ref,
                 st_ref, qd_ref, kd_ref, od_ref, dt_ref, u_ref):
    d = pl.program_id(1)
    j = pl.program_id(2)

    @pl.when(j == 0)
    def _():
        st_ref[...] = jnp.zeros(st_ref.shape, F32)

    lb = lb_ref[0]
    e = e_ref[...]
    bd = bd_ref[...]
    c = HG_CHUNK
    nch = SCAN_BLK // c

    def rows(i):
        return pl.ds(jnp.where(d == 0, i, c - 1 - i), nch, stride=c)

    def load2(r0, r1, sl):
        return jnp.concatenate([r0[0, sl, :], r1[0, sl, :]], axis=1)

    def load_halves(ref, sl):
        return jnp.concatenate([ref[0, sl, :], ref[1, sl, :]], axis=1)

    def store_halves(ref, sl, x):
        ref[0, sl, :] = x[:, :128]
        ref[1, sl, :] = x[:, 128:]

    qs, ks, vs, bs = [], [], [], []
    b = None
    for i in range(c):
        f = lb + (1.0 - lb) * _sigmoid(load2(z0_ref, z1_ref, rows(i)))
        lf = jnp.log2(f)
        b = lf if b is None else b + lf
        qs.append(load2(q0_ref, q1_ref, rows(i)))
        vs.append(load2(v0_ref, v1_ref, rows(i)))
        ks.append(1.0 - f)
        bs.append(b)
    b_tot = bs[-1]
    dt_ref[...] = jnp.exp2(b_tot)
    for i in range(c):
        store_halves(qd_ref, rows(i), qs[i] * jnp.exp2(bs[i]))
        store_halves(kd_ref, rows(i), ks[i] * jnp.exp2(b_tot - bs[i]))
    for cc in range(nch):
        sl = pl.ds(cc * c, c)
        vc = load2(v0_ref, v1_ref, sl).astype(BF16)
        kc = load_halves(kd_ref, sl).astype(BF16)
        u_ref[cc] = _dot_tn(vc, kc) * bd
    for i in range(c):
        ps = [qs[i] * ks[jj] * jnp.exp2(bs[i] - bs[jj]) for jj in range(i)] + [qs[i] * ks[i]]
        a = _dot(jnp.concatenate(ps, axis=0).astype(BF16), e)
        o = a[0:nch] * vs[0]
        for jj in range(1, i + 1):
            o = o + a[jj * nch:(jj + 1) * nch] * vs[jj]
        store_halves(od_ref, rows(i), o)

    for n in range(nch):
        cc = jnp.where(d == 0, n, nch - 1 - n)
        sl = pl.ds(pl.multiple_of(cc * c, c), c)
        st = st_ref[...]
        o_inter = _dot_nt(load_halves(qd_ref, sl).astype(BF16), st.astype(BF16))
        o_ref[0, 0, sl, :] = load_halves(od_ref, sl) + o_inter
        st_ref[...] = st * dt_ref[pl.ds(cc, 1), :] + u_ref[cc]


def _hgrn_call(hgp, lbs, e, bd, nct):
    bsz, t, _ = hgp.shape
    nb = t // SCAN_BLK
    w = HG_W
    blk = lambda d, j: _scan_block(d, j, nct, nb)
    half = lambda col: pl.BlockSpec((1, SCAN_BLK, 128), lambda b, d, j: (b, blk(d, j), col(d)))
    return pl.pallas_call(
        _hgrn_kernel,
        grid=(bsz, 2, nb),
        in_specs=[half(lambda d: 0), half(lambda d: 1),
                  half(lambda d: 2), half(lambda d: 3),
                  half(lambda d: 4 + 2 * d), half(lambda d: 5 + 2 * d),
                  pl.BlockSpec((1, 1, w), lambda b, d, j: (d, 0, 0)),
                  pl.BlockSpec((w, w), lambda b, d, j: (0, 0)),
                  pl.BlockSpec((w, w), lambda b, d, j: (0, 0))],
        out_specs=pl.BlockSpec((1, 1, SCAN_BLK, w), lambda b, d, j: (d, b, blk(d, j), 0)),
        out_shape=jax.ShapeDtypeStruct((2, bsz, t, w), F32),
        scratch_shapes=[pltpu.VMEM((w, w), F32),
                        pltpu.VMEM((2, SCAN_BLK, 128), F32),
                        pltpu.VMEM((2, SCAN_BLK, 128), F32),
                        pltpu.VMEM((2, SCAN_BLK, 128), F32),
                        pltpu.VMEM((SCAN_BLK // HG_CHUNK, w), F32),
                        pltpu.VMEM((SCAN_BLK // HG_CHUNK, w, w), F32)],
        compiler_params=_cparams(("parallel", "parallel", "arbitrary")),
        name="hgrn2_scan",
    )(hgp, hgp, hgp, hgp, hgp, hgp, lbs, e, bd)


def _mlstm_kernel(q_ref, k_ref, vt_ref, g_ref, gt_ref, m_ref, mt_ref, o_ref, c_ref, ms_ref):
    j = pl.program_id(2)

    @pl.when(j == 0)
    def _():
        c_ref[...] = jnp.zeros(c_ref.shape, F32)
        ms_ref[...] = jnp.zeros(ms_ref.shape, F32)

    length = SCAN_BLK
    mask = m_ref[0]
    mask_t = mt_ref[0]
    valid_t = mask_t > 0
    gates = g_ref[0, 0]
    gates_t = gt_ref[0, 0]
    b_col_all = _mask_dot(mask, _log_sigmoid(gates))
    lf_t = _log_sigmoid(gates_t)
    b_row_all = _dot_mask(lf_t, mask_t)
    tot_all = jnp.sum(lf_t, axis=1, keepdims=True)
    row = lax.broadcasted_iota(jnp.int32, (128, length), 0)

    heads = range(ML_HEADS)
    kh = [k_ref[0, :, hh * ML_DK:(hh + 1) * ML_DK] for hh in heads]
    qh = [q_ref[0, :, hh * ML_DK:(hh + 1) * ML_DK] for hh in heads]
    s_raw = [_dot_nt(kh[hh], qh[hh]) for hh in heads]
    inter = [_dot_nt(c_ref[hh].astype(BF16), qh[hh]) for hh in heads]
    for pair in range(ML_HEADS // 2):
        v2t = vt_ref[0, pair * 128:(pair + 1) * 128, :]
        out_t = jnp.zeros((128, length), F32)
        for sub in range(2):
            hh = pair * 2 + sub
            own = (row < 64) if sub == 0 else (row >= 64)
            ones_row = 64 if sub == 0 else 0
            v_aug_t = jnp.where(own, v2t, jnp.where(row == ones_row, 1.0, 0.0).astype(BF16))
            m_prev = ms_ref[hh, 0:1, 0:1]
            cs = gates[:, hh:hh + 1] - b_col_all[:, 4 + hh:5 + hh]
            br = b_row_all[4 + hh:5 + hh, :]
            tot = tot_all[4 + hh:5 + hh, :]
            log_d = jnp.where(valid_t, br + cs, NEG_BIG)
            m_t = jnp.maximum(br + m_prev, jnp.max(log_d, axis=0, keepdims=True))
            s_t = (s_raw[hh] * jnp.exp(log_d - m_t)).astype(BF16)
            r_t = jnp.exp(br + m_prev - m_t) * inter[hh] + _dot(v_aug_t, s_t)
            den = r_t[ones_row:ones_row + 1, :]
            out_t = jnp.where(own, r_t / jnp.maximum(jnp.abs(den), jnp.exp(-m_t)), out_t)
            log_e = tot + cs
            m_new = jnp.maximum(tot + m_prev, jnp.max(log_e, axis=0, keepdims=True))
            ke = (kh[hh].astype(F32) * jnp.exp(log_e - m_new)).astype(BF16)
            c_ref[hh] = jnp.exp(tot + m_prev - m_new) * c_ref[hh] + _dot(v_aug_t, ke)
            ms_ref[hh] = jnp.broadcast_to(m_new, ms_ref.shape[1:])
        o_ref[0, 0, :, pair * 128:(pair + 1) * 128] = out_t.T


def _mlstm_call(qk, vt, gcol, grow, mask, mask_t, nct):
    bsz, t, _ = qk.shape
    nb = t // SCAN_BLK
    w = ML_W
    blk = lambda d, j: _scan_block(d, j, nct, nb)
    return pl.pallas_call(
        _mlstm_kernel,
        grid=(bsz, 2, nb),
        in_specs=[pl.BlockSpec((1, SCAN_BLK, w), lambda b, d, j: (b, blk(d, j), 0)),
                  pl.BlockSpec((1, SCAN_BLK, w), lambda b, d, j: (b, blk(d, j), 1)),
                  pl.BlockSpec((1, w, SCAN_BLK), lambda b, d, j: (b, 0, blk(d, j))),
                  pl.BlockSpec((1, 1, SCAN_BLK, 128), lambda b, d, j: (b, d, blk(d, j), 0)),
                  pl.BlockSpec((1, 1, 8, SCAN_BLK), lambda b, d, j: (b, d, 0, blk(d, j))),
                  pl.BlockSpec((1, SCAN_BLK, SCAN_BLK), lambda b, d, j: (d, 0, 0)),
                  pl.BlockSpec((1, SCAN_BLK, SCAN_BLK), lambda b, d, j: (d, 0, 0))],
        out_specs=pl.BlockSpec((1, 1, SCAN_BLK, w), lambda b, d, j: (d, b, blk(d, j), 0)),
        out_shape=jax.ShapeDtypeStruct((2, bsz, t, w), F32),
        scratch_shapes=[pltpu.VMEM((ML_HEADS, 128, ML_DK), F32),
                        pltpu.VMEM((ML_HEADS, 8, 128), F32)],
        compiler_params=_cparams(("parallel", "parallel", "arbitrary")),
        name="mlstm_scan",
    )(qk, qk, vt, gcol, grow, mask, mask_t)


def _out_kernel(hc_ref, hl_ref, mod_ref, at_ref, hf_ref, hb_ref, hgate_ref, gn_ref, e_ref, mf_ref, mb_ref, mgate_ref,
                wa_ref, wr_ref, wm_ref, o_ref, *, nct, skip):
    mod = mod_ref[0, 0]
    o = hf_ref[0, 0] + hb_ref[0, 0]
    ss = _dot((o * o).astype(BF16), e_ref[...])
    gate = hgate_ref[0]
    r = o * lax.rsqrt(ss + EPS) * gn_ref[...] * (gate * _sigmoid(gate))
    m = _sigmoid(mgate_ref[0]) * (mf_ref[0, 0] + mb_ref[0, 0])
    y = _dot_tn(at_ref[0], wa_ref[...]) + _dot(r.astype(BF16), wr_ref[...]) + _dot(m.astype(BF16), wm_ref[...])
    h = jnp.where(pl.program_id(1) + skip < nct, hc_ref[0], hl_ref[0])
    o_ref[0] = h + mod[2:3] * y


def _out_call(h_ctx, h_lat, lat_off, t, modsel, at, hg_o, hgp, gn, e, ml_o, mvo, wa, wr, wm, nct, skip):
    bsz, _, d = h_ctx.shape
    nt = t // TM - skip
    full = lambda a: pl.BlockSpec(a.shape, lambda bi, i: (0,) * a.ndim)
    hg_col = (OFF_MQK - OFF_HG) // HG_W - 1
    ctx_spec, lat_spec = _stream_spec(nct, lat_off, skip)
    return pl.pallas_call(
        functools.partial(_out_kernel, nct=nct, skip=skip),
        grid=(bsz, nt),
        in_specs=[ctx_spec(h_ctx), lat_spec(h_lat),
                  pl.BlockSpec((1, 1, N_MOD, d), lambda bi, i: (bi, (i + skip >= nct).astype(jnp.int32), 0, 0)),
                  pl.BlockSpec((1, ATTN_W, TM), lambda bi, i: (bi, 0, i + skip)),
                  pl.BlockSpec((1, 1, TM, HG_W), lambda bi, i: (0, bi, i + skip, 0)),
                  pl.BlockSpec((1, 1, TM, HG_W), lambda bi, i: (1, bi, i + skip, 0)),
                  pl.BlockSpec((1, TM, HG_W), lambda bi, i: (bi, i + skip, hg_col)),
                  full(gn), full(e),
                  pl.BlockSpec((1, 1, TM, ML_W), lambda bi, i: (0, bi, i + skip, 0)),
                  pl.BlockSpec((1, 1, TM, ML_W), lambda bi, i: (1, bi, i + skip, 0)),
                  pl.BlockSpec((1, TM, ML_W), lambda bi, i: (bi, i + skip, 1)),
                  full(wa), full(wr), full(wm)],
        out_specs=pl.BlockSpec((1, TM, d), lambda bi, i: (bi, i, 0)),
        out_shape=jax.ShapeDtypeStruct((bsz, nt * TM, d), F32),
        compiler_params=_cparams(("parallel", "parallel")),
        name="readout_out_proj",
    )(h_ctx, h_lat, modsel, at, hg_o, hg_o, hgp, gn, e, ml_o, ml_o, mvo, wa, wr, wm)


def _ffn_kernel(h_ref, mod_ref, g_ref, w1_ref, w2_ref, o_ref):
    x = h_ref[0]
    mod = mod_ref[0, 0]
    ms = jnp.mean(x * x, axis=-1, keepdims=True)
    y = x * lax.rsqrt(ms + EPS) * g_ref[...]
    u = (y * (1.0 + mod[4:5]) + mod[3:4]).astype(BF16)
    a = jnp.maximum(_dot(u, w1_ref[...]), 0.0)
    a = (a * a).astype(BF16)
    o_ref[0] = x + mod[5:6] * _dot(a, w2_ref[...])


def _ffn_call(h, modsel, g, w1, w2, nct):
    bsz, t, d = h.shape
    nt = t // TM
    full = lambda a: pl.BlockSpec(a.shape, lambda bi, i: (0,) * a.ndim, pipeline_mode=pl.Buffered(1))
    return pl.pallas_call(
        _ffn_kernel,
        grid=(bsz, nt),
        in_specs=[pl.BlockSpec((1, TM, d), lambda bi, i: (bi, i, 0)),
                  pl.BlockSpec((1, 1, N_MOD, d), lambda bi, i: (bi, (i >= nct).astype(jnp.int32), 0, 0)),
                  pl.BlockSpec(g.shape, lambda bi, i: (0, 0)),
                  full(w1), full(w2)],
        out_specs=pl.BlockSpec((1, TM, d), lambda bi, i: (bi, i, 0)),
        out_shape=jax.ShapeDtypeStruct((bsz, t, d), F32),
        compiler_params=_cparams(("parallel", "parallel")),
        name="ffn",
    )(h, modsel, g, w1, w2)


def _rope_tables(n_ctx, n_lat):
    inv_freq = ROPE_THETA ** (-np.arange(0, ROPE_AXIS_DIM, 2, dtype=np.float32) / ROPE_AXIS_DIM)
    inv_freq = jnp.asarray(inv_freq, F32)
    tok = jnp.arange(n_lat)
    ang_row = (tok // GRID_W).astype(F32)[:, None] * inv_freq[None, :]
    ang_col = (tok % GRID_W).astype(F32)[:, None] * inv_freq[None, :]
    ang = jnp.concatenate([ang_row, ang_row, ang_col, ang_col], axis=1)
    ang = jnp.concatenate([jnp.zeros((n_ctx, HEAD_DIM), F32), ang], axis=0)
    sign = np.where((np.arange(HEAD_DIM) % 32) < 16, -1.0, 1.0).astype(np.float32)
    cos = jnp.cos(ang)
    sin = jnp.sin(ang) * jnp.asarray(sign)[None, :]
    return jnp.concatenate([cos, cos], axis=1), jnp.concatenate([sin, sin], axis=1)


def _block_ones(width, head, scale, dtype):
    idx = np.arange(width) // head
    return jnp.asarray((idx[:, None] == idx[None, :]).astype(np.float32) * scale, dtype)


def _scan_masks(n):
    tril = np.tril(np.ones((n, n), np.float32))
    return np.stack([tril, tril.T])


def kernel(x, c, ctx, c_ctx, w_mod, b_mod, norm_mix, norm_ffn, w_in, b_in, q_norm, k_norm,
           hg_lb, hg_norm, ml_conv, w_out, w_ff1, w_ff2):
    bsz, n_lat, d = x.shape
    n_ctx = ctx.shape[1]
    depth = w_mod.shape[0]
    t = n_ctx + n_lat
    nct = n_ctx // TM
    assert n_ctx % SCAN_BLK == 0 and n_lat % SCAN_BLK == 0 and n_lat % GRID_W == 0 and bsz + 1 <= 8

    cvec = jnp.concatenate([c, c_ctx[None, :], jnp.zeros((8 - bsz - 1, d), F32)], axis=0)
    mods = _mod_call(cvec, w_mod, b_mod).reshape(depth, 8, N_MOD, d)
    modsel = jnp.stack([jnp.broadcast_to(mods[:, bsz][:, None], (depth, bsz, N_MOD, d)), mods[:, :bsz]], axis=2)

    cos, sin = _rope_tables(n_ctx, n_lat)
    e_attn = _block_ones(ATTN_W, HEAD_DIM, 1.0 / HEAD_DIM, BF16)
    e_hg_mean = _block_ones(HG_W, HG_DK, 1.0 / HG_DK, BF16)
    e_hg = _block_ones(HG_W, HG_DK, 1.0, BF16)
    bd_hg = _block_ones(HG_W, HG_DK, 1.0, F32)
    scan_mask = _scan_masks(SCAN_BLK)
    mask_ml = jnp.asarray(scan_mask, BF16)
    mask_ml_t = jnp.asarray(np.transpose(scan_mask, (0, 2, 1)).copy(), BF16)

    lbs = jnp.cumsum(jax.nn.softmax(hg_lb.astype(F32), axis=0), axis=0)
    lbs = (lbs - lbs[:1]).reshape(depth, 2, 1, HG_W)

    w_in_p = jnp.pad(w_in, ((0, 0), (0, 0), (0, N_IN_PAD - N_IN))).astype(BF16)
    b_in_p = jnp.pad(b_in, ((0, 0), (0, N_IN_PAD - N_IN))).reshape(depth, 1, N_IN_PAD)
    w_out_b = w_out.astype(BF16)
    w1_b = w_ff1.astype(BF16)
    w2_b = w_ff2.astype(BF16)
    gq = jnp.tile(q_norm, (1, ATTN_HEADS)).reshape(depth, 1, ATTN_W)
    gk = jnp.tile(k_norm, (1, ATTN_KV_HEADS)).reshape(depth, 1, KV_W)
    gn = jnp.tile(hg_norm, (1, HG_HEADS)).reshape(depth, 1, HG_W)

    h_ctx, h_lat, lat_off = ctx, x, 0
    for l in range(depth):
        last = l == depth - 1
        attn_consts, unscale, use_fixed_shift = _attn_constants(q_norm[l], k_norm[l])
        qa, ka, vt, hgp, mqk, mvo, mg, mvt = _in_call(
            h_ctx, h_lat, lat_off, t, modsel[l], norm_mix[l].reshape(1, d), w_in_p[l], b_in_p[l], cos, sin,
            gq[l], gk[l], e_attn, attn_consts, nct)

        at = _attention(qa, ka, vt, unscale, use_fixed_shift, n_ctx)

        hg_o = _hgrn_call(hgp, lbs[l], e_hg, bd_hg, nct)

        qk = _conv_call(mqk, ml_conv[l], nct)
        g16 = mg[:, :, :16].reshape(bsz, t, 2, 2, ML_HEADS)
        gdir = g16.transpose(0, 3, 1, 2, 4).reshape(bsz, 2, t, 8)
        gcol = jnp.pad(gdir, ((0, 0), (0, 0), (0, 0), (0, 120)))
        grow = gdir.transpose(0, 1, 3, 2)
        ml_o = _mlstm_call(qk, mvt, gcol, grow, mask_ml, mask_ml_t, nct)

        skip = nct if last else 0
        h1 = _out_call(h_ctx, h_lat, lat_off, t, modsel[l], at, hg_o, hgp, gn[l], e_hg_mean, ml_o, mvo,
                       w_out_b[l, :ATTN_W], w_out_b[l, ATTN_W:ATTN_W + HG_W], w_out_b[l, ATTN_W + HG_W:], nct, skip)
        h = _ffn_call(h1, modsel[l], norm_ffn[l].reshape(1, d), w1_b[l], w2_b[l], 0 if last else nct)
        h_ctx, h_lat, lat_off = h, h, nct
    return h
```

```python
import functools

import jax
import jax.numpy as jnp
import numpy as np
from jax import lax
from jax.experimental import pallas as pl
from jax.experimental.pallas import tpu as pltpu

F32 = jnp.float32
BF16 = jnp.bfloat16

GRID_W = 64
HEAD_DIM = 64
ATTN_HEADS = 8
ATTN_KV_HEADS = 2
ATTN_REP = ATTN_HEADS // ATTN_KV_HEADS
ROPE_THETA = 10000.0
ROPE_AXIS_DIM = HEAD_DIM // 2
HG_HEADS = 4
HG_DK = 64
ML_HEADS = 4
ML_DK = 64
N_MOD = 6
EPS = 1e-6
ATTN_W = ATTN_HEADS * HEAD_DIM
KV_W = ATTN_KV_HEADS * HEAD_DIM
HG_W = HG_HEADS * HG_DK
ML_W = ML_HEADS * ML_DK

OFF_AQ, OFF_AK, OFF_AV = 0, 512, 640
OFF_HG = 768
OFF_MQK = 2048
OFF_MVO = 2560
OFF_MG = 3072
N_IN = 3088
N_IN_PAD = 3200

TM = 256
TQ = 256
TK = 256
ATTN_UNROLL = 32
SCAN_BLK = 256
CONV_BLKS = (1280, 1024, 768, 512, 256)
HG_CHUNK = 16
V_ROWS = 80
FP8 = jnp.float8_e4m3fn
QK_ROWS = 256
SHIFT_ROWS = QK_ROWS - 3 * HEAD_DIM
FP8_TARGET = 128.0
MAX_FIXED_SHIFT = 32.0
SHIFT_FRACTION = 0.5
NEG_BIG = -1e30
Q_SCALE = float(np.log2(np.e)) * HEAD_DIM ** -0.5
VMEM_LIMIT = 56 * 1024 * 1024


def _cparams(sem):
    return pltpu.CompilerParams(dimension_semantics=sem, vmem_limit_bytes=VMEM_LIMIT)


def _split3(x):
    hi = x.astype(BF16)
    r1 = x - hi.astype(F32)
    mid = r1.astype(BF16)
    lo = (r1 - mid.astype(F32)).astype(BF16)
    return hi, mid, lo


def _dot(a, b):
    return jnp.dot(a, b, preferred_element_type=F32)


def _dot_nt(a, b):
    return lax.dot_general(a, b, (((1,), (1,)), ((), ())), preferred_element_type=F32)


def _dot_tn(a, b):
    return lax.dot_general(a, b, (((0,), (0,)), ((), ())), preferred_element_type=F32)


def _mask_dot(mask_bf16, x):
    hi, mid, lo = _split3(x)
    return _dot(mask_bf16, hi) + _dot(mask_bf16, mid) + _dot(mask_bf16, lo)


def _dot_mask(x, mask_bf16):
    hi, mid, lo = _split3(x)
    return _dot(hi, mask_bf16) + _dot(mid, mask_bf16) + _dot(lo, mask_bf16)


def _sigmoid(x):
    return 1.0 / (1.0 + jnp.exp(-x))


def _log_sigmoid(x):
    return jnp.minimum(x, 0.0) - jnp.log(1.0 + jnp.exp(-jnp.abs(x)))


def _mod_kernel(c_ref, w_ref, b_ref, o_ref):
    cv = c_ref[...]
    a = cv * _sigmoid(cv)
    o_ref[0] = jnp.dot(a, w_ref[0], preferred_element_type=F32, precision=lax.Precision.HIGHEST) + b_ref[0]


def _mod_call(cvec, w_mod, b_mod):
    depth, d, nm = w_mod.shape
    tn = 1024
    return pl.pallas_call(
        _mod_kernel,
        grid=(depth, nm // tn),
        in_specs=[pl.BlockSpec((8, d), lambda l, j: (0, 0)),
                  pl.BlockSpec((1, d, tn), lambda l, j: (l, 0, j)),
                  pl.BlockSpec((1, 1, tn), lambda l, j: (l, 0, j))],
        out_specs=pl.BlockSpec((1, 8, tn), lambda l, j: (l, 0, j)),
        out_shape=jax.ShapeDtypeStruct((depth, 8, nm), F32),
        compiler_params=_cparams(("arbitrary", "arbitrary")),
        name="mod_vectors",
    )(cvec, w_mod, b_mod.reshape(depth, 1, nm))


def _head_rms(t, gain, e):
    ss = _dot((t * t).astype(BF16), e)
    return t * lax.rsqrt(ss + EPS) * gain


def _rope(t, cos, sin):
    w = t.shape[1]
    reps = w // 128
    cosw = jnp.concatenate([cos] * reps, axis=1) if reps > 1 else cos
    sinw = jnp.concatenate([sin] * reps, axis=1) if reps > 1 else sin
    lane = lax.broadcasted_iota(jnp.int32, t.shape, 1)
    first_half = (lane % 32) < 16
    partner = jnp.where(first_half, pltpu.roll(t, w - 16, 1), pltpu.roll(t, 16, 1))
    return t * cosw + partner * sinw


def _fp8_operand(d, first_half, fill, rem_first):
    rem = d - d.astype(FP8).astype(F32)
    mixed = jnp.where(first_half, rem, d) if rem_first else jnp.where(first_half, d, rem)
    filled = jnp.where(first_half, d, fill)
    return jnp.concatenate([filled.astype(FP8), mixed.astype(FP8)], axis=1)


def _stream_spec(nct, lat_off, skip=0):
    d_last = lambda a: a.shape[-1]
    ctx_spec = lambda a: pl.BlockSpec((1, TM, d_last(a)), lambda bi, i: (bi, jnp.minimum(i + skip, nct - 1), 0))
    lat_spec = lambda a: pl.BlockSpec((1, TM, d_last(a)), lambda bi, i: (bi, jnp.maximum(i + skip - nct, 0) + lat_off, 0))
    return ctx_spec, lat_spec


def _in_kernel(hc_ref, hl_ref, mod_ref, g_ref, w_ref, b_ref, cos_ref, sin_ref, gq_ref, gk_ref, e_ref, ac_ref,
               qa_ref, ka_ref, vt_ref, hg_ref, mqk_ref, mvo_ref, mvt_ref, gcol_ref, grow_ref, *, nct):
    x = jnp.where(pl.program_id(1) < nct, hc_ref[0], hl_ref[0])
    ms = jnp.mean(x * x, axis=-1, keepdims=True)
    y = x * lax.rsqrt(ms + EPS) * g_ref[...]
    mod = mod_ref[0, 0]
    u = y * (1.0 + mod[1:2]) + mod[0:1]
    p = _dot(u.astype(BF16), w_ref[...]) + b_ref[...]
    cos = cos_ref[...]
    sin = sin_ref[...]
    e = e_ref[...]
    q = _rope(_head_rms(p[:, OFF_AQ:OFF_AQ + ATTN_W], gq_ref[...], e), cos, sin)
    k = _rope(_head_rms(p[:, OFF_AK:OFF_AK + KV_W], gk_ref[...], e[:KV_W, :KV_W]), cos, sin)

    consts = ac_ref[...]
    first_half = lax.broadcasted_iota(jnp.int32, (x.shape[0], 128), 1) < HEAD_DIM
    qs = q * consts[:, 0:1]
    neg_r = -consts[:, 2:3]
    for g in range(ATTN_W // 128):
        grp = qs[:, g * 128:(g + 1) * 128]
        swp = pltpu.roll(grp, HEAD_DIM, 1)
        qa_ref[0, 2 * g] = _fp8_operand(jnp.where(first_half, grp, swp), first_half, neg_r, False)
        qa_ref[0, 2 * g + 1] = _fp8_operand(jnp.where(first_half, swp, grp), first_half, neg_r, False)
    ks = k * consts[:, 1:2]
    ksw = pltpu.roll(ks, HEAD_DIM, 1)
    ka_ref[0, 0] = _fp8_operand(jnp.where(first_half, ks, ksw), first_half, 1.0, True)
    ka_ref[0, 1] = _fp8_operand(jnp.where(first_half, ksw, ks), first_half, 1.0, True)
    v_t = p[:, OFF_AV:OFF_AV + KV_W].T.astype(BF16)
    for hh in range(ATTN_KV_HEADS):
        vt_ref[0, hh, :HEAD_DIM, :] = v_t[hh * HEAD_DIM:(hh + 1) * HEAD_DIM]
        pad_row = lax.broadcasted_iota(jnp.int32, (V_ROWS - HEAD_DIM, x.shape[0]), 0)
        vt_ref[0, hh, HEAD_DIM:, :] = jnp.where(pad_row == 0, 1.0, 0.0).astype(BF16)
    hg_ref[0] = p[:, OFF_HG:OFF_MQK]
    mqk_ref[0] = p[:, OFF_MQK:OFF_MVO]
    mvo_ref[0] = p[:, OFF_MVO:OFF_MG]
    mvt_ref[0] = p[:, OFF_MVO:OFF_MVO + ML_W].T.astype(BF16)
    gates = p[:, OFF_MG:OFF_MG + 128]
    lane = lax.broadcasted_iota(jnp.int32, gates.shape, 1)
    by_heads = pltpu.roll(gates, 128 - ML_HEADS, 1)
    by_2heads = pltpu.roll(gates, 128 - 2 * ML_HEADS, 1)
    for direction, (igate, fgate) in enumerate(((gates, by_heads), (by_heads, by_2heads))):
        g_dir = jnp.where(lane < ML_HEADS, igate, jnp.where(lane < 2 * ML_HEADS, fgate, 0.0))
        gcol_ref[0, direction] = g_dir
        grow_ref[0, direction] = g_dir.T[:2 * ML_HEADS]


def _in_call(h_ctx, h_lat, lat_off, t, modsel, g, w, b, cos, sin, gq, gk, e, attn_consts, nct):
    bsz, _, d = h_ctx.shape
    nt = t // TM
    row = lambda width: pl.BlockSpec((1, TM, width), lambda bi, i: (bi, i, 0))
    full = lambda a: pl.BlockSpec(a.shape, lambda bi, i: (0,) * a.ndim)
    out_widths = (OFF_MQK - OFF_HG, 2 * ML_W, 2 * ML_W)
    ctx_spec, lat_spec = _stream_spec(nct, lat_off)
    return pl.pallas_call(
        functools.partial(_in_kernel, nct=nct),
        grid=(bsz, nt),
        in_specs=[ctx_spec(h_ctx), lat_spec(h_lat),
                  pl.BlockSpec((1, 1, N_MOD, d), lambda bi, i: (bi, (i >= nct).astype(jnp.int32), 0, 0)),
                  full(g), full(w), full(b),
                  pl.BlockSpec((TM, 128), lambda bi, i: (i, 0)),
                  pl.BlockSpec((TM, 128), lambda bi, i: (i, 0)),
                  full(gq), full(gk), full(e), full(attn_consts)],
        out_specs=[pl.BlockSpec((1, ATTN_HEADS, TM, QK_ROWS), lambda bi, i: (bi, 0, i, 0)),
                   pl.BlockSpec((1, ATTN_KV_HEADS, TM, QK_ROWS), lambda bi, i: (bi, 0, i, 0)),
                   pl.BlockSpec((1, ATTN_KV_HEADS, V_ROWS, TM), lambda bi, i: (bi, 0, 0, i))]
        + [row(wd) for wd in out_widths]
        + [pl.BlockSpec((1, ML_W, TM), lambda bi, i: (bi, 0, i)),
           pl.BlockSpec((1, 2, TM, 128), lambda bi, i: (bi, 0, i, 0)),
           pl.BlockSpec((1, 2, 2 * ML_HEADS, TM), lambda bi, i: (bi, 0, 0, i))],
        out_shape=[jax.ShapeDtypeStruct((bsz, ATTN_HEADS, t, QK_ROWS), FP8),
                   jax.ShapeDtypeStruct((bsz, ATTN_KV_HEADS, t, QK_ROWS), FP8),
                   jax.ShapeDtypeStruct((bsz, ATTN_KV_HEADS, V_ROWS, t), BF16)]
        + [jax.ShapeDtypeStruct((bsz, t, wd), F32) for wd in out_widths]
        + [jax.ShapeDtypeStruct((bsz, ML_W, t), BF16),
           jax.ShapeDtypeStruct((bsz, 2, t, 128), F32),
           jax.ShapeDtypeStruct((bsz, 2, 2 * ML_HEADS, t), F32)],
        compiler_params=_cparams(("parallel", "parallel")),
        name="norm_mod_in_proj",
    )(h_ctx, h_lat, modsel, g, w, b, cos, sin, gq, gk, e, attn_consts)


def _attn_kernel(qa_ref, ka_ref, vt_ref, unscale_ref, o_ref, acc_ref, s_ref, *maybe_m_ref,
                 online, n_ctx_q, n_ctx_k, n_k):
    qi = pl.program_id(2)
    unscale = unscale_ref[...]
    acc_ref[...] = jnp.zeros(acc_ref.shape, F32)
    if online:
        m_ref, = maybe_m_ref
        m_ref[...] = jnp.full(m_ref.shape, NEG_BIG, F32)
    n_blocks = jnp.where(qi < n_ctx_q, n_ctx_k, n_k)

    def key_tile(kb):
        return ka_ref[0, 0, pl.ds(pl.multiple_of(kb * TK, TK), TK), :]

    def produce(kt, slot, hh):
        s = _dot_nt(kt, qa_ref[0, hh]) * unscale
        s_ref[slot, hh] = s if online else jnp.exp2(s.astype(BF16))

    def step(kb, cur, nxt):
        vt = vt_ref[0, 0, :, pl.ds(pl.multiple_of(kb * TK, TK), TK)]
        kt = key_tile(jnp.minimum(kb + 1, n_blocks - 1))
        for hh in range(ATTN_REP):
            produce(kt, nxt, hh)
            if online:
                s = s_ref[cur, hh]
                m_old = m_ref[hh]
                m_new = jnp.maximum(m_old, jnp.max(s, axis=0, keepdims=True))
                p = jnp.exp2(s - m_new).astype(BF16)
                acc_ref[hh] = jnp.exp2(m_old - m_new) * acc_ref[hh] + _dot(vt, p)
                m_ref[hh] = m_new
            else:
                acc_ref[hh] += _dot(vt, s_ref[cur, hh])

    kt0 = key_tile(0)
    for hh in range(ATTN_REP):
        produce(kt0, 0, hh)

    def group(i, carry):
        for u in range(ATTN_UNROLL):
            step(ATTN_UNROLL * i + u, u % 2, (u + 1) % 2)
        return carry

    def pair(i, carry):
        step(2 * i, 0, 1)
        step(2 * i + 1, 1, 0)
        return carry

    n_groups = n_blocks // ATTN_UNROLL
    lax.fori_loop(0, n_groups, group, 0)
    lax.fori_loop(n_groups * (ATTN_UNROLL // 2), n_blocks // 2, pair, 0)

    @pl.when(n_blocks % 2 == 1)
    def _():
        step(n_blocks - 1, 0, 1)

    for hh in range(ATTN_REP):
        acc = acc_ref[hh]
        o = acc[:HEAD_DIM] / acc[HEAD_DIM:HEAD_DIM + 1]
        o_ref[0, hh * HEAD_DIM:(hh + 1) * HEAD_DIM, :] = o.astype(o_ref.dtype)


def _attn_call(qa, ka, vt, unscale, n_ctx, online):
    bsz, _, t, _ = qa.shape
    kern = functools.partial(_attn_kernel, online=online, n_ctx_q=n_ctx // TQ, n_ctx_k=n_ctx // TK, n_k=t // TK)
    gw = ATTN_REP * HEAD_DIM
    scratch = [pltpu.VMEM((ATTN_REP, V_ROWS, TQ), F32),
               pltpu.VMEM((2, ATTN_REP, TK, TQ), F32 if online else BF16)]
    if online:
        scratch.append(pltpu.VMEM((ATTN_REP, 1, TQ), F32))
    return pl.pallas_call(
        kern,
        grid=(bsz, ATTN_KV_HEADS, t // TQ),
        in_specs=[pl.BlockSpec((1, ATTN_REP, TQ, QK_ROWS), lambda b, g, i: (b, g, i, 0)),
                  pl.BlockSpec((1, 1, t, QK_ROWS), lambda b, g, i: (b, g, 0, 0)),
                  pl.BlockSpec((1, 1, V_ROWS, t), lambda b, g, i: (b, g, 0, 0)),
                  pl.BlockSpec((1, 1), lambda b, g, i: (0, 0))],
        out_specs=pl.BlockSpec((1, gw, TQ), lambda b, g, i: (b, g, i)),
        out_shape=jax.ShapeDtypeStruct((bsz, ATTN_W, t), BF16),
        scratch_shapes=scratch,
        compiler_params=_cparams(("parallel", "parallel", "arbitrary")),
        name="attention_online" if online else "attention",
    )(qa, ka, vt, unscale)


def _attn_constants(q_gain, k_gain):
    qmax = jnp.maximum(jnp.sqrt(float(HEAD_DIM)) * Q_SCALE * jnp.max(jnp.abs(q_gain)), 1e-6)
    kmax = jnp.maximum(jnp.sqrt(float(HEAD_DIM)) * jnp.max(jnp.abs(k_gain)), 1e-6)
    shift = qmax * kmax
    q_scale = jnp.exp2(jnp.floor(jnp.log2(FP8_TARGET / qmax)))
    k_scale = jnp.exp2(jnp.floor(jnp.log2(FP8_TARGET / kmax)))
    r = (jnp.minimum(shift, MAX_FIXED_SHIFT) * q_scale * k_scale * (SHIFT_FRACTION / SHIFT_ROWS)).astype(FP8).astype(F32)
    in_proj_consts = jnp.pad(jnp.stack([Q_SCALE * q_scale, k_scale, r]), (0, 125)).reshape(1, 128)
    unscale = (1.0 / (q_scale * k_scale)).reshape(1, 1)
    return in_proj_consts, unscale, shift <= MAX_FIXED_SHIFT


def _attention(qa, ka, vt, unscale, use_fixed_shift, n_ctx):
    return lax.cond(use_fixed_shift,
                    lambda: _attn_call(qa, ka, vt, unscale, n_ctx, False),
                    lambda: _attn_call(qa, ka, vt, unscale, n_ctx, True))


def _conv_kernel(x_ref, prev_ref, next_ref, w_ref, o_ref, *, n_ctx, t):
    x = x_ref[0]
    rows = x.shape[0]
    ridx = lax.broadcasted_iota(jnp.int32, x.shape, 0)
    tok = ridx + pl.program_id(1) * rows
    xm1 = jnp.where(ridx == 0, prev_ref[0, 7:8, :], pltpu.roll(x, 1, 0))
    xp1 = jnp.where(ridx == rows - 1, next_ref[0, 0:1, :], pltpu.roll(x, rows - 1, 0))
    xm1 = jnp.where(jnp.logical_or(tok == 0, tok == n_ctx), 0.0, xm1)
    xp1 = jnp.where(jnp.logical_or(tok == n_ctx - 1, tok == t - 1), 0.0, xp1)
    w = w_ref[...]
    y = xm1 * w[0:1] + x * w[1:2] + xp1 * w[2:3]
    y = y * _sigmoid(y)
    lane = lax.broadcasted_iota(jnp.int32, x.shape, 1)
    y = jnp.where(lane >= ML_W, y * (ML_DK ** -0.5), y)
    o_ref[0] = y.astype(o_ref.dtype)


def _conv_call(mqk, w, n_ctx):
    bsz, t, width = mqk.shape
    blk = next(b for b in CONV_BLKS if t % b == 0)
    nb = t // blk
    per8 = blk // 8
    last8 = t // 8 - 1
    kern = functools.partial(_conv_kernel, n_ctx=n_ctx, t=t)
    return pl.pallas_call(
        kern,
        grid=(bsz, nb),
        in_specs=[pl.BlockSpec((1, blk, width), lambda b, i: (b, i, 0)),
                  pl.BlockSpec((1, 8, width), lambda b, i: (b, jnp.maximum(i * per8 - 1, 0), 0)),
                  pl.BlockSpec((1, 8, width), lambda b, i: (b, jnp.minimum((i + 1) * per8, last8), 0)),
                  pl.BlockSpec(w.shape, lambda b, i: (0, 0))],
        out_specs=pl.BlockSpec((1, blk, width), lambda b, i: (b, i, 0)),
        out_shape=jax.ShapeDtypeStruct((bsz, t, width), BF16),
        compiler_params=_cparams(("parallel", "parallel")),
        name="mlstm_qk_conv",
    )(mqk, mqk, mqk, w)


def _scan_block(d, j, nct, nb):
    bwd = jnp.where(j < nct, nct - 1 - j, nb - 1 - (j - nct))
    return jnp.where(d == 0, j, bwd)


def _hgrn_kernel(q0_ref, q1_ref, v0_ref, v1_ref, z0_ref, z1_ref, lb_ref, e_ref, bd_ref, o_ref,
                 st_ref, qd_ref, kd_ref, od_ref, dt_ref, u_ref):
    d = pl.program_id(1)
    j = pl.program_id(2)

    @pl.when(j == 0)
    def _():
        st_ref[...] = jnp.zeros(st_ref.shape, F32)

    lb = lb_ref[0]
    e = e_ref[...]
    bd = bd_ref[...]
    c = HG_CHUNK
    nch = SCAN_BLK // c

    def rows(i):
        return pl.ds(jnp.where(d == 0, i, c - 1 - i), nch, stride=c)

    def load2(r0, r1, sl):
        return jnp.concatenate([r0[0, sl, :], r1[0, sl, :]], axis=1)

    def load_halves(ref, sl):
        return jnp.concatenate([ref[0, sl, :], ref[1, sl, :]], axis=1)

    def store_halves(ref, sl, x):
        ref[0, sl, :] = x[:, :128]
        ref[1, sl, :] = x[:, 128:]

    qs, ks, vs, bs = [], [], [], []
    b = None
    for i in range(c):
        f = lb + (1.0 - lb) * _sigmoid(load2(z0_ref, z1_ref, rows(i)))
        lf = jnp.log2(f)
        b = lf if b is None else b + lf
        qs.append(load2(q0_ref, q1_ref, rows(i)))
        vs.append(load2(v0_ref, v1_ref, rows(i)))
        ks.append(1.0 - f)
        bs.append(b)
    b_tot = bs[-1]
    dt_ref[...] = jnp.exp2(b_tot)
    for i in range(c):
        store_halves(qd_ref, rows(i), qs[i] * jnp.exp2(bs[i]))
        store_halves(kd_ref, rows(i), ks[i] * jnp.exp2(b_tot - bs[i]))
    for cc in range(nch):
        sl = pl.ds(cc * c, c)
        vc = load2(v0_ref, v1_ref, sl).astype(BF16)
        kc = load_halves(kd_ref, sl).astype(BF16)
        u_ref[cc] = _dot_tn(vc, kc) * bd
    for i in range(c):
        ps = [qs[i] * ks[jj] * jnp.exp2(bs[i] - bs[jj]) for jj in range(i)] + [qs[i] * ks[i]]
        a = _dot(jnp.concatenate(ps, axis=0).astype(BF16), e)
        o = a[0:nch] * vs[0]
        for jj in range(1, i + 1):
            o = o + a[jj * nch:(jj + 1) * nch] * vs[jj]
        store_halves(od_ref, rows(i), o)

    for n in range(nch):
        cc = jnp.where(d == 0, n, nch - 1 - n)
        sl = pl.ds(pl.multiple_of(cc * c, c), c)
        st = st_ref[...]
        o_inter = _dot_nt(load_halves(qd_ref, sl).astype(BF16), st.astype(BF16))
        o_ref[0, 0, sl, :] = load_halves(od_ref, sl) + o_inter
        st_ref[...] = st * dt_ref[pl.ds(cc, 1), :] + u_ref[cc]


def _hgrn_call(hgp, lbs, e, bd, nct):
    bsz, t, _ = hgp.shape
    nb = t // SCAN_BLK
    w = HG_W
    blk = lambda d, j: _scan_block(d, j, nct, nb)
    half = lambda col: pl.BlockSpec((1, SCAN_BLK, 128), lambda b, d, j: (b, blk(d, j), col(d)))
    return pl.pallas_call(
        _hgrn_kernel,
        grid=(bsz, 2, nb),
        in_specs=[half(lambda d: 0), half(lambda d: 1),
                  half(lambda d: 2), half(lambda d: 3),
                  half(lambda d: 4 + 2 * d), half(lambda d: 5 + 2 * d),
                  pl.BlockSpec((1, 1, w), lambda b, d, j: (d, 0, 0)),
                  pl.BlockSpec((w, w), lambda b, d, j: (0, 0)),
                  pl.BlockSpec((w, w), lambda b, d, j: (0, 0))],
        out_specs=pl.BlockSpec((1, 1, SCAN_BLK, w), lambda b, d, j: (d, b, blk(d, j), 0)),
        out_shape=jax.ShapeDtypeStruct((2, bsz, t, w), F32),
        scratch_shapes=[pltpu.VMEM((w, w), F32),
                        pltpu.VMEM((2, SCAN_BLK, 128), F32),
                        pltpu.VMEM((2, SCAN_BLK, 128), F32),
                        pltpu.VMEM((2, SCAN_BLK, 128), F32),
                        pltpu.VMEM((SCAN_BLK // HG_CHUNK, w), F32),
                        pltpu.VMEM((SCAN_BLK // HG_CHUNK, w, w), F32)],
        compiler_params=_cparams(("parallel", "parallel", "arbitrary")),
        name="hgrn2_scan",
    )(hgp, hgp, hgp, hgp, hgp, hgp, lbs, e, bd)


def _mlstm_kernel(q_ref, k_ref, vt_ref, g_ref, gt_ref, m_ref, mt_ref, o_ref, c_ref, ms_ref):
    j = pl.program_id(2)

    @pl.when(j == 0)
    def _():
        c_ref[...] = jnp.zeros(c_ref.shape, F32)
        ms_ref[...] = jnp.zeros(ms_ref.shape, F32)

    length = SCAN_BLK
    mask = m_ref[0]
    mask_t = mt_ref[0]
    valid_t = mask_t > 0
    gates = g_ref[0, 0]
    gates_t = gt_ref[0, 0]
    b_col_all = _mask_dot(mask, _log_sigmoid(gates))
    lf_t = _log_sigmoid(gates_t)
    b_row_all = _dot_mask(lf_t, mask_t)
    tot_all = jnp.sum(lf_t, axis=1, keepdims=True)
    row = lax.broadcasted_iota(jnp.int32, (128, length), 0)

    heads = range(ML_HEADS)
    kh = [k_ref[0, :, hh * ML_DK:(hh + 1) * ML_DK] for hh in heads]
    qh = [q_ref[0, :, hh * ML_DK:(hh + 1) * ML_DK] for hh in heads]
    s_raw = [_dot_nt(kh[hh], qh[hh]) for hh in heads]
    inter = [_dot_nt(c_ref[hh].astype(BF16), qh[hh]) for hh in heads]
    for pair in range(ML_HEADS // 2):
        v2t = vt_ref[0, pair * 128:(pair + 1) * 128, :]
        out_t = jnp.zeros((128, length), F32)
        for sub in range(2):
            hh = pair * 2 + sub
            own = (row < 64) if sub == 0 else (row >= 64)
            ones_row = 64 if sub == 0 else 0
            v_aug_t = jnp.where(own, v2t, jnp.where(row == ones_row, 1.0, 0.0).astype(BF16))
            m_prev = ms_ref[hh, 0:1, 0:1]
            cs = gates[:, hh:hh + 1] - b_col_all[:, 4 + hh:5 + hh]
            br = b_row_all[4 + hh:5 + hh, :]
            tot = tot_all[4 + hh:5 + hh, :]
            log_d = jnp.where(valid_t, br + cs, NEG_BIG)
            m_t = jnp.maximum(br + m_prev, jnp.max(log_d, axis=0, keepdims=True))
            s_t = (s_raw[hh] * jnp.exp(log_d - m_t)).astype(BF16)
            r_t = jnp.exp(br + m_prev - m_t) * inter[hh] + _dot(v_aug_t, s_t)
            den = r_t[ones_row:ones_row + 1, :]
            out_t = jnp.where(own, r_t / jnp.maximum(jnp.abs(den), jnp.exp(-m_t)), out_t)
            log_e = tot + cs
            m_new = jnp.maximum(tot + m_prev, jnp.max(log_e, axis=0, keepdims=True))
            ke = (kh[hh].astype(F32) * jnp.exp(log_e - m_new)).astype(BF16)
            c_ref[hh] = jnp.exp(tot + m_prev - m_new) * c_ref[hh] + _dot(v_aug_t, ke)
            ms_ref[hh] = jnp.broadcast_to(m_new, ms_ref.shape[1:])
        o_ref[0, 0, :, pair * 128:(pair + 1) * 128] = out_t.T


def _mlstm_call(qk, vt, gcol, grow, mask, mask_t, nct):
    bsz, t, _ = qk.shape
    nb = t // SCAN_BLK
    w = ML_W
    blk = lambda d, j: _scan_block(d, j, nct, nb)
    return pl.pallas_call(
        _mlstm_kernel,
        grid=(bsz, 2, nb),
        in_specs=[pl.BlockSpec((1, SCAN_BLK, w), lambda b, d, j: (b, blk(d, j), 0)),
                  pl.BlockSpec((1, SCAN_BLK, w), lambda b, d, j: (b, blk(d, j), 1)),
                  pl.BlockSpec((1, w, SCAN_BLK), lambda b, d, j: (b, 0, blk(d, j))),
                  pl.BlockSpec((1, 1, SCAN_BLK, 128), lambda b, d, j: (b, d, blk(d, j), 0)),
                  pl.BlockSpec((1, 1, 8, SCAN_BLK), lambda b, d, j: (b, d, 0, blk(d, j))),
                  pl.BlockSpec((1, SCAN_BLK, SCAN_BLK), lambda b, d, j: (d, 0, 0)),
                  pl.BlockSpec((1, SCAN_BLK, SCAN_BLK), lambda b, d, j: (d, 0, 0))],
        out_specs=pl.BlockSpec((1, 1, SCAN_BLK, w), lambda b, d, j: (d, b, blk(d, j), 0)),
        out_shape=jax.ShapeDtypeStruct((2, bsz, t, w), F32),
        scratch_shapes=[pltpu.VMEM((ML_HEADS, 128, ML_DK), F32),
                        pltpu.VMEM((ML_HEADS, 8, 128), F32)],
        compiler_params=_cparams(("parallel", "parallel", "arbitrary")),
        name="mlstm_scan",
    )(qk, qk, vt, gcol, grow, mask, mask_t)


def _out_kernel(hc_ref, hl_ref, mod_ref, at_ref, hf_ref, hb_ref, hgate_ref, gn_ref, e_ref, mf_ref, mb_ref, mgate_ref,
                wa_ref, wr_ref, wm_ref, o_ref, *, nct, skip):
    mod = mod_ref[0, 0]
    o = hf_ref[0, 0] + hb_ref[0, 0]
    ss = _dot((o * o).astype(BF16), e_ref[...])
    gate = hgate_ref[0]
    r = o * lax.rsqrt(ss + EPS) * gn_ref[...] * (gate * _sigmoid(gate))
    m = _sigmoid(mgate_ref[0]) * (mf_ref[0, 0] + mb_ref[0, 0])
    y = _dot_tn(at_ref[0], wa_ref[...]) + _dot(r.astype(BF16), wr_ref[...]) + _dot(m.astype(BF16), wm_ref[...])
    h = jnp.where(pl.program_id(1) + skip < nct, hc_ref[0], hl_ref[0])
    o_ref[0] = h + mod[2:3] * y


def _out_call(h_ctx, h_lat, lat_off, t, modsel, at, hg_o, hgp, gn, e, ml_o, mvo, wa, wr, wm, nct, skip):
    bsz, _, d = h_ctx.shape
    nt = t // TM - skip
    full = lambda a: pl.BlockSpec(a.shape, lambda bi, i: (0,) * a.ndim)
    hg_col = (OFF_MQK - OFF_HG) // HG_W - 1
    ctx_spec, lat_spec = _stream_spec(nct, lat_off, skip)
    return pl.pallas_call(
        functools.partial(_out_kernel, nct=nct, skip=skip),
        grid=(bsz, nt),
        in_specs=[ctx_spec(h_ctx), lat_spec(h_lat),
                  pl.BlockSpec((1, 1, N_MOD, d), lambda bi, i: (bi, (i + skip >= nct).astype(jnp.int32), 0, 0)),
                  pl.BlockSpec((1, ATTN_W, TM), lambda bi, i: (bi, 0, i + skip)),
                  pl.BlockSpec((1, 1, TM, HG_W), lambda bi, i: (0, bi, i + skip, 0)),
                  pl.BlockSpec((1, 1, TM, HG_W), lambda bi, i: (1, bi, i + skip, 0)),
                  pl.BlockSpec((1, TM, HG_W), lambda bi, i: (bi, i + skip, hg_col)),
                  full(gn), full(e),
                  pl.BlockSpec((1, 1, TM, ML_W), lambda bi, i: (0, bi, i + skip, 0)),
                  pl.BlockSpec((1, 1, TM, ML_W), lambda bi, i: (1, bi, i + skip, 0)),
                  pl.BlockSpec((1, TM, ML_W), lambda bi, i: (bi, i + skip, 1)),
                  full(wa), full(wr), full(wm)],
        out_specs=pl.BlockSpec((1, TM, d), lambda bi, i: (bi, i, 0)),
        out_shape=jax.ShapeDtypeStruct((bsz, nt * TM, d), F32),
        compiler_params=_cparams(("parallel", "parallel")),
        name="readout_out_proj",
    )(h_ctx, h_lat, modsel, at, hg_o, hg_o, hgp, gn, e, ml_o, ml_o, mvo, wa, wr, wm)


def _ffn_kernel(h_ref, mod_ref, g_ref, w1_ref, w2_ref, o_ref):
    x = h_ref[0]
    mod = mod_ref[0, 0]
    ms = jnp.mean(x * x, axis=-1, keepdims=True)
    y = x * lax.rsqrt(ms + EPS) * g_ref[...]
    u = (y * (1.0 + mod[4:5]) + mod[3:4]).astype(BF16)
    a = jnp.maximum(_dot(u, w1_ref[...]), 0.0)
    a = (a * a).astype(BF16)
    o_ref[0] = x + mod[5:6] * _dot(a, w2_ref[...])


def _ffn_call(h, modsel, g, w1, w2, nct):
    bsz, t, d = h.shape
    nt = t // TM
    full = lambda a: pl.BlockSpec(a.shape, lambda bi, i: (0,) * a.ndim, pipeline_mode=pl.Buffered(1))
    return pl.pallas_call(
        _ffn_kernel,
        grid=(bsz, nt),
        in_specs=[pl.BlockSpec((1, TM, d), lambda bi, i: (bi, i, 0)),
                  pl.BlockSpec((1, 1, N_MOD, d), lambda bi, i: (bi, (i >= nct).astype(jnp.int32), 0, 0)),
                  pl.BlockSpec(g.shape, lambda bi, i: (0, 0)),
                  full(w1), full(w2)],
        out_specs=pl.BlockSpec((1, TM, d), lambda bi, i: (bi, i, 0)),
        out_shape=jax.ShapeDtypeStruct((bsz, t, d), F32),
        compiler_params=_cparams(("parallel", "parallel")),
        name="ffn",
    )(h, modsel, g, w1, w2)


def _rope_tables(n_ctx, n_lat):
    inv_freq = ROPE_THETA ** (-np.arange(0, ROPE_AXIS_DIM, 2, dtype=np.float32) / ROPE_AXIS_DIM)
    inv_freq = jnp.asarray(inv_freq, F32)
    tok = jnp.arange(n_lat)
    ang_row = (tok // GRID_W).astype(F32)[:, None] * inv_freq[None, :]
    ang_col = (tok % GRID_W).astype(F32)[:, None] * inv_freq[None, :]
    ang = jnp.concatenate([ang_row, ang_row, ang_col, ang_col], axis=1)
    ang = jnp.concatenate([jnp.zeros((n_ctx, HEAD_DIM), F32), ang], axis=0)
    sign = np.where((np.arange(HEAD_DIM) % 32) < 16, -1.0, 1.0).astype(np.float32)
    cos = jnp.cos(ang)
    sin = jnp.sin(ang) * jnp.asarray(sign)[None, :]
    return jnp.concatenate([cos, cos], axis=1), jnp.concatenate([sin, sin], axis=1)


def _block_ones(width, head, scale, dtype):
    idx = np.arange(width) // head
    return jnp.asarray((idx[:, None] == idx[None, :]).astype(np.float32) * scale, dtype)


def _scan_masks(n):
    tril = np.tril(np.ones((n, n), np.float32))
    return np.stack([tril, tril.T])


def kernel(x, c, ctx, c_ctx, w_mod, b_mod, norm_mix, norm_ffn, w_in, b_in, q_norm, k_norm,
           hg_lb, hg_norm, ml_conv, w_out, w_ff1, w_ff2):
    bsz, n_lat, d = x.shape
    n_ctx = ctx.shape[1]
    depth = w_mod.shape[0]
    t = n_ctx + n_lat
    nct = n_ctx // TM
    assert n_ctx % SCAN_BLK == 0 and n_lat % SCAN_BLK == 0 and n_lat % GRID_W == 0 and bsz + 1 <= 8

    cvec = jnp.concatenate([c, c_ctx[None, :], jnp.zeros((8 - bsz - 1, d), F32)], axis=0)
    mods = _mod_call(cvec, w_mod, b_mod).reshape(depth, 8, N_MOD, d)
    modsel = jnp.stack([jnp.broadcast_to(mods[:, bsz][:, None], (depth, bsz, N_MOD, d)), mods[:, :bsz]], axis=2)

    cos, sin = _rope_tables(n_ctx, n_lat)
    e_attn = _block_ones(ATTN_W, HEAD_DIM, 1.0 / HEAD_DIM, BF16)
    e_hg_mean = _block_ones(HG_W, HG_DK, 1.0 / HG_DK, BF16)
    e_hg = _block_ones(HG_W, HG_DK, 1.0, BF16)
    bd_hg = _block_ones(HG_W, HG_DK, 1.0, F32)
    scan_mask = _scan_masks(SCAN_BLK)
    mask_ml = jnp.asarray(scan_mask, BF16)
    mask_ml_t = jnp.asarray(np.transpose(scan_mask, (0, 2, 1)).copy(), BF16)

    lbs = jnp.cumsum(jax.nn.softmax(hg_lb.astype(F32), axis=0), axis=0)
    lbs = (lbs - lbs[:1]).reshape(depth, 2, 1, HG_W)

    w_in_p = jnp.pad(w_in, ((0, 0), (0, 0), (0, N_IN_PAD - N_IN))).astype(BF16)
    b_in_p = jnp.pad(b_in, ((0, 0), (0, N_IN_PAD - N_IN))).reshape(depth, 1, N_IN_PAD)
    w_out_b = w_out.astype(BF16)
    w1_b = w_ff1.astype(BF16)
    w2_b = w_ff2.astype(BF16)
    gq = jnp.tile(q_norm, (1, ATTN_HEADS)).reshape(depth, 1, ATTN_W)
    gk = jnp.tile(k_norm, (1, ATTN_KV_HEADS)).reshape(depth, 1, KV_W)
    gn = jnp.tile(hg_norm, (1, HG_HEADS)).reshape(depth, 1, HG_W)

    h_ctx, h_lat, lat_off = ctx, x, 0
    for l in range(depth):
        last = l == depth - 1
        attn_consts, unscale, use_fixed_shift = _attn_constants(q_norm[l], k_norm[l])
        qa, ka, vt, hgp, mqk, mvo, mvt, gcol, grow = _in_call(
            h_ctx, h_lat, lat_off, t, modsel[l], norm_mix[l].reshape(1, d), w_in_p[l], b_in_p[l], cos, sin,
            gq[l], gk[l], e_attn, attn_consts, nct)

        at = _attention(qa, ka, vt, unscale, use_fixed_shift, n_ctx)

        hg_o = _hgrn_call(hgp, lbs[l], e_hg, bd_hg, nct)

        qk = _conv_call(mqk, ml_conv[l], n_ctx)
        ml_o = _mlstm_call(qk, mvt, gcol, grow, mask_ml, mask_ml_t, nct)

        skip = nct if last else 0
        h1 = _out_call(h_ctx, h_lat, lat_off, t, modsel[l], at, hg_o, hgp, gn[l], e_hg_mean, ml_o, mvo,
                       w_out_b[l, :ATTN_W], w_out_b[l, ATTN_W:ATTN_W + HG_W], w_out_b[l, ATTN_W + HG_W:], nct, skip)
        h = _ffn_call(h1, modsel[l], norm_ffn[l].reshape(1, d), w1_b[l], w2_b[l], 0 if last else nct)
        h_ctx, h_lat, lat_off = h, h, nct
    return h
```

```python
import functools

import jax
import jax.numpy as jnp
import numpy as np
from jax import lax
from jax.experimental import pallas as pl
from jax.experimental.pallas import tpu as pltpu

F32 = jnp.float32
BF16 = jnp.bfloat16

GRID_W = 64
HEAD_DIM = 64
ATTN_HEADS = 8
ATTN_KV_HEADS = 2
ATTN_REP = ATTN_HEADS // ATTN_KV_HEADS
ROPE_THETA = 10000.0
ROPE_AXIS_DIM = HEAD_DIM // 2
HG_HEADS = 4
HG_DK = 64
ML_HEADS = 4
ML_DK = 64
N_MOD = 6
EPS = 1e-6
ATTN_W = ATTN_HEADS * HEAD_DIM
KV_W = ATTN_KV_HEADS * HEAD_DIM
HG_W = HG_HEADS * HG_DK
ML_W = ML_HEADS * ML_DK

OFF_AQ, OFF_AK, OFF_AV = 0, 512, 640
OFF_HG = 768
OFF_MQK = 2048
OFF_MVO = 2560
OFF_MG = 3072
N_IN = 3088
N_IN_PAD = 3200

TM = 256
TQ = 256
TK = 256
ATTN_UNROLL = 16
SCAN_BLK = 256
CONV_BLKS = (1280, 1024, 768, 512, 256)
HG_CHUNK = 16
V_ROWS = 80
FP8 = jnp.float8_e4m3fn
QK_ROWS = 256
SHIFT_ROWS = QK_ROWS - 3 * HEAD_DIM
FP8_TARGET = 128.0
MAX_FIXED_SHIFT = 32.0
SHIFT_FRACTION = 0.5
NEG_BIG = -1e30
Q_SCALE = float(np.log2(np.e)) * HEAD_DIM ** -0.5
VMEM_LIMIT = 56 * 1024 * 1024


def _cparams(sem):
    return pltpu.CompilerParams(dimension_semantics=sem, vmem_limit_bytes=VMEM_LIMIT)


def _split3(x):
    hi = x.astype(BF16)
    r1 = x - hi.astype(F32)
    mid = r1.astype(BF16)
    lo = (r1 - mid.astype(F32)).astype(BF16)
    return hi, mid, lo


def _dot(a, b):
    return jnp.dot(a, b, preferred_element_type=F32)


def _dot_nt(a, b):
    return lax.dot_general(a, b, (((1,), (1,)), ((), ())), preferred_element_type=F32)


def _dot_tn(a, b):
    return lax.dot_general(a, b, (((0,), (0,)), ((), ())), preferred_element_type=F32)


def _mask_dot(mask_bf16, x):
    hi, mid, lo = _split3(x)
    return _dot(mask_bf16, hi) + _dot(mask_bf16, mid) + _dot(mask_bf16, lo)


def _dot_mask(x, mask_bf16):
    hi, mid, lo = _split3(x)
    return _dot(hi, mask_bf16) + _dot(mid, mask_bf16) + _dot(lo, mask_bf16)


def _sigmoid(x):
    return 1.0 / (1.0 + jnp.exp(-x))


def _log_sigmoid(x):
    return jnp.minimum(x, 0.0) - jnp.log(1.0 + jnp.exp(-jnp.abs(x)))


def _mod_kernel(c_ref, w_ref, b_ref, o_ref):
    cv = c_ref[...]
    a = cv * _sigmoid(cv)
    o_ref[0] = jnp.dot(a, w_ref[0], preferred_element_type=F32, precision=lax.Precision.HIGHEST) + b_ref[0]


def _mod_call(cvec, w_mod, b_mod):
    depth, d, nm = w_mod.shape
    tn = 1024
    return pl.pallas_call(
        _mod_kernel,
        grid=(depth, nm // tn),
        in_specs=[pl.BlockSpec((8, d), lambda l, j: (0, 0)),
                  pl.BlockSpec((1, d, tn), lambda l, j: (l, 0, j)),
                  pl.BlockSpec((1, 1, tn), lambda l, j: (l, 0, j))],
        out_specs=pl.BlockSpec((1, 8, tn), lambda l, j: (l, 0, j)),
        out_shape=jax.ShapeDtypeStruct((depth, 8, nm), F32),
        compiler_params=_cparams(("arbitrary", "arbitrary")),
        name="mod_vectors",
    )(cvec, w_mod, b_mod.reshape(depth, 1, nm))


def _head_rms(t, gain, e):
    ss = _dot((t * t).astype(BF16), e)
    return t * lax.rsqrt(ss + EPS) * gain


def _rope(t, cos, sin):
    w = t.shape[1]
    reps = w // 128
    cosw = jnp.concatenate([cos] * reps, axis=1) if reps > 1 else cos
    sinw = jnp.concatenate([sin] * reps, axis=1) if reps > 1 else sin
    lane = lax.broadcasted_iota(jnp.int32, t.shape, 1)
    first_half = (lane % 32) < 16
    partner = jnp.where(first_half, pltpu.roll(t, w - 16, 1), pltpu.roll(t, 16, 1))
    return t * cosw + partner * sinw


def _fp8_operand(d, first_half, fill, rem_first):
    rem = d - d.astype(FP8).astype(F32)
    mixed = jnp.where(first_half, rem, d) if rem_first else jnp.where(first_half, d, rem)
    filled = jnp.where(first_half, d, fill)
    return jnp.concatenate([filled.astype(FP8), mixed.astype(FP8)], axis=1)


def _stream_spec(nct, lat_off, skip=0):
    d_last = lambda a: a.shape[-1]
    ctx_spec = lambda a: pl.BlockSpec((1, TM, d_last(a)), lambda bi, i: (bi, jnp.minimum(i + skip, nct - 1), 0))
    lat_spec = lambda a: pl.BlockSpec((1, TM, d_last(a)), lambda bi, i: (bi, jnp.maximum(i + skip - nct, 0) + lat_off, 0))
    return ctx_spec, lat_spec


def _in_kernel(hc_ref, hl_ref, mod_ref, g_ref, w_ref, b_ref, cos_ref, sin_ref, gq_ref, gk_ref, e_ref, ac_ref,
               qa_ref, ka_ref, vt_ref, hg_ref, mqk_ref, mvo_ref, mvt_ref, gcol_ref, grow_ref, *, nct):
    x = jnp.where(pl.program_id(1) < nct, hc_ref[0], hl_ref[0])
    ms = jnp.mean(x * x, axis=-1, keepdims=True)
    y = x * lax.rsqrt(ms + EPS) * g_ref[...]
    mod = mod_ref[0, 0]
    u = y * (1.0 + mod[1:2]) + mod[0:1]
    p = _dot(u.astype(BF16), w_ref[...]) + b_ref[...]
    cos = cos_ref[...]
    sin = sin_ref[...]
    e = e_ref[...]
    q = _rope(_head_rms(p[:, OFF_AQ:OFF_AQ + ATTN_W], gq_ref[...], e), cos, sin)
    k = _rope(_head_rms(p[:, OFF_AK:OFF_AK + KV_W], gk_ref[...], e[:KV_W, :KV_W]), cos, sin)

    consts = ac_ref[...]
    first_half = lax.broadcasted_iota(jnp.int32, (x.shape[0], 128), 1) < HEAD_DIM
    qs = q * consts[:, 0:1]
    neg_r = -consts[:, 2:3]
    for g in range(ATTN_W // 128):
        grp = qs[:, g * 128:(g + 1) * 128]
        swp = pltpu.roll(grp, HEAD_DIM, 1)
        qa_ref[0, 2 * g] = _fp8_operand(jnp.where(first_half, grp, swp), first_half, neg_r, False)
        qa_ref[0, 2 * g + 1] = _fp8_operand(jnp.where(first_half, swp, grp), first_half, neg_r, False)
    ks = k * consts[:, 1:2]
    ksw = pltpu.roll(ks, HEAD_DIM, 1)
    ka_ref[0, 0] = _fp8_operand(jnp.where(first_half, ks, ksw), first_half, 1.0, True)
    ka_ref[0, 1] = _fp8_operand(jnp.where(first_half, ksw, ks), first_half, 1.0, True)
    v_t = p[:, OFF_AV:OFF_AV + KV_W].T.astype(BF16)
    for hh in range(ATTN_KV_HEADS):
        vt_ref[0, hh, :HEAD_DIM, :] = v_t[hh * HEAD_DIM:(hh + 1) * HEAD_DIM]
        pad_row = lax.broadcasted_iota(jnp.int32, (V_ROWS - HEAD_DIM, x.shape[0]), 0)
        vt_ref[0, hh, HEAD_DIM:, :] = jnp.where(pad_row == 0, 1.0, 0.0).astype(BF16)
    hg_ref[0] = p[:, OFF_HG:OFF_MQK]
    mqk_ref[0] = p[:, OFF_MQK:OFF_MVO]
    mvo_ref[0] = p[:, OFF_MVO:OFF_MG]
    mvt_ref[0] = p[:, OFF_MVO:OFF_MVO + ML_W].T.astype(BF16)
    gates = p[:, OFF_MG:OFF_MG + 128]
    lane = lax.broadcasted_iota(jnp.int32, gates.shape, 1)
    by_heads = pltpu.roll(gates, 128 - ML_HEADS, 1)
    by_2heads = pltpu.roll(gates, 128 - 2 * ML_HEADS, 1)
    for direction, (igate, fgate) in enumerate(((gates, by_heads), (by_heads, by_2heads))):
        g_dir = jnp.where(lane < ML_HEADS, igate, jnp.where(lane < 2 * ML_HEADS, fgate, 0.0))
        gcol_ref[0, direction] = g_dir
        grow_ref[0, direction] = g_dir.T[:2 * ML_HEADS]


def _in_call(h_ctx, h_lat, lat_off, t, modsel, g, w, b, cos, sin, gq, gk, e, attn_consts, nct):
    bsz, _, d = h_ctx.shape
    nt = t // TM
    row = lambda width: pl.BlockSpec((1, TM, width), lambda bi, i: (bi, i, 0))
    full = lambda a: pl.BlockSpec(a.shape, lambda bi, i: (0,) * a.ndim)
    out_widths = (OFF_MQK - OFF_HG, 2 * ML_W, 2 * ML_W)
    ctx_spec, lat_spec = _stream_spec(nct, lat_off)
    return pl.pallas_call(
        functools.partial(_in_kernel, nct=nct),
        grid=(bsz, nt),
        in_specs=[ctx_spec(h_ctx), lat_spec(h_lat),
                  pl.BlockSpec((1, 1, N_MOD, d), lambda bi, i: (bi, (i >= nct).astype(jnp.int32), 0, 0)),
                  full(g), full(w), full(b),
                  pl.BlockSpec((TM, 128), lambda bi, i: (i, 0)),
                  pl.BlockSpec((TM, 128), lambda bi, i: (i, 0)),
                  full(gq), full(gk), full(e), full(attn_consts)],
        out_specs=[pl.BlockSpec((1, ATTN_HEADS, TM, QK_ROWS), lambda bi, i: (bi, 0, i, 0)),
                   pl.BlockSpec((1, ATTN_KV_HEADS, TM, QK_ROWS), lambda bi, i: (bi, 0, i, 0)),
                   pl.BlockSpec((1, ATTN_KV_HEADS, V_ROWS, TM), lambda bi, i: (bi, 0, 0, i))]
        + [row(wd) for wd in out_widths]
        + [pl.BlockSpec((1, ML_W, TM), lambda bi, i: (bi, 0, i)),
           pl.BlockSpec((1, 2, TM, 128), lambda bi, i: (bi, 0, i, 0)),
           pl.BlockSpec((1, 2, 2 * ML_HEADS, TM), lambda bi, i: (bi, 0, 0, i))],
        out_shape=[jax.ShapeDtypeStruct((bsz, ATTN_HEADS, t, QK_ROWS), FP8),
                   jax.ShapeDtypeStruct((bsz, ATTN_KV_HEADS, t, QK_ROWS), FP8),
                   jax.ShapeDtypeStruct((bsz, ATTN_KV_HEADS, V_ROWS, t), BF16)]
        + [jax.ShapeDtypeStruct((bsz, t, wd), F32) for wd in out_widths]
        + [jax.ShapeDtypeStruct((bsz, ML_W, t), BF16),
           jax.ShapeDtypeStruct((bsz, 2, t, 128), F32),
           jax.ShapeDtypeStruct((bsz, 2, 2 * ML_HEADS, t), F32)],
        compiler_params=_cparams(("parallel", "parallel")),
        name="norm_mod_in_proj",
    )(h_ctx, h_lat, modsel, g, w, b, cos, sin, gq, gk, e, attn_consts)


def _attn_kernel(qa_ref, ka_ref, vt_ref, unscale_ref, o_ref, acc_ref, s_ref, *maybe_m_ref,
                 online, n_ctx_q, n_ctx_k, n_k):
    qi = pl.program_id(1)
    unscale = unscale_ref[...]
    acc_ref[...] = jnp.zeros(acc_ref.shape, F32)
    if online:
        m_ref, = maybe_m_ref
        m_ref[...] = jnp.full(m_ref.shape, NEG_BIG, F32)
    n_blocks = jnp.where(qi < n_ctx_q, n_ctx_k, n_k)

    def key_tile(kb, g):
        return ka_ref[0, g, pl.ds(pl.multiple_of(kb * TK, TK), TK), :]

    def produce(kt, slot, hh):
        s = _dot_nt(kt, qa_ref[0, hh]) * unscale
        s_ref[slot, hh] = s if online else jnp.exp2(s.astype(BF16))

    def step(kb, cur, nxt):
        for g in range(ATTN_KV_HEADS):
            vt = vt_ref[0, g, :, pl.ds(pl.multiple_of(kb * TK, TK), TK)]
            kt = key_tile(jnp.minimum(kb + 1, n_blocks - 1), g)
            for hh in range(g * ATTN_REP, (g + 1) * ATTN_REP):
                produce(kt, nxt, hh)
                if online:
                    s = s_ref[cur, hh]
                    m_old = m_ref[hh]
                    m_new = jnp.maximum(m_old, jnp.max(s, axis=0, keepdims=True))
                    p = jnp.exp2(s - m_new).astype(BF16)
                    acc_ref[hh] = jnp.exp2(m_old - m_new) * acc_ref[hh] + _dot(vt, p)
                    m_ref[hh] = m_new
                else:
                    acc_ref[hh] += _dot(vt, s_ref[cur, hh])

    for g in range(ATTN_KV_HEADS):
        kt0 = key_tile(0, g)
        for hh in range(g * ATTN_REP, (g + 1) * ATTN_REP):
            produce(kt0, 0, hh)

    def group(i, carry):
        for u in range(ATTN_UNROLL):
            step(ATTN_UNROLL * i + u, u % 2, (u + 1) % 2)
        return carry

    def pair(i, carry):
        step(2 * i, 0, 1)
        step(2 * i + 1, 1, 0)
        return carry

    n_groups = n_blocks // ATTN_UNROLL
    lax.fori_loop(0, n_groups, group, 0)
    lax.fori_loop(n_groups * (ATTN_UNROLL // 2), n_blocks // 2, pair, 0)

    @pl.when(n_blocks % 2 == 1)
    def _():
        step(n_blocks - 1, 0, 1)

    for hh in range(ATTN_HEADS):
        acc = acc_ref[hh]
        o = acc[:HEAD_DIM] / acc[HEAD_DIM:HEAD_DIM + 1]
        o_ref[0, hh * HEAD_DIM:(hh + 1) * HEAD_DIM, :] = o.astype(o_ref.dtype)


def _attn_call(qa, ka, vt, unscale, n_ctx, online):
    bsz, _, t, _ = qa.shape
    kern = functools.partial(_attn_kernel, online=online, n_ctx_q=n_ctx // TQ, n_ctx_k=n_ctx // TK, n_k=t // TK)
    scratch = [pltpu.VMEM((ATTN_HEADS, V_ROWS, TQ), F32),
               pltpu.VMEM((2, ATTN_HEADS, TK, TQ), F32 if online else BF16)]
    if online:
        scratch.append(pltpu.VMEM((ATTN_HEADS, 1, TQ), F32))
    return pl.pallas_call(
        kern,
        grid=(bsz, t // TQ),
        in_specs=[pl.BlockSpec((1, ATTN_HEADS, TQ, QK_ROWS), lambda b, i: (b, 0, i, 0)),
                  pl.BlockSpec((1, ATTN_KV_HEADS, t, QK_ROWS), lambda b, i: (b, 0, 0, 0)),
                  pl.BlockSpec((1, ATTN_KV_HEADS, V_ROWS, t), lambda b, i: (b, 0, 0, 0)),
                  pl.BlockSpec((1, 1), lambda b, i: (0, 0))],
        out_specs=pl.BlockSpec((1, ATTN_W, TQ), lambda b, i: (b, 0, i)),
        out_shape=jax.ShapeDtypeStruct((bsz, ATTN_W, t), BF16),
        scratch_shapes=scratch,
        compiler_params=_cparams(("parallel", "arbitrary")),
        name="attention_online" if online else "attention",
    )(qa, ka, vt, unscale)


def _attn_constants(q_gain, k_gain):
    qmax = jnp.maximum(jnp.sqrt(float(HEAD_DIM)) * Q_SCALE * jnp.max(jnp.abs(q_gain)), 1e-6)
    kmax = jnp.maximum(jnp.sqrt(float(HEAD_DIM)) * jnp.max(jnp.abs(k_gain)), 1e-6)
    shift = qmax * kmax
    q_scale = jnp.exp2(jnp.floor(jnp.log2(FP8_TARGET / qmax)))
    k_scale = jnp.exp2(jnp.floor(jnp.log2(FP8_TARGET / kmax)))
    r = (jnp.minimum(shift, MAX_FIXED_SHIFT) * q_scale * k_scale * (SHIFT_FRACTION / SHIFT_ROWS)).astype(FP8).astype(F32)
    in_proj_consts = jnp.pad(jnp.stack([Q_SCALE * q_scale, k_scale, r]), (0, 125)).reshape(1, 128)
    unscale = (1.0 / (q_scale * k_scale)).reshape(1, 1)
    return in_proj_consts, unscale, shift <= MAX_FIXED_SHIFT


def _attention(qa, ka, vt, unscale, use_fixed_shift, n_ctx):
    return lax.cond(use_fixed_shift,
                    lambda: _attn_call(qa, ka, vt, unscale, n_ctx, False),
                    lambda: _attn_call(qa, ka, vt, unscale, n_ctx, True))


def _conv_kernel(x_ref, prev_ref, next_ref, w_ref, o_ref, *, n_ctx, t):
    x = x_ref[0]
    rows = x.shape[0]
    ridx = lax.broadcasted_iota(jnp.int32, x.shape, 0)
    tok = ridx + pl.program_id(1) * rows
    xm1 = jnp.where(ridx == 0, prev_ref[0, 7:8, :], pltpu.roll(x, 1, 0))
    xp1 = jnp.where(ridx == rows - 1, next_ref[0, 0:1, :], pltpu.roll(x, rows - 1, 0))
    xm1 = jnp.where(jnp.logical_or(tok == 0, tok == n_ctx), 0.0, xm1)
    xp1 = jnp.where(jnp.logical_or(tok == n_ctx - 1, tok == t - 1), 0.0, xp1)
    w = w_ref[...]
    y = xm1 * w[0:1] + x * w[1:2] + xp1 * w[2:3]
    y = y * _sigmoid(y)
    lane = lax.broadcasted_iota(jnp.int32, x.shape, 1)
    y = jnp.where(lane >= ML_W, y * (ML_DK ** -0.5), y)
    o_ref[0] = y.astype(o_ref.dtype)


def _conv_call(mqk, w, n_ctx):
    bsz, t, width = mqk.shape
    blk = next(b for b in CONV_BLKS if t % b == 0)
    nb = t // blk
    per8 = blk // 8
    last8 = t // 8 - 1
    kern = functools.partial(_conv_kernel, n_ctx=n_ctx, t=t)
    return pl.pallas_call(
        kern,
        grid=(bsz, nb),
        in_specs=[pl.BlockSpec((1, blk, width), lambda b, i: (b, i, 0)),
                  pl.BlockSpec((1, 8, width), lambda b, i: (b, jnp.maximum(i * per8 - 1, 0), 0)),
                  pl.BlockSpec((1, 8, width), lambda b, i: (b, jnp.minimum((i + 1) * per8, last8), 0)),
                  pl.BlockSpec(w.shape, lambda b, i: (0, 0))],
        out_specs=pl.BlockSpec((1, blk, width), lambda b, i: (b, i, 0)),
        out_shape=jax.ShapeDtypeStruct((bsz, t, width), BF16),
        compiler_params=_cparams(("parallel", "parallel")),
        name="mlstm_qk_conv",
    )(mqk, mqk, mqk, w)


def _scan_block(d, j, nct, nb):
    bwd = jnp.where(j < nct, nct - 1 - j, nb - 1 - (j - nct))
    return jnp.where(d == 0, j, bwd)


def _hgrn_kernel(q0_ref, q1_ref, v0_ref, v1_ref, z0_ref, z1_ref, lb_ref, e_ref, bd_ref, o_ref,
                 st_ref, qd_ref, kd_ref, od_ref, dt_ref, u_ref):
    d = pl.program_id(1)
    j = pl.program_id(2)

    @pl.when(j == 0)
    def _():
        st_ref[...] = jnp.zeros(st_ref.shape, F32)

    lb = lb_ref[0]
    e = e_ref[...]
    bd = bd_ref[...]
    c = HG_CHUNK
    nch = SCAN_BLK // c

    def rows(i):
        return pl.ds(jnp.where(d == 0, i, c - 1 - i), nch, stride=c)

    def load2(r0, r1, sl):
        return jnp.concatenate([r0[0, sl, :], r1[0, sl, :]], axis=1)

    def load_halves(ref, sl):
        return jnp.concatenate([ref[0, sl, :], ref[1, sl, :]], axis=1)

    def store_halves(ref, sl, x):
        ref[0, sl, :] = x[:, :128]
        ref[1, sl, :] = x[:, 128:]

    qs, ks, vs, bs = [], [], [], []
    b = None
    for i in range(c):
        f = lb + (1.0 - lb) * _sigmoid(load2(z0_ref, z1_ref, rows(i)))
        lf = jnp.log2(f)
        b = lf if b is None else b + lf
        qs.append(load2(q0_ref, q1_ref, rows(i)))
        vs.append(load2(v0_ref, v1_ref, rows(i)))
        ks.append(1.0 - f)
        bs.append(b)
    b_tot = bs[-1]
    dt_ref[...] = jnp.exp2(b_tot)
    for i in range(c):
        store_halves(qd_ref, rows(i), qs[i] * jnp.exp2(bs[i]))
        store_halves(kd_ref, rows(i), ks[i] * jnp.exp2(b_tot - bs[i]))
    for cc in range(nch):
        sl = pl.ds(cc * c, c)
        vc = load2(v0_ref, v1_ref, sl).astype(BF16)
        kc = load_halves(kd_ref, sl).astype(BF16)
        u_ref[cc] = _dot_tn(vc, kc) * bd
    for i in range(c):
        ps = [qs[i] * ks[jj] * jnp.exp2(bs[i] - bs[jj]) for jj in range(i)] + [qs[i] * ks[i]]
        a = _dot(jnp.concatenate(ps, axis=0).astype(BF16), e)
        o = a[0:nch] * vs[0]
        for jj in range(1, i + 1):
            o = o + a[jj * nch:(jj + 1) * nch] * vs[jj]
        store_halves(od_ref, rows(i), o)

    for n in range(nch):
        cc = jnp.where(d == 0, n, nch - 1 - n)
        sl = pl.ds(pl.multiple_of(cc * c, c), c)
        st = st_ref[...]
        o_inter = _dot_nt(load_halves(qd_ref, sl).astype(BF16), st.astype(BF16))
        o_ref[0, 0, sl, :] = load_halves(od_ref, sl) + o_inter
        st_ref[...] = st * dt_ref[pl.ds(cc, 1), :] + u_ref[cc]


def _hgrn_call(hgp, lbs, e, bd, nct):
    bsz, t, _ = hgp.shape
    nb = t // SCAN_BLK
    w = HG_W
    blk = lambda d, j: _scan_block(d, j, nct, nb)
    half = lambda col: pl.BlockSpec((1, SCAN_BLK, 128), lambda b, d, j: (b, blk(d, j), col(d)))
    return pl.pallas_call(
        _hgrn_kernel,
        grid=(bsz, 2, nb),
        in_specs=[half(lambda d: 0), half(lambda d: 1),
                  half(lambda d: 2), half(lambda d: 3),
                  half(lambda d: 4 + 2 * d), half(lambda d: 5 + 2 * d),
                  pl.BlockSpec((1, 1, w), lambda b, d, j: (d, 0, 0)),
                  pl.BlockSpec((w, w), lambda b, d, j: (0, 0)),
                  pl.BlockSpec((w, w), lambda b, d, j: (0, 0))],
        out_specs=pl.BlockSpec((1, 1, SCAN_BLK, w), lambda b, d, j: (d, b, blk(d, j), 0)),
        out_shape=jax.ShapeDtypeStruct((2, bsz, t, w), F32),
        scratch_shapes=[pltpu.VMEM((w, w), F32),
                        pltpu.VMEM((2, SCAN_BLK, 128), F32),
                        pltpu.VMEM((2, SCAN_BLK, 128), F32),
                        pltpu.VMEM((2, SCAN_BLK, 128), F32),
                        pltpu.VMEM((SCAN_BLK // HG_CHUNK, w), F32),
                        pltpu.VMEM((SCAN_BLK // HG_CHUNK, w, w), F32)],
        compiler_params=_cparams(("parallel", "parallel", "arbitrary")),
        name="hgrn2_scan",
    )(hgp, hgp, hgp, hgp, hgp, hgp, lbs, e, bd)


def _mlstm_kernel(q_ref, k_ref, vt_ref, g_ref, gt_ref, m_ref, mt_ref, o_ref, c_ref, ms_ref):
    j = pl.program_id(2)

    @pl.when(j == 0)
    def _():
        c_ref[...] = jnp.zeros(c_ref.shape, F32)
        ms_ref[...] = jnp.zeros(ms_ref.shape, F32)

    length = SCAN_BLK
    mask = m_ref[0]
    mask_t = mt_ref[0]
    valid_t = mask_t > 0
    gates = g_ref[0, 0]
    gates_t = gt_ref[0, 0]
    b_col_all = _mask_dot(mask, _log_sigmoid(gates))
    lf_t = _log_sigmoid(gates_t)
    b_row_all = _dot_mask(lf_t, mask_t)
    tot_all = jnp.sum(lf_t, axis=1, keepdims=True)
    row = lax.broadcasted_iota(jnp.int32, (128, length), 0)

    heads = range(ML_HEADS)
    kh = [k_ref[0, :, hh * ML_DK:(hh + 1) * ML_DK] for hh in heads]
    qh = [q_ref[0, :, hh * ML_DK:(hh + 1) * ML_DK] for hh in heads]
    s_raw = [_dot_nt(kh[hh], qh[hh]) for hh in heads]
    inter = [_dot_nt(c_ref[hh].astype(BF16), qh[hh]) for hh in heads]
    for pair in range(ML_HEADS // 2):
        v2t = vt_ref[0, pair * 128:(pair + 1) * 128, :]
        out_t = jnp.zeros((128, length), F32)
        for sub in range(2):
            hh = pair * 2 + sub
            own = (row < 64) if sub == 0 else (row >= 64)
            ones_row = 64 if sub == 0 else 0
            v_aug_t = jnp.where(own, v2t, jnp.where(row == ones_row, 1.0, 0.0).astype(BF16))
            m_prev = ms_ref[hh, 0:1, 0:1]
            cs = gates[:, hh:hh + 1] - b_col_all[:, 4 + hh:5 + hh]
            br = b_row_all[4 + hh:5 + hh, :]
            tot = tot_all[4 + hh:5 + hh, :]
            log_d = jnp.where(valid_t, br + cs, NEG_BIG)
            m_t = jnp.maximum(br + m_prev, jnp.max(log_d, axis=0, keepdims=True))
            s_t = (s_raw[hh] * jnp.exp(log_d - m_t)).astype(BF16)
            r_t = jnp.exp(br + m_prev - m_t) * inter[hh] + _dot(v_aug_t, s_t)
            den = r_t[ones_row:ones_row + 1, :]
            out_t = jnp.where(own, r_t / jnp.maximum(jnp.abs(den), jnp.exp(-m_t)), out_t)
            log_e = tot + cs
            m_new = jnp.maximum(tot + m_prev, jnp.max(log_e, axis=0, keepdims=True))
            ke = (kh[hh].astype(F32) * jnp.exp(log_e - m_new)).astype(BF16)
            c_ref[hh] = jnp.exp(tot + m_prev - m_new) * c_ref[hh] + _dot(v_aug_t, ke)
            ms_ref[hh] = jnp.broadcast_to(m_new, ms_ref.shape[1:])
        o_ref[0, 0, :, pair * 128:(pair + 1) * 128] = out_t.T


def _mlstm_call(qk, vt, gcol, grow, mask, mask_t, nct):
    bsz, t, _ = qk.shape
    nb = t // SCAN_BLK
    w = ML_W
    blk = lambda d, j: _scan_block(d, j, nct, nb)
    return pl.pallas_call(
        _mlstm_kernel,
        grid=(bsz, 2, nb),
        in_specs=[pl.BlockSpec((1, SCAN_BLK, w), lambda b, d, j: (b, blk(d, j), 0)),
                  pl.BlockSpec((1, SCAN_BLK, w), lambda b, d, j: (b, blk(d, j), 1)),
                  pl.BlockSpec((1, w, SCAN_BLK), lambda b, d, j: (b, 0, blk(d, j))),
                  pl.BlockSpec((1, 1, SCAN_BLK, 128), lambda b, d, j: (b, d, blk(d, j), 0)),
                  pl.BlockSpec((1, 1, 8, SCAN_BLK), lambda b, d, j: (b, d, 0, blk(d, j))),
                  pl.BlockSpec((1, SCAN_BLK, SCAN_BLK), lambda b, d, j: (d, 0, 0)),
                  pl.BlockSpec((1, SCAN_BLK, SCAN_BLK), lambda b, d, j: (d, 0, 0))],
        out_specs=pl.BlockSpec((1, 1, SCAN_BLK, w), lambda b, d, j: (d, b, blk(d, j), 0)),
        out_shape=jax.ShapeDtypeStruct((2, bsz, t, w), F32),
        scratch_shapes=[pltpu.VMEM((ML_HEADS, 128, ML_DK), F32),
                        pltpu.VMEM((ML_HEADS, 8, 128), F32)],
        compiler_params=_cparams(("parallel", "parallel", "arbitrary")),
        name="mlstm_scan",
    )(qk, qk, vt, gcol, grow, mask, mask_t)


def _out_kernel(hc_ref, hl_ref, mod_ref, at_ref, hf_ref, hb_ref, hgate_ref, gn_ref, e_ref, mf_ref, mb_ref, mgate_ref,
                wa_ref, wr_ref, wm_ref, o_ref, *, nct, skip):
    mod = mod_ref[0, 0]
    o = hf_ref[0, 0] + hb_ref[0, 0]
    ss = _dot((o * o).astype(BF16), e_ref[...])
    gate = hgate_ref[0]
    r = o * lax.rsqrt(ss + EPS) * gn_ref[...] * (gate * _sigmoid(gate))
    m = _sigmoid(mgate_ref[0]) * (mf_ref[0, 0] + mb_ref[0, 0])
    y = _dot_tn(at_ref[0], wa_ref[...]) + _dot(r.astype(BF16), wr_ref[...]) + _dot(m.astype(BF16), wm_ref[...])
    h = jnp.where(pl.program_id(1) + skip < nct, hc_ref[0], hl_ref[0])
    o_ref[0] = h + mod[2:3] * y


def _out_call(h_ctx, h_lat, lat_off, t, modsel, at, hg_o, hgp, gn, e, ml_o, mvo, wa, wr, wm, nct, skip):
    bsz, _, d = h_ctx.shape
    nt = t // TM - skip
    full = lambda a: pl.BlockSpec(a.shape, lambda bi, i: (0,) * a.ndim)
    hg_col = (OFF_MQK - OFF_HG) // HG_W - 1
    ctx_spec, lat_spec = _stream_spec(nct, lat_off, skip)
    return pl.pallas_call(
        functools.partial(_out_kernel, nct=nct, skip=skip),
        grid=(bsz, nt),
        in_specs=[ctx_spec(h_ctx), lat_spec(h_lat),
                  pl.BlockSpec((1, 1, N_MOD, d), lambda bi, i: (bi, (i + skip >= nct).astype(jnp.int32), 0, 0)),
                  pl.BlockSpec((1, ATTN_W, TM), lambda bi, i: (bi, 0, i + skip)),
                  pl.BlockSpec((1, 1, TM, HG_W), lambda bi, i: (0, bi, i + skip, 0)),
                  pl.BlockSpec((1, 1, TM, HG_W), lambda bi, i: (1, bi, i + skip, 0)),
                  pl.BlockSpec((1, TM, HG_W), lambda bi, i: (bi, i + skip, hg_col)),
                  full(gn), full(e),
                  pl.BlockSpec((1, 1, TM, ML_W), lambda bi, i: (0, bi, i + skip, 0)),
                  pl.BlockSpec((1, 1, TM, ML_W), lambda bi, i: (1, bi, i + skip, 0)),
                  pl.BlockSpec((1, TM, ML_W), lambda bi, i: (bi, i + skip, 1)),
                  full(wa), full(wr), full(wm)],
        out_specs=pl.BlockSpec((1, TM, d), lambda bi, i: (bi, i, 0)),
        out_shape=jax.ShapeDtypeStruct((bsz, nt * TM, d), F32),
        compiler_params=_cparams(("parallel", "parallel")),
        name="readout_out_proj",
    )(h_ctx, h_lat, modsel, at, hg_o, hg_o, hgp, gn, e, ml_o, ml_o, mvo, wa, wr, wm)


def _ffn_kernel(h_ref, mod_ref, g_ref, w1_ref, w2_ref, o_ref):
    x = h_ref[0]
    mod = mod_ref[0, 0]
    ms = jnp.mean(x * x, axis=-1, keepdims=True)
    y = x * lax.rsqrt(ms + EPS) * g_ref[...]
    u = (y * (1.0 + mod[4:5]) + mod[3:4]).astype(BF16)
    a = jnp.maximum(_dot(u, w1_ref[...]), 0.0)
    a = (a * a).astype(BF16)
    o_ref[0] = x + mod[5:6] * _dot(a, w2_ref[...])


def _ffn_call(h, modsel, g, w1, w2, nct):
    bsz, t, d = h.shape
    nt = t // TM
    full = lambda a: pl.BlockSpec(a.shape, lambda bi, i: (0,) * a.ndim, pipeline_mode=pl.Buffered(1))
    return pl.pallas_call(
        _ffn_kernel,
        grid=(bsz, nt),
        in_specs=[pl.BlockSpec((1, TM, d), lambda bi, i: (bi, i, 0)),
                  pl.BlockSpec((1, 1, N_MOD, d), lambda bi, i: (bi, (i >= nct).astype(jnp.int32), 0, 0)),
                  pl.BlockSpec(g.shape, lambda bi, i: (0, 0)),
                  full(w1), full(w2)],
        out_specs=pl.BlockSpec((1, TM, d), lambda bi, i: (bi, i, 0)),
        out_shape=jax.ShapeDtypeStruct((bsz, t, d), F32),
        compiler_params=_cparams(("parallel", "parallel")),
        name="ffn",
    )(h, modsel, g, w1, w2)


def _rope_tables(n_ctx, n_lat):
    inv_freq = ROPE_THETA ** (-np.arange(0, ROPE_AXIS_DIM, 2, dtype=np.float32) / ROPE_AXIS_DIM)
    inv_freq = jnp.asarray(inv_freq, F32)
    n_rows = n_lat // GRID_W
    ang_row = jnp.arange(n_rows, dtype=F32)[:, None] * inv_freq[None, :]
    ang_col = jnp.arange(GRID_W, dtype=F32)[:, None] * inv_freq[None, :]
    sign = np.where((np.arange(HEAD_DIM) % 32) < 16, -1.0, 1.0).astype(np.float32)

    def table(fn, ctx_value, signs):
        per_row = jnp.broadcast_to(fn(ang_row)[:, None, :], (n_rows, GRID_W, inv_freq.shape[0])).reshape(n_lat, -1)
        per_col = jnp.broadcast_to(fn(ang_col)[None, :, :], (n_rows, GRID_W, inv_freq.shape[0])).reshape(n_lat, -1)
        lat = jnp.concatenate([per_row, per_row, per_col, per_col], axis=1) * signs
        full = jnp.concatenate([jnp.full((n_ctx, HEAD_DIM), ctx_value, F32), lat], axis=0)
        return jnp.concatenate([full, full], axis=1)

    return table(jnp.cos, 1.0, 1.0), table(jnp.sin, 0.0, jnp.asarray(sign)[None, :])


def _block_ones(width, head, scale, dtype):
    idx = np.arange(width) // head
    return jnp.asarray((idx[:, None] == idx[None, :]).astype(np.float32) * scale, dtype)


def _scan_masks(n):
    tril = np.tril(np.ones((n, n), np.float32))
    return np.stack([tril, tril.T])


def kernel(x, c, ctx, c_ctx, w_mod, b_mod, norm_mix, norm_ffn, w_in, b_in, q_norm, k_norm,
           hg_lb, hg_norm, ml_conv, w_out, w_ff1, w_ff2):
    bsz, n_lat, d = x.shape
    n_ctx = ctx.shape[1]
    depth = w_mod.shape[0]
    t = n_ctx + n_lat
    nct = n_ctx // TM
    assert n_ctx % SCAN_BLK == 0 and n_lat % SCAN_BLK == 0 and n_lat % GRID_W == 0 and bsz + 1 <= 8

    cvec = jnp.concatenate([c, c_ctx[None, :], jnp.zeros((8 - bsz - 1, d), F32)], axis=0)
    mods = _mod_call(cvec, w_mod, b_mod).reshape(depth, 8, N_MOD, d)
    modsel = jnp.stack([jnp.broadcast_to(mods[:, bsz][:, None], (depth, bsz, N_MOD, d)), mods[:, :bsz]], axis=2)

    cos, sin = _rope_tables(n_ctx, n_lat)
    e_attn = _block_ones(ATTN_W, HEAD_DIM, 1.0 / HEAD_DIM, BF16)
    e_hg_mean = _block_ones(HG_W, HG_DK, 1.0 / HG_DK, BF16)
    e_hg = _block_ones(HG_W, HG_DK, 1.0, BF16)
    bd_hg = _block_ones(HG_W, HG_DK, 1.0, F32)
    scan_mask = _scan_masks(SCAN_BLK)
    mask_ml = jnp.asarray(scan_mask, BF16)
    mask_ml_t = jnp.asarray(np.transpose(scan_mask, (0, 2, 1)).copy(), BF16)

    lbs = jnp.cumsum(jax.nn.softmax(hg_lb.astype(F32), axis=0), axis=0)
    lbs = (lbs - lbs[:1]).reshape(depth, 2, 1, HG_W)

    w_in_p = jnp.pad(w_in, ((0, 0), (0, 0), (0, N_IN_PAD - N_IN))).astype(BF16)
    b_in_p = jnp.pad(b_in, ((0, 0), (0, N_IN_PAD - N_IN))).reshape(depth, 1, N_IN_PAD)
    w_out_b = w_out.astype(BF16)
    w1_b = w_ff1.astype(BF16)
    w2_b = w_ff2.astype(BF16)
    gq = jnp.tile(q_norm, (1, ATTN_HEADS)).reshape(depth, 1, ATTN_W)
    gk = jnp.tile(k_norm, (1, ATTN_KV_HEADS)).reshape(depth, 1, KV_W)
    gn = jnp.tile(hg_norm, (1, HG_HEADS)).reshape(depth, 1, HG_W)

    h_ctx, h_lat, lat_off = ctx, x, 0
    for l in range(depth):
        last = l == depth - 1
        attn_consts, unscale, use_fixed_shift = _attn_constants(q_norm[l], k_norm[l])
        qa, ka, vt, hgp, mqk, mvo, mvt, gcol, grow = _in_call(
            h_ctx, h_lat, lat_off, t, modsel[l], norm_mix[l].reshape(1, d), w_in_p[l], b_in_p[l], cos, sin,
            gq[l], gk[l], e_attn, attn_consts, nct)

        at = _attention(qa, ka, vt, unscale, use_fixed_shift, n_ctx)

        hg_o = _hgrn_call(hgp, lbs[l], e_hg, bd_hg, nct)

        qk = _conv_call(mqk, ml_conv[l], n_ctx)
        ml_o = _mlstm_call(qk, mvt, gcol, grow, mask_ml, mask_ml_t, nct)

        skip = nct if last else 0
        h1 = _out_call(h_ctx, h_lat, lat_off, t, modsel[l], at, hg_o, hgp, gn[l], e_hg_mean, ml_o, mvo,
                       w_out_b[l, :ATTN_W], w_out_b[l, ATTN_W:ATTN_W + HG_W], w_out_b[l, ATTN_W + HG_W:], nct, skip)
        h = _ffn_call(h1, modsel[l], norm_ffn[l].reshape(1, d), w1_b[l], w2_b[l], 0 if last else nct)
        h_ctx, h_lat, lat_off = h, h, nct
    return h
```

```python
import functools

import jax
import jax.numpy as jnp
import numpy as np
from jax import lax
from jax.experimental import pallas as pl
from jax.experimental.pallas import tpu as pltpu

F32 = jnp.float32
BF16 = jnp.bfloat16
LANES = 128
SUBLANES = 8

GRID_W = 64
HEAD_DIM = 64
ATTN_HEADS = 8
ATTN_KV_HEADS = 2
ATTN_REP = ATTN_HEADS // ATTN_KV_HEADS
ROPE_THETA = 10000.0
ROPE_AXIS_DIM = HEAD_DIM // 2
HG_HEADS = 4
HG_DK = 64
ML_HEADS = 4
ML_DK = 64
N_MOD = 6
EPS = 1e-6
ATTN_W = ATTN_HEADS * HEAD_DIM
KV_W = ATTN_KV_HEADS * HEAD_DIM
HG_W = HG_HEADS * HG_DK
ML_W = ML_HEADS * ML_DK

OFF_AQ, OFF_AK, OFF_AV = 0, 512, 640
OFF_HG = 768
OFF_MQK = 2048
OFF_MVO = 2560
OFF_MG = 3072
N_IN = 3088
N_IN_PAD = 3200

TM = 256
TQ = 256
TK = 256
ATTN_UNROLL = 32
SCAN_BLK = 256
CONV_BLKS = (1280, 1024, 768, 512, 256)
HG_CHUNK = 16
V_ROWS = 80
FP8 = jnp.float8_e4m3fn
QK_ROWS = 256
SHIFT_ROWS = QK_ROWS - 3 * HEAD_DIM
FP8_TARGET = 128.0
MAX_FIXED_SHIFT = 32.0
SHIFT_FRACTION = 0.5
NEG_BIG = -1e30
Q_SCALE = float(np.log2(np.e)) * HEAD_DIM ** -0.5
VMEM_LIMIT = 56 * 1024 * 1024


def _cparams(sem):
    return pltpu.CompilerParams(dimension_semantics=sem, vmem_limit_bytes=VMEM_LIMIT)


def _split3(x):
    hi = x.astype(BF16)
    r1 = x - hi.astype(F32)
    mid = r1.astype(BF16)
    lo = (r1 - mid.astype(F32)).astype(BF16)
    return hi, mid, lo


def _dot(a, b):
    return jnp.dot(a, b, preferred_element_type=F32)


def _dot_nt(a, b):
    return lax.dot_general(a, b, (((1,), (1,)), ((), ())), preferred_element_type=F32)


def _dot_tn(a, b):
    return lax.dot_general(a, b, (((0,), (0,)), ((), ())), preferred_element_type=F32)


def _mask_dot(mask_bf16, x):
    hi, mid, lo = _split3(x)
    return _dot(mask_bf16, hi) + _dot(mask_bf16, mid) + _dot(mask_bf16, lo)


def _dot_mask(x, mask_bf16):
    hi, mid, lo = _split3(x)
    return _dot(hi, mask_bf16) + _dot(mid, mask_bf16) + _dot(lo, mask_bf16)


def _sigmoid(x):
    return 1.0 / (1.0 + jnp.exp(-x))


def _log_sigmoid(x):
    return jnp.minimum(x, 0.0) - jnp.log(1.0 + jnp.exp(-jnp.abs(x)))


def _mod_kernel(c_ref, w_ref, b_ref, o_ref):
    cv = c_ref[...]
    a = cv * _sigmoid(cv)
    o_ref[0] = jnp.dot(a, w_ref[0], preferred_element_type=F32, precision=lax.Precision.HIGHEST) + b_ref[0]


def _mod_call(cvec, w_mod, b_mod):
    depth, d, nm = w_mod.shape
    tn = 1024
    return pl.pallas_call(
        _mod_kernel,
        grid=(depth, nm // tn),
        in_specs=[pl.BlockSpec((8, d), lambda l, j: (0, 0)),
                  pl.BlockSpec((1, d, tn), lambda l, j: (l, 0, j)),
                  pl.BlockSpec((1, 1, tn), lambda l, j: (l, 0, j))],
        out_specs=pl.BlockSpec((1, 8, tn), lambda l, j: (l, 0, j)),
        out_shape=jax.ShapeDtypeStruct((depth, 8, nm), F32),
        compiler_params=_cparams(("arbitrary", "arbitrary")),
        name="mod_vectors",
    )(cvec, w_mod, b_mod.reshape(depth, 1, nm))


def _head_rms(t, gain, e):
    ss = _dot((t * t).astype(BF16), e)
    return t * lax.rsqrt(ss + EPS) * gain


def _rope(t, cos, sin):
    w = t.shape[1]
    reps = w // LANES
    cosw = jnp.concatenate([cos] * reps, axis=1) if reps > 1 else cos
    sinw = jnp.concatenate([sin] * reps, axis=1) if reps > 1 else sin
    lane = lax.broadcasted_iota(jnp.int32, t.shape, 1)
    first_half = (lane % 32) < 16
    partner = jnp.where(first_half, pltpu.roll(t, w - 16, 1), pltpu.roll(t, 16, 1))
    return t * cosw + partner * sinw


def _fp8_operand(d, first_half, fill, rem_first):
    rem = d - d.astype(FP8).astype(F32)
    mixed = jnp.where(first_half, rem, d) if rem_first else jnp.where(first_half, d, rem)
    filled = jnp.where(first_half, d, fill)
    return jnp.concatenate([filled.astype(FP8), mixed.astype(FP8)], axis=1)


def _stream_spec(nct, lat_off, skip=0):
    d_last = lambda a: a.shape[-1]
    ctx_spec = lambda a: pl.BlockSpec((1, TM, d_last(a)), lambda bi, i: (bi, jnp.minimum(i + skip, nct - 1), 0))
    lat_spec = lambda a: pl.BlockSpec((1, TM, d_last(a)), lambda bi, i: (bi, jnp.maximum(i + skip - nct, 0) + lat_off, 0))
    return ctx_spec, lat_spec


def _in_kernel(hc_ref, hl_ref, mod_ref, g_ref, w_ref, b_ref, cos_ref, sin_ref, gq_ref, gk_ref, e_ref, ac_ref,
               qa_ref, ka_ref, vt_ref, hg_ref, mqk_ref, mvo_ref, mvt_ref, gcol_ref, grow_ref, *, nct):
    x = jnp.where(pl.program_id(1) < nct, hc_ref[0], hl_ref[0])
    ms = jnp.mean(x * x, axis=-1, keepdims=True)
    y = x * lax.rsqrt(ms + EPS) * g_ref[...]
    mod = mod_ref[0, 0]
    u = y * (1.0 + mod[1:2]) + mod[0:1]
    p = _dot(u.astype(BF16), w_ref[...]) + b_ref[...]
    cos = cos_ref[...]
    sin = sin_ref[...]
    e = e_ref[...]
    q = _rope(_head_rms(p[:, OFF_AQ:OFF_AQ + ATTN_W], gq_ref[...], e), cos, sin)
    k = _rope(_head_rms(p[:, OFF_AK:OFF_AK + KV_W], gk_ref[...], e[:KV_W, :KV_W]), cos, sin)

    consts = ac_ref[...]
    first_half = lax.broadcasted_iota(jnp.int32, (x.shape[0], LANES), 1) < HEAD_DIM
    qs = q * consts[:, 0:1]
    neg_r = -consts[:, 2:3]
    for g in range(ATTN_W // LANES):
        grp = qs[:, g * LANES:(g + 1) * LANES]
        swp = pltpu.roll(grp, HEAD_DIM, 1)
        qa_ref[0, 2 * g] = _fp8_operand(jnp.where(first_half, grp, swp), first_half, neg_r, False)
        qa_ref[0, 2 * g + 1] = _fp8_operand(jnp.where(first_half, swp, grp), first_half, neg_r, False)
    ks = k * consts[:, 1:2]
    ksw = pltpu.roll(ks, HEAD_DIM, 1)
    ka_ref[0, 0] = _fp8_operand(jnp.where(first_half, ks, ksw), first_half, 1.0, True)
    ka_ref[0, 1] = _fp8_operand(jnp.where(first_half, ksw, ks), first_half, 1.0, True)
    v_t = p[:, OFF_AV:OFF_AV + KV_W].T.astype(BF16)
    for hh in range(ATTN_KV_HEADS):
        vt_ref[0, hh, :HEAD_DIM, :] = v_t[hh * HEAD_DIM:(hh + 1) * HEAD_DIM]
        pad_row = lax.broadcasted_iota(jnp.int32, (V_ROWS - HEAD_DIM, x.shape[0]), 0)
        vt_ref[0, hh, HEAD_DIM:, :] = jnp.where(pad_row == 0, 1.0, 0.0).astype(BF16)
    hg_ref[0] = p[:, OFF_HG:OFF_MQK]
    mqk_ref[0] = p[:, OFF_MQK:OFF_MVO]
    mvo_ref[0] = p[:, OFF_MVO:OFF_MG]
    mvt_ref[0] = p[:, OFF_MVO:OFF_MVO + ML_W].T.astype(BF16)
    gates = p[:, OFF_MG:OFF_MG + LANES]
    lane = lax.broadcasted_iota(jnp.int32, gates.shape, 1)
    by_heads = pltpu.roll(gates, LANES - ML_HEADS, 1)
    by_2heads = pltpu.roll(gates, LANES - 2 * ML_HEADS, 1)
    for direction, (igate, fgate) in enumerate(((gates, by_heads), (by_heads, by_2heads))):
        g_dir = jnp.where(lane < ML_HEADS, igate, jnp.where(lane < 2 * ML_HEADS, fgate, 0.0))
        gcol_ref[0, direction] = g_dir
        grow_ref[0, direction] = g_dir.T[:2 * ML_HEADS]


def _in_call(h_ctx, h_lat, lat_off, t, modsel, g, w, b, cos, sin, gq, gk, e, attn_consts, nct):
    bsz, _, d = h_ctx.shape
    nt = t // TM
    row = lambda width: pl.BlockSpec((1, TM, width), lambda bi, i: (bi, i, 0))
    full = lambda a: pl.BlockSpec(a.shape, lambda bi, i: (0,) * a.ndim)
    out_widths = (OFF_MQK - OFF_HG, 2 * ML_W, 2 * ML_W)
    ctx_spec, lat_spec = _stream_spec(nct, lat_off)
    return pl.pallas_call(
        functools.partial(_in_kernel, nct=nct),
        grid=(bsz, nt),
        in_specs=[ctx_spec(h_ctx), lat_spec(h_lat),
                  pl.BlockSpec((1, 1, N_MOD, d), lambda bi, i: (bi, (i >= nct).astype(jnp.int32), 0, 0)),
                  full(g), full(w), full(b),
                  pl.BlockSpec((TM, LANES), lambda bi, i: (i, 0)),
                  pl.BlockSpec((TM, LANES), lambda bi, i: (i, 0)),
                  full(gq), full(gk), full(e), full(attn_consts)],
        out_specs=[pl.BlockSpec((1, ATTN_HEADS, TM, QK_ROWS), lambda bi, i: (bi, 0, i, 0)),
                   pl.BlockSpec((1, ATTN_KV_HEADS, TM, QK_ROWS), lambda bi, i: (bi, 0, i, 0)),
                   pl.BlockSpec((1, ATTN_KV_HEADS, V_ROWS, TM), lambda bi, i: (bi, 0, 0, i))]
        + [row(wd) for wd in out_widths]
        + [pl.BlockSpec((1, ML_W, TM), lambda bi, i: (bi, 0, i)),
           pl.BlockSpec((1, 2, TM, LANES), lambda bi, i: (bi, 0, i, 0)),
           pl.BlockSpec((1, 2, 2 * ML_HEADS, TM), lambda bi, i: (bi, 0, 0, i))],
        out_shape=[jax.ShapeDtypeStruct((bsz, ATTN_HEADS, t, QK_ROWS), FP8),
                   jax.ShapeDtypeStruct((bsz, ATTN_KV_HEADS, t, QK_ROWS), FP8),
                   jax.ShapeDtypeStruct((bsz, ATTN_KV_HEADS, V_ROWS, t), BF16)]
        + [jax.ShapeDtypeStruct((bsz, t, wd), F32) for wd in out_widths]
        + [jax.ShapeDtypeStruct((bsz, ML_W, t), BF16),
           jax.ShapeDtypeStruct((bsz, 2, t, LANES), F32),
           jax.ShapeDtypeStruct((bsz, 2, 2 * ML_HEADS, t), F32)],
        compiler_params=_cparams(("parallel", "parallel")),
        name="norm_mod_in_proj",
    )(h_ctx, h_lat, modsel, g, w, b, cos, sin, gq, gk, e, attn_consts)


def _attn_kernel(qa_ref, ka_ref, vt_ref, unscale_ref, o_ref, acc_ref, s_ref, *maybe_m_ref,
                 online, n_ctx_q, n_ctx_k, n_k):
    qi = pl.program_id(2)
    unscale = unscale_ref[...]
    acc_ref[...] = jnp.zeros(acc_ref.shape, F32)
    if online:
        m_ref, = maybe_m_ref
        m_ref[...] = jnp.full(m_ref.shape, NEG_BIG, F32)
    n_blocks = jnp.where(qi < n_ctx_q, n_ctx_k, n_k)

    def key_tile(kb):
        return ka_ref[0, 0, pl.ds(pl.multiple_of(kb * TK, TK), TK), :]

    def produce(kt, slot, hh):
        s = _dot_nt(kt, qa_ref[0, hh]) * unscale
        s_ref[slot, hh] = s if online else jnp.exp2(s.astype(BF16))

    def step(kb, cur, nxt):
        vt = vt_ref[0, 0, :, pl.ds(pl.multiple_of(kb * TK, TK), TK)]
        kt = key_tile(jnp.minimum(kb + 1, n_blocks - 1))
        for hh in range(ATTN_REP):
            produce(kt, nxt, hh)
            if online:
                s = s_ref[cur, hh]
                m_old = m_ref[hh]
                m_new = jnp.maximum(m_old, jnp.max(s, axis=0, keepdims=True))
                p = jnp.exp2(s - m_new).astype(BF16)
                acc_ref[hh] = jnp.exp2(m_old - m_new) * acc_ref[hh] + _dot(vt, p)
                m_ref[hh] = m_new
            else:
                acc_ref[hh] += _dot(vt, s_ref[cur, hh])

    kt0 = key_tile(0)
    for hh in range(ATTN_REP):
        produce(kt0, 0, hh)

    def group(i, carry):
        for u in range(ATTN_UNROLL):
            step(ATTN_UNROLL * i + u, u % 2, (u + 1) % 2)
        return carry

    def pair(i, carry):
        step(2 * i, 0, 1)
        step(2 * i + 1, 1, 0)
        return carry

    n_groups = n_blocks // ATTN_UNROLL
    lax.fori_loop(0, n_groups, group, 0)
    lax.fori_loop(n_groups * (ATTN_UNROLL // 2), n_blocks // 2, pair, 0)

    @pl.when(n_blocks % 2 == 1)
    def _():
        step(n_blocks - 1, 0, 1)

    for hh in range(ATTN_REP):
        acc = acc_ref[hh]
        o = acc[:HEAD_DIM] / acc[HEAD_DIM:HEAD_DIM + 1]
        o_ref[0, hh * HEAD_DIM:(hh + 1) * HEAD_DIM, :] = o.astype(o_ref.dtype)


def _attn_call(qa, ka, vt, unscale, n_ctx, online):
    bsz, _, t, _ = qa.shape
    kern = functools.partial(_attn_kernel, online=online, n_ctx_q=n_ctx // TQ, n_ctx_k=n_ctx // TK, n_k=t // TK)
    gw = ATTN_REP * HEAD_DIM
    scratch = [pltpu.VMEM((ATTN_REP, V_ROWS, TQ), F32),
               pltpu.VMEM((2, ATTN_REP, TK, TQ), F32 if online else BF16)]
    if online:
        scratch.append(pltpu.VMEM((ATTN_REP, 1, TQ), F32))
    return pl.pallas_call(
        kern,
        grid=(bsz, ATTN_KV_HEADS, t // TQ),
        in_specs=[pl.BlockSpec((1, ATTN_REP, TQ, QK_ROWS), lambda b, g, i: (b, g, i, 0)),
                  pl.BlockSpec((1, 1, t, QK_ROWS), lambda b, g, i: (b, g, 0, 0)),
                  pl.BlockSpec((1, 1, V_ROWS, t), lambda b, g, i: (b, g, 0, 0)),
                  pl.BlockSpec((1, 1), lambda b, g, i: (0, 0))],
        out_specs=pl.BlockSpec((1, gw, TQ), lambda b, g, i: (b, g, i)),
        out_shape=jax.ShapeDtypeStruct((bsz, ATTN_W, t), BF16),
        scratch_shapes=scratch,
        compiler_params=_cparams(("parallel", "parallel", "arbitrary")),
        name="attention_online" if online else "attention",
    )(qa, ka, vt, unscale)


def _attn_constants(q_gain, k_gain):
    qmax = jnp.maximum(jnp.sqrt(float(HEAD_DIM)) * Q_SCALE * jnp.max(jnp.abs(q_gain)), 1e-6)
    kmax = jnp.maximum(jnp.sqrt(float(HEAD_DIM)) * jnp.max(jnp.abs(k_gain)), 1e-6)
    shift = qmax * kmax
    q_scale = jnp.exp2(jnp.floor(jnp.log2(FP8_TARGET / qmax)))
    k_scale = jnp.exp2(jnp.floor(jnp.log2(FP8_TARGET / kmax)))
    r = (jnp.minimum(shift, MAX_FIXED_SHIFT) * q_scale * k_scale * (SHIFT_FRACTION / SHIFT_ROWS)).astype(FP8).astype(F32)
    in_proj_consts = jnp.stack([Q_SCALE * q_scale, k_scale, r])
    in_proj_consts = jnp.pad(in_proj_consts, (0, LANES - in_proj_consts.shape[0])).reshape(1, LANES)
    unscale = (1.0 / (q_scale * k_scale)).reshape(1, 1)
    return in_proj_consts, unscale, shift <= MAX_FIXED_SHIFT


def _attention(qa, ka, vt, unscale, use_fixed_shift, n_ctx):
    return lax.cond(use_fixed_shift,
                    lambda: _attn_call(qa, ka, vt, unscale, n_ctx, False),
                    lambda: _attn_call(qa, ka, vt, unscale, n_ctx, True))


def _conv_kernel(x_ref, prev_ref, next_ref, w_ref, o_ref, *, n_ctx, t):
    x = x_ref[0]
    rows = x.shape[0]
    ridx = lax.broadcasted_iota(jnp.int32, x.shape, 0)
    tok = ridx + pl.program_id(1) * rows
    xm1 = jnp.where(ridx == 0, prev_ref[0, 7:8, :], pltpu.roll(x, 1, 0))
    xp1 = jnp.where(ridx == rows - 1, next_ref[0, 0:1, :], pltpu.roll(x, rows - 1, 0))
    xm1 = jnp.where(jnp.logical_or(tok == 0, tok == n_ctx), 0.0, xm1)
    xp1 = jnp.where(jnp.logical_or(tok == n_ctx - 1, tok == t - 1), 0.0, xp1)
    w = w_ref[...]
    y = xm1 * w[0:1] + x * w[1:2] + xp1 * w[2:3]
    y = y * _sigmoid(y)
    lane = lax.broadcasted_iota(jnp.int32, x.shape, 1)
    y = jnp.where(lane >= ML_W, y * (ML_DK ** -0.5), y)
    o_ref[0] = y.astype(o_ref.dtype)


def _conv_call(mqk, w, n_ctx):
    bsz, t, width = mqk.shape
    blk = next(b for b in CONV_BLKS if t % b == 0)
    nb = t // blk
    per8 = blk // 8
    last8 = t // 8 - 1
    kern = functools.partial(_conv_kernel, n_ctx=n_ctx, t=t)
    return pl.pallas_call(
        kern,
        grid=(bsz, nb),
        in_specs=[pl.BlockSpec((1, blk, width), lambda b, i: (b, i, 0)),
                  pl.BlockSpec((1, 8, width), lambda b, i: (b, jnp.maximum(i * per8 - 1, 0), 0)),
                  pl.BlockSpec((1, 8, width), lambda b, i: (b, jnp.minimum((i + 1) * per8, last8), 0)),
                  pl.BlockSpec(w.shape, lambda b, i: (0, 0))],
        out_specs=pl.BlockSpec((1, blk, width), lambda b, i: (b, i, 0)),
        out_shape=jax.ShapeDtypeStruct((bsz, t, width), BF16),
        compiler_params=_cparams(("parallel", "parallel")),
        name="mlstm_qk_conv",
    )(mqk, mqk, mqk, w)


def _scan_block(d, j, nct, nb):
    bwd = jnp.where(j < nct, nct - 1 - j, nb - 1 - (j - nct))
    return jnp.where(d == 0, j, bwd)


def _hgrn_kernel(q0_ref, q1_ref, v0_ref, v1_ref, z0_ref, z1_ref, lb_ref, e_ref, bd_ref, o_ref,
                 st_ref, qd_ref, kd_ref, od_ref, dt_ref, u_ref):
    d = pl.program_id(1)
    j = pl.program_id(2)

    @pl.when(j == 0)
    def _():
        st_ref[...] = jnp.zeros(st_ref.shape, F32)

    lb = lb_ref[0]
    e = e_ref[...]
    bd = bd_ref[...]
    c = HG_CHUNK
    nch = SCAN_BLK // c

    def rows(i):
        return pl.ds(jnp.where(d == 0, i, c - 1 - i), nch, stride=c)

    def load2(r0, r1, sl):
        return jnp.concatenate([r0[0, sl, :], r1[0, sl, :]], axis=1)

    def load_halves(ref, sl):
        return jnp.concatenate([ref[0, sl, :], ref[1, sl, :]], axis=1)

    def store_halves(ref, sl, x):
        ref[0, sl, :] = x[:, :LANES]
        ref[1, sl, :] = x[:, LANES:]

    qs, ks, vs, bs = [], [], [], []
    b = None
    for i in range(c):
        f = lb + (1.0 - lb) * _sigmoid(load2(z0_ref, z1_ref, rows(i)))
        lf = jnp.log2(f)
        b = lf if b is None else b + lf
        qs.append(load2(q0_ref, q1_ref, rows(i)))
        vs.append(load2(v0_ref, v1_ref, rows(i)))
        ks.append(1.0 - f)
        bs.append(b)
    b_tot = bs[-1]
    dt_ref[...] = jnp.exp2(b_tot)
    for i in range(c):
        store_halves(qd_ref, rows(i), qs[i] * jnp.exp2(bs[i]))
        store_halves(kd_ref, rows(i), ks[i] * jnp.exp2(b_tot - bs[i]))
    for cc in range(nch):
        sl = pl.ds(cc * c, c)
        vc = load2(v0_ref, v1_ref, sl).astype(BF16)
        kc = load_halves(kd_ref, sl).astype(BF16)
        u_ref[cc] = _dot_tn(vc, kc) * bd
    for i in range(c):
        ps = [qs[i] * ks[jj] * jnp.exp2(bs[i] - bs[jj]) for jj in range(i)] + [qs[i] * ks[i]]
        a = _dot(jnp.concatenate(ps, axis=0).astype(BF16), e)
        o = a[0:nch] * vs[0]
        for jj in range(1, i + 1):
            o = o + a[jj * nch:(jj + 1) * nch] * vs[jj]
        store_halves(od_ref, rows(i), o)

    for n in range(nch):
        cc = jnp.where(d == 0, n, nch - 1 - n)
        sl = pl.ds(pl.multiple_of(cc * c, c), c)
        st = st_ref[...]
        o_inter = _dot_nt(load_halves(qd_ref, sl).astype(BF16), st.astype(BF16))
        o_ref[0, 0, sl, :] = load_halves(od_ref, sl) + o_inter
        st_ref[...] = st * dt_ref[pl.ds(cc, 1), :] + u_ref[cc]


def _hgrn_call(hgp, lbs, e, bd, nct):
    bsz, t, _ = hgp.shape
    nb = t // SCAN_BLK
    w = HG_W
    blk = lambda d, j: _scan_block(d, j, nct, nb)
    half = lambda col: pl.BlockSpec((1, SCAN_BLK, LANES), lambda b, d, j: (b, blk(d, j), col(d)))
    return pl.pallas_call(
        _hgrn_kernel,
        grid=(bsz, 2, nb),
        in_specs=[half(lambda d: 0), half(lambda d: 1),
                  half(lambda d: 2), half(lambda d: 3),
                  half(lambda d: 4 + 2 * d), half(lambda d: 5 + 2 * d),
                  pl.BlockSpec((1, 1, w), lambda b, d, j: (d, 0, 0)),
                  pl.BlockSpec((w, w), lambda b, d, j: (0, 0)),
                  pl.BlockSpec((w, w), lambda b, d, j: (0, 0))],
        out_specs=pl.BlockSpec((1, 1, SCAN_BLK, w), lambda b, d, j: (d, b, blk(d, j), 0)),
        out_shape=jax.ShapeDtypeStruct((2, bsz, t, w), F32),
        scratch_shapes=[pltpu.VMEM((w, w), F32),
                        pltpu.VMEM((2, SCAN_BLK, LANES), F32),
                        pltpu.VMEM((2, SCAN_BLK, LANES), F32),
                        pltpu.VMEM((2, SCAN_BLK, LANES), F32),
                        pltpu.VMEM((SCAN_BLK // HG_CHUNK, w), F32),
                        pltpu.VMEM((SCAN_BLK // HG_CHUNK, w, w), F32)],
        compiler_params=_cparams(("parallel", "parallel", "arbitrary")),
        name="hgrn2_scan",
    )(hgp, hgp, hgp, hgp, hgp, hgp, lbs, e, bd)


def _mlstm_kernel(q_ref, k_ref, vt_ref, g_ref, gt_ref, m_ref, mt_ref, o_ref, c_ref, ms_ref):
    j = pl.program_id(2)

    @pl.when(j == 0)
    def _():
        c_ref[...] = jnp.zeros(c_ref.shape, F32)
        ms_ref[...] = jnp.zeros(ms_ref.shape, F32)

    length = SCAN_BLK
    mask = m_ref[0]
    mask_t = mt_ref[0]
    valid_t = mask_t > 0
    gates = g_ref[0, 0]
    gates_t = gt_ref[0, 0]
    b_col_all = _mask_dot(mask, _log_sigmoid(gates))
    lf_t = _log_sigmoid(gates_t)
    b_row_all = _dot_mask(lf_t, mask_t)
    tot_all = jnp.sum(lf_t, axis=1, keepdims=True)
    row = lax.broadcasted_iota(jnp.int32, (LANES, length), 0)

    heads = range(ML_HEADS)
    kh = [k_ref[0, :, hh * ML_DK:(hh + 1) * ML_DK] for hh in heads]
    qh = [q_ref[0, :, hh * ML_DK:(hh + 1) * ML_DK] for hh in heads]
    s_raw = [_dot_nt(kh[hh], qh[hh]) for hh in heads]
    inter = [_dot_nt(c_ref[hh].astype(BF16), qh[hh]) for hh in heads]
    for pair in range(ML_HEADS // 2):
        v2t = vt_ref[0, pair * LANES:(pair + 1) * LANES, :]
        out_t = jnp.zeros((LANES, length), F32)
        for sub in range(2):
            hh = pair * 2 + sub
            own = (row < 64) if sub == 0 else (row >= 64)
            ones_row = 64 if sub == 0 else 0
            v_aug_t = jnp.where(own, v2t, jnp.where(row == ones_row, 1.0, 0.0).astype(BF16))
            m_prev = ms_ref[hh, 0:1, 0:1]
            cs = gates[:, hh:hh + 1] - b_col_all[:, 4 + hh:5 + hh]
            br = b_row_all[4 + hh:5 + hh, :]
            tot = tot_all[4 + hh:5 + hh, :]
            log_d = jnp.where(valid_t, br + cs, NEG_BIG)
            m_t = jnp.maximum(br + m_prev, jnp.max(log_d, axis=0, keepdims=True))
            s_t = (s_raw[hh] * jnp.exp(log_d - m_t)).astype(BF16)
            r_t = jnp.exp(br + m_prev - m_t) * inter[hh] + _dot(v_aug_t, s_t)
            den = r_t[ones_row:ones_row + 1, :]
            out_t = jnp.where(own, r_t / jnp.maximum(jnp.abs(den), jnp.exp(-m_t)), out_t)
            log_e = tot + cs
            m_new = jnp.maximum(tot + m_prev, jnp.max(log_e, axis=0, keepdims=True))
            ke = (kh[hh].astype(F32) * jnp.exp(log_e - m_new)).astype(BF16)
            c_ref[hh] = jnp.exp(tot + m_prev - m_new) * c_ref[hh] + _dot(v_aug_t, ke)
            ms_ref[hh] = jnp.broadcast_to(m_new, ms_ref.shape[1:])
        o_ref[0, 0, :, pair * LANES:(pair + 1) * LANES] = out_t.T


def _mlstm_call(qk, vt, gcol, grow, mask, mask_t, nct):
    bsz, t, _ = qk.shape
    nb = t // SCAN_BLK
    w = ML_W
    blk = lambda d, j: _scan_block(d, j, nct, nb)
    return pl.pallas_call(
        _mlstm_kernel,
        grid=(bsz, 2, nb),
        in_specs=[pl.BlockSpec((1, SCAN_BLK, w), lambda b, d, j: (b, blk(d, j), 0)),
                  pl.BlockSpec((1, SCAN_BLK, w), lambda b, d, j: (b, blk(d, j), 1)),
                  pl.BlockSpec((1, w, SCAN_BLK), lambda b, d, j: (b, 0, blk(d, j))),
                  pl.BlockSpec((1, 1, SCAN_BLK, LANES), lambda b, d, j: (b, d, blk(d, j), 0)),
                  pl.BlockSpec((1, 1, 8, SCAN_BLK), lambda b, d, j: (b, d, 0, blk(d, j))),
                  pl.BlockSpec((1, SCAN_BLK, SCAN_BLK), lambda b, d, j: (d, 0, 0)),
                  pl.BlockSpec((1, SCAN_BLK, SCAN_BLK), lambda b, d, j: (d, 0, 0))],
        out_specs=pl.BlockSpec((1, 1, SCAN_BLK, w), lambda b, d, j: (d, b, blk(d, j), 0)),
        out_shape=jax.ShapeDtypeStruct((2, bsz, t, w), F32),
        scratch_shapes=[pltpu.VMEM((ML_HEADS, LANES, ML_DK), F32),
                        pltpu.VMEM((ML_HEADS, SUBLANES, LANES), F32)],
        compiler_params=_cparams(("parallel", "parallel", "arbitrary")),
        name="mlstm_scan",
    )(qk, qk, vt, gcol, grow, mask, mask_t)


def _out_kernel(hc_ref, hl_ref, mod_ref, at_ref, hf_ref, hb_ref, hgate_ref, gn_ref, e_ref, mf_ref, mb_ref, mgate_ref,
                wa_ref, wr_ref, wm_ref, o_ref, *, nct, skip):
    mod = mod_ref[0, 0]
    o = hf_ref[0, 0] + hb_ref[0, 0]
    ss = _dot((o * o).astype(BF16), e_ref[...])
    gate = hgate_ref[0]
    r = o * lax.rsqrt(ss + EPS) * gn_ref[...] * (gate * _sigmoid(gate))
    m = _sigmoid(mgate_ref[0]) * (mf_ref[0, 0] + mb_ref[0, 0])
    y = _dot_tn(at_ref[0], wa_ref[...]) + _dot(r.astype(BF16), wr_ref[...]) + _dot(m.astype(BF16), wm_ref[...])
    h = jnp.where(pl.program_id(1) + skip < nct, hc_ref[0], hl_ref[0])
    o_ref[0] = h + mod[2:3] * y


def _out_call(h_ctx, h_lat, lat_off, t, modsel, at, hg_o, hgp, gn, e, ml_o, mvo, wa, wr, wm, nct, skip):
    bsz, _, d = h_ctx.shape
    nt = t // TM - skip
    full = lambda a: pl.BlockSpec(a.shape, lambda bi, i: (0,) * a.ndim)
    hg_col = (OFF_MQK - OFF_HG) // HG_W - 1
    ctx_spec, lat_spec = _stream_spec(nct, lat_off, skip)
    return pl.pallas_call(
        functools.partial(_out_kernel, nct=nct, skip=skip),
        grid=(bsz, nt),
        in_specs=[ctx_spec(h_ctx), lat_spec(h_lat),
                  pl.BlockSpec((1, 1, N_MOD, d), lambda bi, i: (bi, (i + skip >= nct).astype(jnp.int32), 0, 0)),
                  pl.BlockSpec((1, ATTN_W, TM), lambda bi, i: (bi, 0, i + skip)),
                  pl.BlockSpec((1, 1, TM, HG_W), lambda bi, i: (0, bi, i + skip, 0)),
                  pl.BlockSpec((1, 1, TM, HG_W), lambda bi, i: (1, bi, i + skip, 0)),
                  pl.BlockSpec((1, TM, HG_W), lambda bi, i: (bi, i + skip, hg_col)),
                  full(gn), full(e),
                  pl.BlockSpec((1, 1, TM, ML_W), lambda bi, i: (0, bi, i + skip, 0)),
                  pl.BlockSpec((1, 1, TM, ML_W), lambda bi, i: (1, bi, i + skip, 0)),
                  pl.BlockSpec((1, TM, ML_W), lambda bi, i: (bi, i + skip, 1)),
                  full(wa), full(wr), full(wm)],
        out_specs=pl.BlockSpec((1, TM, d), lambda bi, i: (bi, i, 0)),
        out_shape=jax.ShapeDtypeStruct((bsz, nt * TM, d), F32),
        compiler_params=_cparams(("parallel", "parallel")),
        name="readout_out_proj",
    )(h_ctx, h_lat, modsel, at, hg_o, hg_o, hgp, gn, e, ml_o, ml_o, mvo, wa, wr, wm)


def _ffn_kernel(h_ref, mod_ref, g_ref, w1_ref, w2_ref, o_ref):
    x = h_ref[0]
    mod = mod_ref[0, 0]
    ms = jnp.mean(x * x, axis=-1, keepdims=True)
    y = x * lax.rsqrt(ms + EPS) * g_ref[...]
    u = (y * (1.0 + mod[4:5]) + mod[3:4]).astype(BF16)
    a = jnp.maximum(_dot(u, w1_ref[...]), 0.0)
    a = (a * a).astype(BF16)
    o_ref[0] = x + mod[5:6] * _dot(a, w2_ref[...])


def _ffn_call(h, modsel, g, w1, w2, nct):
    bsz, t, d = h.shape
    nt = t // TM
    full = lambda a: pl.BlockSpec(a.shape, lambda bi, i: (0,) * a.ndim, pipeline_mode=pl.Buffered(1))
    return pl.pallas_call(
        _ffn_kernel,
        grid=(bsz, nt),
        in_specs=[pl.BlockSpec((1, TM, d), lambda bi, i: (bi, i, 0)),
                  pl.BlockSpec((1, 1, N_MOD, d), lambda bi, i: (bi, (i >= nct).astype(jnp.int32), 0, 0)),
                  pl.BlockSpec(g.shape, lambda bi, i: (0, 0)),
                  full(w1), full(w2)],
        out_specs=pl.BlockSpec((1, TM, d), lambda bi, i: (bi, i, 0)),
        out_shape=jax.ShapeDtypeStruct((bsz, t, d), F32),
        compiler_params=_cparams(("parallel", "parallel")),
        name="ffn",
    )(h, modsel, g, w1, w2)


def _rope_tables(n_ctx, n_lat):
    inv_freq = ROPE_THETA ** (-np.arange(0, ROPE_AXIS_DIM, 2, dtype=np.float32) / ROPE_AXIS_DIM)
    inv_freq = jnp.asarray(inv_freq, F32)
    n_rows = n_lat // GRID_W
    ang_row = jnp.arange(n_rows, dtype=F32)[:, None] * inv_freq[None, :]
    ang_col = jnp.arange(GRID_W, dtype=F32)[:, None] * inv_freq[None, :]
    sign = np.where((np.arange(HEAD_DIM) % 32) < 16, -1.0, 1.0).astype(np.float32)

    def table(fn, ctx_value, signs):
        per_row = jnp.broadcast_to(fn(ang_row)[:, None, :], (n_rows, GRID_W, inv_freq.shape[0])).reshape(n_lat, -1)
        per_col = jnp.broadcast_to(fn(ang_col)[None, :, :], (n_rows, GRID_W, inv_freq.shape[0])).reshape(n_lat, -1)
        lat = jnp.concatenate([per_row, per_row, per_col, per_col], axis=1) * signs
        full = jnp.concatenate([jnp.full((n_ctx, HEAD_DIM), ctx_value, F32), lat], axis=0)
        return jnp.concatenate([full, full], axis=1)

    return table(jnp.cos, 1.0, 1.0), table(jnp.sin, 0.0, jnp.asarray(sign)[None, :])


def _block_ones(width, head, scale, dtype):
    idx = np.arange(width) // head
    return jnp.asarray((idx[:, None] == idx[None, :]).astype(np.float32) * scale, dtype)


def _scan_masks(n):
    tril = np.tril(np.ones((n, n), np.float32))
    return np.stack([tril, tril.T])


def kernel(x, c, ctx, c_ctx, w_mod, b_mod, norm_mix, norm_ffn, w_in, b_in, q_norm, k_norm,
           hg_lb, hg_norm, ml_conv, w_out, w_ff1, w_ff2):
    bsz, n_lat, d = x.shape
    n_ctx = ctx.shape[1]
    depth = w_mod.shape[0]
    t = n_ctx + n_lat
    nct = n_ctx // TM
    assert n_ctx % SCAN_BLK == 0 and n_lat % SCAN_BLK == 0 and n_lat % GRID_W == 0 and bsz + 1 <= 8

    cvec = jnp.concatenate([c, c_ctx[None, :], jnp.zeros((8 - bsz - 1, d), F32)], axis=0)
    mods = _mod_call(cvec, w_mod, b_mod).reshape(depth, 8, N_MOD, d)
    modsel = jnp.stack([jnp.broadcast_to(mods[:, bsz][:, None], (depth, bsz, N_MOD, d)), mods[:, :bsz]], axis=2)

    cos, sin = _rope_tables(n_ctx, n_lat)
    e_attn = _block_ones(ATTN_W, HEAD_DIM, 1.0 / HEAD_DIM, BF16)
    e_hg_mean = _block_ones(HG_W, HG_DK, 1.0 / HG_DK, BF16)
    e_hg = _block_ones(HG_W, HG_DK, 1.0, BF16)
    bd_hg = _block_ones(HG_W, HG_DK, 1.0, F32)
    scan_mask = _scan_masks(SCAN_BLK)
    mask_ml = jnp.asarray(scan_mask, BF16)
    mask_ml_t = jnp.asarray(np.transpose(scan_mask, (0, 2, 1)).copy(), BF16)

    lbs = jnp.cumsum(jax.nn.softmax(hg_lb.astype(F32), axis=0), axis=0)
    lbs = (lbs - lbs[:1]).reshape(depth, 2, 1, HG_W)

    w_in_p = jnp.pad(w_in, ((0, 0), (0, 0), (0, N_IN_PAD - N_IN))).astype(BF16)
    b_in_p = jnp.pad(b_in, ((0, 0), (0, N_IN_PAD - N_IN))).reshape(depth, 1, N_IN_PAD)
    w_out_b = w_out.astype(BF16)
    w1_b = w_ff1.astype(BF16)
    w2_b = w_ff2.astype(BF16)
    gq = jnp.tile(q_norm, (1, ATTN_HEADS)).reshape(depth, 1, ATTN_W)
    gk = jnp.tile(k_norm, (1, ATTN_KV_HEADS)).reshape(depth, 1, KV_W)
    gn = jnp.tile(hg_norm, (1, HG_HEADS)).reshape(depth, 1, HG_W)

    h_ctx, h_lat, lat_off = ctx, x, 0
    for l in range(depth):
        last = l == depth - 1
        attn_consts, unscale, use_fixed_shift = _attn_constants(q_norm[l], k_norm[l])
        qa, ka, vt, hgp, mqk, mvo, mvt, gcol, grow = _in_call(
            h_ctx, h_lat, lat_off, t, modsel[l], norm_mix[l].reshape(1, d), w_in_p[l], b_in_p[l], cos, sin,
            gq[l], gk[l], e_attn, attn_consts, nct)

        at = _attention(qa, ka, vt, unscale, use_fixed_shift, n_ctx)

        hg_o = _hgrn_call(hgp, lbs[l], e_hg, bd_hg, nct)

        qk = _conv_call(mqk, ml_conv[l], n_ctx)
        ml_o = _mlstm_call(qk, mvt, gcol, grow, mask_ml, mask_ml_t, nct)

        skip = nct if last else 0
        h1 = _out_call(h_ctx, h_lat, lat_off, t, modsel[l], at, hg_o, hgp, gn[l], e_hg_mean, ml_o, mvo,
                       w_out_b[l, :ATTN_W], w_out_b[l, ATTN_W:ATTN_W + HG_W], w_out_b[l, ATTN_W + HG_W:], nct, skip)
        h = _ffn_call(h1, modsel[l], norm_ffn[l].reshape(1, d), w1_b[l], w2_b[l], 0 if last else nct)
        h_ctx, h_lat, lat_off = h, h, nct
    return h
```

```python
import functools

import jax
import jax.numpy as jnp
import numpy as np
from jax import lax
from jax.experimental import pallas as pl
from jax.experimental.pallas import tpu as pltpu

F32 = jnp.float32
BF16 = jnp.bfloat16
LANES = 128
SUBLANES = 8

GRID_W = 64
HEAD_DIM = 64
ATTN_HEADS = 8
ATTN_KV_HEADS = 2
ATTN_REP = ATTN_HEADS // ATTN_KV_HEADS
ROPE_THETA = 10000.0
ROPE_AXIS_DIM = HEAD_DIM // 2
HG_HEADS = 4
HG_DK = 64
ML_HEADS = 4
ML_DK = 64
N_MOD = 6
EPS = 1e-6
ATTN_W = ATTN_HEADS * HEAD_DIM
KV_W = ATTN_KV_HEADS * HEAD_DIM
HG_W = HG_HEADS * HG_DK
ML_W = ML_HEADS * ML_DK

OFF_AQ, OFF_AK, OFF_AV = 0, 512, 640
OFF_HG = 768
OFF_MQK = 2048
OFF_MVO = 2560
OFF_MG = 3072
N_IN = 3088
N_IN_PAD = 3200

TM = 256
TQ = 256
TK = 256
ATTN_UNROLL = 32
SCAN_BLK = 256
CONV_BLKS = (1280, 1024, 768, 512, 256)
HG_CHUNK = 16
V_ROWS = 80
FP8 = jnp.float8_e4m3fn
QK_ROWS = 256
SHIFT_ROWS = QK_ROWS - 3 * HEAD_DIM
FP8_TARGET = 128.0
MAX_FIXED_SHIFT = 32.0
SHIFT_FRACTION = 0.5
NEG_BIG = -1e30
Q_SCALE = float(np.log2(np.e)) * HEAD_DIM ** -0.5
VMEM_LIMIT = 56 * 1024 * 1024


def _cparams(sem):
    return pltpu.CompilerParams(dimension_semantics=sem, vmem_limit_bytes=VMEM_LIMIT)


def _split3(x):
    hi = x.astype(BF16)
    r1 = x - hi.astype(F32)
    mid = r1.astype(BF16)
    lo = (r1 - mid.astype(F32)).astype(BF16)
    return hi, mid, lo


def _dot(a, b):
    return jnp.dot(a, b, preferred_element_type=F32)


def _dot_nt(a, b):
    return lax.dot_general(a, b, (((1,), (1,)), ((), ())), preferred_element_type=F32)


def _dot_tn(a, b):
    return lax.dot_general(a, b, (((0,), (0,)), ((), ())), preferred_element_type=F32)


def _mask_dot(mask_bf16, x):
    hi, mid, lo = _split3(x)
    return _dot(mask_bf16, hi) + _dot(mask_bf16, mid) + _dot(mask_bf16, lo)


def _dot_mask(x, mask_bf16):
    hi, mid, lo = _split3(x)
    return _dot(hi, mask_bf16) + _dot(mid, mask_bf16) + _dot(lo, mask_bf16)


def _sigmoid(x):
    return 1.0 / (1.0 + jnp.exp(-x))


def _log_sigmoid(x):
    return jnp.minimum(x, 0.0) - jnp.log(1.0 + jnp.exp(-jnp.abs(x)))


def _mod_kernel(c_ref, w_ref, b_ref, o_ref):
    cv = c_ref[...]
    a = cv * _sigmoid(cv)
    o_ref[0] = jnp.dot(a, w_ref[0], preferred_element_type=F32, precision=lax.Precision.HIGHEST) + b_ref[0]


def _mod_call(cvec, w_mod, b_mod):
    depth, d, nm = w_mod.shape
    tn = 1024
    return pl.pallas_call(
        _mod_kernel,
        grid=(depth, nm // tn),
        in_specs=[pl.BlockSpec((8, d), lambda l, j: (0, 0)),
                  pl.BlockSpec((1, d, tn), lambda l, j: (l, 0, j)),
                  pl.BlockSpec((1, 1, tn), lambda l, j: (l, 0, j))],
        out_specs=pl.BlockSpec((1, 8, tn), lambda l, j: (l, 0, j)),
        out_shape=jax.ShapeDtypeStruct((depth, 8, nm), F32),
        compiler_params=_cparams(("arbitrary", "arbitrary")),
        name="mod_vectors",
    )(cvec, w_mod, b_mod.reshape(depth, 1, nm))


def _head_rms(t, gain, e):
    ss = _dot((t * t).astype(BF16), e)
    return t * lax.rsqrt(ss + EPS) * gain


def _rope(t, cos, sin):
    w = t.shape[1]
    reps = w // LANES
    cosw = jnp.concatenate([cos] * reps, axis=1) if reps > 1 else cos
    sinw = jnp.concatenate([sin] * reps, axis=1) if reps > 1 else sin
    lane = lax.broadcasted_iota(jnp.int32, t.shape, 1)
    first_half = (lane % 32) < 16
    partner = jnp.where(first_half, pltpu.roll(t, w - 16, 1), pltpu.roll(t, 16, 1))
    return t * cosw + partner * sinw


def _fp8_operand(d, first_half, fill, rem_first):
    rem = d - d.astype(FP8).astype(F32)
    mixed = jnp.where(first_half, rem, d) if rem_first else jnp.where(first_half, d, rem)
    filled = jnp.where(first_half, d, fill)
    return jnp.concatenate([filled.astype(FP8), mixed.astype(FP8)], axis=1)


def _stream_spec(nct, lat_off, skip=0):
    d_last = lambda a: a.shape[-1]
    ctx_spec = lambda a: pl.BlockSpec((1, TM, d_last(a)), lambda bi, i: (bi, jnp.minimum(i + skip, nct - 1), 0))
    lat_spec = lambda a: pl.BlockSpec((1, TM, d_last(a)), lambda bi, i: (bi, jnp.maximum(i + skip - nct, 0) + lat_off, 0))
    return ctx_spec, lat_spec


def _in_kernel(hc_ref, hl_ref, mod_ref, g_ref, w_ref, b_ref, cos_ref, sin_ref, gq_ref, gk_ref, e_ref, ac_ref,
               qa_ref, ka_ref, vt_ref, hg_ref, mqk_ref, mvo_ref, mvt_ref, gcol_ref, grow_ref, *, nct):
    x = jnp.where(pl.program_id(1) < nct, hc_ref[0], hl_ref[0])
    ms = jnp.mean(x * x, axis=-1, keepdims=True)
    y = x * lax.rsqrt(ms + EPS) * g_ref[...]
    mod = mod_ref[0, 0]
    u = y * (1.0 + mod[1:2]) + mod[0:1]
    p = _dot(u.astype(BF16), w_ref[...]) + b_ref[...]
    cos = cos_ref[...]
    sin = sin_ref[...]
    e = e_ref[...]
    q = _rope(_head_rms(p[:, OFF_AQ:OFF_AQ + ATTN_W], gq_ref[...], e), cos, sin)
    k = _rope(_head_rms(p[:, OFF_AK:OFF_AK + KV_W], gk_ref[...], e[:KV_W, :KV_W]), cos, sin)

    consts = ac_ref[...]
    first_half = lax.broadcasted_iota(jnp.int32, (x.shape[0], LANES), 1) < HEAD_DIM
    qs = q * consts[:, 0:1]
    neg_r = -consts[:, 2:3]
    for g in range(ATTN_W // LANES):
        grp = qs[:, g * LANES:(g + 1) * LANES]
        swp = pltpu.roll(grp, HEAD_DIM, 1)
        qa_ref[0, 2 * g] = _fp8_operand(jnp.where(first_half, grp, swp), first_half, neg_r, False)
        qa_ref[0, 2 * g + 1] = _fp8_operand(jnp.where(first_half, swp, grp), first_half, neg_r, False)
    ks = k * consts[:, 1:2]
    ksw = pltpu.roll(ks, HEAD_DIM, 1)
    ka_ref[0, 0] = _fp8_operand(jnp.where(first_half, ks, ksw), first_half, 1.0, True)
    ka_ref[0, 1] = _fp8_operand(jnp.where(first_half, ksw, ks), first_half, 1.0, True)
    v_t = p[:, OFF_AV:OFF_AV + KV_W].T.astype(BF16)
    for hh in range(ATTN_KV_HEADS):
        vt_ref[0, hh, :HEAD_DIM, :] = v_t[hh * HEAD_DIM:(hh + 1) * HEAD_DIM]
        pad_row = lax.broadcasted_iota(jnp.int32, (V_ROWS - HEAD_DIM, x.shape[0]), 0)
        vt_ref[0, hh, HEAD_DIM:, :] = jnp.where(pad_row == 0, 1.0, 0.0).astype(BF16)
    hg_ref[0] = p[:, OFF_HG:OFF_MQK]
    mqk_ref[0] = p[:, OFF_MQK:OFF_MVO]
    mvo_ref[0] = p[:, OFF_MVO:OFF_MG]
    mvt_ref[0] = p[:, OFF_MVO:OFF_MVO + ML_W].T.astype(BF16)
    gates = p[:, OFF_MG:OFF_MG + LANES]
    lane = lax.broadcasted_iota(jnp.int32, gates.shape, 1)
    by_heads = pltpu.roll(gates, LANES - ML_HEADS, 1)
    by_2heads = pltpu.roll(gates, LANES - 2 * ML_HEADS, 1)
    for direction, (igate, fgate) in enumerate(((gates, by_heads), (by_heads, by_2heads))):
        g_dir = jnp.where(lane < ML_HEADS, igate, jnp.where(lane < 2 * ML_HEADS, fgate, 0.0))
        gcol_ref[0, direction] = g_dir
        grow_ref[0, direction] = g_dir.T[:2 * ML_HEADS]


def _in_call(h_ctx, h_lat, lat_off, t, modsel, g, w, b, cos, sin, gq, gk, e, attn_consts, nct):
    bsz, _, d = h_ctx.shape
    nt = t // TM
    row = lambda width: pl.BlockSpec((1, TM, width), lambda bi, i: (bi, i, 0))
    full = lambda a: pl.BlockSpec(a.shape, lambda bi, i: (0,) * a.ndim)
    out_widths = (OFF_MQK - OFF_HG, 2 * ML_W, 2 * ML_W)
    ctx_spec, lat_spec = _stream_spec(nct, lat_off)
    return pl.pallas_call(
        functools.partial(_in_kernel, nct=nct),
        grid=(bsz, nt),
        in_specs=[ctx_spec(h_ctx), lat_spec(h_lat),
                  pl.BlockSpec((1, 1, N_MOD, d), lambda bi, i: (bi, (i >= nct).astype(jnp.int32), 0, 0)),
                  full(g), full(w), full(b),
                  pl.BlockSpec((TM, LANES), lambda bi, i: (i, 0)),
                  pl.BlockSpec((TM, LANES), lambda bi, i: (i, 0)),
                  full(gq), full(gk), full(e), full(attn_consts)],
        out_specs=[pl.BlockSpec((1, ATTN_HEADS, TM, QK_ROWS), lambda bi, i: (bi, 0, i, 0)),
                   pl.BlockSpec((1, ATTN_KV_HEADS, TM, QK_ROWS), lambda bi, i: (bi, 0, i, 0)),
                   pl.BlockSpec((1, ATTN_KV_HEADS, V_ROWS, TM), lambda bi, i: (bi, 0, 0, i))]
        + [row(wd) for wd in out_widths]
        + [pl.BlockSpec((1, ML_W, TM), lambda bi, i: (bi, 0, i)),
           pl.BlockSpec((1, 2, TM, LANES), lambda bi, i: (bi, 0, i, 0)),
           pl.BlockSpec((1, 2, 2 * ML_HEADS, TM), lambda bi, i: (bi, 0, 0, i))],
        out_shape=[jax.ShapeDtypeStruct((bsz, ATTN_HEADS, t, QK_ROWS), FP8),
                   jax.ShapeDtypeStruct((bsz, ATTN_KV_HEADS, t, QK_ROWS), FP8),
                   jax.ShapeDtypeStruct((bsz, ATTN_KV_HEADS, V_ROWS, t), BF16)]
        + [jax.ShapeDtypeStruct((bsz, t, wd), F32) for wd in out_widths]
        + [jax.ShapeDtypeStruct((bsz, ML_W, t), BF16),
           jax.ShapeDtypeStruct((bsz, 2, t, LANES), F32),
           jax.ShapeDtypeStruct((bsz, 2, 2 * ML_HEADS, t), F32)],
        compiler_params=_cparams(("parallel", "parallel")),
        name="norm_mod_in_proj",
    )(h_ctx, h_lat, modsel, g, w, b, cos, sin, gq, gk, e, attn_consts)


def _attn_kernel(qa_ref, ka_ref, vt_ref, unscale_ref, o_ref, acc_ref, s_ref, *maybe_m_ref,
                 online, n_ctx_q, n_ctx_k, n_k):
    qi = pl.program_id(2)
    unscale = unscale_ref[...]
    acc_ref[...] = jnp.zeros(acc_ref.shape, F32)
    if online:
        m_ref, = maybe_m_ref
        m_ref[...] = jnp.full(m_ref.shape, NEG_BIG, F32)
    n_blocks = jnp.where(qi < n_ctx_q, n_ctx_k, n_k)

    def key_tile(kb):
        return ka_ref[0, 0, pl.ds(pl.multiple_of(kb * TK, TK), TK), :]

    def produce(kt, slot, hh):
        s = _dot_nt(kt, qa_ref[0, hh]) * unscale
        s_ref[slot, hh] = s if online else jnp.exp2(s.astype(BF16))

    def step(kb, cur, nxt):
        vt = vt_ref[0, 0, :, pl.ds(pl.multiple_of(kb * TK, TK), TK)]
        kt = key_tile(jnp.minimum(kb + 1, n_blocks - 1))
        for hh in range(ATTN_REP):
            produce(kt, nxt, hh)
            if online:
                s = s_ref[cur, hh]
                m_old = m_ref[hh]
                m_new = jnp.maximum(m_old, jnp.max(s, axis=0, keepdims=True))
                p = jnp.exp2(s - m_new).astype(BF16)
                acc_ref[hh] = jnp.exp2(m_old - m_new) * acc_ref[hh] + _dot(vt, p)
                m_ref[hh] = m_new
            else:
                acc_ref[hh] += _dot(vt, s_ref[cur, hh])

    kt0 = key_tile(0)
    for hh in range(ATTN_REP):
        produce(kt0, 0, hh)

    def group(i, carry):
        for u in range(ATTN_UNROLL):
            step(ATTN_UNROLL * i + u, u % 2, (u + 1) % 2)
        return carry

    def pair(i, carry):
        step(2 * i, 0, 1)
        step(2 * i + 1, 1, 0)
        return carry

    n_groups = n_blocks // ATTN_UNROLL
    lax.fori_loop(0, n_groups, group, 0)
    lax.fori_loop(n_groups * (ATTN_UNROLL // 2), n_blocks // 2, pair, 0)

    @pl.when(n_blocks % 2 == 1)
    def _():
        step(n_blocks - 1, 0, 1)

    for hh in range(ATTN_REP):
        acc = acc_ref[hh]
        o = acc[:HEAD_DIM] / acc[HEAD_DIM:HEAD_DIM + 1]
        o_ref[0, hh * HEAD_DIM:(hh + 1) * HEAD_DIM, :] = o.astype(o_ref.dtype)


def _attn_call(qa, ka, vt, unscale, n_ctx, online):
    bsz, _, t, _ = qa.shape
    kern = functools.partial(_attn_kernel, online=online, n_ctx_q=n_ctx // TQ, n_ctx_k=n_ctx // TK, n_k=t // TK)
    gw = ATTN_REP * HEAD_DIM
    scratch = [pltpu.VMEM((ATTN_REP, V_ROWS, TQ), F32),
               pltpu.VMEM((2, ATTN_REP, TK, TQ), F32 if online else BF16)]
    if online:
        scratch.append(pltpu.VMEM((ATTN_REP, 1, TQ), F32))
    return pl.pallas_call(
        kern,
        grid=(bsz, ATTN_KV_HEADS, t // TQ),
        in_specs=[pl.BlockSpec((1, ATTN_REP, TQ, QK_ROWS), lambda b, g, i: (b, g, i, 0)),
                  pl.BlockSpec((1, 1, t, QK_ROWS), lambda b, g, i: (b, g, 0, 0)),
                  pl.BlockSpec((1, 1, V_ROWS, t), lambda b, g, i: (b, g, 0, 0)),
                  pl.BlockSpec((1, 1), lambda b, g, i: (0, 0))],
        out_specs=pl.BlockSpec((1, gw, TQ), lambda b, g, i: (b, g, i)),
        out_shape=jax.ShapeDtypeStruct((bsz, ATTN_W, t), BF16),
        scratch_shapes=scratch,
        compiler_params=_cparams(("parallel", "parallel", "arbitrary")),
        name="attention_online" if online else "attention",
    )(qa, ka, vt, unscale)


def _attn_constants(q_gain, k_gain):
    qmax = jnp.maximum(jnp.sqrt(float(HEAD_DIM)) * Q_SCALE * jnp.max(jnp.abs(q_gain)), 1e-6)
    kmax = jnp.maximum(jnp.sqrt(float(HEAD_DIM)) * jnp.max(jnp.abs(k_gain)), 1e-6)
    shift = qmax * kmax
    q_scale = jnp.exp2(jnp.floor(jnp.log2(FP8_TARGET / qmax)))
    k_scale = jnp.exp2(jnp.floor(jnp.log2(FP8_TARGET / kmax)))
    r = (jnp.minimum(shift, MAX_FIXED_SHIFT) * q_scale * k_scale * (SHIFT_FRACTION / SHIFT_ROWS)).astype(FP8).astype(F32)
    in_proj_consts = jnp.stack([Q_SCALE * q_scale, k_scale, r])
    in_proj_consts = jnp.pad(in_proj_consts, (0, LANES - in_proj_consts.shape[0])).reshape(1, LANES)
    unscale = (1.0 / (q_scale * k_scale)).reshape(1, 1)
    return in_proj_consts, unscale, shift <= MAX_FIXED_SHIFT


def _attention(qa, ka, vt, unscale, use_fixed_shift, n_ctx):
    return lax.cond(use_fixed_shift,
                    lambda: _attn_call(qa, ka, vt, unscale, n_ctx, False),
                    lambda: _attn_call(qa, ka, vt, unscale, n_ctx, True))


def _conv_kernel(x_ref, prev_ref, next_ref, w_ref, o_ref, *, n_ctx, t):
    x = x_ref[0]
    rows = x.shape[0]
    ridx = lax.broadcasted_iota(jnp.int32, x.shape, 0)
    tok = ridx + pl.program_id(1) * rows
    xm1 = jnp.where(ridx == 0, prev_ref[0, 7:8, :], pltpu.roll(x, 1, 0))
    xp1 = jnp.where(ridx == rows - 1, next_ref[0, 0:1, :], pltpu.roll(x, rows - 1, 0))
    xm1 = jnp.where(jnp.logical_or(tok == 0, tok == n_ctx), 0.0, xm1)
    xp1 = jnp.where(jnp.logical_or(tok == n_ctx - 1, tok == t - 1), 0.0, xp1)
    w = w_ref[...]
    y = xm1 * w[0:1] + x * w[1:2] + xp1 * w[2:3]
    y = y * _sigmoid(y)
    lane = lax.broadcasted_iota(jnp.int32, x.shape, 1)
    y = jnp.where(lane >= ML_W, y * (ML_DK ** -0.5), y)
    o_ref[0] = y.astype(o_ref.dtype)


def _conv_call(mqk, w, n_ctx):
    bsz, t, width = mqk.shape
    blk = next(b for b in CONV_BLKS if t % b == 0)
    nb = t // blk
    per8 = blk // 8
    last8 = t // 8 - 1
    kern = functools.partial(_conv_kernel, n_ctx=n_ctx, t=t)
    return pl.pallas_call(
        kern,
        grid=(bsz, nb),
        in_specs=[pl.BlockSpec((1, blk, width), lambda b, i: (b, i, 0)),
                  pl.BlockSpec((1, 8, width), lambda b, i: (b, jnp.maximum(i * per8 - 1, 0), 0)),
                  pl.BlockSpec((1, 8, width), lambda b, i: (b, jnp.minimum((i + 1) * per8, last8), 0)),
                  pl.BlockSpec(w.shape, lambda b, i: (0, 0))],
        out_specs=pl.BlockSpec((1, blk, width), lambda b, i: (b, i, 0)),
        out_shape=jax.ShapeDtypeStruct((bsz, t, width), BF16),
        compiler_params=_cparams(("parallel", "parallel")),
        name="mlstm_qk_conv",
    )(mqk, mqk, mqk, w)


def _scan_block(d, j, nct, nb):
    bwd = jnp.where(j < nct, nct - 1 - j, nb - 1 - (j - nct))
    return jnp.where(d == 0, j, bwd)


def _hgrn_kernel(q0_ref, q1_ref, v0_ref, v1_ref, z0_ref, z1_ref, lb_ref, e_ref, bd_ref, o_ref,
                 st_ref, qd_ref, kd_ref, od_ref, dt_ref, u_ref):
    d = pl.program_id(1)
    j = pl.program_id(2)

    @pl.when(j == 0)
    def _():
        st_ref[...] = jnp.zeros(st_ref.shape, F32)

    lb = lb_ref[0]
    e = e_ref[...]
    bd = bd_ref[...]
    c = HG_CHUNK
    nch = SCAN_BLK // c

    def rows(i):
        return pl.ds(jnp.where(d == 0, i, c - 1 - i), nch, stride=c)

    def load2(r0, r1, sl):
        return jnp.concatenate([r0[0, sl, :], r1[0, sl, :]], axis=1)

    def store_halves(ref, sl, x):
        ref[0, sl, :] = x[:, :LANES]
        ref[1, sl, :] = x[:, LANES:]

    qs, ks, vs, bs = [], [], [], []
    b = None
    for i in range(c):
        f = lb + (1.0 - lb) * _sigmoid(load2(z0_ref, z1_ref, rows(i)))
        lf = jnp.log2(f)
        b = lf if b is None else b + lf
        qs.append(load2(q0_ref, q1_ref, rows(i)))
        vs.append(load2(v0_ref, v1_ref, rows(i)))
        ks.append(1.0 - f)
        bs.append(b)
    b_tot = bs[-1]
    chunk_decay = jnp.exp2(b_tot)
    dt_ref[0] = chunk_decay[:, :LANES]
    dt_ref[1] = chunk_decay[:, LANES:]
    for i in range(c):
        store_halves(qd_ref, rows(i), qs[i] * jnp.exp2(bs[i]))
        store_halves(kd_ref, rows(i), ks[i] * jnp.exp2(b_tot - bs[i]))
    for cc in range(nch):
        sl = pl.ds(cc * c, c)
        for half, v_half_ref in enumerate((v0_ref, v1_ref)):
            vc = v_half_ref[0, sl, :].astype(BF16)
            kc = kd_ref[half, sl, :].astype(BF16)
            u_ref[cc, half] = _dot_tn(vc, kc) * bd
    for i in range(c):
        ps = [qs[i] * ks[jj] * jnp.exp2(bs[i] - bs[jj]) for jj in range(i)] + [qs[i] * ks[i]]
        a = _dot(jnp.concatenate(ps, axis=0).astype(BF16), e)
        o = a[0:nch] * vs[0]
        for jj in range(1, i + 1):
            o = o + a[jj * nch:(jj + 1) * nch] * vs[jj]
        store_halves(od_ref, rows(i), o)

    for n in range(nch):
        cc = jnp.where(d == 0, n, nch - 1 - n)
        sl = pl.ds(pl.multiple_of(cc * c, c), c)
        for half in range(2):
            lanes = slice(half * LANES, (half + 1) * LANES)
            st = st_ref[half]
            o_inter = _dot_nt(qd_ref[half, sl, :].astype(BF16), st.astype(BF16))
            o_ref[0, 0, sl, lanes] = od_ref[half, sl, :] + o_inter
            st_ref[half] = st * dt_ref[half, pl.ds(cc, 1), :] + u_ref[cc, half]


def _hgrn_call(hgp, lbs, e, bd, nct):
    bsz, t, _ = hgp.shape
    nb = t // SCAN_BLK
    w = HG_W
    blk = lambda d, j: _scan_block(d, j, nct, nb)
    half = lambda col: pl.BlockSpec((1, SCAN_BLK, LANES), lambda b, d, j: (b, blk(d, j), col(d)))
    return pl.pallas_call(
        _hgrn_kernel,
        grid=(bsz, 2, nb),
        in_specs=[half(lambda d: 0), half(lambda d: 1),
                  half(lambda d: 2), half(lambda d: 3),
                  half(lambda d: 4 + 2 * d), half(lambda d: 5 + 2 * d),
                  pl.BlockSpec((1, 1, w), lambda b, d, j: (d, 0, 0)),
                  pl.BlockSpec((w, w), lambda b, d, j: (0, 0)),
                  pl.BlockSpec((LANES, LANES), lambda b, d, j: (0, 0))],
        out_specs=pl.BlockSpec((1, 1, SCAN_BLK, w), lambda b, d, j: (d, b, blk(d, j), 0)),
        out_shape=jax.ShapeDtypeStruct((2, bsz, t, w), F32),
        scratch_shapes=[pltpu.VMEM((2, LANES, LANES), F32),
                        pltpu.VMEM((2, SCAN_BLK, LANES), F32),
                        pltpu.VMEM((2, SCAN_BLK, LANES), F32),
                        pltpu.VMEM((2, SCAN_BLK, LANES), F32),
                        pltpu.VMEM((2, SCAN_BLK // HG_CHUNK, LANES), F32),
                        pltpu.VMEM((SCAN_BLK // HG_CHUNK, 2, LANES, LANES), F32)],
        compiler_params=_cparams(("parallel", "parallel", "arbitrary")),
        name="hgrn2_scan",
    )(hgp, hgp, hgp, hgp, hgp, hgp, lbs, e, bd)


def _mlstm_kernel(q_ref, k_ref, vt_ref, g_ref, gt_ref, m_ref, mt_ref, o_ref, c_ref, ms_ref):
    j = pl.program_id(2)

    @pl.when(j == 0)
    def _():
        c_ref[...] = jnp.zeros(c_ref.shape, F32)
        ms_ref[...] = jnp.zeros(ms_ref.shape, F32)

    length = SCAN_BLK
    mask = m_ref[0]
    mask_t = mt_ref[0]
    valid_t = mask_t > 0
    gates = g_ref[0, 0]
    gates_t = gt_ref[0, 0]
    b_col_all = _mask_dot(mask, _log_sigmoid(gates))
    lf_t = _log_sigmoid(gates_t)
    b_row_all = _dot_mask(lf_t, mask_t)
    tot_all = jnp.sum(lf_t, axis=1, keepdims=True)
    row = lax.broadcasted_iota(jnp.int32, (LANES, length), 0)

    heads = range(ML_HEADS)
    kh = [k_ref[0, :, hh * ML_DK:(hh + 1) * ML_DK] for hh in heads]
    qh = [q_ref[0, :, hh * ML_DK:(hh + 1) * ML_DK] for hh in heads]
    s_raw = [_dot_nt(kh[hh], qh[hh]) for hh in heads]
    inter = [_dot_nt(c_ref[hh].astype(BF16), qh[hh]) for hh in heads]
    for pair in range(ML_HEADS // 2):
        v2t = vt_ref[0, pair * LANES:(pair + 1) * LANES, :]
        out_t = jnp.zeros((LANES, length), F32)
        for sub in range(2):
            hh = pair * 2 + sub
            own = (row < 64) if sub == 0 else (row >= 64)
            ones_row = 64 if sub == 0 else 0
            v_aug_t = jnp.where(own, v2t, jnp.where(row == ones_row, 1.0, 0.0).astype(BF16))
            m_prev = ms_ref[hh, 0:1, 0:1]
            cs = gates[:, hh:hh + 1] - b_col_all[:, 4 + hh:5 + hh]
            br = b_row_all[4 + hh:5 + hh, :]
            tot = tot_all[4 + hh:5 + hh, :]
            log_d = jnp.where(valid_t, br + cs, NEG_BIG)
            m_t = jnp.maximum(br + m_prev, jnp.max(log_d, axis=0, keepdims=True))
            s_t = (s_raw[hh] * jnp.exp(log_d - m_t)).astype(BF16)
            r_t = jnp.exp(br + m_prev - m_t) * inter[hh] + _dot(v_aug_t, s_t)
            den = r_t[ones_row:ones_row + 1, :]
            out_t = jnp.where(own, r_t / jnp.maximum(jnp.abs(den), jnp.exp(-m_t)), out_t)
            log_e = tot + cs
            m_new = jnp.maximum(tot + m_prev, jnp.max(log_e, axis=0, keepdims=True))
            ke = (kh[hh].astype(F32) * jnp.exp(log_e - m_new)).astype(BF16)
            c_ref[hh] = jnp.exp(tot + m_prev - m_new) * c_ref[hh] + _dot(v_aug_t, ke)
            ms_ref[hh] = jnp.broadcast_to(m_new, ms_ref.shape[1:])
        o_ref[0, 0, :, pair * LANES:(pair + 1) * LANES] = out_t.T


def _mlstm_call(qk, vt, gcol, grow, mask, mask_t, nct):
    bsz, t, _ = qk.shape
    nb = t // SCAN_BLK
    w = ML_W
    blk = lambda d, j: _scan_block(d, j, nct, nb)
    return pl.pallas_call(
        _mlstm_kernel,
        grid=(bsz, 2, nb),
        in_specs=[pl.BlockSpec((1, SCAN_BLK, w), lambda b, d, j: (b, blk(d, j), 0)),
                  pl.BlockSpec((1, SCAN_BLK, w), lambda b, d, j: (b, blk(d, j), 1)),
                  pl.BlockSpec((1, w, SCAN_BLK), lambda b, d, j: (b, 0, blk(d, j))),
                  pl.BlockSpec((1, 1, SCAN_BLK, LANES), lambda b, d, j: (b, d, blk(d, j), 0)),
                  pl.BlockSpec((1, 1, 8, SCAN_BLK), lambda b, d, j: (b, d, 0, blk(d, j))),
                  pl.BlockSpec((1, SCAN_BLK, SCAN_BLK), lambda b, d, j: (d, 0, 0)),
                  pl.BlockSpec((1, SCAN_BLK, SCAN_BLK), lambda b, d, j: (d, 0, 0))],
        out_specs=pl.BlockSpec((1, 1, SCAN_BLK, w), lambda b, d, j: (d, b, blk(d, j), 0)),
        out_shape=jax.ShapeDtypeStruct((2, bsz, t, w), F32),
        scratch_shapes=[pltpu.VMEM((ML_HEADS, LANES, ML_DK), F32),
                        pltpu.VMEM((ML_HEADS, SUBLANES, LANES), F32)],
        compiler_params=_cparams(("parallel", "parallel", "arbitrary")),
        name="mlstm_scan",
    )(qk, qk, vt, gcol, grow, mask, mask_t)


def _out_kernel(hc_ref, hl_ref, mod_ref, at_ref, hf_ref, hb_ref, hgate_ref, gn_ref, e_ref, mf_ref, mb_ref, mgate_ref,
                wa_ref, wr_ref, wm_ref, o_ref, *, nct, skip):
    mod = mod_ref[0, 0]
    o = hf_ref[0, 0] + hb_ref[0, 0]
    ss = _dot((o * o).astype(BF16), e_ref[...])
    gate = hgate_ref[0]
    r = o * lax.rsqrt(ss + EPS) * gn_ref[...] * (gate * _sigmoid(gate))
    m = _sigmoid(mgate_ref[0]) * (mf_ref[0, 0] + mb_ref[0, 0])
    y = _dot_tn(at_ref[0], wa_ref[...]) + _dot(r.astype(BF16), wr_ref[...]) + _dot(m.astype(BF16), wm_ref[...])
    h = jnp.where(pl.program_id(1) + skip < nct, hc_ref[0], hl_ref[0])
    o_ref[0] = h + mod[2:3] * y


def _out_call(h_ctx, h_lat, lat_off, t, modsel, at, hg_o, hgp, gn, e, ml_o, mvo, wa, wr, wm, nct, skip):
    bsz, _, d = h_ctx.shape
    nt = t // TM - skip
    full = lambda a: pl.BlockSpec(a.shape, lambda bi, i: (0,) * a.ndim)
    hg_col = (OFF_MQK - OFF_HG) // HG_W - 1
    ctx_spec, lat_spec = _stream_spec(nct, lat_off, skip)
    return pl.pallas_call(
        functools.partial(_out_kernel, nct=nct, skip=skip),
        grid=(bsz, nt),
        in_specs=[ctx_spec(h_ctx), lat_spec(h_lat),
                  pl.BlockSpec((1, 1, N_MOD, d), lambda bi, i: (bi, (i + skip >= nct).astype(jnp.int32), 0, 0)),
                  pl.BlockSpec((1, ATTN_W, TM), lambda bi, i: (bi, 0, i + skip)),
                  pl.BlockSpec((1, 1, TM, HG_W), lambda bi, i: (0, bi, i + skip, 0)),
                  pl.BlockSpec((1, 1, TM, HG_W), lambda bi, i: (1, bi, i + skip, 0)),
                  pl.BlockSpec((1, TM, HG_W), lambda bi, i: (bi, i + skip, hg_col)),
                  full(gn), full(e),
                  pl.BlockSpec((1, 1, TM, ML_W), lambda bi, i: (0, bi, i + skip, 0)),
                  pl.BlockSpec((1, 1, TM, ML_W), lambda bi, i: (1, bi, i + skip, 0)),
                  pl.BlockSpec((1, TM, ML_W), lambda bi, i: (bi, i + skip, 1)),
                  full(wa), full(wr), full(wm)],
        out_specs=pl.BlockSpec((1, TM, d), lambda bi, i: (bi, i, 0)),
        out_shape=jax.ShapeDtypeStruct((bsz, nt * TM, d), F32),
        compiler_params=_cparams(("parallel", "parallel")),
        name="readout_out_proj",
    )(h_ctx, h_lat, modsel, at, hg_o, hg_o, hgp, gn, e, ml_o, ml_o, mvo, wa, wr, wm)


def _ffn_kernel(h_ref, mod_ref, g_ref, w1_ref, w2_ref, o_ref):
    x = h_ref[0]
    mod = mod_ref[0, 0]
    ms = jnp.mean(x * x, axis=-1, keepdims=True)
    y = x * lax.rsqrt(ms + EPS) * g_ref[...]
    u = (y * (1.0 + mod[4:5]) + mod[3:4]).astype(BF16)
    a = jnp.maximum(_dot(u, w1_ref[...]), 0.0)
    a = (a * a).astype(BF16)
    o_ref[0] = x + mod[5:6] * _dot(a, w2_ref[...])


def _ffn_call(h, modsel, g, w1, w2, nct):
    bsz, t, d = h.shape
    nt = t // TM
    full = lambda a: pl.BlockSpec(a.shape, lambda bi, i: (0,) * a.ndim, pipeline_mode=pl.Buffered(1))
    return pl.pallas_call(
        _ffn_kernel,
        grid=(bsz, nt),
        in_specs=[pl.BlockSpec((1, TM, d), lambda bi, i: (bi, i, 0)),
                  pl.BlockSpec((1, 1, N_MOD, d), lambda bi, i: (bi, (i >= nct).astype(jnp.int32), 0, 0)),
                  pl.BlockSpec(g.shape, lambda bi, i: (0, 0)),
                  full(w1), full(w2)],
        out_specs=pl.BlockSpec((1, TM, d), lambda bi, i: (bi, i, 0)),
        out_shape=jax.ShapeDtypeStruct((bsz, t, d), F32),
        compiler_params=_cparams(("parallel", "parallel")),
        name="ffn",
    )(h, modsel, g, w1, w2)


def _rope_tables(n_ctx, n_lat):
    inv_freq = ROPE_THETA ** (-np.arange(0, ROPE_AXIS_DIM, 2, dtype=np.float32) / ROPE_AXIS_DIM)
    inv_freq = jnp.asarray(inv_freq, F32)
    n_rows = n_lat // GRID_W
    ang_row = jnp.arange(n_rows, dtype=F32)[:, None] * inv_freq[None, :]
    ang_col = jnp.arange(GRID_W, dtype=F32)[:, None] * inv_freq[None, :]
    sign = np.where((np.arange(HEAD_DIM) % 32) < 16, -1.0, 1.0).astype(np.float32)

    def table(fn, ctx_value, signs):
        per_row = jnp.broadcast_to(fn(ang_row)[:, None, :], (n_rows, GRID_W, inv_freq.shape[0])).reshape(n_lat, -1)
        per_col = jnp.broadcast_to(fn(ang_col)[None, :, :], (n_rows, GRID_W, inv_freq.shape[0])).reshape(n_lat, -1)
        lat = jnp.concatenate([per_row, per_row, per_col, per_col], axis=1) * signs
        full = jnp.concatenate([jnp.full((n_ctx, HEAD_DIM), ctx_value, F32), lat], axis=0)
        return jnp.concatenate([full, full], axis=1)

    return table(jnp.cos, 1.0, 1.0), table(jnp.sin, 0.0, jnp.asarray(sign)[None, :])


def _block_ones(width, head, scale, dtype):
    idx = np.arange(width) // head
    return jnp.asarray((idx[:, None] == idx[None, :]).astype(np.float32) * scale, dtype)


def _scan_masks(n):
    tril = np.tril(np.ones((n, n), np.float32))
    return np.stack([tril, tril.T])


def kernel(x, c, ctx, c_ctx, w_mod, b_mod, norm_mix, norm_ffn, w_in, b_in, q_norm, k_norm,
           hg_lb, hg_norm, ml_conv, w_out, w_ff1, w_ff2):
    bsz, n_lat, d = x.shape
    n_ctx = ctx.shape[1]
    depth = w_mod.shape[0]
    t = n_ctx + n_lat
    nct = n_ctx // TM
    assert n_ctx % SCAN_BLK == 0 and n_lat % SCAN_BLK == 0 and n_lat % GRID_W == 0 and bsz + 1 <= 8

    cvec = jnp.concatenate([c, c_ctx[None, :], jnp.zeros((8 - bsz - 1, d), F32)], axis=0)
    mods = _mod_call(cvec, w_mod, b_mod).reshape(depth, 8, N_MOD, d)
    modsel = jnp.stack([jnp.broadcast_to(mods[:, bsz][:, None], (depth, bsz, N_MOD, d)), mods[:, :bsz]], axis=2)

    cos, sin = _rope_tables(n_ctx, n_lat)
    e_attn = _block_ones(ATTN_W, HEAD_DIM, 1.0 / HEAD_DIM, BF16)
    e_hg_mean = _block_ones(HG_W, HG_DK, 1.0 / HG_DK, BF16)
    e_hg = _block_ones(HG_W, HG_DK, 1.0, BF16)
    bd_hg = _block_ones(LANES, HG_DK, 1.0, F32)
    scan_mask = _scan_masks(SCAN_BLK)
    mask_ml = jnp.asarray(scan_mask, BF16)
    mask_ml_t = jnp.asarray(np.transpose(scan_mask, (0, 2, 1)).copy(), BF16)

    lbs = jnp.cumsum(jax.nn.softmax(hg_lb.astype(F32), axis=0), axis=0)
    lbs = (lbs - lbs[:1]).reshape(depth, 2, 1, HG_W)

    w_in_p = jnp.pad(w_in, ((0, 0), (0, 0), (0, N_IN_PAD - N_IN))).astype(BF16)
    b_in_p = jnp.pad(b_in, ((0, 0), (0, N_IN_PAD - N_IN))).reshape(depth, 1, N_IN_PAD)
    w_out_b = w_out.astype(BF16)
    w1_b = w_ff1.astype(BF16)
    w2_b = w_ff2.astype(BF16)
    gq = jnp.tile(q_norm, (1, ATTN_HEADS)).reshape(depth, 1, ATTN_W)
    gk = jnp.tile(k_norm, (1, ATTN_KV_HEADS)).reshape(depth, 1, KV_W)
    gn = jnp.tile(hg_norm, (1, HG_HEADS)).reshape(depth, 1, HG_W)

    h_ctx, h_lat, lat_off = ctx, x, 0
    for l in range(depth):
        last = l == depth - 1
        attn_consts, unscale, use_fixed_shift = _attn_constants(q_norm[l], k_norm[l])
        qa, ka, vt, hgp, mqk, mvo, mvt, gcol, grow = _in_call(
            h_ctx, h_lat, lat_off, t, modsel[l], norm_mix[l].reshape(1, d), w_in_p[l], b_in_p[l], cos, sin,
            gq[l], gk[l], e_attn, attn_consts, nct)

        at = _attention(qa, ka, vt, unscale, use_fixed_shift, n_ctx)

        hg_o = _hgrn_call(hgp, lbs[l], e_hg, bd_hg, nct)

        qk = _conv_call(mqk, ml_conv[l], n_ctx)
        ml_o = _mlstm_call(qk, mvt, gcol, grow, mask_ml, mask_ml_t, nct)

        skip = nct if last else 0
        h1 = _out_call(h_ctx, h_lat, lat_off, t, modsel[l], at, hg_o, hgp, gn[l], e_hg_mean, ml_o, mvo,
                       w_out_b[l, :ATTN_W], w_out_b[l, ATTN_W:ATTN_W + HG_W], w_out_b[l, ATTN_W + HG_W:], nct, skip)
        h = _ffn_call(h1, modsel[l], norm_ffn[l].reshape(1, d), w1_b[l], w2_b[l], 0 if last else nct)
        h_ctx, h_lat, lat_off = h, h, nct
    return h
```

```python
import functools

import jax
import jax.numpy as jnp
import numpy as np
from jax import lax
from jax.experimental import pallas as pl
from jax.experimental.pallas import tpu as pltpu

F32 = jnp.float32
BF16 = jnp.bfloat16
LANES = 128
SUBLANES = 8

GRID_W = 64
HEAD_DIM = 64
ATTN_HEADS = 8
ATTN_KV_HEADS = 2
ATTN_REP = ATTN_HEADS // ATTN_KV_HEADS
ROPE_THETA = 10000.0
ROPE_AXIS_DIM = HEAD_DIM // 2
HG_HEADS = 4
HG_DK = 64
ML_HEADS = 4
ML_DK = 64
N_MOD = 6
EPS = 1e-6
ATTN_W = ATTN_HEADS * HEAD_DIM
KV_W = ATTN_KV_HEADS * HEAD_DIM
HG_W = HG_HEADS * HG_DK
ML_W = ML_HEADS * ML_DK

OFF_AQ, OFF_AK, OFF_AV = 0, 512, 640
OFF_HG = 768
OFF_MQK = 2048
OFF_MVO = 2560
OFF_MG = 3072
N_IN = 3088
N_IN_PAD = 3200

TM = 256
TQ = 256
TK = 256
ATTN_UNROLL = 32
SCAN_BLK = 256
CONV_BLKS = (1280, 1024, 768, 512, 256)
HG_CHUNK = 16
V_ROWS = 80
FP8 = jnp.float8_e4m3fn
QK_ROWS = 256
SHIFT_ROWS = QK_ROWS - 3 * HEAD_DIM
FP8_TARGET = 128.0
MAX_FIXED_SHIFT = 32.0
SHIFT_FRACTION = 0.5
NEG_BIG = -1e30
Q_SCALE = float(np.log2(np.e)) * HEAD_DIM ** -0.5
VMEM_LIMIT = 56 * 1024 * 1024


def _cparams(sem):
    return pltpu.CompilerParams(dimension_semantics=sem, vmem_limit_bytes=VMEM_LIMIT)


def _split3(x):
    hi = x.astype(BF16)
    r1 = x - hi.astype(F32)
    mid = r1.astype(BF16)
    lo = (r1 - mid.astype(F32)).astype(BF16)
    return hi, mid, lo


def _dot(a, b):
    return jnp.dot(a, b, preferred_element_type=F32)


def _dot_nt(a, b):
    return lax.dot_general(a, b, (((1,), (1,)), ((), ())), preferred_element_type=F32)


def _dot_tn(a, b):
    return lax.dot_general(a, b, (((0,), (0,)), ((), ())), preferred_element_type=F32)


def _mask_dot(mask_bf16, x):
    hi, mid, lo = _split3(x)
    return _dot(mask_bf16, hi) + _dot(mask_bf16, mid) + _dot(mask_bf16, lo)


def _dot_mask(x, mask_bf16):
    hi, mid, lo = _split3(x)
    return _dot(hi, mask_bf16) + _dot(mid, mask_bf16) + _dot(lo, mask_bf16)


def _sigmoid(x):
    return 1.0 / (1.0 + jnp.exp(-x))


def _log_sigmoid(x):
    return jnp.minimum(x, 0.0) - jnp.log(1.0 + jnp.exp(-jnp.abs(x)))


def _mod_kernel(c_ref, w_ref, b_ref, o_ref):
    cv = c_ref[...]
    a = cv * _sigmoid(cv)
    o_ref[0] = jnp.dot(a, w_ref[0], preferred_element_type=F32, precision=lax.Precision.HIGHEST) + b_ref[0]


def _mod_call(cvec, w_mod, b_mod):
    depth, d, nm = w_mod.shape
    tn = 1024
    return pl.pallas_call(
        _mod_kernel,
        grid=(depth, nm // tn),
        in_specs=[pl.BlockSpec((8, d), lambda l, j: (0, 0)),
                  pl.BlockSpec((1, d, tn), lambda l, j: (l, 0, j)),
                  pl.BlockSpec((1, 1, tn), lambda l, j: (l, 0, j))],
        out_specs=pl.BlockSpec((1, 8, tn), lambda l, j: (l, 0, j)),
        out_shape=jax.ShapeDtypeStruct((depth, 8, nm), F32),
        compiler_params=_cparams(("arbitrary", "arbitrary")),
        name="mod_vectors",
    )(cvec, w_mod, b_mod.reshape(depth, 1, nm))


def _head_rms(t, gain, e):
    ss = _dot((t * t).astype(BF16), e)
    return t * lax.rsqrt(ss + EPS) * gain


def _rope(t, cos, sin):
    w = t.shape[1]
    reps = w // LANES
    cosw = jnp.concatenate([cos] * reps, axis=1) if reps > 1 else cos
    sinw = jnp.concatenate([sin] * reps, axis=1) if reps > 1 else sin
    lane = lax.broadcasted_iota(jnp.int32, t.shape, 1)
    first_half = (lane % 32) < 16
    partner = jnp.where(first_half, pltpu.roll(t, w - 16, 1), pltpu.roll(t, 16, 1))
    return t * cosw + partner * sinw


def _fp8_operand(d, first_half, fill, rem_first):
    rem = d - d.astype(FP8).astype(F32)
    mixed = jnp.where(first_half, rem, d) if rem_first else jnp.where(first_half, d, rem)
    filled = jnp.where(first_half, d, fill)
    return jnp.concatenate([filled.astype(FP8), mixed.astype(FP8)], axis=1)


def _stream_spec(nct, lat_off, skip=0):
    d_last = lambda a: a.shape[-1]
    ctx_spec = lambda a: pl.BlockSpec((1, TM, d_last(a)), lambda bi, i: (bi, jnp.minimum(i + skip, nct - 1), 0))
    lat_spec = lambda a: pl.BlockSpec((1, TM, d_last(a)), lambda bi, i: (bi, jnp.maximum(i + skip - nct, 0) + lat_off, 0))
    return ctx_spec, lat_spec


def _in_kernel(hc_ref, hl_ref, mod_ref, g_ref, w_ref, b_ref, cos_ref, sin_ref, gq_ref, gk_ref, e_ref, ac_ref,
               qa_ref, ka_ref, vt_ref, hg_ref, mqk_ref, mvo_ref, mvt_ref, gcol_ref, grow_ref, *, nct):
    x = jnp.where(pl.program_id(1) < nct, hc_ref[0], hl_ref[0])
    ms = jnp.mean(x * x, axis=-1, keepdims=True)
    y = x * lax.rsqrt(ms + EPS) * g_ref[...]
    mod = mod_ref[0, 0]
    u = y * (1.0 + mod[1:2]) + mod[0:1]
    p = _dot(u.astype(BF16), w_ref[...]) + b_ref[...]
    cos = cos_ref[...]
    sin = sin_ref[...]
    e = e_ref[...]
    q = _rope(_head_rms(p[:, OFF_AQ:OFF_AQ + ATTN_W], gq_ref[...], e), cos, sin)
    k = _rope(_head_rms(p[:, OFF_AK:OFF_AK + KV_W], gk_ref[...], e[:KV_W, :KV_W]), cos, sin)

    consts = ac_ref[...]
    first_half = lax.broadcasted_iota(jnp.int32, (x.shape[0], LANES), 1) < HEAD_DIM
    qs = q * consts[:, 0:1]
    neg_r = -consts[:, 2:3]
    for g in range(ATTN_W // LANES):
        grp = qs[:, g * LANES:(g + 1) * LANES]
        swp = pltpu.roll(grp, HEAD_DIM, 1)
        qa_ref[0, 2 * g] = _fp8_operand(jnp.where(first_half, grp, swp), first_half, neg_r, False)
        qa_ref[0, 2 * g + 1] = _fp8_operand(jnp.where(first_half, swp, grp), first_half, neg_r, False)
    ks = k * consts[:, 1:2]
    ksw = pltpu.roll(ks, HEAD_DIM, 1)
    ka_ref[0, 0] = _fp8_operand(jnp.where(first_half, ks, ksw), first_half, 1.0, True)
    ka_ref[0, 1] = _fp8_operand(jnp.where(first_half, ksw, ks), first_half, 1.0, True)
    v_t = p[:, OFF_AV:OFF_AV + KV_W].T.astype(BF16)
    for hh in range(ATTN_KV_HEADS):
        vt_ref[0, hh, :HEAD_DIM, :] = v_t[hh * HEAD_DIM:(hh + 1) * HEAD_DIM]
        pad_row = lax.broadcasted_iota(jnp.int32, (V_ROWS - HEAD_DIM, x.shape[0]), 0)
        vt_ref[0, hh, HEAD_DIM:, :] = jnp.where(pad_row == 0, 1.0, 0.0).astype(BF16)
    hg_ref[0] = p[:, OFF_HG:OFF_MQK]
    mqk_ref[0] = p[:, OFF_MQK:OFF_MVO]
    mvo_ref[0] = p[:, OFF_MVO:OFF_MG]
    mvt_ref[0] = p[:, OFF_MVO:OFF_MVO + ML_W].T.astype(BF16)
    gates = p[:, OFF_MG:OFF_MG + LANES]
    lane = lax.broadcasted_iota(jnp.int32, gates.shape, 1)
    by_heads = pltpu.roll(gates, LANES - ML_HEADS, 1)
    by_2heads = pltpu.roll(gates, LANES - 2 * ML_HEADS, 1)
    for direction, (igate, fgate) in enumerate(((gates, by_heads), (by_heads, by_2heads))):
        g_dir = jnp.where(lane < ML_HEADS, igate, jnp.where(lane < 2 * ML_HEADS, fgate, 0.0))
        gcol_ref[0, direction] = g_dir
        grow_ref[0, direction] = g_dir.T[:2 * ML_HEADS]


def _in_call(h_ctx, h_lat, lat_off, t, modsel, g, w, b, cos, sin, gq, gk, e, attn_consts, nct):
    bsz, _, d = h_ctx.shape
    nt = t // TM
    row = lambda width: pl.BlockSpec((1, TM, width), lambda bi, i: (bi, i, 0))
    full = lambda a: pl.BlockSpec(a.shape, lambda bi, i: (0,) * a.ndim)
    out_widths = (OFF_MQK - OFF_HG, 2 * ML_W, 2 * ML_W)
    ctx_spec, lat_spec = _stream_spec(nct, lat_off)
    return pl.pallas_call(
        functools.partial(_in_kernel, nct=nct),
        grid=(bsz, nt),
        in_specs=[ctx_spec(h_ctx), lat_spec(h_lat),
                  pl.BlockSpec((1, 1, N_MOD, d), lambda bi, i: (bi, (i >= nct).astype(jnp.int32), 0, 0)),
                  full(g), full(w), full(b),
                  pl.BlockSpec((TM, LANES), lambda bi, i: (i, 0)),
                  pl.BlockSpec((TM, LANES), lambda bi, i: (i, 0)),
                  full(gq), full(gk), full(e), full(attn_consts)],
        out_specs=[pl.BlockSpec((1, ATTN_HEADS, TM, QK_ROWS), lambda bi, i: (bi, 0, i, 0)),
                   pl.BlockSpec((1, ATTN_KV_HEADS, TM, QK_ROWS), lambda bi, i: (bi, 0, i, 0)),
                   pl.BlockSpec((1, ATTN_KV_HEADS, V_ROWS, TM), lambda bi, i: (bi, 0, 0, i))]
        + [row(wd) for wd in out_widths]
        + [pl.BlockSpec((1, ML_W, TM), lambda bi, i: (bi, 0, i)),
           pl.BlockSpec((1, 2, TM, LANES), lambda bi, i: (bi, 0, i, 0)),
           pl.BlockSpec((1, 2, 2 * ML_HEADS, TM), lambda bi, i: (bi, 0, 0, i))],
        out_shape=[jax.ShapeDtypeStruct((bsz, ATTN_HEADS, t, QK_ROWS), FP8),
                   jax.ShapeDtypeStruct((bsz, ATTN_KV_HEADS, t, QK_ROWS), FP8),
                   jax.ShapeDtypeStruct((bsz, ATTN_KV_HEADS, V_ROWS, t), BF16)]
        + [jax.ShapeDtypeStruct((bsz, t, wd), F32) for wd in out_widths]
        + [jax.ShapeDtypeStruct((bsz, ML_W, t), BF16),
           jax.ShapeDtypeStruct((bsz, 2, t, LANES), F32),
           jax.ShapeDtypeStruct((bsz, 2, 2 * ML_HEADS, t), F32)],
        compiler_params=_cparams(("parallel", "parallel")),
        name="norm_mod_in_proj",
    )(h_ctx, h_lat, modsel, g, w, b, cos, sin, gq, gk, e, attn_consts)


def _attn_kernel(qa_ref, ka_ref, vt_ref, unscale_ref, o_ref, acc_ref, s_ref, *maybe_m_ref,
                 online, n_ctx_q, n_ctx_k, n_k):
    qi = pl.program_id(2)
    unscale = unscale_ref[...]
    acc_ref[...] = jnp.zeros(acc_ref.shape, F32)
    if online:
        m_ref, = maybe_m_ref
        m_ref[...] = jnp.full(m_ref.shape, NEG_BIG, F32)
    n_blocks = jnp.where(qi < n_ctx_q, n_ctx_k, n_k)

    def key_tile(kb):
        return ka_ref[0, 0, pl.ds(pl.multiple_of(kb * TK, TK), TK), :]

    def produce(kt, slot, hh):
        s = _dot_nt(kt, qa_ref[0, hh]) * unscale
        s_ref[slot, hh] = s if online else jnp.exp2(s.astype(BF16))

    def step(kb, cur, nxt):
        vt = vt_ref[0, 0, :, pl.ds(pl.multiple_of(kb * TK, TK), TK)]
        kt = key_tile(jnp.minimum(kb + 1, n_blocks - 1))
        for hh in range(ATTN_REP):
            produce(kt, nxt, hh)
            if online:
                s = s_ref[cur, hh]
                m_old = m_ref[hh]
                m_new = jnp.maximum(m_old, jnp.max(s, axis=0, keepdims=True))
                p = jnp.exp2(s - m_new).astype(BF16)
                acc_ref[hh] = jnp.exp2(m_old - m_new) * acc_ref[hh] + _dot(vt, p)
                m_ref[hh] = m_new
            else:
                acc_ref[hh] += _dot(vt, s_ref[cur, hh])

    kt0 = key_tile(0)
    for hh in range(ATTN_REP):
        produce(kt0, 0, hh)

    def group(i, carry):
        for u in range(ATTN_UNROLL):
            step(ATTN_UNROLL * i + u, u % 2, (u + 1) % 2)
        return carry

    def pair(i, carry):
        step(2 * i, 0, 1)
        step(2 * i + 1, 1, 0)
        return carry

    n_groups = n_blocks // ATTN_UNROLL
    lax.fori_loop(0, n_groups, group, 0)
    lax.fori_loop(n_groups * (ATTN_UNROLL // 2), n_blocks // 2, pair, 0)

    @pl.when(n_blocks % 2 == 1)
    def _():
        step(n_blocks - 1, 0, 1)

    for hh in range(ATTN_REP):
        acc = acc_ref[hh]
        o = acc[:HEAD_DIM] / acc[HEAD_DIM:HEAD_DIM + 1]
        o_ref[0, hh * HEAD_DIM:(hh + 1) * HEAD_DIM, :] = o.astype(o_ref.dtype)


def _attn_call(qa, ka, vt, unscale, n_ctx, online):
    bsz, _, t, _ = qa.shape
    kern = functools.partial(_attn_kernel, online=online, n_ctx_q=n_ctx // TQ, n_ctx_k=n_ctx // TK, n_k=t // TK)
    gw = ATTN_REP * HEAD_DIM
    scratch = [pltpu.VMEM((ATTN_REP, V_ROWS, TQ), F32),
               pltpu.VMEM((2, ATTN_REP, TK, TQ), F32 if online else BF16)]
    if online:
        scratch.append(pltpu.VMEM((ATTN_REP, 1, TQ), F32))
    return pl.pallas_call(
        kern,
        grid=(bsz, ATTN_KV_HEADS, t // TQ),
        in_specs=[pl.BlockSpec((1, ATTN_REP, TQ, QK_ROWS), lambda b, g, i: (b, g, i, 0)),
                  pl.BlockSpec((1, 1, t, QK_ROWS), lambda b, g, i: (b, g, 0, 0)),
                  pl.BlockSpec((1, 1, V_ROWS, t), lambda b, g, i: (b, g, 0, 0)),
                  pl.BlockSpec((1, 1), lambda b, g, i: (0, 0))],
        out_specs=pl.BlockSpec((1, gw, TQ), lambda b, g, i: (b, g, i)),
        out_shape=jax.ShapeDtypeStruct((bsz, ATTN_W, t), BF16),
        scratch_shapes=scratch,
        compiler_params=_cparams(("parallel", "parallel", "arbitrary")),
        name="attention_online" if online else "attention",
    )(qa, ka, vt, unscale)


def _attn_constants(q_gain, k_gain):
    qmax = jnp.maximum(jnp.sqrt(float(HEAD_DIM)) * Q_SCALE * jnp.max(jnp.abs(q_gain)), 1e-6)
    kmax = jnp.maximum(jnp.sqrt(float(HEAD_DIM)) * jnp.max(jnp.abs(k_gain)), 1e-6)
    shift = qmax * kmax
    q_scale = jnp.exp2(jnp.floor(jnp.log2(FP8_TARGET / qmax)))
    k_scale = jnp.exp2(jnp.floor(jnp.log2(FP8_TARGET / kmax)))
    r = (jnp.minimum(shift, MAX_FIXED_SHIFT) * q_scale * k_scale * (SHIFT_FRACTION / SHIFT_ROWS)).astype(FP8).astype(F32)
    in_proj_consts = jnp.stack([Q_SCALE * q_scale, k_scale, r])
    in_proj_consts = jnp.pad(in_proj_consts, (0, LANES - in_proj_consts.shape[0])).reshape(1, LANES)
    unscale = (1.0 / (q_scale * k_scale)).reshape(1, 1)
    return in_proj_consts, unscale, shift <= MAX_FIXED_SHIFT


def _attention(qa, ka, vt, unscale, use_fixed_shift, n_ctx):
    return lax.cond(use_fixed_shift,
                    lambda: _attn_call(qa, ka, vt, unscale, n_ctx, False),
                    lambda: _attn_call(qa, ka, vt, unscale, n_ctx, True))


def _conv_kernel(x_ref, prev_ref, next_ref, w_ref, o_ref, *, n_ctx, t):
    x = x_ref[0]
    rows = x.shape[0]
    ridx = lax.broadcasted_iota(jnp.int32, x.shape, 0)
    tok = ridx + pl.program_id(1) * rows
    xm1 = jnp.where(ridx == 0, prev_ref[0, 7:8, :], pltpu.roll(x, 1, 0))
    xp1 = jnp.where(ridx == rows - 1, next_ref[0, 0:1, :], pltpu.roll(x, rows - 1, 0))
    xm1 = jnp.where(jnp.logical_or(tok == 0, tok == n_ctx), 0.0, xm1)
    xp1 = jnp.where(jnp.logical_or(tok == n_ctx - 1, tok == t - 1), 0.0, xp1)
    w = w_ref[...]
    y = xm1 * w[0:1] + x * w[1:2] + xp1 * w[2:3]
    y = y * _sigmoid(y)
    lane = lax.broadcasted_iota(jnp.int32, x.shape, 1)
    y = jnp.where(lane >= ML_W, y * (ML_DK ** -0.5), y)
    o_ref[0] = y.astype(o_ref.dtype)


def _conv_call(mqk, w, n_ctx):
    bsz, t, width = mqk.shape
    blk = next(b for b in CONV_BLKS if t % b == 0)
    nb = t // blk
    per8 = blk // 8
    last8 = t // 8 - 1
    kern = functools.partial(_conv_kernel, n_ctx=n_ctx, t=t)
    return pl.pallas_call(
        kern,
        grid=(bsz, nb),
        in_specs=[pl.BlockSpec((1, blk, width), lambda b, i: (b, i, 0)),
                  pl.BlockSpec((1, 8, width), lambda b, i: (b, jnp.maximum(i * per8 - 1, 0), 0)),
                  pl.BlockSpec((1, 8, width), lambda b, i: (b, jnp.minimum((i + 1) * per8, last8), 0)),
                  pl.BlockSpec(w.shape, lambda b, i: (0, 0))],
        out_specs=pl.BlockSpec((1, blk, width), lambda b, i: (b, i, 0)),
        out_shape=jax.ShapeDtypeStruct((bsz, t, width), BF16),
        compiler_params=_cparams(("parallel", "parallel")),
        name="mlstm_qk_conv",
    )(mqk, mqk, mqk, w)


def _scan_block(d, j, nct, nb):
    bwd = jnp.where(j < nct, nct - 1 - j, nb - 1 - (j - nct))
    return jnp.where(d == 0, j, bwd)


def _hgrn_kernel(q0_ref, q1_ref, v0_ref, v1_ref, z0_ref, z1_ref, lb_ref, e_ref, bd_ref, o_ref,
                 st_ref, qd_ref, kd_ref, od_ref, dt_ref, u_ref):
    d = pl.program_id(1)
    j = pl.program_id(2)

    @pl.when(j == 0)
    def _():
        st_ref[...] = jnp.zeros(st_ref.shape, F32)

    lb = lb_ref[0]
    e = e_ref[...]
    bd = bd_ref[...]
    c = HG_CHUNK
    nch = SCAN_BLK // c

    def rows(i):
        return pl.ds(jnp.where(d == 0, i, c - 1 - i), nch, stride=c)

    def load2(r0, r1, sl):
        return jnp.concatenate([r0[0, sl, :], r1[0, sl, :]], axis=1)

    def store_halves(ref, sl, x):
        ref[0, sl, :] = x[:, :LANES]
        ref[1, sl, :] = x[:, LANES:]

    qs, ks, vs, bs = [], [], [], []
    b = None
    for i in range(c):
        f = lb + (1.0 - lb) * _sigmoid(load2(z0_ref, z1_ref, rows(i)))
        lf = jnp.log2(f)
        b = lf if b is None else b + lf
        qs.append(load2(q0_ref, q1_ref, rows(i)))
        vs.append(load2(v0_ref, v1_ref, rows(i)))
        ks.append(1.0 - f)
        bs.append(b)
    b_tot = bs[-1]
    chunk_decay = jnp.exp2(b_tot)
    dt_ref[0] = chunk_decay[:, :LANES]
    dt_ref[1] = chunk_decay[:, LANES:]
    for i in range(c):
        store_halves(qd_ref, rows(i), qs[i] * jnp.exp2(bs[i]))
        store_halves(kd_ref, rows(i), ks[i] * jnp.exp2(b_tot - bs[i]))
    for cc in range(nch):
        sl = pl.ds(cc * c, c)
        for half, v_half_ref in enumerate((v0_ref, v1_ref)):
            vc = v_half_ref[0, sl, :].astype(BF16)
            kc = kd_ref[half, sl, :].astype(BF16)
            u_ref[cc, half] = _dot_tn(vc, kc) * bd
    for i in range(c):
        ps = [qs[i] * ks[jj] * jnp.exp2(bs[i] - bs[jj]) for jj in range(i)] + [qs[i] * ks[i]]
        a = _dot(jnp.concatenate(ps, axis=0).astype(BF16), e)
        o = a[0:nch] * vs[0]
        for jj in range(1, i + 1):
            o = o + a[jj * nch:(jj + 1) * nch] * vs[jj]
        store_halves(od_ref, rows(i), o)

    for n in range(nch):
        cc = jnp.where(d == 0, n, nch - 1 - n)
        sl = pl.ds(pl.multiple_of(cc * c, c), c)
        for half in range(2):
            lanes = slice(half * LANES, (half + 1) * LANES)
            st = st_ref[half]
            o_inter = _dot_nt(qd_ref[half, sl, :].astype(BF16), st.astype(BF16))
            o_ref[0, 0, sl, lanes] = od_ref[half, sl, :] + o_inter
            st_ref[half] = st * dt_ref[half, pl.ds(cc, 1), :] + u_ref[cc, half]


def _hgrn_call(hgp, lbs, e, bd, nct):
    bsz, t, _ = hgp.shape
    nb = t // SCAN_BLK
    w = HG_W
    blk = lambda d, j: _scan_block(d, j, nct, nb)
    half = lambda col: pl.BlockSpec((1, SCAN_BLK, LANES), lambda b, d, j: (b, blk(d, j), col(d)))
    return pl.pallas_call(
        _hgrn_kernel,
        grid=(bsz, 2, nb),
        in_specs=[half(lambda d: 0), half(lambda d: 1),
                  half(lambda d: 2), half(lambda d: 3),
                  half(lambda d: 4 + 2 * d), half(lambda d: 5 + 2 * d),
                  pl.BlockSpec((1, 1, w), lambda b, d, j: (d, 0, 0)),
                  pl.BlockSpec((w, w), lambda b, d, j: (0, 0)),
                  pl.BlockSpec((LANES, LANES), lambda b, d, j: (0, 0))],
        out_specs=pl.BlockSpec((1, 1, SCAN_BLK, w), lambda b, d, j: (d, b, blk(d, j), 0)),
        out_shape=jax.ShapeDtypeStruct((2, bsz, t, w), F32),
        scratch_shapes=[pltpu.VMEM((2, LANES, LANES), F32),
                        pltpu.VMEM((2, SCAN_BLK, LANES), F32),
                        pltpu.VMEM((2, SCAN_BLK, LANES), F32),
                        pltpu.VMEM((2, SCAN_BLK, LANES), F32),
                        pltpu.VMEM((2, SCAN_BLK // HG_CHUNK, LANES), F32),
                        pltpu.VMEM((SCAN_BLK // HG_CHUNK, 2, LANES, LANES), F32)],
        compiler_params=_cparams(("parallel", "parallel", "arbitrary")),
        name="hgrn2_scan",
    )(hgp, hgp, hgp, hgp, hgp, hgp, lbs, e, bd)


def _mlstm_kernel(q_ref, k_ref, vt_ref, g_ref, gt_ref, m_ref, mt_ref, o_ref, c_ref, ms_ref):
    j = pl.program_id(2)

    @pl.when(j == 0)
    def _():
        c_ref[...] = jnp.zeros(c_ref.shape, F32)
        ms_ref[...] = jnp.zeros(ms_ref.shape, F32)

    length = SCAN_BLK
    mask = m_ref[0]
    mask_t = mt_ref[0]
    valid_t = mask_t > 0
    gates = g_ref[0, 0]
    gates_t = gt_ref[0, 0]
    b_col_all = _mask_dot(mask, _log_sigmoid(gates))
    lf_t = _log_sigmoid(gates_t)
    b_row_all = _dot_mask(lf_t, mask_t)
    tot_all = jnp.sum(lf_t, axis=1, keepdims=True)
    row = lax.broadcasted_iota(jnp.int32, (LANES, length), 0)

    heads = range(ML_HEADS)
    kh = [k_ref[0, :, hh * ML_DK:(hh + 1) * ML_DK] for hh in heads]
    qh = [q_ref[0, :, hh * ML_DK:(hh + 1) * ML_DK] for hh in heads]
    s_raw = [_dot_nt(kh[hh], qh[hh]) for hh in heads]
    inter = [_dot_nt(c_ref[hh].astype(BF16), qh[hh]) for hh in heads]
    for pair in range(ML_HEADS // 2):
        v2t = vt_ref[0, pair * LANES:(pair + 1) * LANES, :]
        out_t = jnp.zeros((LANES, length), F32)
        for sub in range(2):
            hh = pair * 2 + sub
            own = (row < 64) if sub == 0 else (row >= 64)
            ones_row = 64 if sub == 0 else 0
            v_aug_t = jnp.where(own, v2t, jnp.where(row == ones_row, 1.0, 0.0).astype(BF16))
            m_prev = ms_ref[hh, 0:1, 0:1]
            cs = gates[:, hh:hh + 1] - b_col_all[:, 4 + hh:5 + hh]
            br = b_row_all[4 + hh:5 + hh, :]
            tot = tot_all[4 + hh:5 + hh, :]
            log_d = jnp.where(valid_t, br + cs, NEG_BIG)
            m_t = jnp.maximum(br + m_prev, jnp.max(log_d, axis=0, keepdims=True))
            s_t = (s_raw[hh] * jnp.exp(log_d - m_t)).astype(BF16)
            r_t = jnp.exp(br + m_prev - m_t) * inter[hh] + _dot(v_aug_t, s_t)
            den = r_t[ones_row:ones_row + 1, :]
            out_t = jnp.where(own, r_t / jnp.maximum(jnp.abs(den), jnp.exp(-m_t)), out_t)
            log_e = tot + cs
            m_new = jnp.maximum(tot + m_prev, jnp.max(log_e, axis=0, keepdims=True))
            ke = (kh[hh].astype(F32) * jnp.exp(log_e - m_new)).astype(BF16)
            c_ref[hh] = jnp.exp(tot + m_prev - m_new) * c_ref[hh] + _dot(v_aug_t, ke)
            ms_ref[hh] = jnp.broadcast_to(m_new, ms_ref.shape[1:])
        o_ref[0, 0, :, pair * LANES:(pair + 1) * LANES] = out_t.T


def _mlstm_call(qk, vt, gcol, grow, mask, mask_t, nct):
    bsz, t, _ = qk.shape
    nb = t // SCAN_BLK
    w = ML_W
    blk = lambda d, j: _scan_block(d, j, nct, nb)
    return pl.pallas_call(
        _mlstm_kernel,
        grid=(bsz, 2, nb),
        in_specs=[pl.BlockSpec((1, SCAN_BLK, w), lambda b, d, j: (b, blk(d, j), 0)),
                  pl.BlockSpec((1, SCAN_BLK, w), lambda b, d, j: (b, blk(d, j), 1)),
                  pl.BlockSpec((1, w, SCAN_BLK), lambda b, d, j: (b, 0, blk(d, j))),
                  pl.BlockSpec((1, 1, SCAN_BLK, LANES), lambda b, d, j: (b, d, blk(d, j), 0)),
                  pl.BlockSpec((1, 1, 8, SCAN_BLK), lambda b, d, j: (b, d, 0, blk(d, j))),
                  pl.BlockSpec((1, SCAN_BLK, SCAN_BLK), lambda b, d, j: (d, 0, 0)),
                  pl.BlockSpec((1, SCAN_BLK, SCAN_BLK), lambda b, d, j: (d, 0, 0))],
        out_specs=pl.BlockSpec((1, 1, SCAN_BLK, w), lambda b, d, j: (d, b, blk(d, j), 0)),
        out_shape=jax.ShapeDtypeStruct((2, bsz, t, w), F32),
        scratch_shapes=[pltpu.VMEM((ML_HEADS, LANES, ML_DK), F32),
                        pltpu.VMEM((ML_HEADS, SUBLANES, LANES), F32)],
        compiler_params=_cparams(("parallel", "parallel", "arbitrary")),
        name="mlstm_scan",
    )(qk, qk, vt, gcol, grow, mask, mask_t)


def _out_ffn_kernel(hc_ref, hl_ref, mod_ref, at_ref, hf_ref, hb_ref, hgate_ref, gn_ref, e_ref, mf_ref, mb_ref, mgate_ref,
                    wa_ref, wr_ref, wm_ref, g2_ref, w1_ref, w2_ref, o_ref, *, nct, skip):
    mod = mod_ref[0, 0]
    o = hf_ref[0, 0] + hb_ref[0, 0]
    ss = _dot((o * o).astype(BF16), e_ref[...])
    gate = hgate_ref[0]
    r = o * lax.rsqrt(ss + EPS) * gn_ref[...] * (gate * _sigmoid(gate))
    m = _sigmoid(mgate_ref[0]) * (mf_ref[0, 0] + mb_ref[0, 0])
    y = _dot_tn(at_ref[0], wa_ref[...]) + _dot(r.astype(BF16), wr_ref[...]) + _dot(m.astype(BF16), wm_ref[...])
    h = jnp.where(pl.program_id(1) + skip < nct, hc_ref[0], hl_ref[0])
    h1 = h + mod[2:3] * y
    ms = jnp.mean(h1 * h1, axis=-1, keepdims=True)
    u = (h1 * lax.rsqrt(ms + EPS) * g2_ref[...] * (1.0 + mod[4:5]) + mod[3:4]).astype(BF16)
    a = jnp.maximum(_dot(u, w1_ref[...]), 0.0)
    a = (a * a).astype(BF16)
    o_ref[0] = h1 + mod[5:6] * _dot(a, w2_ref[...])


def _out_ffn_call(h_ctx, h_lat, lat_off, t, modsel, at, hg_o, hgp, gn, e, ml_o, mvo, wa, wr, wm, g2, w1, w2, nct, skip):
    bsz, _, d = h_ctx.shape
    nt = t // TM - skip
    full = lambda a: pl.BlockSpec(a.shape, lambda bi, i: (0,) * a.ndim)
    resident = lambda a: pl.BlockSpec(a.shape, lambda bi, i: (0,) * a.ndim, pipeline_mode=pl.Buffered(1))
    hg_col = (OFF_MQK - OFF_HG) // HG_W - 1
    ctx_spec, lat_spec = _stream_spec(nct, lat_off, skip)
    return pl.pallas_call(
        functools.partial(_out_ffn_kernel, nct=nct, skip=skip),
        grid=(bsz, nt),
        in_specs=[ctx_spec(h_ctx), lat_spec(h_lat),
                  pl.BlockSpec((1, 1, N_MOD, d), lambda bi, i: (bi, (i + skip >= nct).astype(jnp.int32), 0, 0)),
                  pl.BlockSpec((1, ATTN_W, TM), lambda bi, i: (bi, 0, i + skip)),
                  pl.BlockSpec((1, 1, TM, HG_W), lambda bi, i: (0, bi, i + skip, 0)),
                  pl.BlockSpec((1, 1, TM, HG_W), lambda bi, i: (1, bi, i + skip, 0)),
                  pl.BlockSpec((1, TM, HG_W), lambda bi, i: (bi, i + skip, hg_col)),
                  full(gn), full(e),
                  pl.BlockSpec((1, 1, TM, ML_W), lambda bi, i: (0, bi, i + skip, 0)),
                  pl.BlockSpec((1, 1, TM, ML_W), lambda bi, i: (1, bi, i + skip, 0)),
                  pl.BlockSpec((1, TM, ML_W), lambda bi, i: (bi, i + skip, 1)),
                  full(wa), full(wr), full(wm), full(g2), resident(w1), resident(w2)],
        out_specs=pl.BlockSpec((1, TM, d), lambda bi, i: (bi, i, 0)),
        out_shape=jax.ShapeDtypeStruct((bsz, nt * TM, d), F32),
        compiler_params=_cparams(("parallel", "parallel")),
        name="out_proj_ffn",
    )(h_ctx, h_lat, modsel, at, hg_o, hg_o, hgp, gn, e, ml_o, ml_o, mvo, wa, wr, wm, g2, w1, w2)


def _rope_tables(n_ctx, n_lat):
    inv_freq = ROPE_THETA ** (-np.arange(0, ROPE_AXIS_DIM, 2, dtype=np.float32) / ROPE_AXIS_DIM)
    inv_freq = jnp.asarray(inv_freq, F32)
    n_rows = n_lat // GRID_W
    ang_row = jnp.arange(n_rows, dtype=F32)[:, None] * inv_freq[None, :]
    ang_col = jnp.arange(GRID_W, dtype=F32)[:, None] * inv_freq[None, :]
    sign = np.where((np.arange(HEAD_DIM) % 32) < 16, -1.0, 1.0).astype(np.float32)

    def table(fn, ctx_value, signs):
        per_row = jnp.broadcast_to(fn(ang_row)[:, None, :], (n_rows, GRID_W, inv_freq.shape[0])).reshape(n_lat, -1)
        per_col = jnp.broadcast_to(fn(ang_col)[None, :, :], (n_rows, GRID_W, inv_freq.shape[0])).reshape(n_lat, -1)
        lat = jnp.concatenate([per_row, per_row, per_col, per_col], axis=1) * signs
        full = jnp.concatenate([jnp.full((n_ctx, HEAD_DIM), ctx_value, F32), lat], axis=0)
        return jnp.concatenate([full, full], axis=1)

    return table(jnp.cos, 1.0, 1.0), table(jnp.sin, 0.0, jnp.asarray(sign)[None, :])


def _block_ones(width, head, scale, dtype):
    idx = np.arange(width) // head
    return jnp.asarray((idx[:, None] == idx[None, :]).astype(np.float32) * scale, dtype)


def _scan_masks(n):
    tril = np.tril(np.ones((n, n), np.float32))
    return np.stack([tril, tril.T])


def kernel(x, c, ctx, c_ctx, w_mod, b_mod, norm_mix, norm_ffn, w_in, b_in, q_norm, k_norm,
           hg_lb, hg_norm, ml_conv, w_out, w_ff1, w_ff2):
    bsz, n_lat, d = x.shape
    n_ctx = ctx.shape[1]
    depth = w_mod.shape[0]
    t = n_ctx + n_lat
    nct = n_ctx // TM
    assert n_ctx % SCAN_BLK == 0 and n_lat % SCAN_BLK == 0 and n_lat % GRID_W == 0 and bsz + 1 <= 8

    cvec = jnp.concatenate([c, c_ctx[None, :], jnp.zeros((8 - bsz - 1, d), F32)], axis=0)
    mods = _mod_call(cvec, w_mod, b_mod).reshape(depth, 8, N_MOD, d)
    modsel = jnp.stack([jnp.broadcast_to(mods[:, bsz][:, None], (depth, bsz, N_MOD, d)), mods[:, :bsz]], axis=2)

    cos, sin = _rope_tables(n_ctx, n_lat)
    e_attn = _block_ones(ATTN_W, HEAD_DIM, 1.0 / HEAD_DIM, BF16)
    e_hg_mean = _block_ones(HG_W, HG_DK, 1.0 / HG_DK, BF16)
    e_hg = _block_ones(HG_W, HG_DK, 1.0, BF16)
    bd_hg = _block_ones(LANES, HG_DK, 1.0, F32)
    scan_mask = _scan_masks(SCAN_BLK)
    mask_ml = jnp.asarray(scan_mask, BF16)
    mask_ml_t = jnp.asarray(np.transpose(scan_mask, (0, 2, 1)).copy(), BF16)

    lbs = jnp.cumsum(jax.nn.softmax(hg_lb.astype(F32), axis=0), axis=0)
    lbs = (lbs - lbs[:1]).reshape(depth, 2, 1, HG_W)

    w_in_p = jnp.pad(w_in, ((0, 0), (0, 0), (0, N_IN_PAD - N_IN))).astype(BF16)
    b_in_p = jnp.pad(b_in, ((0, 0), (0, N_IN_PAD - N_IN))).reshape(depth, 1, N_IN_PAD)
    w_out_b = w_out.astype(BF16)
    w1_b = w_ff1.astype(BF16)
    w2_b = w_ff2.astype(BF16)
    gq = jnp.tile(q_norm, (1, ATTN_HEADS)).reshape(depth, 1, ATTN_W)
    gk = jnp.tile(k_norm, (1, ATTN_KV_HEADS)).reshape(depth, 1, KV_W)
    gn = jnp.tile(hg_norm, (1, HG_HEADS)).reshape(depth, 1, HG_W)

    h_ctx, h_lat, lat_off = ctx, x, 0
    for l in range(depth):
        last = l == depth - 1
        attn_consts, unscale, use_fixed_shift = _attn_constants(q_norm[l], k_norm[l])
        qa, ka, vt, hgp, mqk, mvo, mvt, gcol, grow = _in_call(
            h_ctx, h_lat, lat_off, t, modsel[l], norm_mix[l].reshape(1, d), w_in_p[l], b_in_p[l], cos, sin,
            gq[l], gk[l], e_attn, attn_consts, nct)

        at = _attention(qa, ka, vt, unscale, use_fixed_shift, n_ctx)

        hg_o = _hgrn_call(hgp, lbs[l], e_hg, bd_hg, nct)

        qk = _conv_call(mqk, ml_conv[l], n_ctx)
        ml_o = _mlstm_call(qk, mvt, gcol, grow, mask_ml, mask_ml_t, nct)

        skip = nct if last else 0
        h = _out_ffn_call(h_ctx, h_lat, lat_off, t, modsel[l], at, hg_o, hgp, gn[l], e_hg_mean, ml_o, mvo,
                          w_out_b[l, :ATTN_W], w_out_b[l, ATTN_W:ATTN_W + HG_W], w_out_b[l, ATTN_W + HG_W:],
                          norm_ffn[l].reshape(1, d), w1_b[l], w2_b[l], nct, skip)
        h_ctx, h_lat, lat_off = h, h, nct
    return h
```

```python
import functools

import jax
import jax.numpy as jnp
import numpy as np
from jax import lax
from jax.experimental import pallas as pl
from jax.experimental.pallas import tpu as pltpu

F32 = jnp.float32
BF16 = jnp.bfloat16
LANES = 128
SUBLANES = 8

GRID_W = 64
HEAD_DIM = 64
ATTN_HEADS = 8
ATTN_KV_HEADS = 2
ATTN_REP = ATTN_HEADS // ATTN_KV_HEADS
ROPE_THETA = 10000.0
ROPE_AXIS_DIM = HEAD_DIM // 2
HG_HEADS = 4
HG_DK = 64
ML_HEADS = 4
ML_DK = 64
N_MOD = 6
EPS = 1e-6
ATTN_W = ATTN_HEADS * HEAD_DIM
KV_W = ATTN_KV_HEADS * HEAD_DIM
HG_W = HG_HEADS * HG_DK
ML_W = ML_HEADS * ML_DK

OFF_AQ, OFF_AK, OFF_AV = 0, 512, 640
OFF_HG = 768
OFF_MQK = 2048
OFF_MVO = 2560
OFF_MG = 3072
N_IN = 3088
N_IN_PAD = 3200

TM = 256
TQ = 256
TK = 256
ATTN_UNROLL = 32
SCAN_BLK = 256
CONV_BLKS = (1280, 1024, 768, 512, 256)
HG_CHUNK = 16
V_ROWS = 80
FP8 = jnp.float8_e4m3fn
QK_ROWS = 256
SHIFT_ROWS = QK_ROWS - 3 * HEAD_DIM
FP8_TARGET = 128.0
MAX_FIXED_SHIFT = 32.0
SHIFT_FRACTION = 0.5
NEG_BIG = -1e30
Q_SCALE = float(np.log2(np.e)) * HEAD_DIM ** -0.5
VMEM_LIMIT = 56 * 1024 * 1024


def _cparams(sem):
    return pltpu.CompilerParams(dimension_semantics=sem, vmem_limit_bytes=VMEM_LIMIT)


def _split3(x):
    hi = x.astype(BF16)
    r1 = x - hi.astype(F32)
    mid = r1.astype(BF16)
    lo = (r1 - mid.astype(F32)).astype(BF16)
    return hi, mid, lo


def _dot(a, b):
    return jnp.dot(a, b, preferred_element_type=F32)


def _dot_nt(a, b):
    return lax.dot_general(a, b, (((1,), (1,)), ((), ())), preferred_element_type=F32)


def _dot_tn(a, b):
    return lax.dot_general(a, b, (((0,), (0,)), ((), ())), preferred_element_type=F32)


def _mask_dot(mask_bf16, x):
    hi, mid, lo = _split3(x)
    return _dot(mask_bf16, hi) + _dot(mask_bf16, mid) + _dot(mask_bf16, lo)


def _dot_mask(x, mask_bf16):
    hi, mid, lo = _split3(x)
    return _dot(hi, mask_bf16) + _dot(mid, mask_bf16) + _dot(lo, mask_bf16)


def _sigmoid(x):
    return 1.0 / (1.0 + jnp.exp(-x))


def _log_sigmoid(x):
    return jnp.minimum(x, 0.0) - jnp.log(1.0 + jnp.exp(-jnp.abs(x)))


def _mod_kernel(c_ref, w_ref, b_ref, o_ref):
    cv = c_ref[...]
    a = cv * _sigmoid(cv)
    o_ref[0] = jnp.dot(a, w_ref[0], preferred_element_type=F32, precision=lax.Precision.HIGHEST) + b_ref[0]


def _mod_call(cvec, w_mod, b_mod):
    depth, d, nm = w_mod.shape
    tn = 1024
    return pl.pallas_call(
        _mod_kernel,
        grid=(depth, nm // tn),
        in_specs=[pl.BlockSpec((8, d), lambda l, j: (0, 0)),
                  pl.BlockSpec((1, d, tn), lambda l, j: (l, 0, j)),
                  pl.BlockSpec((1, 1, tn), lambda l, j: (l, 0, j))],
        out_specs=pl.BlockSpec((1, 8, tn), lambda l, j: (l, 0, j)),
        out_shape=jax.ShapeDtypeStruct((depth, 8, nm), F32),
        compiler_params=_cparams(("arbitrary", "arbitrary")),
        name="mod_vectors",
    )(cvec, w_mod, b_mod.reshape(depth, 1, nm))


def _head_rms(t, gain, e):
    ss = _dot((t * t).astype(BF16), e)
    return t * lax.rsqrt(ss + EPS) * gain


def _rope(t, cos, sin):
    w = t.shape[1]
    reps = w // LANES
    cosw = jnp.concatenate([cos] * reps, axis=1) if reps > 1 else cos
    sinw = jnp.concatenate([sin] * reps, axis=1) if reps > 1 else sin
    lane = lax.broadcasted_iota(jnp.int32, t.shape, 1)
    first_half = (lane % 32) < 16
    partner = jnp.where(first_half, pltpu.roll(t, w - 16, 1), pltpu.roll(t, 16, 1))
    return t * cosw + partner * sinw


def _fp8_operand(d, first_half, fill, rem_first):
    rem = d - d.astype(FP8).astype(F32)
    mixed = jnp.where(first_half, rem, d) if rem_first else jnp.where(first_half, d, rem)
    filled = jnp.where(first_half, d, fill)
    return jnp.concatenate([filled.astype(FP8), mixed.astype(FP8)], axis=1)


def _stream_spec(nct, lat_off, skip=0):
    d_last = lambda a: a.shape[-1]
    ctx_spec = lambda a: pl.BlockSpec((1, TM, d_last(a)), lambda bi, i: (bi, jnp.minimum(i + skip, nct - 1), 0))
    lat_spec = lambda a: pl.BlockSpec((1, TM, d_last(a)), lambda bi, i: (bi, jnp.maximum(i + skip - nct, 0) + lat_off, 0))
    return ctx_spec, lat_spec


def _in_kernel(hc_ref, hl_ref, mod_ref, g_ref, w_ref, b_ref, cos_ref, sin_ref, gq_ref, gk_ref, e_ref, ac_ref,
               qa_ref, ka_ref, vt_ref, hg_ref, mqk_ref, mvo_ref, mvt_ref, gcol_ref, grow_ref, *, nct):
    x = jnp.where(pl.program_id(1) < nct, hc_ref[0], hl_ref[0])
    ms = jnp.mean(x * x, axis=-1, keepdims=True)
    y = x * lax.rsqrt(ms + EPS) * g_ref[...]
    mod = mod_ref[0, 0]
    u = y * (1.0 + mod[1:2]) + mod[0:1]
    p = _dot(u.astype(BF16), w_ref[...]) + b_ref[...]
    cos = cos_ref[...]
    sin = sin_ref[...]
    e = e_ref[...]
    q = _rope(_head_rms(p[:, OFF_AQ:OFF_AQ + ATTN_W], gq_ref[...], e), cos, sin)
    k = _rope(_head_rms(p[:, OFF_AK:OFF_AK + KV_W], gk_ref[...], e[:KV_W, :KV_W]), cos, sin)

    consts = ac_ref[...]
    first_half = lax.broadcasted_iota(jnp.int32, (x.shape[0], LANES), 1) < HEAD_DIM
    qs = q * consts[:, 0:1]
    neg_r = -consts[:, 2:3]
    for g in range(ATTN_W // LANES):
        grp = qs[:, g * LANES:(g + 1) * LANES]
        swp = pltpu.roll(grp, HEAD_DIM, 1)
        qa_ref[0, 2 * g] = _fp8_operand(jnp.where(first_half, grp, swp), first_half, neg_r, False)
        qa_ref[0, 2 * g + 1] = _fp8_operand(jnp.where(first_half, swp, grp), first_half, neg_r, False)
    ks = k * consts[:, 1:2]
    ksw = pltpu.roll(ks, HEAD_DIM, 1)
    ka_ref[0, 0] = _fp8_operand(jnp.where(first_half, ks, ksw), first_half, 1.0, True)
    ka_ref[0, 1] = _fp8_operand(jnp.where(first_half, ksw, ks), first_half, 1.0, True)
    v_t = p[:, OFF_AV:OFF_AV + KV_W].T.astype(BF16)
    for hh in range(ATTN_KV_HEADS):
        vt_ref[0, hh, :HEAD_DIM, :] = v_t[hh * HEAD_DIM:(hh + 1) * HEAD_DIM]
        pad_row = lax.broadcasted_iota(jnp.int32, (V_ROWS - HEAD_DIM, x.shape[0]), 0)
        vt_ref[0, hh, HEAD_DIM:, :] = jnp.where(pad_row == 0, 1.0, 0.0).astype(BF16)
    hg_ref[0] = p[:, OFF_HG:OFF_MQK]
    mqk_ref[0] = p[:, OFF_MQK:OFF_MVO]
    mvo_ref[0] = p[:, OFF_MVO:OFF_MG]
    mvt_ref[0] = p[:, OFF_MVO:OFF_MVO + ML_W].T.astype(BF16)
    gates = p[:, OFF_MG:OFF_MG + LANES]
    lane = lax.broadcasted_iota(jnp.int32, gates.shape, 1)
    by_heads = pltpu.roll(gates, LANES - ML_HEADS, 1)
    by_2heads = pltpu.roll(gates, LANES - 2 * ML_HEADS, 1)
    for direction, (igate, fgate) in enumerate(((gates, by_heads), (by_heads, by_2heads))):
        g_dir = jnp.where(lane < ML_HEADS, igate, jnp.where(lane < 2 * ML_HEADS, fgate, 0.0))
        gcol_ref[0, direction] = g_dir
        grow_ref[0, direction] = g_dir.T[:2 * ML_HEADS]


def _in_call(h_ctx, h_lat, lat_off, t, modsel, g, w, b, cos, sin, gq, gk, e, attn_consts, nct):
    bsz, _, d = h_ctx.shape
    nt = t // TM
    row = lambda width: pl.BlockSpec((1, TM, width), lambda bi, i: (bi, i, 0))
    full = lambda a: pl.BlockSpec(a.shape, lambda bi, i: (0,) * a.ndim)
    out_widths = (OFF_MQK - OFF_HG, 2 * ML_W, 2 * ML_W)
    ctx_spec, lat_spec = _stream_spec(nct, lat_off)
    return pl.pallas_call(
        functools.partial(_in_kernel, nct=nct),
        grid=(bsz, nt),
        in_specs=[ctx_spec(h_ctx), lat_spec(h_lat),
                  pl.BlockSpec((1, 1, N_MOD, d), lambda bi, i: (bi, (i >= nct).astype(jnp.int32), 0, 0)),
                  full(g), full(w), full(b),
                  pl.BlockSpec((TM, LANES), lambda bi, i: (i, 0)),
                  pl.BlockSpec((TM, LANES), lambda bi, i: (i, 0)),
                  full(gq), full(gk), full(e), full(attn_consts)],
        out_specs=[pl.BlockSpec((1, ATTN_HEADS, TM, QK_ROWS), lambda bi, i: (bi, 0, i, 0)),
                   pl.BlockSpec((1, ATTN_KV_HEADS, TM, QK_ROWS), lambda bi, i: (bi, 0, i, 0)),
                   pl.BlockSpec((1, ATTN_KV_HEADS, V_ROWS, TM), lambda bi, i: (bi, 0, 0, i))]
        + [row(wd) for wd in out_widths]
        + [pl.BlockSpec((1, ML_W, TM), lambda bi, i: (bi, 0, i)),
           pl.BlockSpec((1, 2, TM, LANES), lambda bi, i: (bi, 0, i, 0)),
           pl.BlockSpec((1, 2, 2 * ML_HEADS, TM), lambda bi, i: (bi, 0, 0, i))],
        out_shape=[jax.ShapeDtypeStruct((bsz, ATTN_HEADS, t, QK_ROWS), FP8),
                   jax.ShapeDtypeStruct((bsz, ATTN_KV_HEADS, t, QK_ROWS), FP8),
                   jax.ShapeDtypeStruct((bsz, ATTN_KV_HEADS, V_ROWS, t), BF16)]
        + [jax.ShapeDtypeStruct((bsz, t, wd), F32) for wd in out_widths]
        + [jax.ShapeDtypeStruct((bsz, ML_W, t), BF16),
           jax.ShapeDtypeStruct((bsz, 2, t, LANES), F32),
           jax.ShapeDtypeStruct((bsz, 2, 2 * ML_HEADS, t), F32)],
        compiler_params=_cparams(("parallel", "parallel")),
        name="norm_mod_in_proj",
    )(h_ctx, h_lat, modsel, g, w, b, cos, sin, gq, gk, e, attn_consts)


def _attn_kernel(qa_ref, ka_ref, vt_ref, unscale_ref, o_ref, acc_ref, s_ref, *maybe_m_ref,
                 online, n_ctx_q, n_ctx_k, n_k):
    qi = pl.program_id(2)
    unscale = unscale_ref[...]
    acc_ref[...] = jnp.zeros(acc_ref.shape, F32)
    if online:
        m_ref, = maybe_m_ref
        m_ref[...] = jnp.full(m_ref.shape, NEG_BIG, F32)
    n_blocks = jnp.where(qi < n_ctx_q, n_ctx_k, n_k)

    def key_tile(kb):
        return ka_ref[0, 0, pl.ds(pl.multiple_of(kb * TK, TK), TK), :]

    def produce(kt, slot, hh):
        s = _dot_nt(kt, qa_ref[0, hh]) * unscale
        s_ref[slot, hh] = s if online else jnp.exp2(s.astype(BF16))

    def step(kb, cur, nxt):
        vt = vt_ref[0, 0, :, pl.ds(pl.multiple_of(kb * TK, TK), TK)]
        kt = key_tile(jnp.minimum(kb + 1, n_blocks - 1)) if nxt is not None else None
        for hh in range(ATTN_REP):
            if nxt is not None:
                produce(kt, nxt, hh)
            if online:
                s = s_ref[cur, hh]
                m_old = m_ref[hh]
                m_new = jnp.maximum(m_old, jnp.max(s, axis=0, keepdims=True))
                p = jnp.exp2(s - m_new).astype(BF16)
                acc_ref[hh] = jnp.exp2(m_old - m_new) * acc_ref[hh] + _dot(vt, p)
                m_ref[hh] = m_new
            else:
                acc_ref[hh] += _dot(vt, s_ref[cur, hh])

    kt0 = key_tile(0)
    for hh in range(ATTN_REP):
        produce(kt0, 0, hh)

    def group(i, carry):
        for u in range(ATTN_UNROLL):
            step(ATTN_UNROLL * i + u, u % 2, (u + 1) % 2)
        return carry

    def pair(i, carry):
        step(2 * i, 0, 1)
        step(2 * i + 1, 1, 0)
        return carry

    n_groups = n_blocks // ATTN_UNROLL
    lax.fori_loop(0, n_groups, group, 0)
    lax.fori_loop(n_groups * (ATTN_UNROLL // 2), n_blocks // 2, pair, 0)

    @pl.when(n_blocks % 2 == 1)
    def _():
        step(n_blocks - 1, 0, None)

    for hh in range(ATTN_REP):
        acc = acc_ref[hh]
        o = acc[:HEAD_DIM] / acc[HEAD_DIM:HEAD_DIM + 1]
        o_ref[0, hh * HEAD_DIM:(hh + 1) * HEAD_DIM, :] = o.astype(o_ref.dtype)


def _attn_call(qa, ka, vt, unscale, n_ctx, online):
    bsz, _, t, _ = qa.shape
    kern = functools.partial(_attn_kernel, online=online, n_ctx_q=n_ctx // TQ, n_ctx_k=n_ctx // TK, n_k=t // TK)
    gw = ATTN_REP * HEAD_DIM
    scratch = [pltpu.VMEM((ATTN_REP, V_ROWS, TQ), F32),
               pltpu.VMEM((2, ATTN_REP, TK, TQ), F32 if online else BF16)]
    if online:
        scratch.append(pltpu.VMEM((ATTN_REP, 1, TQ), F32))
    return pl.pallas_call(
        kern,
        grid=(bsz, ATTN_KV_HEADS, t // TQ),
        in_specs=[pl.BlockSpec((1, ATTN_REP, TQ, QK_ROWS), lambda b, g, i: (b, g, i, 0)),
                  pl.BlockSpec((1, 1, t, QK_ROWS), lambda b, g, i: (b, g, 0, 0)),
                  pl.BlockSpec((1, 1, V_ROWS, t), lambda b, g, i: (b, g, 0, 0)),
                  pl.BlockSpec((1, 1), lambda b, g, i: (0, 0))],
        out_specs=pl.BlockSpec((1, gw, TQ), lambda b, g, i: (b, g, i)),
        out_shape=jax.ShapeDtypeStruct((bsz, ATTN_W, t), BF16),
        scratch_shapes=scratch,
        compiler_params=_cparams(("parallel", "parallel", "arbitrary")),
        name="attention_online" if online else "attention",
    )(qa, ka, vt, unscale)


def _attn_constants(q_gain, k_gain):
    qmax = jnp.maximum(jnp.sqrt(float(HEAD_DIM)) * Q_SCALE * jnp.max(jnp.abs(q_gain)), 1e-6)
    kmax = jnp.maximum(jnp.sqrt(float(HEAD_DIM)) * jnp.max(jnp.abs(k_gain)), 1e-6)
    shift = qmax * kmax
    q_scale = jnp.exp2(jnp.floor(jnp.log2(FP8_TARGET / qmax)))
    k_scale = jnp.exp2(jnp.floor(jnp.log2(FP8_TARGET / kmax)))
    r = (jnp.minimum(shift, MAX_FIXED_SHIFT) * q_scale * k_scale * (SHIFT_FRACTION / SHIFT_ROWS)).astype(FP8).astype(F32)
    in_proj_consts = jnp.stack([Q_SCALE * q_scale, k_scale, r])
    in_proj_consts = jnp.pad(in_proj_consts, (0, LANES - in_proj_consts.shape[0])).reshape(1, LANES)
    unscale = (1.0 / (q_scale * k_scale)).reshape(1, 1)
    return in_proj_consts, unscale, shift <= MAX_FIXED_SHIFT


def _attention(qa, ka, vt, unscale, use_fixed_shift, n_ctx):
    return lax.cond(use_fixed_shift,
                    lambda: _attn_call(qa, ka, vt, unscale, n_ctx, False),
                    lambda: _attn_call(qa, ka, vt, unscale, n_ctx, True))


def _conv_kernel(x_ref, prev_ref, next_ref, w_ref, o_ref, *, n_ctx, t):
    x = x_ref[0]
    rows = x.shape[0]
    ridx = lax.broadcasted_iota(jnp.int32, x.shape, 0)
    tok = ridx + pl.program_id(1) * rows
    xm1 = jnp.where(ridx == 0, prev_ref[0, 7:8, :], pltpu.roll(x, 1, 0))
    xp1 = jnp.where(ridx == rows - 1, next_ref[0, 0:1, :], pltpu.roll(x, rows - 1, 0))
    xm1 = jnp.where(jnp.logical_or(tok == 0, tok == n_ctx), 0.0, xm1)
    xp1 = jnp.where(jnp.logical_or(tok == n_ctx - 1, tok == t - 1), 0.0, xp1)
    w = w_ref[...]
    y = xm1 * w[0:1] + x * w[1:2] + xp1 * w[2:3]
    y = y * _sigmoid(y)
    lane = lax.broadcasted_iota(jnp.int32, x.shape, 1)
    y = jnp.where(lane >= ML_W, y * (ML_DK ** -0.5), y)
    o_ref[0] = y.astype(o_ref.dtype)


def _conv_call(mqk, w, n_ctx):
    bsz, t, width = mqk.shape
    blk = next(b for b in CONV_BLKS if t % b == 0)
    nb = t // blk
    per8 = blk // 8
    last8 = t // 8 - 1
    kern = functools.partial(_conv_kernel, n_ctx=n_ctx, t=t)
    return pl.pallas_call(
        kern,
        grid=(bsz, nb),
        in_specs=[pl.BlockSpec((1, blk, width), lambda b, i: (b, i, 0)),
                  pl.BlockSpec((1, 8, width), lambda b, i: (b, jnp.maximum(i * per8 - 1, 0), 0)),
                  pl.BlockSpec((1, 8, width), lambda b, i: (b, jnp.minimum((i + 1) * per8, last8), 0)),
                  pl.BlockSpec(w.shape, lambda b, i: (0, 0))],
        out_specs=pl.BlockSpec((1, blk, width), lambda b, i: (b, i, 0)),
        out_shape=jax.ShapeDtypeStruct((bsz, t, width), BF16),
        compiler_params=_cparams(("parallel", "parallel")),
        name="mlstm_qk_conv",
    )(mqk, mqk, mqk, w)


def _scan_block(d, j, nct, nb):
    bwd = jnp.where(j < nct, nct - 1 - j, nb - 1 - (j - nct))
    return jnp.where(d == 0, j, bwd)


def _hgrn_kernel(q0_ref, q1_ref, v0_ref, v1_ref, z0_ref, z1_ref, lb_ref, e_ref, bd_ref, o_ref,
                 st_ref, qd_ref, kd_ref, od_ref, dt_ref, u_ref):
    d = pl.program_id(1)
    j = pl.program_id(2)

    @pl.when(j == 0)
    def _():
        st_ref[...] = jnp.zeros(st_ref.shape, F32)

    lb = lb_ref[0]
    e = e_ref[...]
    bd = bd_ref[...]
    c = HG_CHUNK
    nch = SCAN_BLK // c

    def rows(i):
        return pl.ds(jnp.where(d == 0, i, c - 1 - i), nch, stride=c)

    def load2(r0, r1, sl):
        return jnp.concatenate([r0[0, sl, :], r1[0, sl, :]], axis=1)

    def store_halves(ref, sl, x):
        ref[0, sl, :] = x[:, :LANES]
        ref[1, sl, :] = x[:, LANES:]

    qs, ks, vs, bs = [], [], [], []
    b = None
    for i in range(c):
        f = lb + (1.0 - lb) * _sigmoid(load2(z0_ref, z1_ref, rows(i)))
        lf = jnp.log2(f)
        b = lf if b is None else b + lf
        qs.append(load2(q0_ref, q1_ref, rows(i)))
        vs.append(load2(v0_ref, v1_ref, rows(i)))
        ks.append(1.0 - f)
        bs.append(b)
    b_tot = bs[-1]
    chunk_decay = jnp.exp2(b_tot)
    dt_ref[0] = chunk_decay[:, :LANES]
    dt_ref[1] = chunk_decay[:, LANES:]
    for i in range(c):
        store_halves(qd_ref, rows(i), qs[i] * jnp.exp2(bs[i]))
        store_halves(kd_ref, rows(i), ks[i] * jnp.exp2(b_tot - bs[i]))
    for cc in range(nch):
        sl = pl.ds(cc * c, c)
        for half, v_half_ref in enumerate((v0_ref, v1_ref)):
            vc = v_half_ref[0, sl, :].astype(BF16)
            kc = kd_ref[half, sl, :].astype(BF16)
            u_ref[cc, half] = _dot_tn(vc, kc) * bd
    for i in range(c):
        ps = [qs[i] * ks[jj] * jnp.exp2(bs[i] - bs[jj]) for jj in range(i)] + [qs[i] * ks[i]]
        a = _dot(jnp.concatenate(ps, axis=0).astype(BF16), e)
        o = a[0:nch] * vs[0]
        for jj in range(1, i + 1):
            o = o + a[jj * nch:(jj + 1) * nch] * vs[jj]
        store_halves(od_ref, rows(i), o)

    for n in range(nch):
        cc = jnp.where(d == 0, n, nch - 1 - n)
        sl = pl.ds(pl.multiple_of(cc * c, c), c)
        for half in range(2):
            lanes = slice(half * LANES, (half + 1) * LANES)
            st = st_ref[half]
            o_inter = _dot_nt(qd_ref[half, sl, :].astype(BF16), st.astype(BF16))
            o_ref[0, 0, sl, lanes] = od_ref[half, sl, :] + o_inter
            st_ref[half] = st * dt_ref[half, pl.ds(cc, 1), :] + u_ref[cc, half]


def _hgrn_call(hgp, lbs, e, bd, nct):
    bsz, t, _ = hgp.shape
    nb = t // SCAN_BLK
    w = HG_W
    blk = lambda d, j: _scan_block(d, j, nct, nb)
    half = lambda col: pl.BlockSpec((1, SCAN_BLK, LANES), lambda b, d, j: (b, blk(d, j), col(d)))
    return pl.pallas_call(
        _hgrn_kernel,
        grid=(bsz, 2, nb),
        in_specs=[half(lambda d: 0), half(lambda d: 1),
                  half(lambda d: 2), half(lambda d: 3),
                  half(lambda d: 4 + 2 * d), half(lambda d: 5 + 2 * d),
                  pl.BlockSpec((1, 1, w), lambda b, d, j: (d, 0, 0)),
                  pl.BlockSpec((w, w), lambda b, d, j: (0, 0)),
                  pl.BlockSpec((LANES, LANES), lambda b, d, j: (0, 0))],
        out_specs=pl.BlockSpec((1, 1, SCAN_BLK, w), lambda b, d, j: (d, b, blk(d, j), 0)),
        out_shape=jax.ShapeDtypeStruct((2, bsz, t, w), F32),
        scratch_shapes=[pltpu.VMEM((2, LANES, LANES), F32),
                        pltpu.VMEM((2, SCAN_BLK, LANES), F32),
                        pltpu.VMEM((2, SCAN_BLK, LANES), F32),
                        pltpu.VMEM((2, SCAN_BLK, LANES), F32),
                        pltpu.VMEM((2, SCAN_BLK // HG_CHUNK, LANES), F32),
                        pltpu.VMEM((SCAN_BLK // HG_CHUNK, 2, LANES, LANES), F32)],
        compiler_params=_cparams(("parallel", "parallel", "arbitrary")),
        name="hgrn2_scan",
    )(hgp, hgp, hgp, hgp, hgp, hgp, lbs, e, bd)


def _mlstm_kernel(q_ref, k_ref, vt_ref, g_ref, gt_ref, m_ref, mt_ref, o_ref, c_ref, ms_ref):
    j = pl.program_id(2)

    @pl.when(j == 0)
    def _():
        c_ref[...] = jnp.zeros(c_ref.shape, F32)
        ms_ref[...] = jnp.zeros(ms_ref.shape, F32)

    length = SCAN_BLK
    mask = m_ref[0]
    mask_t = mt_ref[0]
    valid_t = mask_t > 0
    gates = g_ref[0, 0]
    gates_t = gt_ref[0, 0]
    b_col_all = _mask_dot(mask, _log_sigmoid(gates))
    lf_t = _log_sigmoid(gates_t)
    b_row_all = _dot_mask(lf_t, mask_t)
    tot_all = jnp.sum(lf_t, axis=1, keepdims=True)
    row = lax.broadcasted_iota(jnp.int32, (LANES, length), 0)

    heads = range(ML_HEADS)
    kh = [k_ref[0, :, hh * ML_DK:(hh + 1) * ML_DK] for hh in heads]
    qh = [q_ref[0, :, hh * ML_DK:(hh + 1) * ML_DK] for hh in heads]
    s_raw = [_dot_nt(kh[hh], qh[hh]) for hh in heads]
    inter = [_dot_nt(c_ref[hh].astype(BF16), qh[hh]) for hh in heads]
    for pair in range(ML_HEADS // 2):
        v2t = vt_ref[0, pair * LANES:(pair + 1) * LANES, :]
        out_t = jnp.zeros((LANES, length), F32)
        for sub in range(2):
            hh = pair * 2 + sub
            own = (row < 64) if sub == 0 else (row >= 64)
            ones_row = 64 if sub == 0 else 0
            v_aug_t = jnp.where(own, v2t, jnp.where(row == ones_row, 1.0, 0.0).astype(BF16))
            m_prev = ms_ref[hh, 0:1, 0:1]
            cs = gates[:, hh:hh + 1] - b_col_all[:, 4 + hh:5 + hh]
            br = b_row_all[4 + hh:5 + hh, :]
            tot = tot_all[4 + hh:5 + hh, :]
            log_d = jnp.where(valid_t, br + cs, NEG_BIG)
            m_t = jnp.maximum(br + m_prev, jnp.max(log_d, axis=0, keepdims=True))
            s_t = (s_raw[hh] * jnp.exp(log_d - m_t)).astype(BF16)
            r_t = jnp.exp(br + m_prev - m_t) * inter[hh] + _dot(v_aug_t, s_t)
            den = r_t[ones_row:ones_row + 1, :]
            out_t = jnp.where(own, r_t / jnp.maximum(jnp.abs(den), jnp.exp(-m_t)), out_t)
            log_e = tot + cs
            m_new = jnp.maximum(tot + m_prev, jnp.max(log_e, axis=0, keepdims=True))
            ke = (kh[hh].astype(F32) * jnp.exp(log_e - m_new)).astype(BF16)
            c_ref[hh] = jnp.exp(tot + m_prev - m_new) * c_ref[hh] + _dot(v_aug_t, ke)
            ms_ref[hh] = jnp.broadcast_to(m_new, ms_ref.shape[1:])
        o_ref[0, 0, :, pair * LANES:(pair + 1) * LANES] = out_t.T


def _mlstm_call(qk, vt, gcol, grow, mask, mask_t, nct):
    bsz, t, _ = qk.shape
    nb = t // SCAN_BLK
    w = ML_W
    blk = lambda d, j: _scan_block(d, j, nct, nb)
    return pl.pallas_call(
        _mlstm_kernel,
        grid=(bsz, 2, nb),
        in_specs=[pl.BlockSpec((1, SCAN_BLK, w), lambda b, d, j: (b, blk(d, j), 0)),
                  pl.BlockSpec((1, SCAN_BLK, w), lambda b, d, j: (b, blk(d, j), 1)),
                  pl.BlockSpec((1, w, SCAN_BLK), lambda b, d, j: (b, 0, blk(d, j))),
                  pl.BlockSpec((1, 1, SCAN_BLK, LANES), lambda b, d, j: (b, d, blk(d, j), 0)),
                  pl.BlockSpec((1, 1, 8, SCAN_BLK), lambda b, d, j: (b, d, 0, blk(d, j))),
                  pl.BlockSpec((1, SCAN_BLK, SCAN_BLK), lambda b, d, j: (d, 0, 0)),
                  pl.BlockSpec((1, SCAN_BLK, SCAN_BLK), lambda b, d, j: (d, 0, 0))],
        out_specs=pl.BlockSpec((1, 1, SCAN_BLK, w), lambda b, d, j: (d, b, blk(d, j), 0)),
        out_shape=jax.ShapeDtypeStruct((2, bsz, t, w), F32),
        scratch_shapes=[pltpu.VMEM((ML_HEADS, LANES, ML_DK), F32),
                        pltpu.VMEM((ML_HEADS, SUBLANES, LANES), F32)],
        compiler_params=_cparams(("parallel", "parallel", "arbitrary")),
        name="mlstm_scan",
    )(qk, qk, vt, gcol, grow, mask, mask_t)


def _out_ffn_kernel(hc_ref, hl_ref, mod_ref, at_ref, hf_ref, hb_ref, hgate_ref, gn_ref, e_ref, mf_ref, mb_ref, mgate_ref,
                    wa_ref, wr_ref, wm_ref, g2_ref, w1_ref, w2_ref, o_ref, *, nct, skip):
    mod = mod_ref[0, 0]
    o = hf_ref[0, 0] + hb_ref[0, 0]
    ss = _dot((o * o).astype(BF16), e_ref[...])
    gate = hgate_ref[0]
    r = o * lax.rsqrt(ss + EPS) * gn_ref[...] * (gate * _sigmoid(gate))
    m = _sigmoid(mgate_ref[0]) * (mf_ref[0, 0] + mb_ref[0, 0])
    y = _dot_tn(at_ref[0], wa_ref[...]) + _dot(r.astype(BF16), wr_ref[...]) + _dot(m.astype(BF16), wm_ref[...])
    h = jnp.where(pl.program_id(1) + skip < nct, hc_ref[0], hl_ref[0])
    h1 = h + mod[2:3] * y
    ms = jnp.mean(h1 * h1, axis=-1, keepdims=True)
    u = (h1 * lax.rsqrt(ms + EPS) * g2_ref[...] * (1.0 + mod[4:5]) + mod[3:4]).astype(BF16)
    a = jnp.maximum(_dot(u, w1_ref[...]), 0.0)
    a = (a * a).astype(BF16)
    o_ref[0] = h1 + mod[5:6] * _dot(a, w2_ref[...])


def _out_ffn_call(h_ctx, h_lat, lat_off, t, modsel, at, hg_o, hgp, gn, e, ml_o, mvo, wa, wr, wm, g2, w1, w2, nct, skip):
    bsz, _, d = h_ctx.shape
    nt = t // TM - skip
    full = lambda a: pl.BlockSpec(a.shape, lambda bi, i: (0,) * a.ndim)
    resident = lambda a: pl.BlockSpec(a.shape, lambda bi, i: (0,) * a.ndim, pipeline_mode=pl.Buffered(1))
    hg_col = (OFF_MQK - OFF_HG) // HG_W - 1
    ctx_spec, lat_spec = _stream_spec(nct, lat_off, skip)
    return pl.pallas_call(
        functools.partial(_out_ffn_kernel, nct=nct, skip=skip),
        grid=(bsz, nt),
        in_specs=[ctx_spec(h_ctx), lat_spec(h_lat),
                  pl.BlockSpec((1, 1, N_MOD, d), lambda bi, i: (bi, (i + skip >= nct).astype(jnp.int32), 0, 0)),
                  pl.BlockSpec((1, ATTN_W, TM), lambda bi, i: (bi, 0, i + skip)),
                  pl.BlockSpec((1, 1, TM, HG_W), lambda bi, i: (0, bi, i + skip, 0)),
                  pl.BlockSpec((1, 1, TM, HG_W), lambda bi, i: (1, bi, i + skip, 0)),
                  pl.BlockSpec((1, TM, HG_W), lambda bi, i: (bi, i + skip, hg_col)),
                  full(gn), full(e),
                  pl.BlockSpec((1, 1, TM, ML_W), lambda bi, i: (0, bi, i + skip, 0)),
                  pl.BlockSpec((1, 1, TM, ML_W), lambda bi, i: (1, bi, i + skip, 0)),
                  pl.BlockSpec((1, TM, ML_W), lambda bi, i: (bi, i + skip, 1)),
                  full(wa), full(wr), full(wm), full(g2), resident(w1), resident(w2)],
        out_specs=pl.BlockSpec((1, TM, d), lambda bi, i: (bi, i, 0)),
        out_shape=jax.ShapeDtypeStruct((bsz, nt * TM, d), F32),
        compiler_params=_cparams(("parallel", "parallel")),
        name="out_proj_ffn",
    )(h_ctx, h_lat, modsel, at, hg_o, hg_o, hgp, gn, e, ml_o, ml_o, mvo, wa, wr, wm, g2, w1, w2)


def _rope_tables(n_ctx, n_lat):
    inv_freq = ROPE_THETA ** (-np.arange(0, ROPE_AXIS_DIM, 2, dtype=np.float32) / ROPE_AXIS_DIM)
    inv_freq = jnp.asarray(inv_freq, F32)
    n_rows = n_lat // GRID_W
    ang_row = jnp.arange(n_rows, dtype=F32)[:, None] * inv_freq[None, :]
    ang_col = jnp.arange(GRID_W, dtype=F32)[:, None] * inv_freq[None, :]
    sign = np.where((np.arange(HEAD_DIM) % 32) < 16, -1.0, 1.0).astype(np.float32)

    def table(fn, ctx_value, signs):
        per_row = jnp.broadcast_to(fn(ang_row)[:, None, :], (n_rows, GRID_W, inv_freq.shape[0])).reshape(n_lat, -1)
        per_col = jnp.broadcast_to(fn(ang_col)[None, :, :], (n_rows, GRID_W, inv_freq.shape[0])).reshape(n_lat, -1)
        lat = jnp.concatenate([per_row, per_row, per_col, per_col], axis=1) * signs
        full = jnp.concatenate([jnp.full((n_ctx, HEAD_DIM), ctx_value, F32), lat], axis=0)
        return jnp.concatenate([full, full], axis=1)

    return table(jnp.cos, 1.0, 1.0), table(jnp.sin, 0.0, jnp.asarray(sign)[None, :])


def _block_ones(width, head, scale, dtype):
    idx = np.arange(width) // head
    return jnp.asarray((idx[:, None] == idx[None, :]).astype(np.float32) * scale, dtype)


def _scan_masks(n):
    tril = np.tril(np.ones((n, n), np.float32))
    return np.stack([tril, tril.T])


def kernel(x, c, ctx, c_ctx, w_mod, b_mod, norm_mix, norm_ffn, w_in, b_in, q_norm, k_norm,
           hg_lb, hg_norm, ml_conv, w_out, w_ff1, w_ff2):
    bsz, n_lat, d = x.shape
    n_ctx = ctx.shape[1]
    depth = w_mod.shape[0]
    t = n_ctx + n_lat
    nct = n_ctx // TM
    assert n_ctx % SCAN_BLK == 0 and n_lat % SCAN_BLK == 0 and n_lat % GRID_W == 0 and bsz + 1 <= 8

    cvec = jnp.concatenate([c, c_ctx[None, :], jnp.zeros((8 - bsz - 1, d), F32)], axis=0)
    mods = _mod_call(cvec, w_mod, b_mod).reshape(depth, 8, N_MOD, d)
    modsel = jnp.stack([jnp.broadcast_to(mods[:, bsz][:, None], (depth, bsz, N_MOD, d)), mods[:, :bsz]], axis=2)

    cos, sin = _rope_tables(n_ctx, n_lat)
    e_attn = _block_ones(ATTN_W, HEAD_DIM, 1.0 / HEAD_DIM, BF16)
    e_hg_mean = _block_ones(HG_W, HG_DK, 1.0 / HG_DK, BF16)
    e_hg = _block_ones(HG_W, HG_DK, 1.0, BF16)
    bd_hg = _block_ones(LANES, HG_DK, 1.0, F32)
    scan_mask = _scan_masks(SCAN_BLK)
    mask_ml = jnp.asarray(scan_mask, BF16)
    mask_ml_t = jnp.asarray(np.transpose(scan_mask, (0, 2, 1)).copy(), BF16)

    lbs = jnp.cumsum(jax.nn.softmax(hg_lb.astype(F32), axis=0), axis=0)
    lbs = (lbs - lbs[:1]).reshape(depth, 2, 1, HG_W)

    w_in_p = jnp.pad(w_in, ((0, 0), (0, 0), (0, N_IN_PAD - N_IN))).astype(BF16)
    b_in_p = jnp.pad(b_in, ((0, 0), (0, N_IN_PAD - N_IN))).reshape(depth, 1, N_IN_PAD)
    w_out_b = w_out.astype(BF16)
    w1_b = w_ff1.astype(BF16)
    w2_b = w_ff2.astype(BF16)
    gq = jnp.tile(q_norm, (1, ATTN_HEADS)).reshape(depth, 1, ATTN_W)
    gk = jnp.tile(k_norm, (1, ATTN_KV_HEADS)).reshape(depth, 1, KV_W)
    gn = jnp.tile(hg_norm, (1, HG_HEADS)).reshape(depth, 1, HG_W)

    h_ctx, h_lat, lat_off = ctx, x, 0
    for l in range(depth):
        last = l == depth - 1
        attn_consts, unscale, use_fixed_shift = _attn_constants(q_norm[l], k_norm[l])
        qa, ka, vt, hgp, mqk, mvo, mvt, gcol, grow = _in_call(
            h_ctx, h_lat, lat_off, t, modsel[l], norm_mix[l].reshape(1, d), w_in_p[l], b_in_p[l], cos, sin,
            gq[l], gk[l], e_attn, attn_consts, nct)

        at = _attention(qa, ka, vt, unscale, use_fixed_shift, n_ctx)

        hg_o = _hgrn_call(hgp, lbs[l], e_hg, bd_hg, nct)

        qk = _conv_call(mqk, ml_conv[l], n_ctx)
        ml_o = _mlstm_call(qk, mvt, gcol, grow, mask_ml, mask_ml_t, nct)

        skip = nct if last else 0
        h = _out_ffn_call(h_ctx, h_lat, lat_off, t, modsel[l], at, hg_o, hgp, gn[l], e_hg_mean, ml_o, mvo,
                          w_out_b[l, :ATTN_W], w_out_b[l, ATTN_W:ATTN_W + HG_W], w_out_b[l, ATTN_W + HG_W:],
                          norm_ffn[l].reshape(1, d), w1_b[l], w2_b[l], nct, skip)
        h_ctx, h_lat, lat_off = h, h, nct
    return h
```
